```python
import math
import jax, jax.numpy as jnp
from jax import lax
import numpy as np

D_MODEL = 1024
BATCH = 4
SEQ = 4096
DEPTH = 1

HEAD_DIM = 64
NSA_HEADS = 8
NSA_KV_HEADS = 2
NSA_GROUP = NSA_HEADS // NSA_KV_HEADS
SB_HEADS = 8
NSA_WIDTH = NSA_HEADS * HEAD_DIM
SB_WIDTH = SB_HEADS * HEAD_DIM
MIX_WIDTH = NSA_WIDTH + SB_WIDTH
NSA_KV_WIDTH = NSA_KV_HEADS * HEAD_DIM
N_BRANCH = 3
CMP_LEN = 32
CMP_STRIDE = 16
CMP_HIDDEN = HEAD_DIM
SLC_LEN = 64
SLC_TOPN = 16
WINDOW = 512
Q_BLOCK = 128
ROPE_DIM = HEAD_DIM // 4
ROPE_THETA = 500000.0
EPS = 1e-6
FORCE_BONUS = 1.0e4
IN_SIZES = (NSA_WIDTH,
            NSA_KV_WIDTH, NSA_KV_WIDTH,
            NSA_KV_WIDTH, NSA_KV_WIDTH,
            NSA_KV_WIDTH, NSA_KV_WIDTH,
            NSA_HEADS * N_BRANCH,
            NSA_WIDTH,
            SB_WIDTH, SB_WIDTH, SB_WIDTH,
            SB_WIDTH)
IN_COLS = sum(IN_SIZES)

kernel_name = "hybrid_nsa_stickbreaking_layer"


def rms_norm(x, g):
    xf = x.astype(jnp.float32)
    y = xf * lax.rsqrt(jnp.mean(xf * xf, axis=-1, keepdims=True) + EPS)
    return (y * g.astype(jnp.float32)).astype(x.dtype)


def rope_partial(x, pos):
    inv_freq = jnp.power(ROPE_THETA, -jnp.arange(0, ROPE_DIM, 2, dtype=jnp.float32) / ROPE_DIM)
    ang = pos.astype(jnp.float32)[:, None] * inv_freq[None, :]
    cos = jnp.cos(ang).astype(x.dtype)
    sin = jnp.sin(ang).astype(x.dtype)
    half = ROPE_DIM // 2
    x1, x2, rest = x[..., :half], x[..., half:ROPE_DIM], x[..., ROPE_DIM:]
    return jnp.concatenate([x1 * cos - x2 * sin, x2 * cos + x1 * sin, rest], axis=-1)


def masked_softmax(logits, mask):
    logits = jnp.where(mask, logits.astype(jnp.float32), -jnp.inf)
    m = jnp.max(logits, axis=-1, keepdims=True)
    m = jnp.where(jnp.isfinite(m), m, 0.0)
    p = jnp.where(mask, jnp.exp(logits - m), 0.0)
    return p / jnp.maximum(jnp.sum(p, axis=-1, keepdims=True), 1e-30)


def compress(blocks, pos_emb, w1, b1, w2):
    b_, g_, n_, l_, d_ = blocks.shape
    flat = (blocks + pos_emb).reshape(b_, g_, n_, l_ * d_)
    return jax.nn.silu(flat @ w1 + b1) @ w2


def setup_inputs(seed: int = 0) -> dict:
    key = jax.random.key(seed)
    ks = jax.random.split(key, 20)
    f32 = jnp.float32
    nrm = lambda k, shape, s: jax.random.normal(k, shape, f32) * s
    return {
        "x": nrm(ks[0], (BATCH, SEQ, D_MODEL), 1.0),
        "norm_gain": 1.0 + nrm(ks[1], (D_MODEL,), 0.01),
        "w_in": nrm(ks[2], (D_MODEL, IN_COLS), D_MODEL ** -0.5),
        "q_norm_gain": 1.0 + nrm(ks[3], (HEAD_DIM,), 0.01),
        "k_norm_cmp": 1.0 + nrm(ks[4], (HEAD_DIM,), 0.01),
        "k_norm_slc": 1.0 + nrm(ks[5], (HEAD_DIM,), 0.01),
        "k_norm_win": 1.0 + nrm(ks[6], (HEAD_DIM,), 0.01),
        "cmp_k_pos": nrm(ks[7], (CMP_LEN, HEAD_DIM), 0.02),
        "cmp_k_w1": nrm(ks[8], (CMP_LEN * HEAD_DIM, CMP_HIDDEN), (CMP_LEN * HEAD_DIM) ** -0.5),
        "cmp_k_b1": nrm(ks[9], (CMP_HIDDEN,), 0.01),
        "cmp_k_w2": nrm(ks[10], (CMP_HIDDEN, HEAD_DIM), CMP_HIDDEN ** -0.5),
        "cmp_v_pos": nrm(ks[11], (CMP_LEN, HEAD_DIM), 0.02),
        "cmp_v_w1": nrm(ks[12], (CMP_LEN * HEAD_DIM, CMP_HIDDEN), (CMP_LEN * HEAD_DIM) ** -0.5),
        "cmp_v_b1": nrm(ks[13], (CMP_HIDDEN,), 0.01),
        "cmp_v_w2": nrm(ks[14], (CMP_HIDDEN, HEAD_DIM), CMP_HIDDEN ** -0.5),
        "w_out": nrm(ks[15], (MIX_WIDTH, D_MODEL), MIX_WIDTH ** -0.5),
    }


def reference(x, norm_gain, w_in, q_norm_gain, k_norm_cmp, k_norm_slc, k_norm_win,
              cmp_k_pos, cmp_k_w1, cmp_k_b1, cmp_k_w2,
              cmp_v_pos, cmp_v_w1, cmp_v_b1, cmp_v_w2, w_out):
    B, S, _ = x.shape
    G, R, D = NSA_KV_HEADS, NSA_GROUP, HEAD_DIM
    nq = S // Q_BLOCK
    scale = 1.0 / math.sqrt(D)
    pos = jnp.arange(S, dtype=jnp.int32)

    def heads(t, n):
        return t.reshape(B, S, n, D).transpose(0, 2, 1, 3)

    for _layer in range(DEPTH):
        h = rms_norm(x, norm_gain)
        proj = h @ w_in
        parts, off = [], 0
        for sz in IN_SIZES:
            parts.append(proj[..., off:off + sz])
            off += sz
        (q_n, kc, vc, ks_, vs_, kw, vw, gate_logits, gate_nsa,
         q_sb, k_sb, v_sb, gate_sb) = parts

        q = rope_partial(rms_norm(heads(q_n, NSA_HEADS), q_norm_gain), pos)
        qg = q.reshape(B, G, R, S, D)

        n_cmp = (S - CMP_LEN) // CMP_STRIDE + 1
        tok_idx = jnp.arange(n_cmp)[:, None] * CMP_STRIDE + jnp.arange(CMP_LEN)[None, :]
        kc_blk = heads(kc, G)[:, :, tok_idx]
        vc_blk = heads(vc, G)[:, :, tok_idx]
        cmp_end = jnp.arange(n_cmp, dtype=jnp.int32) * CMP_STRIDE + (CMP_LEN - 1)
        k_cmp = rope_partial(rms_norm(compress(kc_blk, cmp_k_pos, cmp_k_w1, cmp_k_b1, cmp_k_w2),
                                      k_norm_cmp), cmp_end)
        v_cmp = compress(vc_blk, cmp_v_pos, cmp_v_w1, cmp_v_b1, cmp_v_w2)
        cmp_logits = jnp.einsum('bgrtd,bgnd->bgrtn', qg, k_cmp) * scale
        cmp_mask = cmp_end[None, :] <= pos[:, None]
        p_cmp = masked_softmax(cmp_logits, cmp_mask)
        o_cmp = jnp.einsum('bgrtn,bgnd->bgrtd', p_cmp.astype(v_cmp.dtype), v_cmp)

        n_slc = S // SLC_LEN
        cs = jnp.arange(n_cmp) * CMP_STRIDE
        ss = jnp.arange(n_slc) * SLC_LEN
        overlap = jnp.clip(jnp.minimum(cs[:, None] + CMP_LEN, ss[None, :] + SLC_LEN)
                           - jnp.maximum(cs[:, None], ss[None, :]), 0, None)
        overlap = overlap.astype(jnp.float32) / CMP_LEN
        p_slc = jnp.einsum('bgrtn,nj->bgtj', p_cmp, overlap)
        blk_t = pos // SLC_LEN
        j_idx = jnp.arange(n_slc)
        slc_valid = j_idx[None, :] <= blk_t[:, None]
        forced = (j_idx[None, :] == 0) | (j_idx[None, :] == blk_t[:, None]) | (j_idx[None, :] == blk_t[:, None] - 1)
        sel_logit = jnp.where(slc_valid, p_slc + jnp.where(forced, FORCE_BONUS, 0.0), -jnp.inf)
        top_n = min(SLC_TOPN, n_slc)
        sel_score, sel_idx = lax.top_k(sel_logit, top_n)
        sel_ok = jnp.isfinite(sel_score)

        k_slc = rope_partial(rms_norm(heads(ks_, G), k_norm_slc), pos).reshape(B, G, n_slc, SLC_LEN, D)
        v_slc = heads(vs_, G).reshape(B, G, n_slc, SLC_LEN, D)
        qb = jnp.moveaxis(qg.reshape(B, G, R, nq, Q_BLOCK, D), 3, 0)
        idx_b = jnp.moveaxis(sel_idx.reshape(B, G, nq, Q_BLOCK, top_n), 2, 0)
        ok_b = jnp.moveaxis(sel_ok.reshape(B, G, nq, Q_BLOCK, top_n), 2, 0)
        pos_b = pos.reshape(nq, Q_BLOCK)
        b_ix = jnp.arange(B)[:, None, None, None]
        g_ix = jnp.arange(G)[None, :, None, None]
        n_keys = top_n * SLC_LEN

        def slc_block(args):
            q_i, idx_i, ok_i, t_i = args
            k_sel = k_slc[b_ix, g_ix, idx_i].reshape(B, G, Q_BLOCK, n_keys, D)
            v_sel = v_slc[b_ix, g_ix, idx_i].reshape(B, G, Q_BLOCK, n_keys, D)
            key_pos = (idx_i[..., None] * SLC_LEN + jnp.arange(SLC_LEN)).reshape(B, G, Q_BLOCK, n_keys)
            ok = jnp.repeat(ok_i, SLC_LEN, axis=-1) & (key_pos <= t_i[None, None, :, None])
            logits = jnp.einsum('bgrtd,bgtkd->bgrtk', q_i, k_sel) * scale
            p = masked_softmax(logits, ok[:, :, None])
            return jnp.einsum('bgrtk,bgtkd->bgrtd', p.astype(v_sel.dtype), v_sel)

        o_slc = lax.map(slc_block, (qb, idx_b, ok_b, pos_b))
        o_slc = jnp.moveaxis(o_slc, 0, 3).reshape(B, G, R, S, D)

        n_wb = WINDOW // Q_BLOCK
        k_win = rope_partial(rms_norm(heads(kw, G), k_norm_win), pos)
        v_win = heads(vw, G)
        padw = ((0, 0), (0, 0), (WINDOW, 0), (0, 0))
        kpad = jnp.pad(k_win, padw).reshape(B, G, nq + n_wb, Q_BLOCK, D)
        vpad = jnp.pad(v_win, padw).reshape(B, G, nq + n_wb, Q_BLOCK, D)
        k_band = jnp.concatenate([kpad[:, :, i:i + nq] for i in range(n_wb + 1)], axis=3)
        v_band = jnp.concatenate([vpad[:, :, i:i + nq] for i in range(n_wb + 1)], axis=3)
        band_len = (n_wb + 1) * Q_BLOCK
        kp = jnp.arange(nq)[:, None] * Q_BLOCK - WINDOW + jnp.arange(band_len)[None, :]
        qp = pos.reshape(nq, Q_BLOCK)
        dist = qp[:, :, None] - kp[:, None, :]
        win_mask = (kp[:, None, :] >= 0) & (dist >= 0) & (dist < WINDOW)
        qw = qg.reshape(B, G, R, nq, Q_BLOCK, D)
        w_logits = jnp.einsum('bgrnqd,bgnkd->bgrnqk', qw, k_band) * scale
        p_win = masked_softmax(w_logits, win_mask)
        o_win = jnp.einsum('bgrnqk,bgnkd->bgrnqd', p_win.astype(v_band.dtype), v_band).reshape(B, G, R, S, D)

        gates = jax.nn.sigmoid(gate_logits.reshape(B, S, NSA_HEADS, N_BRANCH).transpose(0, 2, 1, 3)
                               .reshape(B, G, R, S, N_BRANCH))
        o_nsa = gates[..., 0:1] * o_cmp + gates[..., 1:2] * o_slc + gates[..., 2:3] * o_win
        o_nsa = o_nsa.reshape(B, NSA_HEADS, S, D).transpose(0, 2, 1, 3).reshape(B, S, NSA_WIDTH)
        o_nsa = o_nsa * jax.nn.silu(gate_nsa)

        q_s = heads(q_sb, SB_HEADS)
        k_s = heads(k_sb, SB_HEADS)
        v_s = heads(v_sb, SB_HEADS)
        qsb_b = jnp.moveaxis(q_s.reshape(B, SB_HEADS, nq, Q_BLOCK, D), 2, 0)

        def sb_block(args):
            q_i, t_i = args
            z = jnp.einsum('bhtd,bhsd->bhts', q_i, k_s).astype(jnp.float32) * scale
            causal = pos[None, :] < t_i[:, None]
            log_1m = jnp.where(causal, jax.nn.log_sigmoid(-z), 0.0)
            after = lax.cumsum(log_1m, axis=3, reverse=True) - log_1m
            a = jnp.where(causal, jnp.exp(jax.nn.log_sigmoid(z) + after), 0.0)
            return jnp.einsum('bhts,bhsd->bhtd', a.astype(v_s.dtype), v_s)

        o_sb = lax.map(sb_block, (qsb_b, pos_b))
        o_sb = jnp.moveaxis(o_sb, 0, 2).reshape(B, SB_HEADS, S, D).transpose(0, 2, 1, 3).reshape(B, S, SB_WIDTH)
        o_sb = o_sb * jax.nn.silu(gate_sb)

        mix = jnp.concatenate([o_nsa, o_sb], axis=-1)
        x = x + mix @ w_out
    return x
```

```python
import functools
import math

import numpy as np
import jax
import jax.numpy as jnp
from jax import lax
from jax.experimental import pallas as pl
from jax.experimental.pallas import tpu as pltpu

HEAD_DIM = 64
NSA_HEADS = 8
NSA_KV_HEADS = 2
NSA_GROUP = NSA_HEADS // NSA_KV_HEADS
SB_HEADS = 8
NSA_WIDTH = NSA_HEADS * HEAD_DIM
SB_WIDTH = SB_HEADS * HEAD_DIM
NSA_KV_WIDTH = NSA_KV_HEADS * HEAD_DIM
N_BRANCH = 3
CMP_LEN = 32
CMP_STRIDE = 16
SLC_LEN = 64
SLC_TOPN = 16
WINDOW = 512
Q_BLOCK = 128
ROPE_DIM = HEAD_DIM // 4
ROPE_HALF = ROPE_DIM // 2
ROPE_THETA = 500000.0
EPS = 1e-6
FORCE_BONUS = 1.0e4
SCALE = 1.0 / math.sqrt(HEAD_DIM)

LANES = 128
MASK_NEG = -1.0e30
M_INIT = -3.0e38
ROW_TILE = 512
VMEM_LIMIT = 56 * 1024 * 1024

_MXU_DTYPE = jnp.bfloat16
F32 = jnp.float32

C_Q = 0
C_KC = 512
C_VC = 640
C_KS = 768
C_VS = 896
C_KW = 1024
C_VW = 1152
C_GN = 1280
C_QSB = 1792
C_KSB = 2304
C_VSB = 2816
C_GSB = 3328
C_GL = 3840
N_COLS = 3968


def _nt_dot(a, b):
    return lax.dot_general(a, b, (((1,), (1,)), ((), ())), preferred_element_type=F32)


def _dot(a, b):
    return jnp.dot(a, b, preferred_element_type=F32)


def _split2(v):
    hi = v.astype(_MXU_DTYPE)
    lo = (v - hi.astype(F32)).astype(_MXU_DTYPE)
    return hi, lo


def _split3(v):
    hi = v.astype(_MXU_DTYPE)
    r1 = v - hi.astype(F32)
    mid = r1.astype(_MXU_DTYPE)
    lo = (r1 - mid.astype(F32)).astype(_MXU_DTYPE)
    return hi, mid, lo


def _inproj_kernel(x_ref, ng_ref, w_ref, qg_ref, kgs_ref, kgw_ref, bd_ref, cos_ref, sin_ref, eg_ref,
                   q_ref, kc_ref, vc_ref, ksa_ref, vsa_ref, kw_ref, vwa_ref,
                   gc_ref, gs_ref, gw_ref, qsb_ref, ksb_ref, vsb_ref, gsb_ref, *, n_sblk):
    tm = x_ref.shape[0]
    x = x_ref[...]
    ms = jnp.mean(x * x, axis=-1, keepdims=True)
    h = (x * lax.rsqrt(ms + EPS) * ng_ref[...]).astype(_MXU_DTYPE)

    def proj(lo, width):
        return _dot(h, w_ref[:, lo:lo + width])

    lane = lax.broadcasted_iota(jnp.int32, (tm, LANES), 1)
    low_half = lane < HEAD_DIM

    def head_norm_rope(y, gain):
        width = y.shape[1]
        rep = width // LANES
        hi, lo = _split2(y * y)
        bd = bd_ref[:width, :width]
        ssum = _dot(hi, bd) + _dot(lo, bd)
        yn = y * lax.rsqrt(ssum * (1.0 / HEAD_DIM) + EPS) * gain
        cos = jnp.concatenate([cos_ref[...]] * rep, axis=1) if rep > 1 else cos_ref[...]
        sin = jnp.concatenate([sin_ref[...]] * rep, axis=1) if rep > 1 else sin_ref[...]
        fwd = pltpu.roll(yn, ROPE_HALF, axis=1)
        bwd = pltpu.roll(yn, width - ROPE_HALF, axis=1)
        lane_w = lax.broadcasted_iota(jnp.int32, (tm, width), 1)
        partner = jnp.where((lane_w & (HEAD_DIM - 1)) < ROPE_HALF, bwd, fwd)
        return yn * cos + partner * sin

    def head_pair(slab, p):
        chunk = slab[:, p * LANES:(p + 1) * LANES]
        return chunk, pltpu.roll(chunk, HEAD_DIM, axis=1)

    qn = head_norm_rope(proj(C_Q, NSA_WIDTH), qg_ref[...]) * SCALE
    for p in range(NSA_HEADS // 2):
        ev, od = head_pair(qn, p)
        q_ref[0, 2 * p] = jnp.where(low_half, ev, 0.0).astype(q_ref.dtype)
        q_ref[0, 2 * p + 1] = jnp.where(low_half, od, 0.0).astype(q_ref.dtype)

    for ref, col in ((kc_ref, C_KC), (vc_ref, C_VC)):
        ev, od = head_pair(proj(col, NSA_KV_WIDTH), 0)
        ref[0, 0] = ev[:, :HEAD_DIM]
        ref[0, 1] = od[:, :HEAD_DIM]

    sblk = lax.rem(pl.program_id(0), n_sblk)
    row = lax.broadcasted_iota(jnp.int32, (tm, LANES), 0)
    key_blk = (sblk * tm + row) >> int(math.log2(SLC_LEN))
    onehot = jnp.where(lane - HEAD_DIM == key_blk, 1.0, 0.0)
    ks = head_norm_rope(proj(C_KS, NSA_KV_WIDTH), kgs_ref[...])
    ev, od = head_pair(ks, 0)
    ksa_ref[0, 0] = jnp.where(low_half, ev, onehot).astype(ksa_ref.dtype)
    ksa_ref[0, 1] = jnp.where(low_half, od, onehot).astype(ksa_ref.dtype)
    ev, od = head_pair(proj(C_VS, NSA_KV_WIDTH), 0)
    vsa_ref[0, 0] = jnp.where(low_half, ev, 1.0).astype(vsa_ref.dtype)
    vsa_ref[0, 1] = jnp.where(low_half, od, 1.0).astype(vsa_ref.dtype)

    kw = head_norm_rope(proj(C_KW, NSA_KV_WIDTH), kgw_ref[...])
    ev, od = head_pair(kw, 0)
    kw_ref[0, 0] = ev[:, :HEAD_DIM].astype(kw_ref.dtype)
    kw_ref[0, 1] = od[:, :HEAD_DIM].astype(kw_ref.dtype)
    ev, od = head_pair(proj(C_VW, NSA_KV_WIDTH), 0)
    vwa_ref[0, 0] = jnp.where(low_half, ev, 1.0).astype(vwa_ref.dtype)
    vwa_ref[0, 1] = jnp.where(low_half, od, 1.0).astype(vwa_ref.dtype)

    gn = proj(C_GN, NSA_WIDTH)
    silu_n = gn * jax.nn.sigmoid(gn)
    gl_hi, gl_lo = _split2(proj(C_GL, LANES))
    for br, ref in enumerate((gc_ref, gs_ref, gw_ref)):
        e = eg_ref[:, br * NSA_WIDTH:(br + 1) * NSA_WIDTH]
        ref[...] = jax.nn.sigmoid(_dot(gl_hi, e) + _dot(gl_lo, e)) * silu_n

    qsb_ref[...] = (proj(C_QSB, SB_WIDTH) * SCALE).astype(qsb_ref.dtype)
    ksb_ref[...] = proj(C_KSB, SB_WIDTH).astype(ksb_ref.dtype)
    vsb_ref[...] = proj(C_VSB, SB_WIDTH).astype(vsb_ref.dtype)
    gsb = proj(C_GSB, SB_WIDTH)
    gsb_ref[...] = gsb * jax.nn.sigmoid(gsb)


def _compress_kernel(kc_ref, vc_ref, posk_ref, posv_ref, w1k_ref, b1k_ref, w2k_ref,
                     w1v_ref, b1v_ref, w2v_ref, kg_ref, cos_ref, sin_ref, perm_ref,
                     kcmp_ref, vcmp_ref):
    half = CMP_STRIDE * HEAD_DIM

    def phi(c_ref, pos_ref, w1_ref, b1_ref, w2_ref):
        c = c_ref[0]
        n = c.shape[0]
        top = _dot((c + pos_ref[0:1, :]).astype(_MXU_DTYPE), w1_ref[:half, :])
        bot = _dot((c + pos_ref[1:2, :]).astype(_MXU_DTYPE), w1_ref[half:, :])
        hid = top + pltpu.roll(bot, n - 1, axis=0) + b1_ref[...]
        return _dot((hid * jax.nn.sigmoid(hid)).astype(_MXU_DTYPE), w2_ref[...])

    k = phi(kc_ref, posk_ref, w1k_ref, b1k_ref, w2k_ref)
    ms = jnp.mean(k * k, axis=-1, keepdims=True)
    kn = k * lax.rsqrt(ms + EPS) * kg_ref[...]
    hi, lo = _split2(kn)
    partner = _dot(hi, perm_ref[...]) + _dot(lo, perm_ref[...])
    kcmp_ref[0] = (kn * cos_ref[...] + partner * sin_ref[...]).astype(kcmp_ref.dtype)
    vcmp_ref[0] = phi(vc_ref, posv_ref, w1v_ref, b1v_ref, w2v_ref).astype(vcmp_ref.dtype)


def _nsa_kernel(q_ref, kc_ref, vc_ref, ksa_ref, vsa_ref, kw_ref, vwa_ref, ovl_ref,
                gc_ref, gs_ref, gw_ref, o_ref):
    i = pl.program_id(2)
    rq = NSA_GROUP * Q_BLOCK
    n_blk_lanes = LANES - HEAD_DIM
    q_pad = q_ref[0].reshape(rq, LANES)
    q = q_pad[:, :HEAD_DIM]
    t_col = i * Q_BLOCK + (lax.broadcasted_iota(jnp.int32, (rq, 1), 0) & (Q_BLOCK - 1))

    kc = kc_ref[0, 0]
    ncp = kc.shape[0]
    lg = _nt_dot(q, kc)
    cmp_end = lax.broadcasted_iota(jnp.int32, (rq, ncp), 1) * CMP_STRIDE + (CMP_LEN - 1)
    valid = cmp_end <= t_col
    lg = jnp.where(valid, lg, MASK_NEG)
    m = jnp.max(lg, axis=-1, keepdims=True)
    p = jnp.where(valid, jnp.exp(lg - m), 0.0)
    p = p / jnp.maximum(jnp.sum(p, axis=-1, keepdims=True), 1e-30)
    o_cmp = _dot(p.astype(_MXU_DTYPE), vc_ref[0, 0])

    p_sum = p[0:Q_BLOCK]
    for r in range(1, NSA_GROUP):
        p_sum = p_sum + p[r * Q_BLOCK:(r + 1) * Q_BLOCK]
    ovl = ovl_ref[...]
    p_slc = sum(_nt_dot(ovl, part) for part in _split3(p_sum))
    j_idx = lax.broadcasted_iota(jnp.int32, (n_blk_lanes, Q_BLOCK), 0)
    blk_t = (i * Q_BLOCK + lax.broadcasted_iota(jnp.int32, (n_blk_lanes, Q_BLOCK), 1)) >> int(math.log2(SLC_LEN))
    slc_valid = j_idx <= blk_t
    forced = (j_idx == 0) | (j_idx == blk_t) | (j_idx == blk_t - 1)
    score = jnp.where(slc_valid, p_slc + jnp.where(forced, FORCE_BONUS, 0.0), -jnp.inf)
    rank = jnp.zeros((n_blk_lanes, Q_BLOCK), F32)
    for ii in range(n_blk_lanes):
        s_i = score[ii:ii + 1, :]
        rank = rank + jnp.where(j_idx > ii, jnp.where(s_i >= score, 1.0, 0.0), jnp.where(s_i > score, 1.0, 0.0))
    sel_t = jnp.where(slc_valid, jnp.where(rank < SLC_TOPN, 1.0, 0.0), 0.0)
    sel = jnp.concatenate([jnp.ones((HEAD_DIM, Q_BLOCK), F32), sel_t], axis=0).T
    lane_q = lax.broadcasted_iota(jnp.int32, (Q_BLOCK, LANES), 1)
    bias = jnp.where(lane_q >= HEAD_DIM, jnp.where(sel > 0.5, 0.0, MASK_NEG), 0.0).astype(q_pad.dtype)
    q_aug = q_pad + jnp.concatenate([bias] * NSA_GROUP, axis=0)

    lane_k = lax.broadcasted_iota(jnp.int32, (rq, Q_BLOCK), 1)

    def flash_step(s, v, carry):
        m_old, acc = carry
        m_new = jnp.maximum(m_old, jnp.max(s, axis=-1, keepdims=True))
        alpha = jnp.exp(m_old - m_new)
        pexp = jnp.exp(s - m_new)
        return m_new, alpha * acc + _dot(pexp.astype(_MXU_DTYPE), v)

    init = (jnp.full((rq, 1), M_INIT, F32), jnp.zeros((rq, LANES), F32))

    def slc_body(kt, carry):
        start = pl.multiple_of(kt * Q_BLOCK, Q_BLOCK)
        s = _nt_dot(q_aug, ksa_ref[0, 0, pl.ds(start, Q_BLOCK), :])
        s = jnp.where(start + lane_k <= t_col, s, MASK_NEG)
        return flash_step(s, vsa_ref[0, 0, pl.ds(start, Q_BLOCK), :], carry)

    _, acc_s = lax.fori_loop(0, i + 1, slc_body, init)

    def win_body(kk, carry):
        start = pl.multiple_of((i - kk) * Q_BLOCK, Q_BLOCK)
        s = _nt_dot(q, kw_ref[0, 0, pl.ds(start, Q_BLOCK), :])
        dist = t_col - (start + lane_k)
        s = jnp.where((dist >= 0) & (dist < WINDOW), s, MASK_NEG)
        return flash_step(s, vwa_ref[0, 0, pl.ds(start, Q_BLOCK), :], carry)

    _, acc_w = lax.fori_loop(0, jnp.minimum(i, WINDOW // Q_BLOCK) + 1, win_body, init)

    def finish(acc):
        return acc[:, :HEAD_DIM] / acc[:, HEAD_DIM:HEAD_DIM + 1]

    def token_major(o):
        return jnp.concatenate([o[r * Q_BLOCK:(r + 1) * Q_BLOCK] for r in range(NSA_GROUP)], axis=1)

    out = (gc_ref[0] * token_major(o_cmp) + gs_ref[0] * token_major(finish(acc_s))
           + gw_ref[0] * token_major(finish(acc_w)))
    o_ref[0] = out.astype(o_ref.dtype)


def _sb_kernel(q_ref, k_ref, v_ref, uu_ref, g_ref, o_ref):
    i = pl.program_id(2)
    t_col = i * Q_BLOCK + lax.broadcasted_iota(jnp.int32, (Q_BLOCK, 1), 0)
    lane = lax.broadcasted_iota(jnp.int32, (Q_BLOCK, LANES), 1)
    low_half = lane < HEAD_DIM
    q_pair = q_ref[0]
    zero = jnp.zeros_like(q_pair)
    q_heads = (jnp.where(low_half, q_pair, zero), jnp.where(low_half, zero, q_pair))

    def body(kk, carry):
        start = pl.multiple_of((i - kk) * Q_BLOCK, Q_BLOCK)
        k_pair = k_ref[0, pl.ds(start, Q_BLOCK), :]
        v_pair = v_ref[0, pl.ds(start, Q_BLOCK), :]
        causal = start + lane < t_col
        new = []
        for q_h, (acc, later) in zip(q_heads, carry):
            z = _nt_dot(q_h, k_pair)
            sp = jnp.maximum(z, 0.0) + jnp.log(1.0 + jnp.exp(-jnp.abs(z)))
            log_1m = jnp.where(causal, -sp, 0.0)
            hi, lo = _split2(log_1m)
            r = _dot(jnp.concatenate([hi, lo], axis=1), uu_ref[...])
            after = r[:, :LANES] + later
            a = jnp.where(causal, jnp.exp(z - sp + after), 0.0)
            new.append((acc + _dot(a.astype(_MXU_DTYPE), v_pair), later + r[:, LANES:]))
        return tuple(new)

    zeros = jnp.zeros((Q_BLOCK, LANES), F32)
    (acc0, _), (acc1, _) = lax.fori_loop(0, i + 1, body, ((zeros, zeros), (zeros, zeros)))
    o_ref[0] = (jnp.where(low_half, acc0, acc1) * g_ref[0]).astype(o_ref.dtype)


def _outproj_kernel(x_ref, on_ref, os_ref, w_ref, o_ref):
    o_ref[...] = (x_ref[...] + _dot(on_ref[...], w_ref[:NSA_WIDTH, :])
                  + _dot(os_ref[...], w_ref[NSA_WIDTH:, :]))


def _rope_tables(pos, reps):
    inv_freq = jnp.power(ROPE_THETA, -jnp.arange(0, ROPE_DIM, 2, dtype=F32) / ROPE_DIM)
    ang = pos.astype(F32)[:, None] * inv_freq[None, :]
    cos, sin = jnp.cos(ang), jnp.sin(ang)
    n = pos.shape[0]
    rest = HEAD_DIM - ROPE_DIM
    cos_h = jnp.concatenate([cos, cos, jnp.ones((n, rest), F32)], axis=1)
    sin_h = jnp.concatenate([-sin, sin, jnp.zeros((n, rest), F32)], axis=1)
    return jnp.tile(cos_h, (1, reps)), jnp.tile(sin_h, (1, reps))


def _const_spec(shape):
    return pl.BlockSpec(shape, lambda *_: (0,) * len(shape))


def kernel(x, norm_gain, w_in, q_norm_gain, k_norm_cmp, k_norm_slc, k_norm_win,
           cmp_k_pos, cmp_k_w1, cmp_k_b1, cmp_k_w2, cmp_v_pos, cmp_v_w1, cmp_v_b1, cmp_v_w2, w_out):
    B, S, DM = x.shape
    D, G, R = HEAD_DIM, NSA_KV_HEADS, NSA_GROUP
    mxu = _MXU_DTYPE
    n_tok = B * S
    tm = ROW_TILE
    n_sblk = S // tm
    nq = S // Q_BLOCK
    ncp = S // CMP_STRIDE
    n_slc = S // SLC_LEN
    n_cmp = (S - CMP_LEN) // CMP_STRIDE + 1
    assert S % tm == 0 and ncp % LANES == 0 and n_slc <= LANES - D and n_slc >= SLC_TOPN

    n_gl = NSA_HEADS * N_BRANCH
    gl0 = C_GN + n_gl
    w_cat = jnp.concatenate([w_in[:, :C_GN], w_in[:, gl0:], w_in[:, C_GN:gl0],
                             jnp.zeros((DM, LANES - n_gl), w_in.dtype)], axis=1).astype(mxu)
    assert w_cat.shape[1] == N_COLS
    pos = jnp.arange(S, dtype=jnp.int32)
    cos_t, sin_t = _rope_tables(pos, LANES // D)
    cmp_end = jnp.arange(ncp, dtype=jnp.int32) * CMP_STRIDE + (CMP_LEN - 1)
    cos_c, sin_c = _rope_tables(cmp_end, 1)
    lane_i = np.arange(NSA_WIDTH)
    bd = jnp.asarray(lane_i[:, None] // D == lane_i[None, :] // D, mxu)
    eg = np.zeros((LANES, N_BRANCH * NSA_WIDTH), np.float32)
    for hh in range(NSA_HEADS):
        for br in range(N_BRANCH):
            eg[hh * N_BRANCH + br, br * NSA_WIDTH + hh * D:br * NSA_WIDTH + (hh + 1) * D] = 1.0
    eg = jnp.asarray(eg, mxu)
    perm = np.zeros((D, D), np.float32)
    for c in range(ROPE_HALF):
        perm[c + ROPE_HALF, c] = 1.0
        perm[c, c + ROPE_HALF] = 1.0
    perm = jnp.asarray(perm, mxu)
    cs = np.arange(ncp) * CMP_STRIDE
    ss = np.arange(LANES - D) * SLC_LEN
    ovl = np.clip(np.minimum(cs[None, :] + CMP_LEN, ss[:, None] + SLC_LEN)
                  - np.maximum(cs[None, :], ss[:, None]), 0, None).astype(np.float32) / CMP_LEN
    ovl[:, n_cmp:] = 0.0
    ovl[n_slc:, :] = 0.0
    ovl = jnp.asarray(ovl, mxu)
    sidx = np.arange(LANES)
    tri = (sidx[:, None] > sidx[None, :]).astype(np.float32)
    uu_half = np.concatenate([tri, np.ones((LANES, LANES), np.float32)], axis=1)
    uu = jnp.asarray(np.concatenate([uu_half, uu_half], axis=0), mxu)

    row = lambda v: v.reshape(1, -1).astype(F32)
    x2 = x.reshape(n_tok, DM)

    tok_spec = lambda w: pl.BlockSpec((tm, w), lambda t: (t, 0))
    head_spec = lambda nh, w: pl.BlockSpec((1, nh, tm, w), lambda t: (t // n_sblk, 0, t % n_sblk, 0))
    tab_spec = pl.BlockSpec((tm, LANES), lambda t: (t % n_sblk, 0))
    sds = jax.ShapeDtypeStruct
    outs = pl.pallas_call(
        functools.partial(_inproj_kernel, n_sblk=n_sblk),
        grid=(n_tok // tm,),
        in_specs=[tok_spec(DM), _const_spec((1, DM)), _const_spec((DM, N_COLS)),
                  _const_spec((1, NSA_WIDTH)), _const_spec((1, NSA_KV_WIDTH)), _const_spec((1, NSA_KV_WIDTH)),
                  _const_spec((NSA_WIDTH, NSA_WIDTH)), tab_spec, tab_spec,
                  _const_spec((LANES, N_BRANCH * NSA_WIDTH))],
        out_specs=[head_spec(NSA_HEADS, LANES), head_spec(G, D), head_spec(G, D),
                   head_spec(G, LANES), head_spec(G, LANES), head_spec(G, D), head_spec(G, LANES),
                   tok_spec(NSA_WIDTH), tok_spec(NSA_WIDTH), tok_spec(NSA_WIDTH),
                   tok_spec(SB_WIDTH), tok_spec(SB_WIDTH), tok_spec(SB_WIDTH), tok_spec(SB_WIDTH)],
        out_shape=[sds((B, NSA_HEADS, S, LANES), mxu), sds((B, G, S, D), F32), sds((B, G, S, D), F32),
                   sds((B, G, S, LANES), mxu), sds((B, G, S, LANES), mxu), sds((B, G, S, D), mxu),
                   sds((B, G, S, LANES), mxu),
                   sds((n_tok, NSA_WIDTH), F32), sds((n_tok, NSA_WIDTH), F32), sds((n_tok, NSA_WIDTH), F32),
                   sds((n_tok, SB_WIDTH), mxu), sds((n_tok, SB_WIDTH), mxu), sds((n_tok, SB_WIDTH), mxu),
                   sds((n_tok, SB_WIDTH), F32)],
        compiler_params=pltpu.CompilerParams(dimension_semantics=("parallel",), vmem_limit_bytes=VMEM_LIMIT),
        name="inproj",
    )(x2, row(norm_gain), w_cat, row(jnp.tile(q_norm_gain, NSA_HEADS)),
      row(jnp.tile(k_norm_slc, G)), row(jnp.tile(k_norm_win, G)), bd, cos_t, sin_t, eg)
    (q_nsa, kc_raw, vc_raw, ks_aug, vs_aug, k_win, vw_aug, g_cmp, g_slc, g_win,
     q_sb, k_sb, v_sb, g_sb) = outs

    chunk_w = CMP_STRIDE * D
    chunks = lambda a: a.reshape(B * G, ncp, chunk_w)
    bg_spec = lambda r, w: pl.BlockSpec((1, r, w), lambda t: (t, 0, 0))
    k_cmp, v_cmp = pl.pallas_call(
        _compress_kernel,
        grid=(B * G,),
        in_specs=[bg_spec(ncp, chunk_w), bg_spec(ncp, chunk_w),
                  _const_spec((2, chunk_w)), _const_spec((2, chunk_w)),
                  _const_spec((CMP_LEN * D, D)), _const_spec((1, D)), _const_spec((D, D)),
                  _const_spec((CMP_LEN * D, D)), _const_spec((1, D)), _const_spec((D, D)),
                  _const_spec((1, D)), _const_spec((ncp, D)), _const_spec((ncp, D)), _const_spec((D, D))],
        out_specs=[bg_spec(ncp, D), bg_spec(ncp, D)],
        out_shape=[sds((B * G, ncp, D), mxu), sds((B * G, ncp, D), mxu)],
        compiler_params=pltpu.CompilerParams(dimension_semantics=("parallel",), vmem_limit_bytes=VMEM_LIMIT),
        name="compress",
    )(chunks(kc_raw), chunks(vc_raw), cmp_k_pos.reshape(2, chunk_w), cmp_v_pos.reshape(2, chunk_w),
      cmp_k_w1.astype(mxu), row(cmp_k_b1), cmp_k_w2.astype(mxu),
      cmp_v_w1.astype(mxu), row(cmp_v_b1), cmp_v_w2.astype(mxu),
      row(k_norm_cmp), cos_c, sin_c, perm)
    k_cmp = k_cmp.reshape(B, G, ncp, D)
    v_cmp = v_cmp.reshape(B, G, ncp, D)

    kv_spec = lambda r, w: pl.BlockSpec((1, 1, r, w), lambda b, g, i: (b, g, 0, 0))
    gate_spec = pl.BlockSpec((1, Q_BLOCK, R * D), lambda b, g, i: (b, i, g))
    g3 = lambda a: a.reshape(B, S, NSA_WIDTH)
    o_nsa = pl.pallas_call(
        _nsa_kernel,
        grid=(B, G, nq),
        in_specs=[pl.BlockSpec((1, R, Q_BLOCK, LANES), lambda b, g, i: (b, g, i, 0)),
                  kv_spec(ncp, D), kv_spec(ncp, D), kv_spec(S, LANES), kv_spec(S, LANES),
                  kv_spec(S, D), kv_spec(S, LANES), _const_spec((LANES - D, ncp)),
                  gate_spec, gate_spec, gate_spec],
        out_specs=gate_spec,
        out_shape=sds((B, S, NSA_WIDTH), mxu),
        compiler_params=pltpu.CompilerParams(dimension_semantics=("parallel", "parallel", "arbitrary"),
                                             vmem_limit_bytes=VMEM_LIMIT),
        name="nsa",
    )(q_nsa, k_cmp, v_cmp, ks_aug, vs_aug, k_win, vw_aug, ovl, g3(g_cmp), g3(g_slc), g3(g_win))

    sb3 = lambda a: a.reshape(B, S, SB_WIDTH)
    pair_q = pl.BlockSpec((1, Q_BLOCK, LANES), lambda b, hp, i: (b, i, hp))
    pair_kv = pl.BlockSpec((1, S, LANES), lambda b, hp, i: (b, 0, hp))
    o_sb = pl.pallas_call(
        _sb_kernel,
        grid=(B, SB_WIDTH // LANES, nq),
        in_specs=[pair_q, pair_kv, pair_kv, _const_spec((2 * LANES, 2 * LANES)), pair_q],
        out_specs=pair_q,
        out_shape=sds((B, S, SB_WIDTH), mxu),
        compiler_params=pltpu.CompilerParams(dimension_semantics=("parallel", "parallel", "arbitrary"),
                                             vmem_limit_bytes=VMEM_LIMIT),
        name="stickbreak",
    )(sb3(q_sb), sb3(k_sb), sb3(v_sb), uu, sb3(g_sb))

    out = pl.pallas_call(
        _outproj_kernel,
        grid=(n_tok // tm,),
        in_specs=[tok_spec(DM), tok_spec(NSA_WIDTH), tok_spec(SB_WIDTH),
                  _const_spec((NSA_WIDTH + SB_WIDTH, DM))],
        out_specs=tok_spec(DM),
        out_shape=sds((n_tok, DM), x.dtype),
        compiler_params=pltpu.CompilerParams(dimension_semantics=("parallel",), vmem_limit_bytes=VMEM_LIMIT),
        name="outproj",
    )(x2, o_nsa.reshape(n_tok, NSA_WIDTH), o_sb.reshape(n_tok, SB_WIDTH), w_out.astype(mxu))
    return out.reshape(B, S, DM)
```

```python
import functools
import math

import numpy as np
import jax
import jax.numpy as jnp
from jax import lax
from jax.experimental import pallas as pl
from jax.experimental.pallas import tpu as pltpu

HEAD_DIM = 64
NSA_HEADS = 8
NSA_KV_HEADS = 2
NSA_GROUP = NSA_HEADS // NSA_KV_HEADS
SB_HEADS = 8
NSA_WIDTH = NSA_HEADS * HEAD_DIM
SB_WIDTH = SB_HEADS * HEAD_DIM
NSA_KV_WIDTH = NSA_KV_HEADS * HEAD_DIM
N_BRANCH = 3
CMP_LEN = 32
CMP_STRIDE = 16
SLC_LEN = 64
SLC_TOPN = 16
WINDOW = 512
Q_BLOCK = 128
ROPE_DIM = HEAD_DIM // 4
ROPE_HALF = ROPE_DIM // 2
ROPE_THETA = 500000.0
EPS = 1e-6
FORCE_BONUS = 1.0e4
SCALE = 1.0 / math.sqrt(HEAD_DIM)
LOG2E = math.log2(math.e)

LANES = 128
MASK_NEG = -1.0e30
M_INIT = -3.0e38
ROW_TILE = 512
SB_ROWS = 512
SB_UNROLL = 4
VMEM_LIMIT = 56 * 1024 * 1024

_MXU_DTYPE = jnp.bfloat16
F32 = jnp.float32

C_Q = 0
C_KC = 512
C_VC = 640
C_KS = 768
C_VS = 896
C_KW = 1024
C_VW = 1152
C_GN = 1280
C_QSB = 1792
C_KSB = 2304
C_VSB = 2816
C_GSB = 3328
C_GL = 3840
N_COLS = 3968


def _nt_dot(a, b):
    return lax.dot_general(a, b, (((1,), (1,)), ((), ())), preferred_element_type=F32)


def _dot(a, b):
    return jnp.dot(a, b, preferred_element_type=F32)


def _split2(v):
    hi = v.astype(_MXU_DTYPE)
    lo = (v - hi.astype(F32)).astype(_MXU_DTYPE)
    return hi, lo


def _split3(v):
    hi = v.astype(_MXU_DTYPE)
    r1 = v - hi.astype(F32)
    mid = r1.astype(_MXU_DTYPE)
    lo = (r1 - mid.astype(F32)).astype(_MXU_DTYPE)
    return hi, mid, lo


def _inproj_kernel(x_ref, ng_ref, w_ref, qg_ref, kgs_ref, kgw_ref, bd_ref, cos_ref, sin_ref, eg_ref,
                   q_ref, kc_ref, vc_ref, ksa_ref, vsa_ref, kw_ref, vwa_ref,
                   gc_ref, gs_ref, gw_ref, qsb_ref, ksb_ref, vsb_ref, gsb_ref, *, n_sblk):
    tm = x_ref.shape[0]
    x = x_ref[...]
    ms = jnp.mean(x * x, axis=-1, keepdims=True)
    h = (x * lax.rsqrt(ms + EPS) * ng_ref[...]).astype(_MXU_DTYPE)

    def proj(lo, width):
        return _dot(h, w_ref[:, lo:lo + width])

    lane = lax.broadcasted_iota(jnp.int32, (tm, LANES), 1)
    low_half = lane < HEAD_DIM

    def head_norm_rope(y, gain):
        width = y.shape[1]
        rep = width // LANES
        hi, lo = _split2(y * y)
        bd = bd_ref[:width, :width]
        ssum = _dot(hi, bd) + _dot(lo, bd)
        yn = y * lax.rsqrt(ssum * (1.0 / HEAD_DIM) + EPS) * gain
        cos = jnp.concatenate([cos_ref[...]] * rep, axis=1) if rep > 1 else cos_ref[...]
        sin = jnp.concatenate([sin_ref[...]] * rep, axis=1) if rep > 1 else sin_ref[...]
        fwd = pltpu.roll(yn, ROPE_HALF, axis=1)
        bwd = pltpu.roll(yn, width - ROPE_HALF, axis=1)
        lane_w = lax.broadcasted_iota(jnp.int32, (tm, width), 1)
        partner = jnp.where((lane_w & (HEAD_DIM - 1)) < ROPE_HALF, bwd, fwd)
        return yn * cos + partner * sin

    def head_pair(slab, p):
        chunk = slab[:, p * LANES:(p + 1) * LANES]
        return chunk, pltpu.roll(chunk, HEAD_DIM, axis=1)

    qn = head_norm_rope(proj(C_Q, NSA_WIDTH), qg_ref[...]) * SCALE
    for p in range(NSA_HEADS // 2):
        ev, od = head_pair(qn, p)
        q_ref[0, 2 * p] = jnp.where(low_half, ev, 0.0).astype(q_ref.dtype)
        q_ref[0, 2 * p + 1] = jnp.where(low_half, od, 0.0).astype(q_ref.dtype)

    for ref, col in ((kc_ref, C_KC), (vc_ref, C_VC)):
        ev, od = head_pair(proj(col, NSA_KV_WIDTH), 0)
        ref[0, 0] = ev[:, :HEAD_DIM]
        ref[0, 1] = od[:, :HEAD_DIM]

    sblk = lax.rem(pl.program_id(0), n_sblk)
    row = lax.broadcasted_iota(jnp.int32, (tm, LANES), 0)
    key_blk = (sblk * tm + row) >> int(math.log2(SLC_LEN))
    onehot = jnp.where(lane - HEAD_DIM == key_blk, 1.0, 0.0)
    ks = head_norm_rope(proj(C_KS, NSA_KV_WIDTH), kgs_ref[...])
    ev, od = head_pair(ks, 0)
    ksa_ref[0, 0] = jnp.where(low_half, ev, onehot).astype(ksa_ref.dtype)
    ksa_ref[0, 1] = jnp.where(low_half, od, onehot).astype(ksa_ref.dtype)
    ev, od = head_pair(proj(C_VS, NSA_KV_WIDTH), 0)
    vsa_ref[0, 0] = jnp.where(low_half, ev, 1.0).astype(vsa_ref.dtype)
    vsa_ref[0, 1] = jnp.where(low_half, od, 1.0).astype(vsa_ref.dtype)

    kw = head_norm_rope(proj(C_KW, NSA_KV_WIDTH), kgw_ref[...])
    ev, od = head_pair(kw, 0)
    kw_ref[0, 0] = ev[:, :HEAD_DIM].astype(kw_ref.dtype)
    kw_ref[0, 1] = od[:, :HEAD_DIM].astype(kw_ref.dtype)
    ev, od = head_pair(proj(C_VW, NSA_KV_WIDTH), 0)
    vwa_ref[0, 0] = jnp.where(low_half, ev, 1.0).astype(vwa_ref.dtype)
    vwa_ref[0, 1] = jnp.where(low_half, od, 1.0).astype(vwa_ref.dtype)

    gn = proj(C_GN, NSA_WIDTH)
    silu_n = gn * jax.nn.sigmoid(gn)
    gl_hi, gl_lo = _split2(proj(C_GL, LANES))
    for br, ref in enumerate((gc_ref, gs_ref, gw_ref)):
        e = eg_ref[:, br * NSA_WIDTH:(br + 1) * NSA_WIDTH]
        ref[...] = jax.nn.sigmoid(_dot(gl_hi, e) + _dot(gl_lo, e)) * silu_n

    qsb_ref[...] = (proj(C_QSB, SB_WIDTH) * (SCALE * LOG2E)).astype(qsb_ref.dtype)
    ksb_ref[...] = proj(C_KSB, SB_WIDTH).astype(ksb_ref.dtype)
    vsb_ref[...] = proj(C_VSB, SB_WIDTH).astype(vsb_ref.dtype)
    gsb = proj(C_GSB, SB_WIDTH)
    gsb_ref[...] = gsb * jax.nn.sigmoid(gsb)


def _compress_kernel(kc_ref, vc_ref, posk_ref, posv_ref, w1k_ref, b1k_ref, w2k_ref,
                     w1v_ref, b1v_ref, w2v_ref, kg_ref, cos_ref, sin_ref, perm_ref,
                     kcmp_ref, vcmp_ref):
    half = CMP_STRIDE * HEAD_DIM

    def phi(c_ref, pos_ref, w1_ref, b1_ref, w2_ref):
        c = c_ref[0]
        n = c.shape[0]
        top = _dot((c + pos_ref[0:1, :]).astype(_MXU_DTYPE), w1_ref[:half, :])
        bot = _dot((c + pos_ref[1:2, :]).astype(_MXU_DTYPE), w1_ref[half:, :])
        hid = top + pltpu.roll(bot, n - 1, axis=0) + b1_ref[...]
        return _dot((hid * jax.nn.sigmoid(hid)).astype(_MXU_DTYPE), w2_ref[...])

    k = phi(kc_ref, posk_ref, w1k_ref, b1k_ref, w2k_ref)
    ms = jnp.mean(k * k, axis=-1, keepdims=True)
    kn = k * lax.rsqrt(ms + EPS) * kg_ref[...]
    hi, lo = _split2(kn)
    partner = _dot(hi, perm_ref[...]) + _dot(lo, perm_ref[...])
    kcmp_ref[0] = (kn * cos_ref[...] + partner * sin_ref[...]).astype(kcmp_ref.dtype)
    vcmp_ref[0] = phi(vc_ref, posv_ref, w1v_ref, b1v_ref, w2v_ref).astype(vcmp_ref.dtype)


def _nsa_kernel(q_ref, kc_ref, vc_ref, ksa_ref, vsa_ref, kw_ref, vwa_ref, ovl_ref,
                gc_ref, gs_ref, gw_ref, o_ref):
    i = pl.program_id(2)
    rq = NSA_GROUP * Q_BLOCK
    n_blk_lanes = LANES - HEAD_DIM
    q_pad = q_ref[0].reshape(rq, LANES)
    q = q_pad[:, :HEAD_DIM]
    t_col = i * Q_BLOCK + (lax.broadcasted_iota(jnp.int32, (rq, 1), 0) & (Q_BLOCK - 1))

    kc = kc_ref[0, 0]
    ncp = kc.shape[0]
    lg = _nt_dot(q, kc)
    cmp_end = lax.broadcasted_iota(jnp.int32, (rq, ncp), 1) * CMP_STRIDE + (CMP_LEN - 1)
    valid = cmp_end <= t_col
    lg = jnp.where(valid, lg, MASK_NEG)
    m = jnp.max(lg, axis=-1, keepdims=True)
    p = jnp.where(valid, jnp.exp(lg - m), 0.0)
    p = p / jnp.maximum(jnp.sum(p, axis=-1, keepdims=True), 1e-30)
    o_cmp = _dot(p.astype(_MXU_DTYPE), vc_ref[0, 0])

    p_sum = p[0:Q_BLOCK]
    for r in range(1, NSA_GROUP):
        p_sum = p_sum + p[r * Q_BLOCK:(r + 1) * Q_BLOCK]
    ovl = ovl_ref[...]
    p_slc = sum(_nt_dot(ovl, part) for part in _split3(p_sum))
    j_idx = lax.broadcasted_iota(jnp.int32, (n_blk_lanes, Q_BLOCK), 0)
    blk_t = (i * Q_BLOCK + lax.broadcasted_iota(jnp.int32, (n_blk_lanes, Q_BLOCK), 1)) >> int(math.log2(SLC_LEN))
    slc_valid = j_idx <= blk_t
    forced = (j_idx == 0) | (j_idx == blk_t) | (j_idx == blk_t - 1)
    score = jnp.where(slc_valid, p_slc + jnp.where(forced, FORCE_BONUS, 0.0), -jnp.inf)
    rank = jnp.zeros((n_blk_lanes, Q_BLOCK), F32)
    for ii in range(n_blk_lanes):
        s_i = score[ii:ii + 1, :]
        rank = rank + jnp.where(j_idx > ii, jnp.where(s_i >= score, 1.0, 0.0), jnp.where(s_i > score, 1.0, 0.0))
    sel_t = jnp.where(slc_valid, jnp.where(rank < SLC_TOPN, 1.0, 0.0), 0.0)
    sel = jnp.concatenate([jnp.ones((HEAD_DIM, Q_BLOCK), F32), sel_t], axis=0).T
    lane_q = lax.broadcasted_iota(jnp.int32, (Q_BLOCK, LANES), 1)
    bias = jnp.where(lane_q >= HEAD_DIM, jnp.where(sel > 0.5, 0.0, MASK_NEG), 0.0).astype(q_pad.dtype)
    q_aug = q_pad + jnp.concatenate([bias] * NSA_GROUP, axis=0)

    lane_k = lax.broadcasted_iota(jnp.int32, (rq, Q_BLOCK), 1)

    def flash_step(s, v, carry):
        m_old, acc = carry
        m_new = jnp.maximum(m_old, jnp.max(s, axis=-1, keepdims=True))
        alpha = jnp.exp(m_old - m_new)
        pexp = jnp.exp(s - m_new)
        return m_new, alpha * acc + _dot(pexp.astype(_MXU_DTYPE), v)

    init = (jnp.full((rq, 1), M_INIT, F32), jnp.zeros((rq, LANES), F32))

    def slc_body(kt, carry):
        start = pl.multiple_of(kt * Q_BLOCK, Q_BLOCK)
        s = _nt_dot(q_aug, ksa_ref[0, 0, pl.ds(start, Q_BLOCK), :])
        s = jnp.where(start + lane_k <= t_col, s, MASK_NEG)
        return flash_step(s, vsa_ref[0, 0, pl.ds(start, Q_BLOCK), :], carry)

    _, acc_s = lax.fori_loop(0, i + 1, slc_body, init)

    def win_body(kk, carry):
        start = pl.multiple_of((i - kk) * Q_BLOCK, Q_BLOCK)
        s = _nt_dot(q, kw_ref[0, 0, pl.ds(start, Q_BLOCK), :])
        dist = t_col - (start + lane_k)
        s = jnp.where((dist >= 0) & (dist < WINDOW), s, MASK_NEG)
        return flash_step(s, vwa_ref[0, 0, pl.ds(start, Q_BLOCK), :], carry)

    _, acc_w = lax.fori_loop(0, jnp.minimum(i, WINDOW // Q_BLOCK) + 1, win_body, init)

    def finish(acc):
        return acc[:, :HEAD_DIM] / acc[:, HEAD_DIM:HEAD_DIM + 1]

    def token_major(o):
        return jnp.concatenate([o[r * Q_BLOCK:(r + 1) * Q_BLOCK] for r in range(NSA_GROUP)], axis=1)

    out = (gc_ref[0] * token_major(o_cmp) + gs_ref[0] * token_major(finish(acc_s))
           + gw_ref[0] * token_major(finish(acc_w)))
    o_ref[0] = out.astype(o_ref.dtype)


def _sb_kernel(q_ref, k_ref, v_ref, uu_ref, g_ref, o_ref):
    i = pl.program_id(2)
    rows = q_ref.shape[1]
    band = rows // Q_BLOCK
    n_tiles = (i + 1) * band
    t_col = i * rows + lax.broadcasted_iota(jnp.int32, (rows, 1), 0)
    lane = lax.broadcasted_iota(jnp.int32, (rows, LANES), 1)
    low_half = lane < HEAD_DIM
    low_half_k = lax.broadcasted_iota(jnp.int32, (Q_BLOCK, LANES), 1) < HEAD_DIM
    q_pair = q_ref[0]
    zero = jnp.zeros_like(q_pair)
    q_heads = (jnp.where(low_half, q_pair, zero), jnp.where(low_half, zero, q_pair))

    def make_body(masked):
        def tile_step(kk, carry):
            acc, laters = carry[0], carry[1:]
            start = pl.multiple_of((n_tiles - 1 - kk) * Q_BLOCK, Q_BLOCK)
            k_pair = k_ref[0, pl.ds(start, Q_BLOCK), :]
            v_pair = v_ref[0, pl.ds(start, Q_BLOCK), :]
            zero_v = jnp.zeros_like(v_pair)
            v_bd = jnp.concatenate([jnp.where(low_half_k, v_pair, zero_v),
                                    jnp.where(low_half_k, zero_v, v_pair)], axis=0)
            causal = start + lane < t_col
            weights, new_laters = [], []
            for q_h, later in zip(q_heads, laters):
                z = _nt_dot(q_h, k_pair)
                sp = jnp.maximum(z, 0.0) + jnp.log2(1.0 + jnp.exp2(-jnp.abs(z)))
                if masked:
                    sp = jnp.where(causal, sp, 0.0)
                hi, lo = _split2(sp)
                r = _dot(jnp.concatenate([hi, lo], axis=1), uu_ref[...])
                after = r[:, :LANES] + later
                a = jnp.exp2(z - sp - after)
                if masked:
                    a = jnp.where(causal, a, 0.0)
                weights.append(a.astype(_MXU_DTYPE))
                new_laters.append(later + r[:, LANES:])
            acc = acc + _dot(jnp.concatenate(weights, axis=1), v_bd)
            return (acc, *new_laters)

        def body(kg, carry):
            for u in range(SB_UNROLL):
                carry = tile_step(kg * SB_UNROLL + u, carry)
            return carry
        return body

    zeros = jnp.zeros((rows, LANES), F32)
    carry = lax.fori_loop(0, band // SB_UNROLL, make_body(True), (zeros, zeros, zeros))
    carry = lax.fori_loop(band // SB_UNROLL, n_tiles // SB_UNROLL, make_body(False), carry)
    o_ref[0] = (carry[0] * g_ref[0]).astype(o_ref.dtype)


def _outproj_kernel(x_ref, on_ref, os_ref, w_ref, o_ref):
    o_ref[...] = (x_ref[...] + _dot(on_ref[...], w_ref[:NSA_WIDTH, :])
                  + _dot(os_ref[...], w_ref[NSA_WIDTH:, :]))


def _rope_tables(pos, reps):
    inv_freq = jnp.power(ROPE_THETA, -jnp.arange(0, ROPE_DIM, 2, dtype=F32) / ROPE_DIM)
    ang = pos.astype(F32)[:, None] * inv_freq[None, :]
    cos, sin = jnp.cos(ang), jnp.sin(ang)
    n = pos.shape[0]
    rest = HEAD_DIM - ROPE_DIM
    cos_h = jnp.concatenate([cos, cos, jnp.ones((n, rest), F32)], axis=1)
    sin_h = jnp.concatenate([-sin, sin, jnp.zeros((n, rest), F32)], axis=1)
    return jnp.tile(cos_h, (1, reps)), jnp.tile(sin_h, (1, reps))


def _const_spec(shape):
    return pl.BlockSpec(shape, lambda *_: (0,) * len(shape))


def kernel(x, norm_gain, w_in, q_norm_gain, k_norm_cmp, k_norm_slc, k_norm_win,
           cmp_k_pos, cmp_k_w1, cmp_k_b1, cmp_k_w2, cmp_v_pos, cmp_v_w1, cmp_v_b1, cmp_v_w2, w_out):
    B, S, DM = x.shape
    D, G, R = HEAD_DIM, NSA_KV_HEADS, NSA_GROUP
    mxu = _MXU_DTYPE
    n_tok = B * S
    tm = ROW_TILE
    n_sblk = S // tm
    nq = S // Q_BLOCK
    ncp = S // CMP_STRIDE
    n_slc = S // SLC_LEN
    n_cmp = (S - CMP_LEN) // CMP_STRIDE + 1
    assert S % tm == 0 and ncp % LANES == 0 and n_slc <= LANES - D and n_slc >= SLC_TOPN

    n_gl = NSA_HEADS * N_BRANCH
    gl0 = C_GN + n_gl
    w_cat = jnp.concatenate([w_in[:, :C_GN], w_in[:, gl0:], w_in[:, C_GN:gl0],
                             jnp.zeros((DM, LANES - n_gl), w_in.dtype)], axis=1).astype(mxu)
    assert w_cat.shape[1] == N_COLS
    pos = jnp.arange(S, dtype=jnp.int32)
    cos_t, sin_t = _rope_tables(pos, LANES // D)
    cmp_end = jnp.arange(ncp, dtype=jnp.int32) * CMP_STRIDE + (CMP_LEN - 1)
    cos_c, sin_c = _rope_tables(cmp_end, 1)
    lane_i = np.arange(NSA_WIDTH)
    bd = jnp.asarray(lane_i[:, None] // D == lane_i[None, :] // D, mxu)
    eg = np.zeros((LANES, N_BRANCH * NSA_WIDTH), np.float32)
    for hh in range(NSA_HEADS):
        for br in range(N_BRANCH):
            eg[hh * N_BRANCH + br, br * NSA_WIDTH + hh * D:br * NSA_WIDTH + (hh + 1) * D] = 1.0
    eg = jnp.asarray(eg, mxu)
    perm = np.zeros((D, D), np.float32)
    for c in range(ROPE_HALF):
        perm[c + ROPE_HALF, c] = 1.0
        perm[c, c + ROPE_HALF] = 1.0
    perm = jnp.asarray(perm, mxu)
    cs = np.arange(ncp) * CMP_STRIDE
    ss = np.arange(LANES - D) * SLC_LEN
    ovl = np.clip(np.minimum(cs[None, :] + CMP_LEN, ss[:, None] + SLC_LEN)
                  - np.maximum(cs[None, :], ss[:, None]), 0, None).astype(np.float32) / CMP_LEN
    ovl[:, n_cmp:] = 0.0
    ovl[n_slc:, :] = 0.0
    ovl = jnp.asarray(ovl, mxu)
    sidx = np.arange(LANES)
    tri = (sidx[:, None] > sidx[None, :]).astype(np.float32)
    uu_half = np.concatenate([tri, np.ones((LANES, LANES), np.float32)], axis=1)
    uu = jnp.asarray(np.concatenate([uu_half, uu_half], axis=0), mxu)

    row = lambda v: v.reshape(1, -1).astype(F32)
    x2 = x.reshape(n_tok, DM)

    tok_spec = lambda w: pl.BlockSpec((tm, w), lambda t: (t, 0))
    head_spec = lambda nh, w: pl.BlockSpec((1, nh, tm, w), lambda t: (t // n_sblk, 0, t % n_sblk, 0))
    tab_spec = pl.BlockSpec((tm, LANES), lambda t: (t % n_sblk, 0))
    sds = jax.ShapeDtypeStruct
    outs = pl.pallas_call(
        functools.partial(_inproj_kernel, n_sblk=n_sblk),
        grid=(n_tok // tm,),
        in_specs=[tok_spec(DM), _const_spec((1, DM)), _const_spec((DM, N_COLS)),
                  _const_spec((1, NSA_WIDTH)), _const_spec((1, NSA_KV_WIDTH)), _const_spec((1, NSA_KV_WIDTH)),
                  _const_spec((NSA_WIDTH, NSA_WIDTH)), tab_spec, tab_spec,
                  _const_spec((LANES, N_BRANCH * NSA_WIDTH))],
        out_specs=[head_spec(NSA_HEADS, LANES), head_spec(G, D), head_spec(G, D),
                   head_spec(G, LANES), head_spec(G, LANES), head_spec(G, D), head_spec(G, LANES),
                   tok_spec(NSA_WIDTH), tok_spec(NSA_WIDTH), tok_spec(NSA_WIDTH),
                   tok_spec(SB_WIDTH), tok_spec(SB_WIDTH), tok_spec(SB_WIDTH), tok_spec(SB_WIDTH)],
        out_shape=[sds((B, NSA_HEADS, S, LANES), mxu), sds((B, G, S, D), F32), sds((B, G, S, D), F32),
                   sds((B, G, S, LANES), mxu), sds((B, G, S, LANES), mxu), sds((B, G, S, D), mxu),
                   sds((B, G, S, LANES), mxu),
                   sds((n_tok, NSA_WIDTH), F32), sds((n_tok, NSA_WIDTH), F32), sds((n_tok, NSA_WIDTH), F32),
                   sds((n_tok, SB_WIDTH), mxu), sds((n_tok, SB_WIDTH), mxu), sds((n_tok, SB_WIDTH), mxu),
                   sds((n_tok, SB_WIDTH), F32)],
        compiler_params=pltpu.CompilerParams(dimension_semantics=("parallel",), vmem_limit_bytes=VMEM_LIMIT),
        name="inproj",
    )(x2, row(norm_gain), w_cat, row(jnp.tile(q_norm_gain, NSA_HEADS)),
      row(jnp.tile(k_norm_slc, G)), row(jnp.tile(k_norm_win, G)), bd, cos_t, sin_t, eg)
    (q_nsa, kc_raw, vc_raw, ks_aug, vs_aug, k_win, vw_aug, g_cmp, g_slc, g_win,
     q_sb, k_sb, v_sb, g_sb) = outs

    chunk_w = CMP_STRIDE * D
    chunks = lambda a: a.reshape(B * G, ncp, chunk_w)
    bg_spec = lambda r, w: pl.BlockSpec((1, r, w), lambda t: (t, 0, 0))
    k_cmp, v_cmp = pl.pallas_call(
        _compress_kernel,
        grid=(B * G,),
        in_specs=[bg_spec(ncp, chunk_w), bg_spec(ncp, chunk_w),
                  _const_spec((2, chunk_w)), _const_spec((2, chunk_w)),
                  _const_spec((CMP_LEN * D, D)), _const_spec((1, D)), _const_spec((D, D)),
                  _const_spec((CMP_LEN * D, D)), _const_spec((1, D)), _const_spec((D, D)),
                  _const_spec((1, D)), _const_spec((ncp, D)), _const_spec((ncp, D)), _const_spec((D, D))],
        out_specs=[bg_spec(ncp, D), bg_spec(ncp, D)],
        out_shape=[sds((B * G, ncp, D), mxu), sds((B * G, ncp, D), mxu)],
        compiler_params=pltpu.CompilerParams(dimension_semantics=("parallel",), vmem_limit_bytes=VMEM_LIMIT),
        name="compress",
    )(chunks(kc_raw), chunks(vc_raw), cmp_k_pos.reshape(2, chunk_w), cmp_v_pos.reshape(2, chunk_w),
      cmp_k_w1.astype(mxu), row(cmp_k_b1), cmp_k_w2.astype(mxu),
      cmp_v_w1.astype(mxu), row(cmp_v_b1), cmp_v_w2.astype(mxu),
      row(k_norm_cmp), cos_c, sin_c, perm)
    k_cmp = k_cmp.reshape(B, G, ncp, D)
    v_cmp = v_cmp.reshape(B, G, ncp, D)

    kv_spec = lambda r, w: pl.BlockSpec((1, 1, r, w), lambda b, g, i: (b, g, 0, 0))
    gate_spec = pl.BlockSpec((1, Q_BLOCK, R * D), lambda b, g, i: (b, i, g))
    g3 = lambda a: a.reshape(B, S, NSA_WIDTH)
    o_nsa = pl.pallas_call(
        _nsa_kernel,
        grid=(B, G, nq),
        in_specs=[pl.BlockSpec((1, R, Q_BLOCK, LANES), lambda b, g, i: (b, g, i, 0)),
                  kv_spec(ncp, D), kv_spec(ncp, D), kv_spec(S, LANES), kv_spec(S, LANES),
                  kv_spec(S, D), kv_spec(S, LANES), _const_spec((LANES - D, ncp)),
                  gate_spec, gate_spec, gate_spec],
        out_specs=gate_spec,
        out_shape=sds((B, S, NSA_WIDTH), mxu),
        compiler_params=pltpu.CompilerParams(dimension_semantics=("parallel", "parallel", "arbitrary"),
                                             vmem_limit_bytes=VMEM_LIMIT),
        name="nsa",
    )(q_nsa, k_cmp, v_cmp, ks_aug, vs_aug, k_win, vw_aug, ovl, g3(g_cmp), g3(g_slc), g3(g_win))

    sb3 = lambda a: a.reshape(B, S, SB_WIDTH)
    pair_q = pl.BlockSpec((1, SB_ROWS, LANES), lambda b, hp, i: (b, i, hp))
    pair_kv = pl.BlockSpec((1, S, LANES), lambda b, hp, i: (b, 0, hp))
    o_sb = pl.pallas_call(
        _sb_kernel,
        grid=(B, SB_WIDTH // LANES, S // SB_ROWS),
        in_specs=[pair_q, pair_kv, pair_kv, _const_spec((2 * LANES, 2 * LANES)), pair_q],
        out_specs=pair_q,
        out_shape=sds((B, S, SB_WIDTH), mxu),
        compiler_params=pltpu.CompilerParams(dimension_semantics=("parallel", "parallel", "arbitrary"),
                                             vmem_limit_bytes=VMEM_LIMIT),
        name="stickbreak",
    )(sb3(q_sb), sb3(k_sb), sb3(v_sb), uu, sb3(g_sb))

    out = pl.pallas_call(
        _outproj_kernel,
        grid=(n_tok // tm,),
        in_specs=[tok_spec(DM), tok_spec(NSA_WIDTH), tok_spec(SB_WIDTH),
                  _const_spec((NSA_WIDTH + SB_WIDTH, DM))],
        out_specs=tok_spec(DM),
        out_shape=sds((n_tok, DM), x.dtype),
        compiler_params=pltpu.CompilerParams(dimension_semantics=("parallel",), vmem_limit_bytes=VMEM_LIMIT),
        name="outproj",
    )(x2, o_nsa.reshape(n_tok, NSA_WIDTH), o_sb.reshape(n_tok, SB_WIDTH), w_out.astype(mxu))
    return out.reshape(B, S, DM)
```

```python
import functools
import math

import numpy as np
import jax
import jax.numpy as jnp
from jax import lax
from jax.experimental import pallas as pl
from jax.experimental.pallas import tpu as pltpu

HEAD_DIM = 64
NSA_HEADS = 8
NSA_KV_HEADS = 2
NSA_GROUP = NSA_HEADS // NSA_KV_HEADS
SB_HEADS = 8
NSA_WIDTH = NSA_HEADS * HEAD_DIM
SB_WIDTH = SB_HEADS * HEAD_DIM
NSA_KV_WIDTH = NSA_KV_HEADS * HEAD_DIM
N_BRANCH = 3
CMP_LEN = 32
CMP_STRIDE = 16
SLC_LEN = 64
SLC_TOPN = 16
WINDOW = 512
Q_BLOCK = 128
ROPE_DIM = HEAD_DIM // 4
ROPE_HALF = ROPE_DIM // 2
ROPE_THETA = 500000.0
EPS = 1e-6
FORCE_BONUS = 1.0e4
SCALE = 1.0 / math.sqrt(HEAD_DIM)
LOG2E = math.log2(math.e)

LANES = 128
MASK_NEG = -1.0e30
M_INIT = -3.0e38
ROW_TILE = 512
NSA_QT = 256
SB_ROWS = 512
SB_UNROLL = 4
VMEM_LIMIT = 56 * 1024 * 1024

_MXU_DTYPE = jnp.bfloat16
F32 = jnp.float32

C_Q = 0
C_KC = 512
C_VC = 640
C_KS = 768
C_VS = 896
C_KW = 1024
C_VW = 1152
C_GN = 1280
C_QSB = 1792
C_KSB = 2304
C_VSB = 2816
C_GSB = 3328
C_GL = 3840
N_COLS = 3968


def _nt_dot(a, b):
    return lax.dot_general(a, b, (((1,), (1,)), ((), ())), preferred_element_type=F32)


def _dot(a, b):
    return jnp.dot(a, b, preferred_element_type=F32)


def _split2(v):
    hi = v.astype(_MXU_DTYPE)
    lo = (v - hi.astype(F32)).astype(_MXU_DTYPE)
    return hi, lo


def _split3(v):
    hi = v.astype(_MXU_DTYPE)
    r1 = v - hi.astype(F32)
    mid = r1.astype(_MXU_DTYPE)
    lo = (r1 - mid.astype(F32)).astype(_MXU_DTYPE)
    return hi, mid, lo


def _inproj_kernel(x_ref, ng_ref, w_ref, qg_ref, kgs_ref, kgw_ref, bd_ref, cos_ref, sin_ref, eg_ref,
                   q_ref, kc_ref, vc_ref, ksa_ref, vsa_ref, kw_ref, vwa_ref,
                   gc_ref, gs_ref, gw_ref, qsb_ref, ksb_ref, vsb_ref, gsb_ref, *, n_sblk):
    tm = x_ref.shape[0]
    x = x_ref[...]
    ms = jnp.mean(x * x, axis=-1, keepdims=True)
    h = (x * lax.rsqrt(ms + EPS) * ng_ref[...]).astype(_MXU_DTYPE)

    def proj(lo, width):
        return _dot(h, w_ref[:, lo:lo + width])

    lane = lax.broadcasted_iota(jnp.int32, (tm, LANES), 1)
    low_half = lane < HEAD_DIM

    def head_norm_rope(y, gain):
        width = y.shape[1]
        rep = width // LANES
        hi, lo = _split2(y * y)
        bd = bd_ref[:width, :width]
        ssum = _dot(hi, bd) + _dot(lo, bd)
        yn = y * lax.rsqrt(ssum * (1.0 / HEAD_DIM) + EPS) * gain
        cos = jnp.concatenate([cos_ref[...]] * rep, axis=1) if rep > 1 else cos_ref[...]
        sin = jnp.concatenate([sin_ref[...]] * rep, axis=1) if rep > 1 else sin_ref[...]
        fwd = pltpu.roll(yn, ROPE_HALF, axis=1)
        bwd = pltpu.roll(yn, width - ROPE_HALF, axis=1)
        lane_w = lax.broadcasted_iota(jnp.int32, (tm, width), 1)
        partner = jnp.where((lane_w & (HEAD_DIM - 1)) < ROPE_HALF, bwd, fwd)
        return yn * cos + partner * sin

    def head_pair(slab, p):
        chunk = slab[:, p * LANES:(p + 1) * LANES]
        return chunk, pltpu.roll(chunk, HEAD_DIM, axis=1)

    qn = head_norm_rope(proj(C_Q, NSA_WIDTH), qg_ref[...]) * (SCALE * LOG2E)
    for p in range(NSA_HEADS // 2):
        ev, od = head_pair(qn, p)
        q_ref[0, 2 * p] = jnp.where(low_half, ev, 0.0).astype(q_ref.dtype)
        q_ref[0, 2 * p + 1] = jnp.where(low_half, od, 0.0).astype(q_ref.dtype)

    for ref, col in ((kc_ref, C_KC), (vc_ref, C_VC)):
        ev, od = head_pair(proj(col, NSA_KV_WIDTH), 0)
        ref[0, 0] = ev[:, :HEAD_DIM]
        ref[0, 1] = od[:, :HEAD_DIM]

    sblk = lax.rem(pl.program_id(0), n_sblk)
    row = lax.broadcasted_iota(jnp.int32, (tm, LANES), 0)
    key_blk = (sblk * tm + row) >> int(math.log2(SLC_LEN))
    onehot = jnp.where(lane - HEAD_DIM == key_blk, 1.0, 0.0)
    ks = head_norm_rope(proj(C_KS, NSA_KV_WIDTH), kgs_ref[...])
    ev, od = head_pair(ks, 0)
    ksa_ref[0, 0] = jnp.where(low_half, ev, onehot).astype(ksa_ref.dtype)
    ksa_ref[0, 1] = jnp.where(low_half, od, onehot).astype(ksa_ref.dtype)
    ev, od = head_pair(proj(C_VS, NSA_KV_WIDTH), 0)
    vsa_ref[0, 0] = jnp.where(low_half, ev, 1.0).astype(vsa_ref.dtype)
    vsa_ref[0, 1] = jnp.where(low_half, od, 1.0).astype(vsa_ref.dtype)

    kw = head_norm_rope(proj(C_KW, NSA_KV_WIDTH), kgw_ref[...])
    ev, od = head_pair(kw, 0)
    kw_ref[0, 0] = ev[:, :HEAD_DIM].astype(kw_ref.dtype)
    kw_ref[0, 1] = od[:, :HEAD_DIM].astype(kw_ref.dtype)
    ev, od = head_pair(proj(C_VW, NSA_KV_WIDTH), 0)
    vwa_ref[0, 0] = jnp.where(low_half, ev, 1.0).astype(vwa_ref.dtype)
    vwa_ref[0, 1] = jnp.where(low_half, od, 1.0).astype(vwa_ref.dtype)

    gn = proj(C_GN, NSA_WIDTH)
    silu_n = gn * jax.nn.sigmoid(gn)
    gl_hi, gl_lo = _split2(proj(C_GL, LANES))
    for br, ref in enumerate((gc_ref, gs_ref, gw_ref)):
        e = eg_ref[:, br * NSA_WIDTH:(br + 1) * NSA_WIDTH]
        ref[...] = jax.nn.sigmoid(_dot(gl_hi, e) + _dot(gl_lo, e)) * silu_n

    qsb_ref[...] = (proj(C_QSB, SB_WIDTH) * (SCALE * LOG2E)).astype(qsb_ref.dtype)
    ksb_ref[...] = proj(C_KSB, SB_WIDTH).astype(ksb_ref.dtype)
    vsb_ref[...] = proj(C_VSB, SB_WIDTH).astype(vsb_ref.dtype)
    gsb = proj(C_GSB, SB_WIDTH)
    gsb_ref[...] = gsb * jax.nn.sigmoid(gsb)


def _compress_kernel(kc_ref, vc_ref, posk_ref, posv_ref, w1k_ref, b1k_ref, w2k_ref,
                     w1v_ref, b1v_ref, w2v_ref, kg_ref, cos_ref, sin_ref, perm_ref,
                     kcmp_ref, vcmp_ref):
    half = CMP_STRIDE * HEAD_DIM

    def phi(c_ref, pos_ref, w1_ref, b1_ref, w2_ref):
        c = c_ref[0]
        n = c.shape[0]
        top = _dot((c + pos_ref[0:1, :]).astype(_MXU_DTYPE), w1_ref[:half, :])
        bot = _dot((c + pos_ref[1:2, :]).astype(_MXU_DTYPE), w1_ref[half:, :])
        hid = top + pltpu.roll(bot, n - 1, axis=0) + b1_ref[...]
        return _dot((hid * jax.nn.sigmoid(hid)).astype(_MXU_DTYPE), w2_ref[...])

    k = phi(kc_ref, posk_ref, w1k_ref, b1k_ref, w2k_ref)
    ms = jnp.mean(k * k, axis=-1, keepdims=True)
    kn = k * lax.rsqrt(ms + EPS) * kg_ref[...]
    hi, lo = _split2(kn)
    partner = _dot(hi, perm_ref[...]) + _dot(lo, perm_ref[...])
    kcmp_ref[0] = (kn * cos_ref[...] + partner * sin_ref[...]).astype(kcmp_ref.dtype)
    vcmp_ref[0] = phi(vc_ref, posv_ref, w1v_ref, b1v_ref, w2v_ref).astype(vcmp_ref.dtype)


def _nsa_kernel(q_ref, kc_ref, vc_ref, ksa_ref, vsa_ref, kw_ref, vwa_ref, ovl_ref,
                gc_ref, gs_ref, gw_ref, o_ref):
    i = pl.program_id(2)
    qt = q_ref.shape[2]
    rq = NSA_GROUP * qt
    n_blk_lanes = LANES - HEAD_DIM
    q_pad = q_ref[0].reshape(rq, LANES)
    q = q_pad[:, :HEAD_DIM]
    t_loc = lax.broadcasted_iota(jnp.int32, (rq, 1), 0) & (qt - 1)
    t_col = i * qt + t_loc

    kc = kc_ref[0, 0]
    ncp = kc.shape[0]
    lg = _nt_dot(q, kc)
    cmp_end = lax.broadcasted_iota(jnp.int32, (rq, ncp), 1) * CMP_STRIDE + (CMP_LEN - 1)
    valid = cmp_end <= t_col
    lg = jnp.where(valid, lg, MASK_NEG)
    m = jnp.max(lg, axis=-1, keepdims=True)
    p = jnp.where(valid, jnp.exp2(lg - m), 0.0)
    p = p / jnp.maximum(jnp.sum(p, axis=-1, keepdims=True), 1e-30)
    o_cmp = _dot(p.astype(_MXU_DTYPE), vc_ref[0, 0])

    p_sum = p[0:qt]
    for r in range(1, NSA_GROUP):
        p_sum = p_sum + p[r * qt:(r + 1) * qt]
    ovl = ovl_ref[...]
    p_slc = sum(_nt_dot(ovl, part) for part in _split3(p_sum))
    j_idx = lax.broadcasted_iota(jnp.int32, (n_blk_lanes, qt), 0)
    blk_t = (i * qt + lax.broadcasted_iota(jnp.int32, (n_blk_lanes, qt), 1)) >> int(math.log2(SLC_LEN))
    slc_valid = j_idx <= blk_t
    forced = (j_idx == 0) | (j_idx == blk_t) | (j_idx == blk_t - 1)
    score = jnp.where(slc_valid, p_slc + jnp.where(forced, FORCE_BONUS, 0.0), -jnp.inf)
    rank = jnp.zeros((n_blk_lanes, qt), F32)
    for ii in range(n_blk_lanes):
        s_i = score[ii:ii + 1, :]
        rank = rank + jnp.where(j_idx > ii, jnp.where(s_i >= score, 1.0, 0.0), jnp.where(s_i > score, 1.0, 0.0))
    sel_t = jnp.where(slc_valid, jnp.where(rank < SLC_TOPN, 1.0, 0.0), 0.0)
    sel = jnp.concatenate([jnp.ones((HEAD_DIM, qt), F32), sel_t], axis=0).T
    lane_q = lax.broadcasted_iota(jnp.int32, (qt, LANES), 1)
    bias = jnp.where(lane_q >= HEAD_DIM, jnp.where(sel > 0.5, 0.0, MASK_NEG), 0.0).astype(q_pad.dtype)
    q_aug = q_pad + jnp.concatenate([bias] * NSA_GROUP, axis=0)

    rel = t_loc - lax.broadcasted_iota(jnp.int32, (rq, qt), 1)

    def flash_step(s, v, carry):
        m_old, acc = carry
        m_new = jnp.maximum(m_old, jnp.max(s, axis=-1, keepdims=True))
        alpha = jnp.exp2(m_old - m_new)
        pexp = jnp.exp2(s - m_new)
        return m_new, alpha * acc + _dot(pexp.astype(_MXU_DTYPE), v)

    def finish(acc):
        return acc[:, :HEAD_DIM] / acc[:, HEAD_DIM:HEAD_DIM + 1]

    def slc_logits(kt):
        start = pl.multiple_of(kt * qt, qt)
        return _nt_dot(q_aug, ksa_ref[0, 0, pl.ds(start, qt), :])

    def slc_values(kt):
        return vsa_ref[0, 0, pl.ds(pl.multiple_of(kt * qt, qt), qt), :]

    def slc_body(kt, carry):
        s_cur, m_run, acc = carry
        s_next = slc_logits(kt + 1)
        m_run, acc = flash_step(s_cur, slc_values(kt), (m_run, acc))
        return s_next, m_run, acc

    init = (jnp.full((rq, 1), M_INIT, F32), jnp.zeros((rq, LANES), F32))
    s_diag, m_s, acc_s = lax.fori_loop(0, i, slc_body, (slc_logits(0),) + init)
    _, acc_s = flash_step(jnp.where(rel >= 0, s_diag, MASK_NEG), slc_values(i), (m_s, acc_s))

    n_win = WINDOW // qt + 1
    s_parts, v_parts = [], []
    for back in range(n_win):
        start = pl.multiple_of(jnp.maximum(i - back, 0) * qt, qt)
        s = _nt_dot(q, kw_ref[0, 0, pl.ds(start, qt), :])
        dist = rel + (back * qt + jnp.where(i >= back, 0, WINDOW))
        s_parts.append(jnp.where((dist >= 0) & (dist < WINDOW), s, MASK_NEG))
        v_parts.append(vwa_ref[0, 0, pl.ds(start, qt), :])
    s_win = jnp.concatenate(s_parts, axis=1)
    m_w = jnp.max(s_win, axis=-1, keepdims=True)
    acc_w = _dot(jnp.exp2(s_win - m_w).astype(_MXU_DTYPE), jnp.concatenate(v_parts, axis=0))

    def token_major(o):
        return jnp.concatenate([o[r * qt:(r + 1) * qt] for r in range(NSA_GROUP)], axis=1)

    out = (gc_ref[0] * token_major(o_cmp) + gs_ref[0] * token_major(finish(acc_s))
           + gw_ref[0] * token_major(finish(acc_w)))
    o_ref[0] = out.astype(o_ref.dtype)


def _sb_kernel(q_ref, k_ref, v_ref, uu_ref, g_ref, o_ref):
    i = pl.program_id(2)
    rows = q_ref.shape[1]
    band = rows // Q_BLOCK
    n_tiles = (i + 1) * band
    t_col = i * rows + lax.broadcasted_iota(jnp.int32, (rows, 1), 0)
    lane = lax.broadcasted_iota(jnp.int32, (rows, LANES), 1)
    low_half = lane < HEAD_DIM
    low_half_k = lax.broadcasted_iota(jnp.int32, (Q_BLOCK, LANES), 1) < HEAD_DIM
    q_pair = q_ref[0]
    zero = jnp.zeros_like(q_pair)
    q_heads = (jnp.where(low_half, q_pair, zero), jnp.where(low_half, zero, q_pair))

    def make_body(masked):
        def tile_step(kk, carry):
            acc, laters = carry[0], carry[1:]
            start = pl.multiple_of((n_tiles - 1 - kk) * Q_BLOCK, Q_BLOCK)
            k_pair = k_ref[0, pl.ds(start, Q_BLOCK), :]
            v_pair = v_ref[0, pl.ds(start, Q_BLOCK), :]
            zero_v = jnp.zeros_like(v_pair)
            v_bd = jnp.concatenate([jnp.where(low_half_k, v_pair, zero_v),
                                    jnp.where(low_half_k, zero_v, v_pair)], axis=0)
            causal = start + lane < t_col
            weights, new_laters = [], []
            for q_h, later in zip(q_heads, laters):
                z = _nt_dot(q_h, k_pair)
                sp = jnp.maximum(z, 0.0) + jnp.log2(1.0 + jnp.exp2(-jnp.abs(z)))
                if masked:
                    sp = jnp.where(causal, sp, 0.0)
                hi, lo = _split2(sp)
                r = _dot(jnp.concatenate([hi, lo], axis=1), uu_ref[...])
                after = r[:, :LANES] + later
                a = jnp.exp2(z - sp - after)
                if masked:
                    a = jnp.where(causal, a, 0.0)
                weights.append(a.astype(_MXU_DTYPE))
                new_laters.append(later + r[:, LANES:])
            acc = acc + _dot(jnp.concatenate(weights, axis=1), v_bd)
            return (acc, *new_laters)

        def body(kg, carry):
            for u in range(SB_UNROLL):
                carry = tile_step(kg * SB_UNROLL + u, carry)
            return carry
        return body

    zeros = jnp.zeros((rows, LANES), F32)
    carry = lax.fori_loop(0, band // SB_UNROLL, make_body(True), (zeros, zeros, zeros))
    carry = lax.fori_loop(band // SB_UNROLL, n_tiles // SB_UNROLL, make_body(False), carry)
    o_ref[0] = (carry[0] * g_ref[0]).astype(o_ref.dtype)


def _outproj_kernel(x_ref, on_ref, os_ref, w_ref, o_ref):
    o_ref[...] = (x_ref[...] + _dot(on_ref[...], w_ref[:NSA_WIDTH, :])
                  + _dot(os_ref[...], w_ref[NSA_WIDTH:, :]))


def _rope_tables(pos, reps):
    inv_freq = jnp.power(ROPE_THETA, -jnp.arange(0, ROPE_DIM, 2, dtype=F32) / ROPE_DIM)
    ang = pos.astype(F32)[:, None] * inv_freq[None, :]
    cos, sin = jnp.cos(ang), jnp.sin(ang)
    n = pos.shape[0]
    rest = HEAD_DIM - ROPE_DIM
    cos_h = jnp.concatenate([cos, cos, jnp.ones((n, rest), F32)], axis=1)
    sin_h = jnp.concatenate([-sin, sin, jnp.zeros((n, rest), F32)], axis=1)
    return jnp.tile(cos_h, (1, reps)), jnp.tile(sin_h, (1, reps))


def _const_spec(shape):
    return pl.BlockSpec(shape, lambda *_: (0,) * len(shape))


def kernel(x, norm_gain, w_in, q_norm_gain, k_norm_cmp, k_norm_slc, k_norm_win,
           cmp_k_pos, cmp_k_w1, cmp_k_b1, cmp_k_w2, cmp_v_pos, cmp_v_w1, cmp_v_b1, cmp_v_w2, w_out):
    B, S, DM = x.shape
    D, G, R = HEAD_DIM, NSA_KV_HEADS, NSA_GROUP
    mxu = _MXU_DTYPE
    n_tok = B * S
    tm = ROW_TILE
    n_sblk = S // tm
    nq = S // Q_BLOCK
    ncp = S // CMP_STRIDE
    n_slc = S // SLC_LEN
    n_cmp = (S - CMP_LEN) // CMP_STRIDE + 1
    assert S % tm == 0 and ncp % LANES == 0 and n_slc <= LANES - D and n_slc >= SLC_TOPN

    n_gl = NSA_HEADS * N_BRANCH
    gl0 = C_GN + n_gl
    w_cat = jnp.concatenate([w_in[:, :C_GN], w_in[:, gl0:], w_in[:, C_GN:gl0],
                             jnp.zeros((DM, LANES - n_gl), w_in.dtype)], axis=1).astype(mxu)
    assert w_cat.shape[1] == N_COLS
    pos = jnp.arange(S, dtype=jnp.int32)
    cos_t, sin_t = _rope_tables(pos, LANES // D)
    cmp_end = jnp.arange(ncp, dtype=jnp.int32) * CMP_STRIDE + (CMP_LEN - 1)
    cos_c, sin_c = _rope_tables(cmp_end, 1)
    lane_i = np.arange(NSA_WIDTH)
    bd = jnp.asarray(lane_i[:, None] // D == lane_i[None, :] // D, mxu)
    eg = np.zeros((LANES, N_BRANCH * NSA_WIDTH), np.float32)
    for hh in range(NSA_HEADS):
        for br in range(N_BRANCH):
            eg[hh * N_BRANCH + br, br * NSA_WIDTH + hh * D:br * NSA_WIDTH + (hh + 1) * D] = 1.0
    eg = jnp.asarray(eg, mxu)
    perm = np.zeros((D, D), np.float32)
    for c in range(ROPE_HALF):
        perm[c + ROPE_HALF, c] = 1.0
        perm[c, c + ROPE_HALF] = 1.0
    perm = jnp.asarray(perm, mxu)
    cs = np.arange(ncp) * CMP_STRIDE
    ss = np.arange(LANES - D) * SLC_LEN
    ovl = np.clip(np.minimum(cs[None, :] + CMP_LEN, ss[:, None] + SLC_LEN)
                  - np.maximum(cs[None, :], ss[:, None]), 0, None).astype(np.float32) / CMP_LEN
    ovl[:, n_cmp:] = 0.0
    ovl[n_slc:, :] = 0.0
    ovl = jnp.asarray(ovl, mxu)
    sidx = np.arange(LANES)
    tri = (sidx[:, None] > sidx[None, :]).astype(np.float32)
    uu_half = np.concatenate([tri, np.ones((LANES, LANES), np.float32)], axis=1)
    uu = jnp.asarray(np.concatenate([uu_half, uu_half], axis=0), mxu)

    row = lambda v: v.reshape(1, -1).astype(F32)
    x2 = x.reshape(n_tok, DM)

    tok_spec = lambda w: pl.BlockSpec((tm, w), lambda t: (t, 0))
    head_spec = lambda nh, w: pl.BlockSpec((1, nh, tm, w), lambda t: (t // n_sblk, 0, t % n_sblk, 0))
    tab_spec = pl.BlockSpec((tm, LANES), lambda t: (t % n_sblk, 0))
    sds = jax.ShapeDtypeStruct
    outs = pl.pallas_call(
        functools.partial(_inproj_kernel, n_sblk=n_sblk),
        grid=(n_tok // tm,),
        in_specs=[tok_spec(DM), _const_spec((1, DM)), _const_spec((DM, N_COLS)),
                  _const_spec((1, NSA_WIDTH)), _const_spec((1, NSA_KV_WIDTH)), _const_spec((1, NSA_KV_WIDTH)),
                  _const_spec((NSA_WIDTH, NSA_WIDTH)), tab_spec, tab_spec,
                  _const_spec((LANES, N_BRANCH * NSA_WIDTH))],
        out_specs=[head_spec(NSA_HEADS, LANES), head_spec(G, D), head_spec(G, D),
                   head_spec(G, LANES), head_spec(G, LANES), head_spec(G, D), head_spec(G, LANES),
                   tok_spec(NSA_WIDTH), tok_spec(NSA_WIDTH), tok_spec(NSA_WIDTH),
                   tok_spec(SB_WIDTH), tok_spec(SB_WIDTH), tok_spec(SB_WIDTH), tok_spec(SB_WIDTH)],
        out_shape=[sds((B, NSA_HEADS, S, LANES), mxu), sds((B, G, S, D), F32), sds((B, G, S, D), F32),
                   sds((B, G, S, LANES), mxu), sds((B, G, S, LANES), mxu), sds((B, G, S, D), mxu),
                   sds((B, G, S, LANES), mxu),
                   sds((n_tok, NSA_WIDTH), F32), sds((n_tok, NSA_WIDTH), F32), sds((n_tok, NSA_WIDTH), F32),
                   sds((n_tok, SB_WIDTH), mxu), sds((n_tok, SB_WIDTH), mxu), sds((n_tok, SB_WIDTH), mxu),
                   sds((n_tok, SB_WIDTH), F32)],
        compiler_params=pltpu.CompilerParams(dimension_semantics=("parallel",), vmem_limit_bytes=VMEM_LIMIT),
        name="inproj",
    )(x2, row(norm_gain), w_cat, row(jnp.tile(q_norm_gain, NSA_HEADS)),
      row(jnp.tile(k_norm_slc, G)), row(jnp.tile(k_norm_win, G)), bd, cos_t, sin_t, eg)
    (q_nsa, kc_raw, vc_raw, ks_aug, vs_aug, k_win, vw_aug, g_cmp, g_slc, g_win,
     q_sb, k_sb, v_sb, g_sb) = outs

    chunk_w = CMP_STRIDE * D
    chunks = lambda a: a.reshape(B * G, ncp, chunk_w)
    bg_spec = lambda r, w: pl.BlockSpec((1, r, w), lambda t: (t, 0, 0))
    k_cmp, v_cmp = pl.pallas_call(
        _compress_kernel,
        grid=(B * G,),
        in_specs=[bg_spec(ncp, chunk_w), bg_spec(ncp, chunk_w),
                  _const_spec((2, chunk_w)), _const_spec((2, chunk_w)),
                  _const_spec((CMP_LEN * D, D)), _const_spec((1, D)), _const_spec((D, D)),
                  _const_spec((CMP_LEN * D, D)), _const_spec((1, D)), _const_spec((D, D)),
                  _const_spec((1, D)), _const_spec((ncp, D)), _const_spec((ncp, D)), _const_spec((D, D))],
        out_specs=[bg_spec(ncp, D), bg_spec(ncp, D)],
        out_shape=[sds((B * G, ncp, D), mxu), sds((B * G, ncp, D), mxu)],
        compiler_params=pltpu.CompilerParams(dimension_semantics=("parallel",), vmem_limit_bytes=VMEM_LIMIT),
        name="compress",
    )(chunks(kc_raw), chunks(vc_raw), cmp_k_pos.reshape(2, chunk_w), cmp_v_pos.reshape(2, chunk_w),
      cmp_k_w1.astype(mxu), row(cmp_k_b1), cmp_k_w2.astype(mxu),
      cmp_v_w1.astype(mxu), row(cmp_v_b1), cmp_v_w2.astype(mxu),
      row(k_norm_cmp), cos_c, sin_c, perm)
    k_cmp = k_cmp.reshape(B, G, ncp, D)
    v_cmp = v_cmp.reshape(B, G, ncp, D)

    kv_spec = lambda r, w: pl.BlockSpec((1, 1, r, w), lambda b, g, i: (b, g, 0, 0))
    gate_spec = pl.BlockSpec((1, NSA_QT, R * D), lambda b, g, i: (b, i, g))
    g3 = lambda a: a.reshape(B, S, NSA_WIDTH)
    o_nsa = pl.pallas_call(
        _nsa_kernel,
        grid=(B, G, S // NSA_QT),
        in_specs=[pl.BlockSpec((1, R, NSA_QT, LANES), lambda b, g, i: (b, g, i, 0)),
                  kv_spec(ncp, D), kv_spec(ncp, D), kv_spec(S, LANES), kv_spec(S, LANES),
                  kv_spec(S, D), kv_spec(S, LANES), _const_spec((LANES - D, ncp)),
                  gate_spec, gate_spec, gate_spec],
        out_specs=gate_spec,
        out_shape=sds((B, S, NSA_WIDTH), mxu),
        compiler_params=pltpu.CompilerParams(dimension_semantics=("parallel", "parallel", "arbitrary"),
                                             vmem_limit_bytes=VMEM_LIMIT),
        name="nsa",
    )(q_nsa, k_cmp, v_cmp, ks_aug, vs_aug, k_win, vw_aug, ovl, g3(g_cmp), g3(g_slc), g3(g_win))

    sb3 = lambda a: a.reshape(B, S, SB_WIDTH)
    pair_q = pl.BlockSpec((1, SB_ROWS, LANES), lambda b, hp, i: (b, i, hp))
    pair_kv = pl.BlockSpec((1, S, LANES), lambda b, hp, i: (b, 0, hp))
    o_sb = pl.pallas_call(
        _sb_kernel,
        grid=(B, SB_WIDTH // LANES, S // SB_ROWS),
        in_specs=[pair_q, pair_kv, pair_kv, _const_spec((2 * LANES, 2 * LANES)), pair_q],
        out_specs=pair_q,
        out_shape=sds((B, S, SB_WIDTH), mxu),
        compiler_params=pltpu.CompilerParams(dimension_semantics=("parallel", "parallel", "arbitrary"),
                                             vmem_limit_bytes=VMEM_LIMIT),
        name="stickbreak",
    )(sb3(q_sb), sb3(k_sb), sb3(v_sb), uu, sb3(g_sb))

    out = pl.pallas_call(
        _outproj_kernel,
        grid=(n_tok // tm,),
        in_specs=[tok_spec(DM), tok_spec(NSA_WIDTH), tok_spec(SB_WIDTH),
                  _const_spec((NSA_WIDTH + SB_WIDTH, DM))],
        out_specs=tok_spec(DM),
        out_shape=sds((n_tok, DM), x.dtype),
        compiler_params=pltpu.CompilerParams(dimension_semantics=("parallel",), vmem_limit_bytes=VMEM_LIMIT),
        name="outproj",
    )(x2, o_nsa.reshape(n_tok, NSA_WIDTH), o_sb.reshape(n_tok, SB_WIDTH), w_out.astype(mxu))
    return out.reshape(B, S, DM)
```

```python
import functools
import math

import numpy as np
import jax
import jax.numpy as jnp
from jax import lax
from jax.experimental import pallas as pl
from jax.experimental.pallas import tpu as pltpu

HEAD_DIM = 64
NSA_HEADS = 8
NSA_KV_HEADS = 2
NSA_GROUP = NSA_HEADS // NSA_KV_HEADS
SB_HEADS = 8
NSA_WIDTH = NSA_HEADS * HEAD_DIM
SB_WIDTH = SB_HEADS * HEAD_DIM
NSA_KV_WIDTH = NSA_KV_HEADS * HEAD_DIM
N_BRANCH = 3
CMP_LEN = 32
CMP_STRIDE = 16
SLC_LEN = 64
SLC_TOPN = 16
WINDOW = 512
Q_BLOCK = 128
ROPE_DIM = HEAD_DIM // 4
ROPE_HALF = ROPE_DIM // 2
ROPE_THETA = 500000.0
EPS = 1e-6
FORCE_BONUS = 1.0e4
SCALE = 1.0 / math.sqrt(HEAD_DIM)
LOG2E = math.log2(math.e)

LANES = 128
MASK_NEG = -1.0e30
M_INIT = -3.0e38
ROW_TILE = 512
NSA_QT = 256
SB_ROWS = 512
SB_UNROLL = 4
SB_UNDERFLOW_BITS = 160.0
VMEM_LIMIT = 56 * 1024 * 1024

_MXU_DTYPE = jnp.bfloat16
F32 = jnp.float32

C_Q = 0
C_KC = 512
C_VC = 640
C_KS = 768
C_VS = 896
C_KW = 1024
C_VW = 1152
C_GN = 1280
C_QSB = 1792
C_KSB = 2304
C_VSB = 2816
C_GSB = 3328
C_GL = 3840
N_COLS = 3968


def _nt_dot(a, b):
    return lax.dot_general(a, b, (((1,), (1,)), ((), ())), preferred_element_type=F32)


def _dot(a, b):
    return jnp.dot(a, b, preferred_element_type=F32)


def _split2(v):
    hi = v.astype(_MXU_DTYPE)
    lo = (v - hi.astype(F32)).astype(_MXU_DTYPE)
    return hi, lo


def _split3(v):
    hi = v.astype(_MXU_DTYPE)
    r1 = v - hi.astype(F32)
    mid = r1.astype(_MXU_DTYPE)
    lo = (r1 - mid.astype(F32)).astype(_MXU_DTYPE)
    return hi, mid, lo


def _inproj_kernel(x_ref, ng_ref, w_ref, qg_ref, kgs_ref, kgw_ref, bd_ref, cos_ref, sin_ref, eg_ref,
                   q_ref, kc_ref, vc_ref, ksa_ref, vsa_ref, kw_ref, vwa_ref,
                   gc_ref, gs_ref, gw_ref, qsb_ref, ksb_ref, vsb_ref, gsb_ref, *, n_sblk):
    tm = x_ref.shape[0]
    x = x_ref[...]
    ms = jnp.mean(x * x, axis=-1, keepdims=True)
    h = (x * lax.rsqrt(ms + EPS) * ng_ref[...]).astype(_MXU_DTYPE)

    def proj(lo, width):
        return _dot(h, w_ref[:, lo:lo + width])

    lane = lax.broadcasted_iota(jnp.int32, (tm, LANES), 1)
    low_half = lane < HEAD_DIM

    def head_norm_rope(y, gain):
        width = y.shape[1]
        rep = width // LANES
        hi, lo = _split2(y * y)
        bd = bd_ref[:width, :width]
        ssum = _dot(hi, bd) + _dot(lo, bd)
        yn = y * lax.rsqrt(ssum * (1.0 / HEAD_DIM) + EPS) * gain
        cos = jnp.concatenate([cos_ref[...]] * rep, axis=1) if rep > 1 else cos_ref[...]
        sin = jnp.concatenate([sin_ref[...]] * rep, axis=1) if rep > 1 else sin_ref[...]
        fwd = pltpu.roll(yn, ROPE_HALF, axis=1)
        bwd = pltpu.roll(yn, width - ROPE_HALF, axis=1)
        lane_w = lax.broadcasted_iota(jnp.int32, (tm, width), 1)
        partner = jnp.where((lane_w & (HEAD_DIM - 1)) < ROPE_HALF, bwd, fwd)
        return yn * cos + partner * sin

    def head_pair(slab, p):
        chunk = slab[:, p * LANES:(p + 1) * LANES]
        return chunk, pltpu.roll(chunk, HEAD_DIM, axis=1)

    qn = head_norm_rope(proj(C_Q, NSA_WIDTH), qg_ref[...]) * (SCALE * LOG2E)
    for p in range(NSA_HEADS // 2):
        ev, od = head_pair(qn, p)
        q_ref[0, 2 * p] = jnp.where(low_half, ev, 0.0).astype(q_ref.dtype)
        q_ref[0, 2 * p + 1] = jnp.where(low_half, od, 0.0).astype(q_ref.dtype)

    for ref, col in ((kc_ref, C_KC), (vc_ref, C_VC)):
        ev, od = head_pair(proj(col, NSA_KV_WIDTH), 0)
        ref[0, 0] = ev[:, :HEAD_DIM]
        ref[0, 1] = od[:, :HEAD_DIM]

    sblk = lax.rem(pl.program_id(0), n_sblk)
    row = lax.broadcasted_iota(jnp.int32, (tm, LANES), 0)
    key_blk = (sblk * tm + row) >> int(math.log2(SLC_LEN))
    onehot = jnp.where(lane - HEAD_DIM == key_blk, 1.0, 0.0)
    ks = head_norm_rope(proj(C_KS, NSA_KV_WIDTH), kgs_ref[...])
    ev, od = head_pair(ks, 0)
    ksa_ref[0, 0] = jnp.where(low_half, ev, onehot).astype(ksa_ref.dtype)
    ksa_ref[0, 1] = jnp.where(low_half, od, onehot).astype(ksa_ref.dtype)
    ev, od = head_pair(proj(C_VS, NSA_KV_WIDTH), 0)
    vsa_ref[0, 0] = jnp.where(low_half, ev, 1.0).astype(vsa_ref.dtype)
    vsa_ref[0, 1] = jnp.where(low_half, od, 1.0).astype(vsa_ref.dtype)

    kw = head_norm_rope(proj(C_KW, NSA_KV_WIDTH), kgw_ref[...])
    ev, od = head_pair(kw, 0)
    kw_ref[0, 0] = ev[:, :HEAD_DIM].astype(kw_ref.dtype)
    kw_ref[0, 1] = od[:, :HEAD_DIM].astype(kw_ref.dtype)
    ev, od = head_pair(proj(C_VW, NSA_KV_WIDTH), 0)
    vwa_ref[0, 0] = jnp.where(low_half, ev, 1.0).astype(vwa_ref.dtype)
    vwa_ref[0, 1] = jnp.where(low_half, od, 1.0).astype(vwa_ref.dtype)

    gn = proj(C_GN, NSA_WIDTH)
    silu_n = gn * jax.nn.sigmoid(gn)
    gl_hi, gl_lo = _split2(proj(C_GL, LANES))
    for br, ref in enumerate((gc_ref, gs_ref, gw_ref)):
        e = eg_ref[:, br * NSA_WIDTH:(br + 1) * NSA_WIDTH]
        ref[...] = jax.nn.sigmoid(_dot(gl_hi, e) + _dot(gl_lo, e)) * silu_n

    qsb_ref[...] = (proj(C_QSB, SB_WIDTH) * (SCALE * LOG2E)).astype(qsb_ref.dtype)
    ksb_ref[...] = proj(C_KSB, SB_WIDTH).astype(ksb_ref.dtype)
    vsb_ref[...] = proj(C_VSB, SB_WIDTH).astype(vsb_ref.dtype)
    gsb = proj(C_GSB, SB_WIDTH)
    gsb_ref[...] = gsb * jax.nn.sigmoid(gsb)


def _compress_kernel(kc_ref, vc_ref, posk_ref, posv_ref, w1k_ref, b1k_ref, w2k_ref,
                     w1v_ref, b1v_ref, w2v_ref, kg_ref, cos_ref, sin_ref, perm_ref,
                     kcmp_ref, vcmp_ref):
    half = CMP_STRIDE * HEAD_DIM

    def phi(c_ref, pos_ref, w1_ref, b1_ref, w2_ref):
        c = c_ref[0]
        n = c.shape[0]
        top = _dot((c + pos_ref[0:1, :]).astype(_MXU_DTYPE), w1_ref[:half, :])
        bot = _dot((c + pos_ref[1:2, :]).astype(_MXU_DTYPE), w1_ref[half:, :])
        hid = top + pltpu.roll(bot, n - 1, axis=0) + b1_ref[...]
        return _dot((hid * jax.nn.sigmoid(hid)).astype(_MXU_DTYPE), w2_ref[...])

    k = phi(kc_ref, posk_ref, w1k_ref, b1k_ref, w2k_ref)
    ms = jnp.mean(k * k, axis=-1, keepdims=True)
    kn = k * lax.rsqrt(ms + EPS) * kg_ref[...]
    hi, lo = _split2(kn)
    partner = _dot(hi, perm_ref[...]) + _dot(lo, perm_ref[...])
    kcmp_ref[0] = (kn * cos_ref[...] + partner * sin_ref[...]).astype(kcmp_ref.dtype)
    vcmp_ref[0] = phi(vc_ref, posv_ref, w1v_ref, b1v_ref, w2v_ref).astype(vcmp_ref.dtype)


def _nsa_kernel(q_ref, kc_ref, vc_ref, ksa_ref, vsa_ref, kw_ref, vwa_ref, ovl_ref,
                gc_ref, gs_ref, gw_ref, o_ref):
    i = pl.program_id(2)
    qt = q_ref.shape[2]
    rq = NSA_GROUP * qt
    n_blk_lanes = LANES - HEAD_DIM
    q_pad = q_ref[0].reshape(rq, LANES)
    q = q_pad[:, :HEAD_DIM]
    t_loc = lax.broadcasted_iota(jnp.int32, (rq, 1), 0) & (qt - 1)
    t_col = i * qt + t_loc

    kc = kc_ref[0, 0]
    ncp = kc.shape[0]
    lg = _nt_dot(q, kc)
    cmp_end = lax.broadcasted_iota(jnp.int32, (rq, ncp), 1) * CMP_STRIDE + (CMP_LEN - 1)
    valid = cmp_end <= t_col
    lg = jnp.where(valid, lg, MASK_NEG)
    m = jnp.max(lg, axis=-1, keepdims=True)
    p = jnp.where(valid, jnp.exp2(lg - m), 0.0)
    p = p / jnp.maximum(jnp.sum(p, axis=-1, keepdims=True), 1e-30)
    o_cmp = _dot(p.astype(_MXU_DTYPE), vc_ref[0, 0])

    p_sum = p[0:qt]
    for r in range(1, NSA_GROUP):
        p_sum = p_sum + p[r * qt:(r + 1) * qt]
    ovl = ovl_ref[...]
    p_slc = sum(_nt_dot(ovl, part) for part in _split3(p_sum))
    j_idx = lax.broadcasted_iota(jnp.int32, (n_blk_lanes, qt), 0)
    blk_t = (i * qt + lax.broadcasted_iota(jnp.int32, (n_blk_lanes, qt), 1)) >> int(math.log2(SLC_LEN))
    slc_valid = j_idx <= blk_t
    forced = (j_idx == 0) | (j_idx == blk_t) | (j_idx == blk_t - 1)
    score = jnp.where(slc_valid, p_slc + jnp.where(forced, FORCE_BONUS, 0.0), -jnp.inf)
    rank = jnp.zeros((n_blk_lanes, qt), F32)
    for ii in range(n_blk_lanes):
        s_i = score[ii:ii + 1, :]
        rank = rank + jnp.where(j_idx > ii, jnp.where(s_i >= score, 1.0, 0.0), jnp.where(s_i > score, 1.0, 0.0))
    sel_t = jnp.where(slc_valid, jnp.where(rank < SLC_TOPN, 1.0, 0.0), 0.0)
    sel = jnp.concatenate([jnp.ones((HEAD_DIM, qt), F32), sel_t], axis=0).T
    lane_q = lax.broadcasted_iota(jnp.int32, (qt, LANES), 1)
    bias = jnp.where(lane_q >= HEAD_DIM, jnp.where(sel > 0.5, 0.0, MASK_NEG), 0.0).astype(q_pad.dtype)
    q_aug = q_pad + jnp.concatenate([bias] * NSA_GROUP, axis=0)

    rel = t_loc - lax.broadcasted_iota(jnp.int32, (rq, qt), 1)

    def flash_step(s, v, carry):
        m_old, acc = carry
        m_new = jnp.maximum(m_old, jnp.max(s, axis=-1, keepdims=True))
        alpha = jnp.exp2(m_old - m_new)
        pexp = jnp.exp2(s - m_new)
        return m_new, alpha * acc + _dot(pexp.astype(_MXU_DTYPE), v)

    def finish(acc):
        return acc[:, :HEAD_DIM] / acc[:, HEAD_DIM:HEAD_DIM + 1]

    def slc_logits(kt):
        start = pl.multiple_of(kt * qt, qt)
        return _nt_dot(q_aug, ksa_ref[0, 0, pl.ds(start, qt), :])

    def slc_values(kt):
        return vsa_ref[0, 0, pl.ds(pl.multiple_of(kt * qt, qt), qt), :]

    def slc_body(kt, carry):
        s_cur, m_run, acc = carry
        s_next = slc_logits(kt + 1)
        m_run, acc = flash_step(s_cur, slc_values(kt), (m_run, acc))
        return s_next, m_run, acc

    init = (jnp.full((rq, 1), M_INIT, F32), jnp.zeros((rq, LANES), F32))
    s_diag, m_s, acc_s = lax.fori_loop(0, i, slc_body, (slc_logits(0),) + init)
    _, acc_s = flash_step(jnp.where(rel >= 0, s_diag, MASK_NEG), slc_values(i), (m_s, acc_s))

    n_win = WINDOW // qt + 1
    s_parts, v_parts = [], []
    for back in range(n_win):
        start = pl.multiple_of(jnp.maximum(i - back, 0) * qt, qt)
        s = _nt_dot(q, kw_ref[0, 0, pl.ds(start, qt), :])
        dist = rel + (back * qt + jnp.where(i >= back, 0, WINDOW))
        s_parts.append(jnp.where((dist >= 0) & (dist < WINDOW), s, MASK_NEG))
        v_parts.append(vwa_ref[0, 0, pl.ds(start, qt), :])
    s_win = jnp.concatenate(s_parts, axis=1)
    m_w = jnp.max(s_win, axis=-1, keepdims=True)
    acc_w = _dot(jnp.exp2(s_win - m_w).astype(_MXU_DTYPE), jnp.concatenate(v_parts, axis=0))

    def token_major(o):
        return jnp.concatenate([o[r * qt:(r + 1) * qt] for r in range(NSA_GROUP)], axis=1)

    out = (gc_ref[0] * token_major(o_cmp) + gs_ref[0] * token_major(finish(acc_s))
           + gw_ref[0] * token_major(finish(acc_w)))
    o_ref[0] = out.astype(o_ref.dtype)


def _sb_kernel(q_ref, k_ref, v_ref, uu_ref, g_ref, o_ref):
    i = pl.program_id(2)
    rows = q_ref.shape[1]
    band = rows // Q_BLOCK
    n_tiles = (i + 1) * band
    t_col = i * rows + lax.broadcasted_iota(jnp.int32, (rows, 1), 0)
    lane = lax.broadcasted_iota(jnp.int32, (rows, LANES), 1)
    low_half = lane < HEAD_DIM
    low_half_k = lax.broadcasted_iota(jnp.int32, (Q_BLOCK, LANES), 1) < HEAD_DIM
    q_pair = q_ref[0]
    zero = jnp.zeros_like(q_pair)
    q_heads = (jnp.where(low_half, q_pair, zero), jnp.where(low_half, zero, q_pair))

    def make_body(masked):
        def tile_step(kk, carry):
            acc, laters = carry[0], carry[1:]
            start = pl.multiple_of((n_tiles - 1 - kk) * Q_BLOCK, Q_BLOCK)
            k_pair = k_ref[0, pl.ds(start, Q_BLOCK), :]
            v_pair = v_ref[0, pl.ds(start, Q_BLOCK), :]
            zero_v = jnp.zeros_like(v_pair)
            v_bd = jnp.concatenate([jnp.where(low_half_k, v_pair, zero_v),
                                    jnp.where(low_half_k, zero_v, v_pair)], axis=0)
            causal = start + lane < t_col
            weights, new_laters = [], []
            for q_h, later in zip(q_heads, laters):
                z = _nt_dot(q_h, k_pair)
                sp = jnp.maximum(z, 0.0) + jnp.log2(1.0 + jnp.exp2(-jnp.abs(z)))
                if masked:
                    sp = jnp.where(causal, sp, 0.0)
                hi, lo = _split2(sp)
                r = _dot(jnp.concatenate([hi, lo], axis=1), uu_ref[...])
                after = r[:, :LANES] + later
                a = jnp.exp2(z - sp - after)
                if masked:
                    a = jnp.where(causal, a, 0.0)
                weights.append(a.astype(_MXU_DTYPE))
                new_laters.append(later + r[:, LANES:])
            acc = acc + _dot(jnp.concatenate(weights, axis=1), v_bd)
            return (acc, *new_laters)

        def body(kg, carry):
            for u in range(SB_UNROLL):
                carry = tile_step(kg * SB_UNROLL + u, carry)
            return carry
        return body

    zeros = jnp.zeros((rows, LANES), F32)
    carry = lax.fori_loop(0, band // SB_UNROLL, make_body(True), (zeros, zeros, zeros))

    def settled(laters):
        return jnp.min(jnp.minimum(*laters)) >= SB_UNDERFLOW_BITS

    main_body = make_body(False)

    def keep_going(state):
        kg, done, _ = state
        return jnp.logical_and(kg < n_tiles // SB_UNROLL, jnp.logical_not(done))

    def main_step(state):
        kg, _, carry = state
        carry = main_body(kg, carry)
        return kg + 1, settled(carry[1:]), carry

    _, _, carry = lax.while_loop(keep_going, main_step, (band // SB_UNROLL, settled(carry[1:]), carry))
    o_ref[0] = (carry[0] * g_ref[0]).astype(o_ref.dtype)


def _outproj_kernel(x_ref, on_ref, os_ref, w_ref, o_ref):
    o_ref[...] = (x_ref[...] + _dot(on_ref[...], w_ref[:NSA_WIDTH, :])
                  + _dot(os_ref[...], w_ref[NSA_WIDTH:, :]))


def _rope_tables(pos, reps):
    inv_freq = jnp.power(ROPE_THETA, -jnp.arange(0, ROPE_DIM, 2, dtype=F32) / ROPE_DIM)
    ang = pos.astype(F32)[:, None] * inv_freq[None, :]
    cos, sin = jnp.cos(ang), jnp.sin(ang)
    n = pos.shape[0]
    rest = HEAD_DIM - ROPE_DIM
    cos_h = jnp.concatenate([cos, cos, jnp.ones((n, rest), F32)], axis=1)
    sin_h = jnp.concatenate([-sin, sin, jnp.zeros((n, rest), F32)], axis=1)
    return jnp.tile(cos_h, (1, reps)), jnp.tile(sin_h, (1, reps))


def _const_spec(shape):
    return pl.BlockSpec(shape, lambda *_: (0,) * len(shape))


def kernel(x, norm_gain, w_in, q_norm_gain, k_norm_cmp, k_norm_slc, k_norm_win,
           cmp_k_pos, cmp_k_w1, cmp_k_b1, cmp_k_w2, cmp_v_pos, cmp_v_w1, cmp_v_b1, cmp_v_w2, w_out):
    B, S, DM = x.shape
    D, G, R = HEAD_DIM, NSA_KV_HEADS, NSA_GROUP
    mxu = _MXU_DTYPE
    n_tok = B * S
    tm = ROW_TILE
    n_sblk = S // tm
    nq = S // Q_BLOCK
    ncp = S // CMP_STRIDE
    n_slc = S // SLC_LEN
    n_cmp = (S - CMP_LEN) // CMP_STRIDE + 1
    assert S % tm == 0 and ncp % LANES == 0 and n_slc <= LANES - D and n_slc >= SLC_TOPN

    n_gl = NSA_HEADS * N_BRANCH
    gl0 = C_GN + n_gl
    w_cat = jnp.concatenate([w_in[:, :C_GN], w_in[:, gl0:], w_in[:, C_GN:gl0],
                             jnp.zeros((DM, LANES - n_gl), w_in.dtype)], axis=1).astype(mxu)
    assert w_cat.shape[1] == N_COLS
    pos = jnp.arange(S, dtype=jnp.int32)
    cos_t, sin_t = _rope_tables(pos, LANES // D)
    cmp_end = jnp.arange(ncp, dtype=jnp.int32) * CMP_STRIDE + (CMP_LEN - 1)
    cos_c, sin_c = _rope_tables(cmp_end, 1)
    lane_i = np.arange(NSA_WIDTH)
    bd = jnp.asarray(lane_i[:, None] // D == lane_i[None, :] // D, mxu)
    eg = np.zeros((LANES, N_BRANCH * NSA_WIDTH), np.float32)
    for hh in range(NSA_HEADS):
        for br in range(N_BRANCH):
            eg[hh * N_BRANCH + br, br * NSA_WIDTH + hh * D:br * NSA_WIDTH + (hh + 1) * D] = 1.0
    eg = jnp.asarray(eg, mxu)
    perm = np.zeros((D, D), np.float32)
    for c in range(ROPE_HALF):
        perm[c + ROPE_HALF, c] = 1.0
        perm[c, c + ROPE_HALF] = 1.0
    perm = jnp.asarray(perm, mxu)
    cs = np.arange(ncp) * CMP_STRIDE
    ss = np.arange(LANES - D) * SLC_LEN
    ovl = np.clip(np.minimum(cs[None, :] + CMP_LEN, ss[:, None] + SLC_LEN)
                  - np.maximum(cs[None, :], ss[:, None]), 0, None).astype(np.float32) / CMP_LEN
    ovl[:, n_cmp:] = 0.0
    ovl[n_slc:, :] = 0.0
    ovl = jnp.asarray(ovl, mxu)
    sidx = np.arange(LANES)
    tri = (sidx[:, None] > sidx[None, :]).astype(np.float32)
    uu_half = np.concatenate([tri, np.ones((LANES, LANES), np.float32)], axis=1)
    uu = jnp.asarray(np.concatenate([uu_half, uu_half], axis=0), mxu)

    row = lambda v: v.reshape(1, -1).astype(F32)
    x2 = x.reshape(n_tok, DM)

    tok_spec = lambda w: pl.BlockSpec((tm, w), lambda t: (t, 0))
    head_spec = lambda nh, w: pl.BlockSpec((1, nh, tm, w), lambda t: (t // n_sblk, 0, t % n_sblk, 0))
    tab_spec = pl.BlockSpec((tm, LANES), lambda t: (t % n_sblk, 0))
    sds = jax.ShapeDtypeStruct
    outs = pl.pallas_call(
        functools.partial(_inproj_kernel, n_sblk=n_sblk),
        grid=(n_tok // tm,),
        in_specs=[tok_spec(DM), _const_spec((1, DM)), _const_spec((DM, N_COLS)),
                  _const_spec((1, NSA_WIDTH)), _const_spec((1, NSA_KV_WIDTH)), _const_spec((1, NSA_KV_WIDTH)),
                  _const_spec((NSA_WIDTH, NSA_WIDTH)), tab_spec, tab_spec,
                  _const_spec((LANES, N_BRANCH * NSA_WIDTH))],
        out_specs=[head_spec(NSA_HEADS, LANES), head_spec(G, D), head_spec(G, D),
                   head_spec(G, LANES), head_spec(G, LANES), head_spec(G, D), head_spec(G, LANES),
                   tok_spec(NSA_WIDTH), tok_spec(NSA_WIDTH), tok_spec(NSA_WIDTH),
                   tok_spec(SB_WIDTH), tok_spec(SB_WIDTH), tok_spec(SB_WIDTH), tok_spec(SB_WIDTH)],
        out_shape=[sds((B, NSA_HEADS, S, LANES), mxu), sds((B, G, S, D), F32), sds((B, G, S, D), F32),
                   sds((B, G, S, LANES), mxu), sds((B, G, S, LANES), mxu), sds((B, G, S, D), mxu),
                   sds((B, G, S, LANES), mxu),
                   sds((n_tok, NSA_WIDTH), F32), sds((n_tok, NSA_WIDTH), F32), sds((n_tok, NSA_WIDTH), F32),
                   sds((n_tok, SB_WIDTH), mxu), sds((n_tok, SB_WIDTH), mxu), sds((n_tok, SB_WIDTH), mxu),
                   sds((n_tok, SB_WIDTH), F32)],
        compiler_params=pltpu.CompilerParams(dimension_semantics=("parallel",), vmem_limit_bytes=VMEM_LIMIT),
        name="inproj",
    )(x2, row(norm_gain), w_cat, row(jnp.tile(q_norm_gain, NSA_HEADS)),
      row(jnp.tile(k_norm_slc, G)), row(jnp.tile(k_norm_win, G)), bd, cos_t, sin_t, eg)
    (q_nsa, kc_raw, vc_raw, ks_aug, vs_aug, k_win, vw_aug, g_cmp, g_slc, g_win,
     q_sb, k_sb, v_sb, g_sb) = outs

    chunk_w = CMP_STRIDE * D
    chunks = lambda a: a.reshape(B * G, ncp, chunk_w)
    bg_spec = lambda r, w: pl.BlockSpec((1, r, w), lambda t: (t, 0, 0))
    k_cmp, v_cmp = pl.pallas_call(
        _compress_kernel,
        grid=(B * G,),
        in_specs=[bg_spec(ncp, chunk_w), bg_spec(ncp, chunk_w),
                  _const_spec((2, chunk_w)), _const_spec((2, chunk_w)),
                  _const_spec((CMP_LEN * D, D)), _const_spec((1, D)), _const_spec((D, D)),
                  _const_spec((CMP_LEN * D, D)), _const_spec((1, D)), _const_spec((D, D)),
                  _const_spec((1, D)), _const_spec((ncp, D)), _const_spec((ncp, D)), _const_spec((D, D))],
        out_specs=[bg_spec(ncp, D), bg_spec(ncp, D)],
        out_shape=[sds((B * G, ncp, D), mxu), sds((B * G, ncp, D), mxu)],
        compiler_params=pltpu.CompilerParams(dimension_semantics=("parallel",), vmem_limit_bytes=VMEM_LIMIT),
        name="compress",
    )(chunks(kc_raw), chunks(vc_raw), cmp_k_pos.reshape(2, chunk_w), cmp_v_pos.reshape(2, chunk_w),
      cmp_k_w1.astype(mxu), row(cmp_k_b1), cmp_k_w2.astype(mxu),
      cmp_v_w1.astype(mxu), row(cmp_v_b1), cmp_v_w2.astype(mxu),
      row(k_norm_cmp), cos_c, sin_c, perm)
    k_cmp = k_cmp.reshape(B, G, ncp, D)
    v_cmp = v_cmp.reshape(B, G, ncp, D)

    kv_spec = lambda r, w: pl.BlockSpec((1, 1, r, w), lambda b, g, i: (b, g, 0, 0))
    gate_spec = pl.BlockSpec((1, NSA_QT, R * D), lambda b, g, i: (b, i, g))
    g3 = lambda a: a.reshape(B, S, NSA_WIDTH)
    o_nsa = pl.pallas_call(
        _nsa_kernel,
        grid=(B, G, S // NSA_QT),
        in_specs=[pl.BlockSpec((1, R, NSA_QT, LANES), lambda b, g, i: (b, g, i, 0)),
                  kv_spec(ncp, D), kv_spec(ncp, D), kv_spec(S, LANES), kv_spec(S, LANES),
                  kv_spec(S, D), kv_spec(S, LANES), _const_spec((LANES - D, ncp)),
                  gate_spec, gate_spec, gate_spec],
        out_specs=gate_spec,
        out_shape=sds((B, S, NSA_WIDTH), mxu),
        compiler_params=pltpu.CompilerParams(dimension_semantics=("parallel", "parallel", "arbitrary"),
                                             vmem_limit_bytes=VMEM_LIMIT),
        name="nsa",
    )(q_nsa, k_cmp, v_cmp, ks_aug, vs_aug, k_win, vw_aug, ovl, g3(g_cmp), g3(g_slc), g3(g_win))

    sb3 = lambda a: a.reshape(B, S, SB_WIDTH)
    pair_q = pl.BlockSpec((1, SB_ROWS, LANES), lambda b, hp, i: (b, i, hp))
    pair_kv = pl.BlockSpec((1, S, LANES), lambda b, hp, i: (b, 0, hp))
    o_sb = pl.pallas_call(
        _sb_kernel,
        grid=(B, SB_WIDTH // LANES, S // SB_ROWS),
        in_specs=[pair_q, pair_kv, pair_kv, _const_spec((2 * LANES, 2 * LANES)), pair_q],
        out_specs=pair_q,
        out_shape=sds((B, S, SB_WIDTH), mxu),
        compiler_params=pltpu.CompilerParams(dimension_semantics=("parallel", "parallel", "arbitrary"),
                                             vmem_limit_bytes=VMEM_LIMIT),
        name="stickbreak",
    )(sb3(q_sb), sb3(k_sb), sb3(v_sb), uu, sb3(g_sb))

    out = pl.pallas_call(
        _outproj_kernel,
        grid=(n_tok // tm,),
        in_specs=[tok_spec(DM), tok_spec(NSA_WIDTH), tok_spec(SB_WIDTH),
                  _const_spec((NSA_WIDTH + SB_WIDTH, DM))],
        out_specs=tok_spec(DM),
        out_shape=sds((n_tok, DM), x.dtype),
        compiler_params=pltpu.CompilerParams(dimension_semantics=("parallel",), vmem_limit_bytes=VMEM_LIMIT),
        name="outproj",
    )(x2, o_nsa.reshape(n_tok, NSA_WIDTH), o_sb.reshape(n_tok, SB_WIDTH), w_out.astype(mxu))
    return out.reshape(B, S, DM)
```

```python
import functools
import math

import numpy as np
import jax
import jax.numpy as jnp
from jax import lax
from jax.experimental import pallas as pl
from jax.experimental.pallas import tpu as pltpu

HEAD_DIM = 64
NSA_HEADS = 8
NSA_KV_HEADS = 2
NSA_GROUP = NSA_HEADS // NSA_KV_HEADS
SB_HEADS = 8
NSA_WIDTH = NSA_HEADS * HEAD_DIM
SB_WIDTH = SB_HEADS * HEAD_DIM
NSA_KV_WIDTH = NSA_KV_HEADS * HEAD_DIM
N_BRANCH = 3
CMP_LEN = 32
CMP_STRIDE = 16
SLC_LEN = 64
SLC_TOPN = 16
WINDOW = 512
Q_BLOCK = 128
ROPE_DIM = HEAD_DIM // 4
ROPE_HALF = ROPE_DIM // 2
ROPE_THETA = 500000.0
EPS = 1e-6
FORCE_BONUS = 1.0e4
SCALE = 1.0 / math.sqrt(HEAD_DIM)
LOG2E = math.log2(math.e)

LANES = 128
SUBLANES = 8
MASK_NEG = -1.0e30
M_INIT = -3.0e38
ROW_TILE = 512
NSA_QT = 256
SB_ROWS = 512
SB_UNROLL = 4
MAX_STATIC_SHIFT = 60.0
SB_UNDERFLOW_BITS = 160.0
VMEM_LIMIT = 56 * 1024 * 1024

_MXU_DTYPE = jnp.bfloat16
F32 = jnp.float32

C_Q = 0
C_KC = 512
C_VC = 640
C_KS = 768
C_VS = 896
C_KW = 1024
C_VW = 1152
C_GN = 1280
C_QSB = 1792
C_KSB = 2304
C_VSB = 2816
C_GSB = 3328
C_GL = 3840
N_COLS = 3968


def _nt_dot(a, b):
    return lax.dot_general(a, b, (((1,), (1,)), ((), ())), preferred_element_type=F32)


def _dot(a, b):
    return jnp.dot(a, b, preferred_element_type=F32)


def _split2(v):
    hi = v.astype(_MXU_DTYPE)
    lo = (v - hi.astype(F32)).astype(_MXU_DTYPE)
    return hi, lo


def _split3(v):
    hi = v.astype(_MXU_DTYPE)
    r1 = v - hi.astype(F32)
    mid = r1.astype(_MXU_DTYPE)
    lo = (r1 - mid.astype(F32)).astype(_MXU_DTYPE)
    return hi, mid, lo


def _inproj_kernel(x_ref, ng_ref, w_ref, qg_ref, kgs_ref, kgw_ref, bd_ref, cos_ref, sin_ref, eg_ref,
                   q_ref, kc_ref, vc_ref, ksa_ref, vsa_ref, kw_ref, vwa_ref,
                   gc_ref, gs_ref, gw_ref, qsb_ref, ksb_ref, vsb_ref, gsb_ref, *, n_sblk):
    tm = x_ref.shape[0]
    x = x_ref[...]
    ms = jnp.mean(x * x, axis=-1, keepdims=True)
    h = (x * lax.rsqrt(ms + EPS) * ng_ref[...]).astype(_MXU_DTYPE)

    def proj(lo, width):
        return _dot(h, w_ref[:, lo:lo + width])

    lane = lax.broadcasted_iota(jnp.int32, (tm, LANES), 1)
    low_half = lane < HEAD_DIM

    def head_norm_rope(y, gain):
        width = y.shape[1]
        rep = width // LANES
        hi, lo = _split2(y * y)
        bd = bd_ref[:width, :width]
        ssum = _dot(hi, bd) + _dot(lo, bd)
        yn = y * lax.rsqrt(ssum * (1.0 / HEAD_DIM) + EPS) * gain
        cos = jnp.concatenate([cos_ref[...]] * rep, axis=1) if rep > 1 else cos_ref[...]
        sin = jnp.concatenate([sin_ref[...]] * rep, axis=1) if rep > 1 else sin_ref[...]
        fwd = pltpu.roll(yn, ROPE_HALF, axis=1)
        bwd = pltpu.roll(yn, width - ROPE_HALF, axis=1)
        lane_w = lax.broadcasted_iota(jnp.int32, (tm, width), 1)
        partner = jnp.where((lane_w & (HEAD_DIM - 1)) < ROPE_HALF, bwd, fwd)
        return yn * cos + partner * sin

    def head_pair(slab, p):
        chunk = slab[:, p * LANES:(p + 1) * LANES]
        return chunk, pltpu.roll(chunk, HEAD_DIM, axis=1)

    qn = head_norm_rope(proj(C_Q, NSA_WIDTH), qg_ref[...]) * (SCALE * LOG2E)
    for p in range(NSA_HEADS // 2):
        ev, od = head_pair(qn, p)
        q_ref[0, 2 * p] = jnp.where(low_half, ev, 0.0).astype(q_ref.dtype)
        q_ref[0, 2 * p + 1] = jnp.where(low_half, od, 0.0).astype(q_ref.dtype)

    for ref, col in ((kc_ref, C_KC), (vc_ref, C_VC)):
        ev, od = head_pair(proj(col, NSA_KV_WIDTH), 0)
        ref[0, 0] = ev[:, :HEAD_DIM]
        ref[0, 1] = od[:, :HEAD_DIM]

    sblk = lax.rem(pl.program_id(0), n_sblk)
    row = lax.broadcasted_iota(jnp.int32, (tm, LANES), 0)
    key_blk = (sblk * tm + row) >> int(math.log2(SLC_LEN))
    onehot = jnp.where(lane - HEAD_DIM == key_blk, 1.0, 0.0)
    ks = head_norm_rope(proj(C_KS, NSA_KV_WIDTH), kgs_ref[...])
    ev, od = head_pair(ks, 0)
    ksa_ref[0, 0] = jnp.where(low_half, ev, onehot).astype(ksa_ref.dtype)
    ksa_ref[0, 1] = jnp.where(low_half, od, onehot).astype(ksa_ref.dtype)
    ev, od = head_pair(proj(C_VS, NSA_KV_WIDTH), 0)
    vsa_ref[0, 0] = jnp.where(low_half, ev, 1.0).astype(vsa_ref.dtype)
    vsa_ref[0, 1] = jnp.where(low_half, od, 1.0).astype(vsa_ref.dtype)

    kw = head_norm_rope(proj(C_KW, NSA_KV_WIDTH), kgw_ref[...])
    ev, od = head_pair(kw, 0)
    kw_ref[0, 0] = ev[:, :HEAD_DIM].astype(kw_ref.dtype)
    kw_ref[0, 1] = od[:, :HEAD_DIM].astype(kw_ref.dtype)
    ev, od = head_pair(proj(C_VW, NSA_KV_WIDTH), 0)
    vwa_ref[0, 0] = jnp.where(low_half, ev, 1.0).astype(vwa_ref.dtype)
    vwa_ref[0, 1] = jnp.where(low_half, od, 1.0).astype(vwa_ref.dtype)

    gn = proj(C_GN, NSA_WIDTH)
    silu_n = gn * jax.nn.sigmoid(gn)
    gl_hi, gl_lo = _split2(proj(C_GL, LANES))
    for br, ref in enumerate((gc_ref, gs_ref, gw_ref)):
        e = eg_ref[:, br * NSA_WIDTH:(br + 1) * NSA_WIDTH]
        ref[...] = jax.nn.sigmoid(_dot(gl_hi, e) + _dot(gl_lo, e)) * silu_n

    qsb_ref[...] = (proj(C_QSB, SB_WIDTH) * (SCALE * LOG2E)).astype(qsb_ref.dtype)
    ksb_ref[...] = proj(C_KSB, SB_WIDTH).astype(ksb_ref.dtype)
    vsb_ref[...] = proj(C_VSB, SB_WIDTH).astype(vsb_ref.dtype)
    gsb = proj(C_GSB, SB_WIDTH)
    gsb_ref[...] = gsb * jax.nn.sigmoid(gsb)


def _compress_kernel(kc_ref, vc_ref, posk_ref, posv_ref, w1k_ref, b1k_ref, w2k_ref,
                     w1v_ref, b1v_ref, w2v_ref, kg_ref, cos_ref, sin_ref, perm_ref,
                     kcmp_ref, vcmp_ref):
    half = CMP_STRIDE * HEAD_DIM

    def phi(c_ref, pos_ref, w1_ref, b1_ref, w2_ref):
        c = c_ref[0]
        n = c.shape[0]
        top = _dot((c + pos_ref[0:1, :]).astype(_MXU_DTYPE), w1_ref[:half, :])
        bot = _dot((c + pos_ref[1:2, :]).astype(_MXU_DTYPE), w1_ref[half:, :])
        hid = top + pltpu.roll(bot, n - 1, axis=0) + b1_ref[...]
        return _dot((hid * jax.nn.sigmoid(hid)).astype(_MXU_DTYPE), w2_ref[...])

    k = phi(kc_ref, posk_ref, w1k_ref, b1k_ref, w2k_ref)
    ms = jnp.mean(k * k, axis=-1, keepdims=True)
    kn = k * lax.rsqrt(ms + EPS) * kg_ref[...]
    hi, lo = _split2(kn)
    partner = _dot(hi, perm_ref[...]) + _dot(lo, perm_ref[...])
    kcmp_ref[0] = (kn * cos_ref[...] + partner * sin_ref[...]).astype(kcmp_ref.dtype)
    vcmp_ref[0] = phi(vc_ref, posv_ref, w1v_ref, b1v_ref, w2v_ref).astype(vcmp_ref.dtype)


def _nsa_kernel(bound_ref, q_ref, kc_ref, vc_ref, ksa_ref, vsa_ref, kw_ref, vwa_ref, ovl_ref,
                gc_ref, gs_ref, gw_ref, o_ref):
    i = pl.program_id(2)
    qt = q_ref.shape[2]
    rq = NSA_GROUP * qt
    n_blk_lanes = LANES - HEAD_DIM
    q_pad = q_ref[0].reshape(rq, LANES)
    q = q_pad[:, :HEAD_DIM]
    t_loc = lax.broadcasted_iota(jnp.int32, (rq, 1), 0) & (qt - 1)
    t_col = i * qt + t_loc

    rel = t_loc - lax.broadcasted_iota(jnp.int32, (rq, qt), 1)

    n_win = WINDOW // qt + 1
    s_parts, v_parts = [], []
    for back in range(n_win):
        start = pl.multiple_of(jnp.maximum(i - back, 0) * qt, qt)
        s = _nt_dot(q, kw_ref[0, 0, pl.ds(start, qt), :])
        if back == 0:
            keep = rel >= 0
        else:
            keep = rel < (WINDOW - back * qt) - jnp.where(i >= back, 0, WINDOW)
        s_parts.append(jnp.where(keep, s, MASK_NEG))
        v_parts.append(vwa_ref[0, 0, pl.ds(start, qt), :])
    s_win = jnp.concatenate(s_parts, axis=1)
    m_w = jnp.max(s_win, axis=-1, keepdims=True)
    acc_w = _dot(jnp.exp2(s_win - m_w).astype(_MXU_DTYPE), jnp.concatenate(v_parts, axis=0))

    kc = kc_ref[0, 0]
    ncp = kc.shape[0]
    lg = _nt_dot(q, kc)
    cmp_end = lax.broadcasted_iota(jnp.int32, (rq, ncp), 1) * CMP_STRIDE + (CMP_LEN - 1)
    valid = cmp_end <= t_col
    lg = jnp.where(valid, lg, MASK_NEG)
    m = jnp.max(lg, axis=-1, keepdims=True)
    p = jnp.where(valid, jnp.exp2(lg - m), 0.0)
    p = p / jnp.maximum(jnp.sum(p, axis=-1, keepdims=True), 1e-30)
    o_cmp = _dot(p.astype(_MXU_DTYPE), vc_ref[0, 0])

    p_sum = p[0:qt]
    for r in range(1, NSA_GROUP):
        p_sum = p_sum + p[r * qt:(r + 1) * qt]
    ovl = ovl_ref[...]
    p_slc = sum(_nt_dot(ovl, part) for part in _split3(p_sum))
    j_idx = lax.broadcasted_iota(jnp.int32, (n_blk_lanes, qt), 0)
    blk_t = (i * qt + lax.broadcasted_iota(jnp.int32, (n_blk_lanes, qt), 1)) >> int(math.log2(SLC_LEN))
    slc_valid = j_idx <= blk_t
    forced = (j_idx == 0) | (j_idx == blk_t) | (j_idx == blk_t - 1)
    score = jnp.where(slc_valid, p_slc + jnp.where(forced, FORCE_BONUS, 0.0), -jnp.inf)
    n_grp = n_blk_lanes // SUBLANES
    grp_rows = [score[g * SUBLANES:(g + 1) * SUBLANES] for g in range(n_grp)]
    grp_rank = [jnp.zeros((SUBLANES, qt), F32) for _ in range(n_grp)]
    row_in_grp = lax.broadcasted_iota(jnp.int32, (SUBLANES, qt), 0)
    for ii in range(n_blk_lanes):
        s_i = score[ii:ii + 1, :]
        for g in range(n_grp):
            rows = grp_rows[g]
            if g * SUBLANES > ii:
                beats = jnp.where(s_i >= rows, 1.0, 0.0)
            elif (g + 1) * SUBLANES - 1 <= ii:
                beats = jnp.where(s_i > rows, 1.0, 0.0)
            else:
                beats = jnp.where(row_in_grp > ii - g * SUBLANES,
                                  jnp.where(s_i >= rows, 1.0, 0.0), jnp.where(s_i > rows, 1.0, 0.0))
            grp_rank[g] = grp_rank[g] + beats
    rank = jnp.concatenate(grp_rank, axis=0)
    sel_t = jnp.where(slc_valid, jnp.where(rank < SLC_TOPN, 1.0, 0.0), 0.0)
    sel = jnp.concatenate([jnp.ones((HEAD_DIM, qt), F32), sel_t], axis=0).T
    lane_q = lax.broadcasted_iota(jnp.int32, (qt, LANES), 1)

    def slc_branch(shift, step, init):
        bias = jnp.where(lane_q >= HEAD_DIM, jnp.where(sel > 0.5, -shift, MASK_NEG), 0.0).astype(q_pad.dtype)
        q_aug = q_pad + jnp.concatenate([bias] * NSA_GROUP, axis=0)

        def logits(kt):
            return _nt_dot(q_aug, ksa_ref[0, 0, pl.ds(pl.multiple_of(kt * qt, qt), qt), :])

        def values(kt):
            return vsa_ref[0, 0, pl.ds(pl.multiple_of(kt * qt, qt), qt), :]

        def body(kt, carry):
            return (logits(kt + 1),) + step(carry[0], values(kt), carry[1:])

        carry = lax.fori_loop(0, i, body, (logits(0),) + init)
        return step(jnp.where(rel >= 0, carry[0], MASK_NEG), values(i), carry[1:])[-1]

    def shifted_step(s, v, carry):
        return (carry[0] + _dot(jnp.exp2(s).astype(_MXU_DTYPE), v),)

    def online_step(s, v, carry):
        m_old, acc = carry
        m_new = jnp.maximum(m_old, jnp.max(s, axis=-1, keepdims=True))
        alpha = jnp.exp2(m_old - m_new)
        return m_new, alpha * acc + _dot(jnp.exp2(s - m_new).astype(_MXU_DTYPE), v)

    logit_bound = bound_ref[0]
    zeros_acc = jnp.zeros((rq, LANES), F32)
    acc_s = lax.cond(
        logit_bound <= MAX_STATIC_SHIFT,
        lambda: slc_branch(logit_bound, shifted_step, (zeros_acc,)),
        lambda: slc_branch(0.0, online_step, (jnp.full((rq, 1), M_INIT, F32), zeros_acc)))

    head = lambda a, r: a[r * qt:(r + 1) * qt]
    low_half = lane_q < HEAD_DIM

    def token_major(o):
        return jnp.concatenate([jnp.where(low_half, head(o, r), head(o, r + 1))
                                for r in range(0, NSA_GROUP, 2)], axis=1)

    def token_major_normalised(acc):
        lane_r = lax.broadcasted_iota(jnp.int32, (rq, LANES), 1)
        inv = 1.0 / jnp.where(lane_r >= HEAD_DIM, acc, 1.0)
        pairs = []
        for r in range(0, NSA_GROUP, 2):
            even = head(acc, r) * pltpu.roll(head(inv, r), HEAD_DIM, axis=1)
            odd = pltpu.roll(head(acc, r + 1), HEAD_DIM, axis=1) * head(inv, r + 1)
            pairs.append(jnp.where(low_half, even, odd))
        return jnp.concatenate(pairs, axis=1)

    out = (gc_ref[0] * token_major(o_cmp) + gs_ref[0] * token_major_normalised(acc_s)
           + gw_ref[0] * token_major_normalised(acc_w))
    o_ref[0] = out.astype(o_ref.dtype)


def _sb_kernel(q_ref, k_ref, v_ref, uu_ref, g_ref, o_ref):
    i = pl.program_id(2)
    rows = q_ref.shape[1]
    band = rows // Q_BLOCK
    n_tiles = (i + 1) * band
    t_col = i * rows + lax.broadcasted_iota(jnp.int32, (rows, 1), 0)
    lane = lax.broadcasted_iota(jnp.int32, (rows, LANES), 1)
    low_half = lane < HEAD_DIM
    low_half_k = lax.broadcasted_iota(jnp.int32, (Q_BLOCK, LANES), 1) < HEAD_DIM
    q_pair = q_ref[0]
    zero = jnp.zeros_like(q_pair)
    q_heads = (jnp.where(low_half, q_pair, zero), jnp.where(low_half, zero, q_pair))

    def make_body(masked):
        def tile_step(kk, carry):
            acc, laters = carry[0], carry[1:]
            start = pl.multiple_of((n_tiles - 1 - kk) * Q_BLOCK, Q_BLOCK)
            k_pair = k_ref[0, pl.ds(start, Q_BLOCK), :]
            v_pair = v_ref[0, pl.ds(start, Q_BLOCK), :]
            zero_v = jnp.zeros_like(v_pair)
            v_bd = jnp.concatenate([jnp.where(low_half_k, v_pair, zero_v),
                                    jnp.where(low_half_k, zero_v, v_pair)], axis=0)
            causal = start + lane < t_col
            weights, new_laters = [], []
            for q_h, later in zip(q_heads, laters):
                z = _nt_dot(q_h, k_pair)
                sp = jnp.maximum(z, 0.0) + jnp.log2(1.0 + jnp.exp2(-jnp.abs(z)))
                if masked:
                    sp = jnp.where(causal, sp, 0.0)
                hi, lo = _split2(sp)
                r = _dot(jnp.concatenate([hi, lo], axis=1), uu_ref[...])
                after = r[:, :LANES] + later
                a = jnp.exp2(z - sp - after)
                if masked:
                    a = jnp.where(causal, a, 0.0)
                weights.append(a.astype(_MXU_DTYPE))
                new_laters.append(later + r[:, LANES:])
            acc = acc + _dot(jnp.concatenate(weights, axis=1), v_bd)
            return (acc, *new_laters)

        def body(kg, carry):
            for u in range(SB_UNROLL):
                carry = tile_step(kg * SB_UNROLL + u, carry)
            return carry
        return body

    zeros = jnp.zeros((rows, LANES), F32)
    carry = lax.fori_loop(0, band // SB_UNROLL, make_body(True), (zeros, zeros, zeros))

    def settled(laters):
        return jnp.min(jnp.minimum(*laters)) >= SB_UNDERFLOW_BITS

    main_body = make_body(False)

    def keep_going(state):
        kg, done, _ = state
        return jnp.logical_and(kg < n_tiles // SB_UNROLL, jnp.logical_not(done))

    def main_step(state):
        kg, _, carry = state
        carry = main_body(kg, carry)
        return kg + 1, settled(carry[1:]), carry

    _, _, carry = lax.while_loop(keep_going, main_step, (band // SB_UNROLL, settled(carry[1:]), carry))
    o_ref[0] = (carry[0] * g_ref[0]).astype(o_ref.dtype)


def _outproj_kernel(x_ref, on_ref, os_ref, w_ref, o_ref):
    o_ref[...] = (x_ref[...] + _dot(on_ref[...], w_ref[:NSA_WIDTH, :])
                  + _dot(os_ref[...], w_ref[NSA_WIDTH:, :]))


def _rope_tables(pos, reps):
    inv_freq = jnp.power(ROPE_THETA, -jnp.arange(0, ROPE_DIM, 2, dtype=F32) / ROPE_DIM)
    ang = pos.astype(F32)[:, None] * inv_freq[None, :]
    cos, sin = jnp.cos(ang), jnp.sin(ang)
    n = pos.shape[0]
    rest = HEAD_DIM - ROPE_DIM
    cos_h = jnp.concatenate([cos, cos, jnp.ones((n, rest), F32)], axis=1)
    sin_h = jnp.concatenate([-sin, sin, jnp.zeros((n, rest), F32)], axis=1)
    return jnp.tile(cos_h, (1, reps)), jnp.tile(sin_h, (1, reps))


def _const_spec(shape):
    return pl.BlockSpec(shape, lambda *_: (0,) * len(shape))


def kernel(x, norm_gain, w_in, q_norm_gain, k_norm_cmp, k_norm_slc, k_norm_win,
           cmp_k_pos, cmp_k_w1, cmp_k_b1, cmp_k_w2, cmp_v_pos, cmp_v_w1, cmp_v_b1, cmp_v_w2, w_out):
    B, S, DM = x.shape
    D, G, R = HEAD_DIM, NSA_KV_HEADS, NSA_GROUP
    mxu = _MXU_DTYPE
    n_tok = B * S
    tm = ROW_TILE
    n_sblk = S // tm
    nq = S // Q_BLOCK
    ncp = S // CMP_STRIDE
    n_slc = S // SLC_LEN
    n_cmp = (S - CMP_LEN) // CMP_STRIDE + 1
    assert S % tm == 0 and ncp % LANES == 0 and n_slc <= LANES - D and n_slc >= SLC_TOPN

    n_gl = NSA_HEADS * N_BRANCH
    gl0 = C_GN + n_gl
    w_cat = jnp.concatenate([w_in[:, :C_GN], w_in[:, gl0:], w_in[:, C_GN:gl0],
                             jnp.zeros((DM, LANES - n_gl), w_in.dtype)], axis=1).astype(mxu)
    assert w_cat.shape[1] == N_COLS
    pos = jnp.arange(S, dtype=jnp.int32)
    cos_t, sin_t = _rope_tables(pos, LANES // D)
    cmp_end = jnp.arange(ncp, dtype=jnp.int32) * CMP_STRIDE + (CMP_LEN - 1)
    cos_c, sin_c = _rope_tables(cmp_end, 1)
    lane_i = np.arange(NSA_WIDTH)
    bd = jnp.asarray(lane_i[:, None] // D == lane_i[None, :] // D, mxu)
    eg = np.zeros((LANES, N_BRANCH * NSA_WIDTH), np.float32)
    for hh in range(NSA_HEADS):
        for br in range(N_BRANCH):
            eg[hh * N_BRANCH + br, br * NSA_WIDTH + hh * D:br * NSA_WIDTH + (hh + 1) * D] = 1.0
    eg = jnp.asarray(eg, mxu)
    perm = np.zeros((D, D), np.float32)
    for c in range(ROPE_HALF):
        perm[c + ROPE_HALF, c] = 1.0
        perm[c, c + ROPE_HALF] = 1.0
    perm = jnp.asarray(perm, mxu)
    cs = np.arange(ncp) * CMP_STRIDE
    ss = np.arange(LANES - D) * SLC_LEN
    ovl = np.clip(np.minimum(cs[None, :] + CMP_LEN, ss[:, None] + SLC_LEN)
                  - np.maximum(cs[None, :], ss[:, None]), 0, None).astype(np.float32) / CMP_LEN
    ovl[:, n_cmp:] = 0.0
    ovl[n_slc:, :] = 0.0
    ovl = jnp.asarray(ovl, mxu)
    sidx = np.arange(LANES)
    tri = (sidx[:, None] > sidx[None, :]).astype(np.float32)
    uu_half = np.concatenate([tri, np.ones((LANES, LANES), np.float32)], axis=1)
    uu = jnp.asarray(np.concatenate([uu_half, uu_half], axis=0), mxu)

    row = lambda v: v.reshape(1, -1).astype(F32)
    x2 = x.reshape(n_tok, DM)

    tok_spec = lambda w: pl.BlockSpec((tm, w), lambda t: (t, 0))
    head_spec = lambda nh, w: pl.BlockSpec((1, nh, tm, w), lambda t: (t // n_sblk, 0, t % n_sblk, 0))
    tab_spec = pl.BlockSpec((tm, LANES), lambda t: (t % n_sblk, 0))
    sds = jax.ShapeDtypeStruct
    outs = pl.pallas_call(
        functools.partial(_inproj_kernel, n_sblk=n_sblk),
        grid=(n_tok // tm,),
        in_specs=[tok_spec(DM), _const_spec((1, DM)), _const_spec((DM, N_COLS)),
                  _const_spec((1, NSA_WIDTH)), _const_spec((1, NSA_KV_WIDTH)), _const_spec((1, NSA_KV_WIDTH)),
                  _const_spec((NSA_WIDTH, NSA_WIDTH)), tab_spec, tab_spec,
                  _const_spec((LANES, N_BRANCH * NSA_WIDTH))],
        out_specs=[head_spec(NSA_HEADS, LANES), head_spec(G, D), head_spec(G, D),
                   head_spec(G, LANES), head_spec(G, LANES), head_spec(G, D), head_spec(G, LANES),
                   tok_spec(NSA_WIDTH), tok_spec(NSA_WIDTH), tok_spec(NSA_WIDTH),
                   tok_spec(SB_WIDTH), tok_spec(SB_WIDTH), tok_spec(SB_WIDTH), tok_spec(SB_WIDTH)],
        out_shape=[sds((B, NSA_HEADS, S, LANES), mxu), sds((B, G, S, D), F32), sds((B, G, S, D), F32),
                   sds((B, G, S, LANES), mxu), sds((B, G, S, LANES), mxu), sds((B, G, S, D), mxu),
                   sds((B, G, S, LANES), mxu),
                   sds((n_tok, NSA_WIDTH), F32), sds((n_tok, NSA_WIDTH), F32), sds((n_tok, NSA_WIDTH), F32),
                   sds((n_tok, SB_WIDTH), mxu), sds((n_tok, SB_WIDTH), mxu), sds((n_tok, SB_WIDTH), mxu),
                   sds((n_tok, SB_WIDTH), F32)],
        compiler_params=pltpu.CompilerParams(dimension_semantics=("parallel",), vmem_limit_bytes=VMEM_LIMIT),
        name="inproj",
    )(x2, row(norm_gain), w_cat, row(jnp.tile(q_norm_gain, NSA_HEADS)),
      row(jnp.tile(k_norm_slc, G)), row(jnp.tile(k_norm_win, G)), bd, cos_t, sin_t, eg)
    (q_nsa, kc_raw, vc_raw, ks_aug, vs_aug, k_win, vw_aug, g_cmp, g_slc, g_win,
     q_sb, k_sb, v_sb, g_sb) = outs

    chunk_w = CMP_STRIDE * D
    chunks = lambda a: a.reshape(B * G, ncp, chunk_w)
    bg_spec = lambda r, w: pl.BlockSpec((1, r, w), lambda t: (t, 0, 0))
    k_cmp, v_cmp = pl.pallas_call(
        _compress_kernel,
        grid=(B * G,),
        in_specs=[bg_spec(ncp, chunk_w), bg_spec(ncp, chunk_w),
                  _const_spec((2, chunk_w)), _const_spec((2, chunk_w)),
                  _const_spec((CMP_LEN * D, D)), _const_spec((1, D)), _const_spec((D, D)),
                  _const_spec((CMP_LEN * D, D)), _const_spec((1, D)), _const_spec((D, LANES)),
                  _const_spec((1, D)), _const_spec((ncp, D)), _const_spec((ncp, D)), _const_spec((D, D))],
        out_specs=[bg_spec(ncp, D), bg_spec(ncp, LANES)],
        out_shape=[sds((B * G, ncp, D), mxu), sds((B * G, ncp, LANES), mxu)],
        compiler_params=pltpu.CompilerParams(dimension_semantics=("parallel",), vmem_limit_bytes=VMEM_LIMIT),
        name="compress",
    )(chunks(kc_raw), chunks(vc_raw), cmp_k_pos.reshape(2, chunk_w), cmp_v_pos.reshape(2, chunk_w),
      cmp_k_w1.astype(mxu), row(cmp_k_b1), cmp_k_w2.astype(mxu),
      cmp_v_w1.astype(mxu), row(cmp_v_b1), jnp.tile(cmp_v_w2, (1, LANES // D)).astype(mxu),
      row(k_norm_cmp), cos_c, sin_c, perm)
    k_cmp = k_cmp.reshape(B, G, ncp, D)
    v_cmp = v_cmp.reshape(B, G, ncp, LANES)

    slc_bound = (jnp.max(jnp.abs(q_norm_gain)) * jnp.max(jnp.abs(k_norm_slc)) * (1.02 * D * SCALE * LOG2E))
    slc_bound = slc_bound.astype(mxu).astype(F32).reshape(1)
    kv_spec = lambda r, w: pl.BlockSpec((1, 1, r, w), lambda b, g, i: (b, g, 0, 0))
    gate_spec = pl.BlockSpec((1, NSA_QT, R * D), lambda b, g, i: (b, i, g))
    g3 = lambda a: a.reshape(B, S, NSA_WIDTH)
    o_nsa = pl.pallas_call(
        _nsa_kernel,
        grid=(B, G, S // NSA_QT),
        in_specs=[pl.BlockSpec(memory_space=pltpu.SMEM),
                  pl.BlockSpec((1, R, NSA_QT, LANES), lambda b, g, i: (b, g, i, 0)),
                  kv_spec(ncp, D), kv_spec(ncp, LANES), kv_spec(S, LANES), kv_spec(S, LANES),
                  kv_spec(S, D), kv_spec(S, LANES), _const_spec((LANES - D, ncp)),
                  gate_spec, gate_spec, gate_spec],
        out_specs=gate_spec,
        out_shape=sds((B, S, NSA_WIDTH), mxu),
        compiler_params=pltpu.CompilerParams(dimension_semantics=("parallel", "parallel", "arbitrary"),
                                             vmem_limit_bytes=VMEM_LIMIT),
        name="nsa",
    )(slc_bound, q_nsa, k_cmp, v_cmp, ks_aug, vs_aug, k_win, vw_aug, ovl, g3(g_cmp), g3(g_slc), g3(g_win))

    sb3 = lambda a: a.reshape(B, S, SB_WIDTH)
    pair_q = pl.BlockSpec((1, SB_ROWS, LANES), lambda b, hp, i: (b, i, hp))
    pair_kv = pl.BlockSpec((1, S, LANES), lambda b, hp, i: (b, 0, hp))
    o_sb = pl.pallas_call(
        _sb_kernel,
        grid=(B, SB_WIDTH // LANES, S // SB_ROWS),
        in_specs=[pair_q, pair_kv, pair_kv, _const_spec((2 * LANES, 2 * LANES)), pair_q],
        out_specs=pair_q,
        out_shape=sds((B, S, SB_WIDTH), mxu),
        compiler_params=pltpu.CompilerParams(dimension_semantics=("parallel", "parallel", "arbitrary"),
                                             vmem_limit_bytes=VMEM_LIMIT),
        name="stickbreak",
    )(sb3(q_sb), sb3(k_sb), sb3(v_sb), uu, sb3(g_sb))

    out = pl.pallas_call(
        _outproj_kernel,
        grid=(n_tok // tm,),
        in_specs=[tok_spec(DM), tok_spec(NSA_WIDTH), tok_spec(SB_WIDTH),
                  _const_spec((NSA_WIDTH + SB_WIDTH, DM))],
        out_specs=tok_spec(DM),
        out_shape=sds((n_tok, DM), x.dtype),
        compiler_params=pltpu.CompilerParams(dimension_semantics=("parallel",), vmem_limit_bytes=VMEM_LIMIT),
        name="outproj",
    )(x2, o_nsa.reshape(n_tok, NSA_WIDTH), o_sb.reshape(n_tok, SB_WIDTH), w_out.astype(mxu))
    return out.reshape(B, S, DM)
```

```python
import functools
import math

import numpy as np
import jax
import jax.numpy as jnp
from jax import lax
from jax.experimental import pallas as pl
from jax.experimental.pallas import tpu as pltpu

HEAD_DIM = 64
NSA_HEADS = 8
NSA_KV_HEADS = 2
NSA_GROUP = NSA_HEADS // NSA_KV_HEADS
SB_HEADS = 8
NSA_WIDTH = NSA_HEADS * HEAD_DIM
SB_WIDTH = SB_HEADS * HEAD_DIM
NSA_KV_WIDTH = NSA_KV_HEADS * HEAD_DIM
N_BRANCH = 3
CMP_LEN = 32
CMP_STRIDE = 16
SLC_LEN = 64
SLC_TOPN = 16
WINDOW = 512
Q_BLOCK = 128
ROPE_DIM = HEAD_DIM // 4
ROPE_HALF = ROPE_DIM // 2
ROPE_THETA = 500000.0
EPS = 1e-6
FORCE_BONUS = 1.0e4
SCALE = 1.0 / math.sqrt(HEAD_DIM)
LOG2E = math.log2(math.e)

LANES = 128
SUBLANES = 8
MASK_NEG = -1.0e30
M_INIT = -3.0e38
ROW_TILE = 512
NSA_QT = 256
SB_ROWS = 512
SB_UNROLL = 4
MAX_STATIC_SHIFT = 60.0
SB_UNDERFLOW_BITS = 160.0
VMEM_LIMIT = 56 * 1024 * 1024

_MXU_DTYPE = jnp.bfloat16
F32 = jnp.float32

C_Q = 0
C_KC = 512
C_VC = 640
C_KS = 768
C_VS = 896
C_KW = 1024
C_VW = 1152
C_GN = 1280
C_QSB = 1792
C_KSB = 2304
C_VSB = 2816
C_GSB = 3328
C_GL = 3840
N_COLS = 3968


def _nt_dot(a, b):
    return lax.dot_general(a, b, (((1,), (1,)), ((), ())), preferred_element_type=F32)


def _dot(a, b):
    return jnp.dot(a, b, preferred_element_type=F32)


def _split2(v):
    hi = v.astype(_MXU_DTYPE)
    lo = (v - hi.astype(F32)).astype(_MXU_DTYPE)
    return hi, lo


def _split3(v):
    hi = v.astype(_MXU_DTYPE)
    r1 = v - hi.astype(F32)
    mid = r1.astype(_MXU_DTYPE)
    lo = (r1 - mid.astype(F32)).astype(_MXU_DTYPE)
    return hi, mid, lo


def _inproj_kernel(x_ref, ng_ref, w_ref, qg_ref, kgs_ref, kgw_ref, bd_ref, cos_ref, sin_ref, eg_ref,
                   q_ref, kc_ref, vc_ref, ksa_ref, vsa_ref, kw_ref, vwa_ref,
                   gc_ref, gs_ref, gw_ref, qsb_ref, ksb_ref, vsb_ref, gsb_ref, *, n_sblk):
    tm = x_ref.shape[0]
    x = x_ref[...]
    ms = jnp.mean(x * x, axis=-1, keepdims=True)
    h = (x * lax.rsqrt(ms + EPS) * ng_ref[...]).astype(_MXU_DTYPE)

    def proj(lo, width):
        return _dot(h, w_ref[:, lo:lo + width])

    lane = lax.broadcasted_iota(jnp.int32, (tm, LANES), 1)
    low_half = lane < HEAD_DIM

    def head_norm_rope(y, gain):
        width = y.shape[1]
        rep = width // LANES
        hi, lo = _split2(y * y)
        bd = bd_ref[:width, :width]
        ssum = _dot(hi, bd) + _dot(lo, bd)
        yn = y * lax.rsqrt(ssum * (1.0 / HEAD_DIM) + EPS) * gain
        cos = jnp.concatenate([cos_ref[...]] * rep, axis=1) if rep > 1 else cos_ref[...]
        sin = jnp.concatenate([sin_ref[...]] * rep, axis=1) if rep > 1 else sin_ref[...]
        fwd = pltpu.roll(yn, ROPE_HALF, axis=1)
        bwd = pltpu.roll(yn, width - ROPE_HALF, axis=1)
        lane_w = lax.broadcasted_iota(jnp.int32, (tm, width), 1)
        partner = jnp.where((lane_w & (HEAD_DIM - 1)) < ROPE_HALF, bwd, fwd)
        return yn * cos + partner * sin

    def head_pair(slab, p):
        chunk = slab[:, p * LANES:(p + 1) * LANES]
        return chunk, pltpu.roll(chunk, HEAD_DIM, axis=1)

    qn = head_norm_rope(proj(C_Q, NSA_WIDTH), qg_ref[...]) * (SCALE * LOG2E)
    for p in range(NSA_HEADS // 2):
        ev, od = head_pair(qn, p)
        q_ref[0, 2 * p] = jnp.where(low_half, ev, 0.0).astype(q_ref.dtype)
        q_ref[0, 2 * p + 1] = jnp.where(low_half, od, 0.0).astype(q_ref.dtype)

    for ref, col in ((kc_ref, C_KC), (vc_ref, C_VC)):
        ev, od = head_pair(proj(col, NSA_KV_WIDTH), 0)
        ref[0, 0] = ev[:, :HEAD_DIM]
        ref[0, 1] = od[:, :HEAD_DIM]

    sblk = lax.rem(pl.program_id(0), n_sblk)
    row = lax.broadcasted_iota(jnp.int32, (tm, LANES), 0)
    key_blk = (sblk * tm + row) >> int(math.log2(SLC_LEN))
    onehot = jnp.where(lane - HEAD_DIM == key_blk, 1.0, 0.0)
    ks = head_norm_rope(proj(C_KS, NSA_KV_WIDTH), kgs_ref[...])
    ev, od = head_pair(ks, 0)
    ksa_ref[0, 0] = jnp.where(low_half, ev, onehot).astype(ksa_ref.dtype)
    ksa_ref[0, 1] = jnp.where(low_half, od, onehot).astype(ksa_ref.dtype)
    ev, od = head_pair(proj(C_VS, NSA_KV_WIDTH), 0)
    vsa_ref[0, 0] = jnp.where(low_half, ev, 1.0).astype(vsa_ref.dtype)
    vsa_ref[0, 1] = jnp.where(low_half, od, 1.0).astype(vsa_ref.dtype)

    kw = head_norm_rope(proj(C_KW, NSA_KV_WIDTH), kgw_ref[...])
    ev, od = head_pair(kw, 0)
    kw_ref[0, 0] = ev[:, :HEAD_DIM].astype(kw_ref.dtype)
    kw_ref[0, 1] = od[:, :HEAD_DIM].astype(kw_ref.dtype)
    ev, od = head_pair(proj(C_VW, NSA_KV_WIDTH), 0)
    vwa_ref[0, 0] = jnp.where(low_half, ev, 1.0).astype(vwa_ref.dtype)
    vwa_ref[0, 1] = jnp.where(low_half, od, 1.0).astype(vwa_ref.dtype)

    gn = proj(C_GN, NSA_WIDTH)
    silu_n = gn * jax.nn.sigmoid(gn)
    gl_hi, gl_lo = _split2(proj(C_GL, LANES))
    for br, ref in enumerate((gc_ref, gs_ref, gw_ref)):
        e = eg_ref[:, br * NSA_WIDTH:(br + 1) * NSA_WIDTH]
        ref[...] = jax.nn.sigmoid(_dot(gl_hi, e) + _dot(gl_lo, e)) * silu_n

    qsb_ref[...] = (proj(C_QSB, SB_WIDTH) * (SCALE * LOG2E)).astype(qsb_ref.dtype)
    ksb_ref[...] = proj(C_KSB, SB_WIDTH).astype(ksb_ref.dtype)
    vsb_ref[...] = proj(C_VSB, SB_WIDTH).astype(vsb_ref.dtype)
    gsb = proj(C_GSB, SB_WIDTH)
    gsb_ref[...] = gsb * jax.nn.sigmoid(gsb)


def _compress_kernel(kc_ref, vc_ref, posk_ref, posv_ref, w1k_ref, b1k_ref, w2k_ref,
                     w1v_ref, b1v_ref, w2v_ref, kg_ref, cos_ref, sin_ref, perm_ref,
                     kcmp_ref, vcmp_ref):
    half = CMP_STRIDE * HEAD_DIM

    def phi(c_ref, pos_ref, w1_ref, b1_ref, w2_ref):
        c = c_ref[0]
        n = c.shape[0]
        top = _dot((c + pos_ref[0:1, :]).astype(_MXU_DTYPE), w1_ref[:half, :])
        bot = _dot((c + pos_ref[1:2, :]).astype(_MXU_DTYPE), w1_ref[half:, :])
        hid = top + pltpu.roll(bot, n - 1, axis=0) + b1_ref[...]
        return _dot((hid * jax.nn.sigmoid(hid)).astype(_MXU_DTYPE), w2_ref[...])

    k = phi(kc_ref, posk_ref, w1k_ref, b1k_ref, w2k_ref)
    ms = jnp.mean(k * k, axis=-1, keepdims=True)
    kn = k * lax.rsqrt(ms + EPS) * kg_ref[...]
    hi, lo = _split2(kn)
    partner = _dot(hi, perm_ref[...]) + _dot(lo, perm_ref[...])
    kcmp_ref[0] = (kn * cos_ref[...] + partner * sin_ref[...]).astype(kcmp_ref.dtype)
    vcmp_ref[0] = phi(vc_ref, posv_ref, w1v_ref, b1v_ref, w2v_ref).astype(vcmp_ref.dtype)


def _nsa_kernel(bound_ref, q_ref, kc_ref, vc_ref, ksa_ref, vsa_ref, kw_ref, vwa_ref, ovl_ref,
                gc_ref, gs_ref, gw_ref, o_ref):
    i = pl.program_id(2)
    qt = q_ref.shape[2]
    rq = NSA_GROUP * qt
    n_blk_lanes = LANES - HEAD_DIM
    q_pad = q_ref[0].reshape(rq, LANES)
    q = q_pad[:, :HEAD_DIM]
    t_loc = lax.broadcasted_iota(jnp.int32, (rq, 1), 0) & (qt - 1)
    t_col = i * qt + t_loc

    rel = t_loc - lax.broadcasted_iota(jnp.int32, (rq, qt), 1)

    logit_bound = bound_ref[0]
    static_shift = logit_bound <= MAX_STATIC_SHIFT

    def row_shift(exact_row_max):
        return lax.cond(static_shift, lambda: jnp.full((rq, 1), logit_bound, F32), exact_row_max)

    def window_logits():
        s_parts, v_parts = [], []
        for back in range(WINDOW // qt + 1):
            start = pl.multiple_of(jnp.maximum(i - back, 0) * qt, qt)
            s = _nt_dot(q, kw_ref[0, 0, pl.ds(start, qt), :])
            if back == 0:
                keep = rel >= 0
            else:
                keep = rel < (WINDOW - back * qt) - jnp.where(i >= back, 0, WINDOW)
            s_parts.append(jnp.where(keep, s, MASK_NEG))
            v_parts.append(vwa_ref[0, 0, pl.ds(start, qt), :])
        return jnp.concatenate(s_parts, axis=1), jnp.concatenate(v_parts, axis=0)

    def cmp_logits():
        kc = kc_ref[0, 0]
        cmp_end = lax.broadcasted_iota(jnp.int32, (rq, kc.shape[0]), 1) * CMP_STRIDE + (CMP_LEN - 1)
        valid = cmp_end <= t_col
        return jnp.where(valid, _nt_dot(q, kc), MASK_NEG), valid

    m_w, m_c = lax.cond(
        static_shift,
        lambda: (jnp.full((rq, 1), logit_bound, F32),) * 2,
        lambda: (jnp.max(window_logits()[0], axis=-1, keepdims=True),
                 jnp.max(cmp_logits()[0], axis=-1, keepdims=True)))

    s_win, v_win = window_logits()
    acc_w = _dot(jnp.exp2(s_win - m_w).astype(_MXU_DTYPE), v_win)

    lg, valid = cmp_logits()
    p = jnp.where(valid, jnp.exp2(lg - m_c), 0.0)
    p = p / jnp.maximum(jnp.sum(p, axis=-1, keepdims=True), 1e-30)
    o_cmp = _dot(p.astype(_MXU_DTYPE), vc_ref[0, 0])

    p_sum = p[0:qt]
    for r in range(1, NSA_GROUP):
        p_sum = p_sum + p[r * qt:(r + 1) * qt]
    ovl = ovl_ref[...]
    p_slc = sum(_nt_dot(ovl, part) for part in _split3(p_sum))
    j_idx = lax.broadcasted_iota(jnp.int32, (n_blk_lanes, qt), 0)
    blk_t = (i * qt + lax.broadcasted_iota(jnp.int32, (n_blk_lanes, qt), 1)) >> int(math.log2(SLC_LEN))
    slc_valid = j_idx <= blk_t
    forced = (j_idx == 0) | (j_idx == blk_t) | (j_idx == blk_t - 1)
    score = jnp.where(slc_valid, p_slc + jnp.where(forced, FORCE_BONUS, 0.0), -jnp.inf)
    n_grp = n_blk_lanes // SUBLANES
    grp_rows = [score[g * SUBLANES:(g + 1) * SUBLANES] for g in range(n_grp)]
    grp_rank = [jnp.zeros((SUBLANES, qt), F32) for _ in range(n_grp)]
    row_in_grp = lax.broadcasted_iota(jnp.int32, (SUBLANES, qt), 0)
    for ii in range(n_blk_lanes):
        s_i = score[ii:ii + 1, :]
        for g in range(n_grp):
            rows = grp_rows[g]
            if g * SUBLANES > ii:
                beats = jnp.where(s_i >= rows, 1.0, 0.0)
            elif (g + 1) * SUBLANES - 1 <= ii:
                beats = jnp.where(s_i > rows, 1.0, 0.0)
            else:
                beats = jnp.where(row_in_grp > ii - g * SUBLANES,
                                  jnp.where(s_i >= rows, 1.0, 0.0), jnp.where(s_i > rows, 1.0, 0.0))
            grp_rank[g] = grp_rank[g] + beats
    rank = jnp.concatenate(grp_rank, axis=0)
    sel_t = jnp.where(slc_valid, jnp.where(rank < SLC_TOPN, 1.0, 0.0), 0.0)
    sel = jnp.concatenate([jnp.ones((HEAD_DIM, qt), F32), sel_t], axis=0).T
    lane_q = lax.broadcasted_iota(jnp.int32, (qt, LANES), 1)

    kw = 2 * qt
    n_kt = (i + 2) // 2
    rel_w = t_loc - lax.broadcasted_iota(jnp.int32, (rq, kw), 1)
    sel_rows = jnp.concatenate([sel] * NSA_GROUP, axis=0)
    lane_r = lax.broadcasted_iota(jnp.int32, (rq, LANES), 1)

    def augmented_q(shift):
        bias = jnp.where(lane_r >= HEAD_DIM, jnp.where(sel_rows > 0.5, -shift, MASK_NEG), 0.0)
        return q_pad + bias.astype(q_pad.dtype)

    def slc_logits(q_aug, kt):
        s = _nt_dot(q_aug, ksa_ref[0, 0, pl.ds(pl.multiple_of(kt * kw, kw), kw), :])
        return jnp.where(rel_w >= kt * kw - i * qt, s, MASK_NEG)

    def slc_row_max():
        q_aug = augmented_q(0.0)
        return lax.fori_loop(
            0, n_kt, lambda kt, m_run: jnp.maximum(m_run, jnp.max(slc_logits(q_aug, kt), axis=-1, keepdims=True)),
            jnp.full((rq, 1), M_INIT, F32))

    q_aug = augmented_q(row_shift(slc_row_max))

    def slc_weights(kt):
        return jnp.exp2(slc_logits(q_aug, kt)).astype(_MXU_DTYPE)

    def slc_accumulate(acc, p_kt, kt):
        return acc + _dot(p_kt, vsa_ref[0, 0, pl.ds(pl.multiple_of(kt * kw, kw), kw), :])

    def slc_body(kt, carry):
        p_cur, acc = carry
        return slc_weights(kt + 1), slc_accumulate(acc, p_cur, kt)

    p_last, acc_s = lax.fori_loop(0, n_kt - 1, slc_body, (slc_weights(0), jnp.zeros((rq, LANES), F32)))
    acc_s = slc_accumulate(acc_s, p_last, n_kt - 1)

    head = lambda a, r: a[r * qt:(r + 1) * qt]
    low_half = lane_q < HEAD_DIM

    def token_major(o):
        return jnp.concatenate([jnp.where(low_half, head(o, r), head(o, r + 1))
                                for r in range(0, NSA_GROUP, 2)], axis=1)

    def token_major_normalised(acc):
        inv = 1.0 / jnp.where(lane_r >= HEAD_DIM, acc, 1.0)
        pairs = []
        for r in range(0, NSA_GROUP, 2):
            even = head(acc, r) * pltpu.roll(head(inv, r), HEAD_DIM, axis=1)
            odd = pltpu.roll(head(acc, r + 1), HEAD_DIM, axis=1) * head(inv, r + 1)
            pairs.append(jnp.where(low_half, even, odd))
        return jnp.concatenate(pairs, axis=1)

    out = (gc_ref[0] * token_major(o_cmp) + gs_ref[0] * token_major_normalised(acc_s)
           + gw_ref[0] * token_major_normalised(acc_w))
    o_ref[0] = out.astype(o_ref.dtype)


def _sb_kernel(q_ref, k_ref, v_ref, uu_ref, g_ref, o_ref):
    i = pl.program_id(2)
    rows = q_ref.shape[1]
    band = rows // Q_BLOCK
    n_tiles = (i + 1) * band
    t_col = i * rows + lax.broadcasted_iota(jnp.int32, (rows, 1), 0)
    lane = lax.broadcasted_iota(jnp.int32, (rows, LANES), 1)
    low_half = lane < HEAD_DIM
    low_half_k = lax.broadcasted_iota(jnp.int32, (Q_BLOCK, LANES), 1) < HEAD_DIM
    q_pair = q_ref[0]
    zero = jnp.zeros_like(q_pair)
    q_heads = (jnp.where(low_half, q_pair, zero), jnp.where(low_half, zero, q_pair))

    def make_body(masked):
        def tile_step(kk, carry):
            acc, laters = carry[0], carry[1:]
            start = pl.multiple_of((n_tiles - 1 - kk) * Q_BLOCK, Q_BLOCK)
            k_pair = k_ref[0, pl.ds(start, Q_BLOCK), :]
            v_pair = v_ref[0, pl.ds(start, Q_BLOCK), :]
            zero_v = jnp.zeros_like(v_pair)
            v_bd = jnp.concatenate([jnp.where(low_half_k, v_pair, zero_v),
                                    jnp.where(low_half_k, zero_v, v_pair)], axis=0)
            causal = start + lane < t_col
            weights, new_laters = [], []
            for q_h, later in zip(q_heads, laters):
                z = _nt_dot(q_h, k_pair)
                sp = jnp.maximum(z, 0.0) + jnp.log2(1.0 + jnp.exp2(-jnp.abs(z)))
                if masked:
                    sp = jnp.where(causal, sp, 0.0)
                hi, lo = _split2(sp)
                r = _dot(jnp.concatenate([hi, lo], axis=1), uu_ref[...])
                after = r[:, :LANES] + later
                a = jnp.exp2(z - sp - after)
                if masked:
                    a = jnp.where(causal, a, 0.0)
                weights.append(a.astype(_MXU_DTYPE))
                new_laters.append(later + r[:, LANES:])
            acc = acc + _dot(jnp.concatenate(weights, axis=1), v_bd)
            return (acc, *new_laters)

        def body(kg, carry):
            for u in range(SB_UNROLL):
                carry = tile_step(kg * SB_UNROLL + u, carry)
            return carry
        return body

    zeros = jnp.zeros((rows, LANES), F32)
    carry = lax.fori_loop(0, band // SB_UNROLL, make_body(True), (zeros, zeros, zeros))

    def settled(laters):
        return jnp.min(jnp.minimum(*laters)) >= SB_UNDERFLOW_BITS

    main_body = make_body(False)

    def keep_going(state):
        kg, done, _ = state
        return jnp.logical_and(kg < n_tiles // SB_UNROLL, jnp.logical_not(done))

    def main_step(state):
        kg, _, carry = state
        carry = main_body(kg, carry)
        return kg + 1, settled(carry[1:]), carry

    _, _, carry = lax.while_loop(keep_going, main_step, (band // SB_UNROLL, settled(carry[1:]), carry))
    o_ref[0] = (carry[0] * g_ref[0]).astype(o_ref.dtype)


def _outproj_kernel(x_ref, on_ref, os_ref, w_ref, o_ref):
    o_ref[...] = (x_ref[...] + _dot(on_ref[...], w_ref[:NSA_WIDTH, :])
                  + _dot(os_ref[...], w_ref[NSA_WIDTH:, :]))


def _rope_tables(pos, reps):
    inv_freq = jnp.power(ROPE_THETA, -jnp.arange(0, ROPE_DIM, 2, dtype=F32) / ROPE_DIM)
    ang = pos.astype(F32)[:, None] * inv_freq[None, :]
    cos, sin = jnp.cos(ang), jnp.sin(ang)
    n = pos.shape[0]
    rest = HEAD_DIM - ROPE_DIM
    cos_h = jnp.concatenate([cos, cos, jnp.ones((n, rest), F32)], axis=1)
    sin_h = jnp.concatenate([-sin, sin, jnp.zeros((n, rest), F32)], axis=1)
    return jnp.tile(cos_h, (1, reps)), jnp.tile(sin_h, (1, reps))


def _const_spec(shape):
    return pl.BlockSpec(shape, lambda *_: (0,) * len(shape))


def kernel(x, norm_gain, w_in, q_norm_gain, k_norm_cmp, k_norm_slc, k_norm_win,
           cmp_k_pos, cmp_k_w1, cmp_k_b1, cmp_k_w2, cmp_v_pos, cmp_v_w1, cmp_v_b1, cmp_v_w2, w_out):
    B, S, DM = x.shape
    D, G, R = HEAD_DIM, NSA_KV_HEADS, NSA_GROUP
    mxu = _MXU_DTYPE
    n_tok = B * S
    tm = ROW_TILE
    n_sblk = S // tm
    nq = S // Q_BLOCK
    ncp = S // CMP_STRIDE
    n_slc = S // SLC_LEN
    n_cmp = (S - CMP_LEN) // CMP_STRIDE + 1
    assert S % tm == 0 and ncp % LANES == 0 and n_slc <= LANES - D and n_slc >= SLC_TOPN

    n_gl = NSA_HEADS * N_BRANCH
    gl0 = C_GN + n_gl
    w_cat = jnp.concatenate([w_in[:, :C_GN], w_in[:, gl0:], w_in[:, C_GN:gl0],
                             jnp.zeros((DM, LANES - n_gl), w_in.dtype)], axis=1).astype(mxu)
    assert w_cat.shape[1] == N_COLS
    pos = jnp.arange(S, dtype=jnp.int32)
    cos_t, sin_t = _rope_tables(pos, LANES // D)
    cmp_end = jnp.arange(ncp, dtype=jnp.int32) * CMP_STRIDE + (CMP_LEN - 1)
    cos_c, sin_c = _rope_tables(cmp_end, 1)
    lane_i = np.arange(NSA_WIDTH)
    bd = jnp.asarray(lane_i[:, None] // D == lane_i[None, :] // D, mxu)
    eg = np.zeros((LANES, N_BRANCH * NSA_WIDTH), np.float32)
    for hh in range(NSA_HEADS):
        for br in range(N_BRANCH):
            eg[hh * N_BRANCH + br, br * NSA_WIDTH + hh * D:br * NSA_WIDTH + (hh + 1) * D] = 1.0
    eg = jnp.asarray(eg, mxu)
    perm = np.zeros((D, D), np.float32)
    for c in range(ROPE_HALF):
        perm[c + ROPE_HALF, c] = 1.0
        perm[c, c + ROPE_HALF] = 1.0
    perm = jnp.asarray(perm, mxu)
    cs = np.arange(ncp) * CMP_STRIDE
    ss = np.arange(LANES - D) * SLC_LEN
    ovl = np.clip(np.minimum(cs[None, :] + CMP_LEN, ss[:, None] + SLC_LEN)
                  - np.maximum(cs[None, :], ss[:, None]), 0, None).astype(np.float32) / CMP_LEN
    ovl[:, n_cmp:] = 0.0
    ovl[n_slc:, :] = 0.0
    ovl = jnp.asarray(ovl, mxu)
    sidx = np.arange(LANES)
    tri = (sidx[:, None] > sidx[None, :]).astype(np.float32)
    uu_half = np.concatenate([tri, np.ones((LANES, LANES), np.float32)], axis=1)
    uu = jnp.asarray(np.concatenate([uu_half, uu_half], axis=0), mxu)

    row = lambda v: v.reshape(1, -1).astype(F32)
    x2 = x.reshape(n_tok, DM)

    tok_spec = lambda w: pl.BlockSpec((tm, w), lambda t: (t, 0))
    head_spec = lambda nh, w: pl.BlockSpec((1, nh, tm, w), lambda t: (t // n_sblk, 0, t % n_sblk, 0))
    tab_spec = pl.BlockSpec((tm, LANES), lambda t: (t % n_sblk, 0))
    sds = jax.ShapeDtypeStruct
    outs = pl.pallas_call(
        functools.partial(_inproj_kernel, n_sblk=n_sblk),
        grid=(n_tok // tm,),
        in_specs=[tok_spec(DM), _const_spec((1, DM)), _const_spec((DM, N_COLS)),
                  _const_spec((1, NSA_WIDTH)), _const_spec((1, NSA_KV_WIDTH)), _const_spec((1, NSA_KV_WIDTH)),
                  _const_spec((NSA_WIDTH, NSA_WIDTH)), tab_spec, tab_spec,
                  _const_spec((LANES, N_BRANCH * NSA_WIDTH))],
        out_specs=[head_spec(NSA_HEADS, LANES), head_spec(G, D), head_spec(G, D),
                   head_spec(G, LANES), head_spec(G, LANES), head_spec(G, D), head_spec(G, LANES),
                   tok_spec(NSA_WIDTH), tok_spec(NSA_WIDTH), tok_spec(NSA_WIDTH),
                   tok_spec(SB_WIDTH), tok_spec(SB_WIDTH), tok_spec(SB_WIDTH), tok_spec(SB_WIDTH)],
        out_shape=[sds((B, NSA_HEADS, S, LANES), mxu), sds((B, G, S, D), F32), sds((B, G, S, D), F32),
                   sds((B, G, S, LANES), mxu), sds((B, G, S, LANES), mxu), sds((B, G, S, D), mxu),
                   sds((B, G, S, LANES), mxu),
                   sds((n_tok, NSA_WIDTH), F32), sds((n_tok, NSA_WIDTH), F32), sds((n_tok, NSA_WIDTH), F32),
                   sds((n_tok, SB_WIDTH), mxu), sds((n_tok, SB_WIDTH), mxu), sds((n_tok, SB_WIDTH), mxu),
                   sds((n_tok, SB_WIDTH), F32)],
        compiler_params=pltpu.CompilerParams(dimension_semantics=("parallel",), vmem_limit_bytes=VMEM_LIMIT),
        name="inproj",
    )(x2, row(norm_gain), w_cat, row(jnp.tile(q_norm_gain, NSA_HEADS)),
      row(jnp.tile(k_norm_slc, G)), row(jnp.tile(k_norm_win, G)), bd, cos_t, sin_t, eg)
    (q_nsa, kc_raw, vc_raw, ks_aug, vs_aug, k_win, vw_aug, g_cmp, g_slc, g_win,
     q_sb, k_sb, v_sb, g_sb) = outs

    chunk_w = CMP_STRIDE * D
    chunks = lambda a: a.reshape(B * G, ncp, chunk_w)
    bg_spec = lambda r, w: pl.BlockSpec((1, r, w), lambda t: (t, 0, 0))
    k_cmp, v_cmp = pl.pallas_call(
        _compress_kernel,
        grid=(B * G,),
        in_specs=[bg_spec(ncp, chunk_w), bg_spec(ncp, chunk_w),
                  _const_spec((2, chunk_w)), _const_spec((2, chunk_w)),
                  _const_spec((CMP_LEN * D, D)), _const_spec((1, D)), _const_spec((D, D)),
                  _const_spec((CMP_LEN * D, D)), _const_spec((1, D)), _const_spec((D, LANES)),
                  _const_spec((1, D)), _const_spec((ncp, D)), _const_spec((ncp, D)), _const_spec((D, D))],
        out_specs=[bg_spec(ncp, D), bg_spec(ncp, LANES)],
        out_shape=[sds((B * G, ncp, D), mxu), sds((B * G, ncp, LANES), mxu)],
        compiler_params=pltpu.CompilerParams(dimension_semantics=("parallel",), vmem_limit_bytes=VMEM_LIMIT),
        name="compress",
    )(chunks(kc_raw), chunks(vc_raw), cmp_k_pos.reshape(2, chunk_w), cmp_v_pos.reshape(2, chunk_w),
      cmp_k_w1.astype(mxu), row(cmp_k_b1), cmp_k_w2.astype(mxu),
      cmp_v_w1.astype(mxu), row(cmp_v_b1), jnp.tile(cmp_v_w2, (1, LANES // D)).astype(mxu),
      row(k_norm_cmp), cos_c, sin_c, perm)
    k_cmp = k_cmp.reshape(B, G, ncp, D)
    v_cmp = v_cmp.reshape(B, G, ncp, LANES)

    k_gain = jnp.max(jnp.abs(jnp.stack([k_norm_cmp, k_norm_slc, k_norm_win])))
    logit_bound = jnp.max(jnp.abs(q_norm_gain)) * k_gain * (1.02 * D * SCALE * LOG2E)
    logit_bound = logit_bound.astype(F32).reshape(1)
    kv_spec = lambda r, w: pl.BlockSpec((1, 1, r, w), lambda b, g, i: (b, g, 0, 0))
    gate_spec = pl.BlockSpec((1, NSA_QT, R * D), lambda b, g, i: (b, i, g))
    g3 = lambda a: a.reshape(B, S, NSA_WIDTH)
    o_nsa = pl.pallas_call(
        _nsa_kernel,
        grid=(B, G, S // NSA_QT),
        in_specs=[pl.BlockSpec(memory_space=pltpu.SMEM),
                  pl.BlockSpec((1, R, NSA_QT, LANES), lambda b, g, i: (b, g, i, 0)),
                  kv_spec(ncp, D), kv_spec(ncp, LANES), kv_spec(S, LANES), kv_spec(S, LANES),
                  kv_spec(S, D), kv_spec(S, LANES), _const_spec((LANES - D, ncp)),
                  gate_spec, gate_spec, gate_spec],
        out_specs=gate_spec,
        out_shape=sds((B, S, NSA_WIDTH), mxu),
        compiler_params=pltpu.CompilerParams(dimension_semantics=("parallel", "parallel", "arbitrary"),
                                             vmem_limit_bytes=VMEM_LIMIT),
        name="nsa",
    )(logit_bound, q_nsa, k_cmp, v_cmp, ks_aug, vs_aug, k_win, vw_aug, ovl, g3(g_cmp), g3(g_slc), g3(g_win))

    sb3 = lambda a: a.reshape(B, S, SB_WIDTH)
    pair_q = pl.BlockSpec((1, SB_ROWS, LANES), lambda b, hp, i: (b, i, hp))
    pair_kv = pl.BlockSpec((1, S, LANES), lambda b, hp, i: (b, 0, hp))
    o_sb = pl.pallas_call(
        _sb_kernel,
        grid=(B, SB_WIDTH // LANES, S // SB_ROWS),
        in_specs=[pair_q, pair_kv, pair_kv, _const_spec((2 * LANES, 2 * LANES)), pair_q],
        out_specs=pair_q,
        out_shape=sds((B, S, SB_WIDTH), mxu),
        compiler_params=pltpu.CompilerParams(dimension_semantics=("parallel", "parallel", "arbitrary"),
                                             vmem_limit_bytes=VMEM_LIMIT),
        name="stickbreak",
    )(sb3(q_sb), sb3(k_sb), sb3(v_sb), uu, sb3(g_sb))

    out = pl.pallas_call(
        _outproj_kernel,
        grid=(n_tok // tm,),
        in_specs=[tok_spec(DM), tok_spec(NSA_WIDTH), tok_spec(SB_WIDTH),
                  _const_spec((NSA_WIDTH + SB_WIDTH, DM))],
        out_specs=tok_spec(DM),
        out_shape=sds((n_tok, DM), x.dtype),
        compiler_params=pltpu.CompilerParams(dimension_semantics=("parallel",), vmem_limit_bytes=VMEM_LIMIT),
        name="outproj",
    )(x2, o_nsa.reshape(n_tok, NSA_WIDTH), o_sb.reshape(n_tok, SB_WIDTH), w_out.astype(mxu))
    return out.reshape(B, S, DM)
```

```python
import functools
import math

import numpy as np
import jax
import jax.numpy as jnp
from jax import lax
from jax.experimental import pallas as pl
from jax.experimental.pallas import tpu as pltpu

HEAD_DIM = 64
NSA_HEADS = 8
NSA_KV_HEADS = 2
NSA_GROUP = NSA_HEADS // NSA_KV_HEADS
SB_HEADS = 8
NSA_WIDTH = NSA_HEADS * HEAD_DIM
SB_WIDTH = SB_HEADS * HEAD_DIM
NSA_KV_WIDTH = NSA_KV_HEADS * HEAD_DIM
N_BRANCH = 3
CMP_LEN = 32
CMP_STRIDE = 16
SLC_LEN = 64
SLC_TOPN = 16
WINDOW = 512
Q_BLOCK = 128
ROPE_DIM = HEAD_DIM // 4
ROPE_HALF = ROPE_DIM // 2
ROPE_THETA = 500000.0
EPS = 1e-6
FORCE_BONUS = 1.0e4
SCALE = 1.0 / math.sqrt(HEAD_DIM)
LOG2E = math.log2(math.e)

LANES = 128
SUBLANES = 8
MASK_NEG = -1.0e30
M_INIT = -3.0e38
ROW_TILE = 512
NSA_QT = 256
SB_ROWS = 512
SB_UNROLL = 2
MAX_STATIC_SHIFT = 60.0
SB_UNDERFLOW_BITS = 160.0
VMEM_LIMIT = 56 * 1024 * 1024

_MXU_DTYPE = jnp.bfloat16
F32 = jnp.float32

C_Q = 0
C_KC = 512
C_VC = 640
C_KS = 768
C_VS = 896
C_KW = 1024
C_VW = 1152
C_GN = 1280
C_QSB = 1792
C_KSB = 2304
C_VSB = 2816
C_GSB = 3328
C_GL = 3840
N_COLS = 3968


def _nt_dot(a, b):
    return lax.dot_general(a, b, (((1,), (1,)), ((), ())), preferred_element_type=F32)


def _dot(a, b):
    return jnp.dot(a, b, preferred_element_type=F32)


def _split2(v):
    hi = v.astype(_MXU_DTYPE)
    lo = (v - hi.astype(F32)).astype(_MXU_DTYPE)
    return hi, lo


def _split3(v):
    hi = v.astype(_MXU_DTYPE)
    r1 = v - hi.astype(F32)
    mid = r1.astype(_MXU_DTYPE)
    lo = (r1 - mid.astype(F32)).astype(_MXU_DTYPE)
    return hi, mid, lo


def _inproj_kernel(x_ref, ng_ref, w_ref, qg_ref, kgs_ref, kgw_ref, bd_ref, cos_ref, sin_ref, eg_ref,
                   q_ref, kc_ref, vc_ref, ksa_ref, vsa_ref, kw_ref, vwa_ref,
                   gc_ref, gs_ref, gw_ref, qsb_ref, ksb_ref, vsb_ref, gsb_ref, *, n_sblk):
    tm = x_ref.shape[0]
    x = x_ref[...]
    ms = jnp.mean(x * x, axis=-1, keepdims=True)
    h = (x * lax.rsqrt(ms + EPS) * ng_ref[...]).astype(_MXU_DTYPE)

    def proj(lo, width):
        return _dot(h, w_ref[:, lo:lo + width])

    lane = lax.broadcasted_iota(jnp.int32, (tm, LANES), 1)
    low_half = lane < HEAD_DIM

    def head_norm_rope(y, gain):
        width = y.shape[1]
        rep = width // LANES
        hi, lo = _split2(y * y)
        bd = bd_ref[:width, :width]
        ssum = _dot(hi, bd) + _dot(lo, bd)
        yn = y * lax.rsqrt(ssum * (1.0 / HEAD_DIM) + EPS) * gain
        cos = jnp.concatenate([cos_ref[...]] * rep, axis=1) if rep > 1 else cos_ref[...]
        sin = jnp.concatenate([sin_ref[...]] * rep, axis=1) if rep > 1 else sin_ref[...]
        fwd = pltpu.roll(yn, ROPE_HALF, axis=1)
        bwd = pltpu.roll(yn, width - ROPE_HALF, axis=1)
        lane_w = lax.broadcasted_iota(jnp.int32, (tm, width), 1)
        partner = jnp.where((lane_w & (HEAD_DIM - 1)) < ROPE_HALF, bwd, fwd)
        return yn * cos + partner * sin

    def head_pair(slab, p):
        chunk = slab[:, p * LANES:(p + 1) * LANES]
        return chunk, pltpu.roll(chunk, HEAD_DIM, axis=1)

    qn = head_norm_rope(proj(C_Q, NSA_WIDTH), qg_ref[...]) * (SCALE * LOG2E)
    for p in range(NSA_HEADS // 2):
        ev, od = head_pair(qn, p)
        q_ref[0, 2 * p] = jnp.where(low_half, ev, 0.0).astype(q_ref.dtype)
        q_ref[0, 2 * p + 1] = jnp.where(low_half, od, 0.0).astype(q_ref.dtype)

    for ref, col in ((kc_ref, C_KC), (vc_ref, C_VC)):
        ev, od = head_pair(proj(col, NSA_KV_WIDTH), 0)
        ref[0, 0] = ev[:, :HEAD_DIM]
        ref[0, 1] = od[:, :HEAD_DIM]

    sblk = lax.rem(pl.program_id(0), n_sblk)
    row = lax.broadcasted_iota(jnp.int32, (tm, LANES), 0)
    key_blk = (sblk * tm + row) >> int(math.log2(SLC_LEN))
    onehot = jnp.where(lane - HEAD_DIM == key_blk, 1.0, 0.0)
    ks = head_norm_rope(proj(C_KS, NSA_KV_WIDTH), kgs_ref[...])
    ev, od = head_pair(ks, 0)
    ksa_ref[0, 0] = jnp.where(low_half, ev, onehot).astype(ksa_ref.dtype)
    ksa_ref[0, 1] = jnp.where(low_half, od, onehot).astype(ksa_ref.dtype)
    ev, od = head_pair(proj(C_VS, NSA_KV_WIDTH), 0)
    vsa_ref[0, 0] = jnp.where(low_half, ev, 1.0).astype(vsa_ref.dtype)
    vsa_ref[0, 1] = jnp.where(low_half, od, 1.0).astype(vsa_ref.dtype)

    kw = head_norm_rope(proj(C_KW, NSA_KV_WIDTH), kgw_ref[...])
    ev, od = head_pair(kw, 0)
    kw_ref[0, 0] = ev[:, :HEAD_DIM].astype(kw_ref.dtype)
    kw_ref[0, 1] = od[:, :HEAD_DIM].astype(kw_ref.dtype)
    ev, od = head_pair(proj(C_VW, NSA_KV_WIDTH), 0)
    vwa_ref[0, 0] = jnp.where(low_half, ev, 1.0).astype(vwa_ref.dtype)
    vwa_ref[0, 1] = jnp.where(low_half, od, 1.0).astype(vwa_ref.dtype)

    gn = proj(C_GN, NSA_WIDTH)
    silu_n = gn * jax.nn.sigmoid(gn)
    gl_hi, gl_lo = _split2(proj(C_GL, LANES))
    for br, ref in enumerate((gc_ref, gs_ref, gw_ref)):
        e = eg_ref[:, br * NSA_WIDTH:(br + 1) * NSA_WIDTH]
        ref[...] = jax.nn.sigmoid(_dot(gl_hi, e) + _dot(gl_lo, e)) * silu_n

    qsb_ref[...] = (proj(C_QSB, SB_WIDTH) * (SCALE * LOG2E)).astype(qsb_ref.dtype)
    ksb_ref[...] = proj(C_KSB, SB_WIDTH).astype(ksb_ref.dtype)
    vsb_ref[...] = proj(C_VSB, SB_WIDTH).astype(vsb_ref.dtype)
    gsb = proj(C_GSB, SB_WIDTH)
    gsb_ref[...] = gsb * jax.nn.sigmoid(gsb)


def _compress_kernel(kc_ref, vc_ref, posk_ref, posv_ref, w1k_ref, b1k_ref, w2k_ref,
                     w1v_ref, b1v_ref, w2v_ref, kg_ref, cos_ref, sin_ref, perm_ref,
                     kcmp_ref, vcmp_ref):
    half = CMP_STRIDE * HEAD_DIM

    def phi(c_ref, pos_ref, w1_ref, b1_ref, w2_ref):
        c = c_ref[0]
        n = c.shape[0]
        top = _dot((c + pos_ref[0:1, :]).astype(_MXU_DTYPE), w1_ref[:half, :])
        bot = _dot((c + pos_ref[1:2, :]).astype(_MXU_DTYPE), w1_ref[half:, :])
        hid = top + pltpu.roll(bot, n - 1, axis=0) + b1_ref[...]
        return _dot((hid * jax.nn.sigmoid(hid)).astype(_MXU_DTYPE), w2_ref[...])

    k = phi(kc_ref, posk_ref, w1k_ref, b1k_ref, w2k_ref)
    ms = jnp.mean(k * k, axis=-1, keepdims=True)
    kn = k * lax.rsqrt(ms + EPS) * kg_ref[...]
    hi, lo = _split2(kn)
    partner = _dot(hi, perm_ref[...]) + _dot(lo, perm_ref[...])
    kcmp_ref[0] = (kn * cos_ref[...] + partner * sin_ref[...]).astype(kcmp_ref.dtype)
    vcmp_ref[0] = phi(vc_ref, posv_ref, w1v_ref, b1v_ref, w2v_ref).astype(vcmp_ref.dtype)


def _nsa_kernel(bound_ref, q_ref, kc_ref, vc_ref, ksa_ref, vsa_ref, kw_ref, vwa_ref, ovl_ref,
                gc_ref, gs_ref, gw_ref, o_ref):
    i = pl.program_id(2)
    qt = q_ref.shape[2]
    rq = NSA_GROUP * qt
    n_blk_lanes = LANES - HEAD_DIM
    q_pad = q_ref[0].reshape(rq, LANES)
    q = q_pad[:, :HEAD_DIM]
    t_loc = lax.broadcasted_iota(jnp.int32, (rq, 1), 0) & (qt - 1)
    t_col = i * qt + t_loc

    rel = t_loc - lax.broadcasted_iota(jnp.int32, (rq, qt), 1)

    logit_bound = bound_ref[0]
    static_shift = logit_bound <= MAX_STATIC_SHIFT

    def row_shift(exact_row_max):
        return lax.cond(static_shift, lambda: jnp.full((rq, 1), logit_bound, F32), exact_row_max)

    def window_logits():
        s_parts, v_parts = [], []
        for back in range(WINDOW // qt + 1):
            start = pl.multiple_of(jnp.maximum(i - back, 0) * qt, qt)
            s = _nt_dot(q, kw_ref[0, 0, pl.ds(start, qt), :])
            if back == 0:
                keep = rel >= 0
            else:
                keep = rel < (WINDOW - back * qt) - jnp.where(i >= back, 0, WINDOW)
            s_parts.append(jnp.where(keep, s, MASK_NEG))
            v_parts.append(vwa_ref[0, 0, pl.ds(start, qt), :])
        return jnp.concatenate(s_parts, axis=1), jnp.concatenate(v_parts, axis=0)

    def cmp_logits():
        kc = kc_ref[0, 0]
        cmp_end = lax.broadcasted_iota(jnp.int32, (rq, kc.shape[0]), 1) * CMP_STRIDE + (CMP_LEN - 1)
        valid = cmp_end <= t_col
        return jnp.where(valid, _nt_dot(q, kc), MASK_NEG), valid

    m_w, m_c = lax.cond(
        static_shift,
        lambda: (jnp.full((rq, 1), logit_bound, F32),) * 2,
        lambda: (jnp.max(window_logits()[0], axis=-1, keepdims=True),
                 jnp.max(cmp_logits()[0], axis=-1, keepdims=True)))

    s_win, v_win = window_logits()
    acc_w = _dot(jnp.exp2(s_win - m_w).astype(_MXU_DTYPE), v_win)

    lg, valid = cmp_logits()
    p = jnp.where(valid, jnp.exp2(lg - m_c), 0.0)
    p = p / jnp.maximum(jnp.sum(p, axis=-1, keepdims=True), 1e-30)
    o_cmp = _dot(p.astype(_MXU_DTYPE), vc_ref[0, 0])

    p_sum = p[0:qt]
    for r in range(1, NSA_GROUP):
        p_sum = p_sum + p[r * qt:(r + 1) * qt]
    ovl = ovl_ref[...]
    p_slc = sum(_nt_dot(ovl, part) for part in _split3(p_sum))
    j_idx = lax.broadcasted_iota(jnp.int32, (n_blk_lanes, qt), 0)
    blk_t = (i * qt + lax.broadcasted_iota(jnp.int32, (n_blk_lanes, qt), 1)) >> int(math.log2(SLC_LEN))
    slc_valid = j_idx <= blk_t
    forced = (j_idx == 0) | (j_idx == blk_t) | (j_idx == blk_t - 1)
    score = jnp.where(slc_valid, p_slc + jnp.where(forced, FORCE_BONUS, 0.0), -jnp.inf)
    n_grp = n_blk_lanes // SUBLANES
    grp_rows = [score[g * SUBLANES:(g + 1) * SUBLANES] for g in range(n_grp)]
    grp_rank = [jnp.zeros((SUBLANES, qt), F32) for _ in range(n_grp)]
    row_in_grp = lax.broadcasted_iota(jnp.int32, (SUBLANES, qt), 0)
    for ii in range(n_blk_lanes):
        s_i = score[ii:ii + 1, :]
        for g in range(n_grp):
            rows = grp_rows[g]
            if g * SUBLANES > ii:
                beats = jnp.where(s_i >= rows, 1.0, 0.0)
            elif (g + 1) * SUBLANES - 1 <= ii:
                beats = jnp.where(s_i > rows, 1.0, 0.0)
            else:
                beats = jnp.where(row_in_grp > ii - g * SUBLANES,
                                  jnp.where(s_i >= rows, 1.0, 0.0), jnp.where(s_i > rows, 1.0, 0.0))
            grp_rank[g] = grp_rank[g] + beats
    rank = jnp.concatenate(grp_rank, axis=0)
    sel_t = jnp.where(slc_valid, jnp.where(rank < SLC_TOPN, 1.0, 0.0), 0.0)
    sel = jnp.concatenate([jnp.ones((HEAD_DIM, qt), F32), sel_t], axis=0).T
    lane_q = lax.broadcasted_iota(jnp.int32, (qt, LANES), 1)

    kw = 2 * qt
    n_kt = (i + 2) // 2
    rel_w = t_loc - lax.broadcasted_iota(jnp.int32, (rq, kw), 1)
    sel_rows = jnp.concatenate([sel] * NSA_GROUP, axis=0)
    lane_r = lax.broadcasted_iota(jnp.int32, (rq, LANES), 1)

    def augmented_q(shift):
        bias = jnp.where(lane_r >= HEAD_DIM, jnp.where(sel_rows > 0.5, -shift, MASK_NEG), 0.0)
        return q_pad + bias.astype(q_pad.dtype)

    def slc_logits(q_aug, kt):
        s = _nt_dot(q_aug, ksa_ref[0, 0, pl.ds(pl.multiple_of(kt * kw, kw), kw), :])
        return jnp.where(rel_w >= kt * kw - i * qt, s, MASK_NEG)

    def slc_row_max():
        q_aug = augmented_q(0.0)
        return lax.fori_loop(
            0, n_kt, lambda kt, m_run: jnp.maximum(m_run, jnp.max(slc_logits(q_aug, kt), axis=-1, keepdims=True)),
            jnp.full((rq, 1), M_INIT, F32))

    q_aug = augmented_q(row_shift(slc_row_max))

    def slc_weights(kt):
        return jnp.exp2(slc_logits(q_aug, kt)).astype(_MXU_DTYPE)

    def slc_accumulate(acc, p_kt, kt):
        return acc + _dot(p_kt, vsa_ref[0, 0, pl.ds(pl.multiple_of(kt * kw, kw), kw), :])

    def slc_body(kt, carry):
        p_cur, acc = carry
        return slc_weights(kt + 1), slc_accumulate(acc, p_cur, kt)

    p_last, acc_s = lax.fori_loop(0, n_kt - 1, slc_body, (slc_weights(0), jnp.zeros((rq, LANES), F32)))
    acc_s = slc_accumulate(acc_s, p_last, n_kt - 1)

    head = lambda a, r: a[r * qt:(r + 1) * qt]
    low_half = lane_q < HEAD_DIM

    def token_major(o):
        return jnp.concatenate([jnp.where(low_half, head(o, r), head(o, r + 1))
                                for r in range(0, NSA_GROUP, 2)], axis=1)

    def token_major_normalised(acc):
        inv = 1.0 / jnp.where(lane_r >= HEAD_DIM, acc, 1.0)
        pairs = []
        for r in range(0, NSA_GROUP, 2):
            even = head(acc, r) * pltpu.roll(head(inv, r), HEAD_DIM, axis=1)
            odd = pltpu.roll(head(acc, r + 1), HEAD_DIM, axis=1) * head(inv, r + 1)
            pairs.append(jnp.where(low_half, even, odd))
        return jnp.concatenate(pairs, axis=1)

    out = (gc_ref[0] * token_major(o_cmp) + gs_ref[0] * token_major_normalised(acc_s)
           + gw_ref[0] * token_major_normalised(acc_w))
    o_ref[0] = out.astype(o_ref.dtype)


def _sb_kernel(q_ref, k_ref, v_ref, uu_ref, g_ref, o_ref):
    i = pl.program_id(2)
    rows = q_ref.shape[1]
    band = rows // Q_BLOCK
    n_tiles = (i + 1) * band
    low_half = lax.broadcasted_iota(jnp.int32, (rows, LANES), 1) < HEAD_DIM
    low_half_k = lax.broadcasted_iota(jnp.int32, (Q_BLOCK, LANES), 1) < HEAD_DIM
    q_pair = q_ref[0]
    zero = jnp.zeros_like(q_pair)
    q_heads = (jnp.where(low_half, q_pair, zero), jnp.where(low_half, zero, q_pair))

    tri = (lax.broadcasted_iota(jnp.int32, (Q_BLOCK, LANES), 1)
           < lax.broadcasted_iota(jnp.int32, (Q_BLOCK, LANES), 0))

    def update_rows(x, r0, fn):
        return fn(x) if r0 == 0 else jnp.concatenate([x[:r0], fn(x[r0:])], axis=0)

    def on_diagonal(x, fill):
        masked = jnp.where(tri, x[:Q_BLOCK], fill)
        return masked if x.shape[0] == Q_BLOCK else jnp.concatenate([masked, x[Q_BLOCK:]], axis=0)

    def tile_step(start, carry, r0=0, diagonal=False):
        acc, laters = carry[0], carry[1:]
        k_pair = k_ref[0, pl.ds(start, Q_BLOCK), :]
        v_pair = v_ref[0, pl.ds(start, Q_BLOCK), :]
        zero_v = jnp.zeros_like(v_pair)
        v_bd = jnp.concatenate([jnp.where(low_half_k, v_pair, zero_v),
                                jnp.where(low_half_k, zero_v, v_pair)], axis=0)
        weights, new_laters = [], []
        for q_h, later in zip(q_heads, laters):
            z = _nt_dot(q_h[r0:] if r0 else q_h, k_pair)
            sp = jnp.maximum(z, 0.0) + jnp.log2(1.0 + jnp.exp2(-jnp.abs(z)))
            if diagonal:
                sp = on_diagonal(sp, 0.0)
            hi, lo = _split2(sp)
            r = _dot(jnp.concatenate([hi, lo], axis=1), uu_ref[...])
            after = r[:, :LANES] + (later[r0:] if r0 else later)
            a = jnp.exp2(z - sp - after)
            if diagonal:
                a = on_diagonal(a, 0.0)
            weights.append(a.astype(_MXU_DTYPE))
            new_laters.append(update_rows(later, r0, lambda part, r=r: part + r[:, LANES:]))
        pv = _dot(jnp.concatenate(weights, axis=1), v_bd)
        return (update_rows(acc, r0, lambda part: part + pv), *new_laters)

    zeros = jnp.zeros((rows, LANES), F32)
    carry = (zeros, zeros, zeros)
    for c in reversed(range(band)):
        carry = tile_step(pl.multiple_of(i * rows + c * Q_BLOCK, Q_BLOCK), carry, r0=c * Q_BLOCK, diagonal=True)

    def main_body(kg, carry):
        for u in range(SB_UNROLL):
            first_key = (n_tiles - band - 1 - (kg * SB_UNROLL + u)) * Q_BLOCK
            carry = tile_step(pl.multiple_of(first_key, Q_BLOCK), carry)
        return carry

    n_trips = (n_tiles - band) // SB_UNROLL

    def settled(laters):
        return jnp.min(jnp.minimum(*laters)) >= SB_UNDERFLOW_BITS

    def keep_going(state):
        kg, done, _ = state
        return jnp.logical_and(kg < n_trips, jnp.logical_not(done))

    def main_step(state):
        kg, _, carry = state
        carry = main_body(kg, carry)
        return kg + 1, settled(carry[1:]), carry

    _, _, carry = lax.while_loop(keep_going, main_step, (0, settled(carry[1:]), carry))
    o_ref[0] = (carry[0] * g_ref[0]).astype(o_ref.dtype)


def _outproj_kernel(x_ref, on_ref, os_ref, w_ref, o_ref):
    o_ref[...] = (x_ref[...] + _dot(on_ref[...], w_ref[:NSA_WIDTH, :])
                  + _dot(os_ref[...], w_ref[NSA_WIDTH:, :]))


def _rope_tables(pos, reps):
    inv_freq = jnp.power(ROPE_THETA, -jnp.arange(0, ROPE_DIM, 2, dtype=F32) / ROPE_DIM)
    ang = pos.astype(F32)[:, None] * inv_freq[None, :]
    cos, sin = jnp.cos(ang), jnp.sin(ang)
    n = pos.shape[0]
    rest = HEAD_DIM - ROPE_DIM
    cos_h = jnp.concatenate([cos, cos, jnp.ones((n, rest), F32)], axis=1)
    sin_h = jnp.concatenate([-sin, sin, jnp.zeros((n, rest), F32)], axis=1)
    return jnp.tile(cos_h, (1, reps)), jnp.tile(sin_h, (1, reps))


def _const_spec(shape):
    return pl.BlockSpec(shape, lambda *_: (0,) * len(shape))


def kernel(x, norm_gain, w_in, q_norm_gain, k_norm_cmp, k_norm_slc, k_norm_win,
           cmp_k_pos, cmp_k_w1, cmp_k_b1, cmp_k_w2, cmp_v_pos, cmp_v_w1, cmp_v_b1, cmp_v_w2, w_out):
    B, S, DM = x.shape
    D, G, R = HEAD_DIM, NSA_KV_HEADS, NSA_GROUP
    mxu = _MXU_DTYPE
    n_tok = B * S
    tm = ROW_TILE
    n_sblk = S // tm
    nq = S // Q_BLOCK
    ncp = S // CMP_STRIDE
    n_slc = S // SLC_LEN
    n_cmp = (S - CMP_LEN) // CMP_STRIDE + 1
    assert S % tm == 0 and ncp % LANES == 0 and n_slc <= LANES - D and n_slc >= SLC_TOPN

    n_gl = NSA_HEADS * N_BRANCH
    gl0 = C_GN + n_gl
    w_cat = jnp.concatenate([w_in[:, :C_GN], w_in[:, gl0:], w_in[:, C_GN:gl0],
                             jnp.zeros((DM, LANES - n_gl), w_in.dtype)], axis=1).astype(mxu)
    assert w_cat.shape[1] == N_COLS
    pos = jnp.arange(S, dtype=jnp.int32)
    cos_t, sin_t = _rope_tables(pos, LANES // D)
    cmp_end = jnp.arange(ncp, dtype=jnp.int32) * CMP_STRIDE + (CMP_LEN - 1)
    cos_c, sin_c = _rope_tables(cmp_end, 1)
    lane_i = np.arange(NSA_WIDTH)
    bd = jnp.asarray(lane_i[:, None] // D == lane_i[None, :] // D, mxu)
    eg = np.zeros((LANES, N_BRANCH * NSA_WIDTH), np.float32)
    for hh in range(NSA_HEADS):
        for br in range(N_BRANCH):
            eg[hh * N_BRANCH + br, br * NSA_WIDTH + hh * D:br * NSA_WIDTH + (hh + 1) * D] = 1.0
    eg = jnp.asarray(eg, mxu)
    perm = np.zeros((D, D), np.float32)
    for c in range(ROPE_HALF):
        perm[c + ROPE_HALF, c] = 1.0
        perm[c, c + ROPE_HALF] = 1.0
    perm = jnp.asarray(perm, mxu)
    cs = np.arange(ncp) * CMP_STRIDE
    ss = np.arange(LANES - D) * SLC_LEN
    ovl = np.clip(np.minimum(cs[None, :] + CMP_LEN, ss[:, None] + SLC_LEN)
                  - np.maximum(cs[None, :], ss[:, None]), 0, None).astype(np.float32) / CMP_LEN
    ovl[:, n_cmp:] = 0.0
    ovl[n_slc:, :] = 0.0
    ovl = jnp.asarray(ovl, mxu)
    sidx = np.arange(LANES)
    tri = (sidx[:, None] > sidx[None, :]).astype(np.float32)
    uu_half = np.concatenate([tri, np.ones((LANES, LANES), np.float32)], axis=1)
    uu = jnp.asarray(np.concatenate([uu_half, uu_half], axis=0), mxu)

    row = lambda v: v.reshape(1, -1).astype(F32)
    x2 = x.reshape(n_tok, DM)

    tok_spec = lambda w: pl.BlockSpec((tm, w), lambda t: (t, 0))
    head_spec = lambda nh, w: pl.BlockSpec((1, nh, tm, w), lambda t: (t // n_sblk, 0, t % n_sblk, 0))
    tab_spec = pl.BlockSpec((tm, LANES), lambda t: (t % n_sblk, 0))
    sds = jax.ShapeDtypeStruct
    outs = pl.pallas_call(
        functools.partial(_inproj_kernel, n_sblk=n_sblk),
        grid=(n_tok // tm,),
        in_specs=[tok_spec(DM), _const_spec((1, DM)), _const_spec((DM, N_COLS)),
                  _const_spec((1, NSA_WIDTH)), _const_spec((1, NSA_KV_WIDTH)), _const_spec((1, NSA_KV_WIDTH)),
                  _const_spec((NSA_WIDTH, NSA_WIDTH)), tab_spec, tab_spec,
                  _const_spec((LANES, N_BRANCH * NSA_WIDTH))],
        out_specs=[head_spec(NSA_HEADS, LANES), head_spec(G, D), head_spec(G, D),
                   head_spec(G, LANES), head_spec(G, LANES), head_spec(G, D), head_spec(G, LANES),
                   tok_spec(NSA_WIDTH), tok_spec(NSA_WIDTH), tok_spec(NSA_WIDTH),
                   tok_spec(SB_WIDTH), tok_spec(SB_WIDTH), tok_spec(SB_WIDTH), tok_spec(SB_WIDTH)],
        out_shape=[sds((B, NSA_HEADS, S, LANES), mxu), sds((B, G, S, D), F32), sds((B, G, S, D), F32),
                   sds((B, G, S, LANES), mxu), sds((B, G, S, LANES), mxu), sds((B, G, S, D), mxu),
                   sds((B, G, S, LANES), mxu),
                   sds((n_tok, NSA_WIDTH), F32), sds((n_tok, NSA_WIDTH), F32), sds((n_tok, NSA_WIDTH), F32),
                   sds((n_tok, SB_WIDTH), mxu), sds((n_tok, SB_WIDTH), mxu), sds((n_tok, SB_WIDTH), mxu),
                   sds((n_tok, SB_WIDTH), F32)],
        compiler_params=pltpu.CompilerParams(dimension_semantics=("parallel",), vmem_limit_bytes=VMEM_LIMIT),
        name="inproj",
    )(x2, row(norm_gain), w_cat, row(jnp.tile(q_norm_gain, NSA_HEADS)),
      row(jnp.tile(k_norm_slc, G)), row(jnp.tile(k_norm_win, G)), bd, cos_t, sin_t, eg)
    (q_nsa, kc_raw, vc_raw, ks_aug, vs_aug, k_win, vw_aug, g_cmp, g_slc, g_win,
     q_sb, k_sb, v_sb, g_sb) = outs

    chunk_w = CMP_STRIDE * D
    chunks = lambda a: a.reshape(B * G, ncp, chunk_w)
    bg_spec = lambda r, w: pl.BlockSpec((1, r, w), lambda t: (t, 0, 0))
    k_cmp, v_cmp = pl.pallas_call(
        _compress_kernel,
        grid=(B * G,),
        in_specs=[bg_spec(ncp, chunk_w), bg_spec(ncp, chunk_w),
                  _const_spec((2, chunk_w)), _const_spec((2, chunk_w)),
                  _const_spec((CMP_LEN * D, D)), _const_spec((1, D)), _const_spec((D, D)),
                  _const_spec((CMP_LEN * D, D)), _const_spec((1, D)), _const_spec((D, LANES)),
                  _const_spec((1, D)), _const_spec((ncp, D)), _const_spec((ncp, D)), _const_spec((D, D))],
        out_specs=[bg_spec(ncp, D), bg_spec(ncp, LANES)],
        out_shape=[sds((B * G, ncp, D), mxu), sds((B * G, ncp, LANES), mxu)],
        compiler_params=pltpu.CompilerParams(dimension_semantics=("parallel",), vmem_limit_bytes=VMEM_LIMIT),
        name="compress",
    )(chunks(kc_raw), chunks(vc_raw), cmp_k_pos.reshape(2, chunk_w), cmp_v_pos.reshape(2, chunk_w),
      cmp_k_w1.astype(mxu), row(cmp_k_b1), cmp_k_w2.astype(mxu),
      cmp_v_w1.astype(mxu), row(cmp_v_b1), jnp.tile(cmp_v_w2, (1, LANES // D)).astype(mxu),
      row(k_norm_cmp), cos_c, sin_c, perm)
    k_cmp = k_cmp.reshape(B, G, ncp, D)
    v_cmp = v_cmp.reshape(B, G, ncp, LANES)

    k_gain = jnp.max(jnp.abs(jnp.stack([k_norm_cmp, k_norm_slc, k_norm_win])))
    logit_bound = jnp.max(jnp.abs(q_norm_gain)) * k_gain * (1.02 * D * SCALE * LOG2E)
    logit_bound = logit_bound.astype(F32).reshape(1)
    kv_spec = lambda r, w: pl.BlockSpec((1, 1, r, w), lambda b, g, i: (b, g, 0, 0))
    gate_spec = pl.BlockSpec((1, NSA_QT, R * D), lambda b, g, i: (b, i, g))
    g3 = lambda a: a.reshape(B, S, NSA_WIDTH)
    o_nsa = pl.pallas_call(
        _nsa_kernel,
        grid=(B, G, S // NSA_QT),
        in_specs=[pl.BlockSpec(memory_space=pltpu.SMEM),
                  pl.BlockSpec((1, R, NSA_QT, LANES), lambda b, g, i: (b, g, i, 0)),
                  kv_spec(ncp, D), kv_spec(ncp, LANES), kv_spec(S, LANES), kv_spec(S, LANES),
                  kv_spec(S, D), kv_spec(S, LANES), _const_spec((LANES - D, ncp)),
                  gate_spec, gate_spec, gate_spec],
        out_specs=gate_spec,
        out_shape=sds((B, S, NSA_WIDTH), mxu),
        compiler_params=pltpu.CompilerParams(dimension_semantics=("parallel", "parallel", "arbitrary"),
                                             vmem_limit_bytes=VMEM_LIMIT),
        name="nsa",
    )(logit_bound, q_nsa, k_cmp, v_cmp, ks_aug, vs_aug, k_win, vw_aug, ovl, g3(g_cmp), g3(g_slc), g3(g_win))

    sb3 = lambda a: a.reshape(B, S, SB_WIDTH)
    pair_q = pl.BlockSpec((1, SB_ROWS, LANES), lambda b, hp, i: (b, i, hp))
    pair_kv = pl.BlockSpec((1, S, LANES), lambda b, hp, i: (b, 0, hp))
    o_sb = pl.pallas_call(
        _sb_kernel,
        grid=(B, SB_WIDTH // LANES, S // SB_ROWS),
        in_specs=[pair_q, pair_kv, pair_kv, _const_spec((2 * LANES, 2 * LANES)), pair_q],
        out_specs=pair_q,
        out_shape=sds((B, S, SB_WIDTH), mxu),
        compiler_params=pltpu.CompilerParams(dimension_semantics=("parallel", "parallel", "arbitrary"),
                                             vmem_limit_bytes=VMEM_LIMIT),
        name="stickbreak",
    )(sb3(q_sb), sb3(k_sb), sb3(v_sb), uu, sb3(g_sb))

    out = pl.pallas_call(
        _outproj_kernel,
        grid=(n_tok // tm,),
        in_specs=[tok_spec(DM), tok_spec(NSA_WIDTH), tok_spec(SB_WIDTH),
                  _const_spec((NSA_WIDTH + SB_WIDTH, DM))],
        out_specs=tok_spec(DM),
        out_shape=sds((n_tok, DM), x.dtype),
        compiler_params=pltpu.CompilerParams(dimension_semantics=("parallel",), vmem_limit_bytes=VMEM_LIMIT),
        name="outproj",
    )(x2, o_nsa.reshape(n_tok, NSA_WIDTH), o_sb.reshape(n_tok, SB_WIDTH), w_out.astype(mxu))
    return out.reshape(B, S, DM)
```

```python
import functools
import math

import numpy as np
import jax
import jax.numpy as jnp
from jax import lax
from jax.experimental import pallas as pl
from jax.experimental.pallas import tpu as pltpu

HEAD_DIM = 64
NSA_HEADS = 8
NSA_KV_HEADS = 2
NSA_GROUP = NSA_HEADS // NSA_KV_HEADS
SB_HEADS = 8
NSA_WIDTH = NSA_HEADS * HEAD_DIM
SB_WIDTH = SB_HEADS * HEAD_DIM
NSA_KV_WIDTH = NSA_KV_HEADS * HEAD_DIM
N_BRANCH = 3
CMP_LEN = 32
CMP_STRIDE = 16
SLC_LEN = 64
SLC_TOPN = 16
WINDOW = 512
Q_BLOCK = 128
ROPE_DIM = HEAD_DIM // 4
ROPE_HALF = ROPE_DIM // 2
ROPE_THETA = 500000.0
EPS = 1e-6
FORCE_BONUS = 1.0e4
SCALE = 1.0 / math.sqrt(HEAD_DIM)
LOG2E = math.log2(math.e)

LANES = 128
SUBLANES = 8
MASK_NEG = -1.0e30
M_INIT = -3.0e38
ROW_TILE = 512
NSA_QT = 256
SB_ROWS = 512
SB_UNROLL = 2
MAX_STATIC_SHIFT = 60.0
SB_UNDERFLOW_BITS = 160.0
VMEM_LIMIT = 56 * 1024 * 1024

_MXU_DTYPE = jnp.bfloat16
F32 = jnp.float32

C_Q = 0
C_KC = 512
C_KS = 768
C_KW = 1024
C_GN = 1280
C_QSB = 1792
C_KSB = 2304
C_VSB = 2816
C_GSB = 3328
C_GL = 3840
N_COLS = 3968


def _nt_dot(a, b):
    return lax.dot_general(a, b, (((1,), (1,)), ((), ())), preferred_element_type=F32)


def _dot(a, b):
    return jnp.dot(a, b, preferred_element_type=F32)


def _split2(v):
    hi = v.astype(_MXU_DTYPE)
    lo = (v - hi.astype(F32)).astype(_MXU_DTYPE)
    return hi, lo


def _split3(v):
    hi = v.astype(_MXU_DTYPE)
    r1 = v - hi.astype(F32)
    mid = r1.astype(_MXU_DTYPE)
    lo = (r1 - mid.astype(F32)).astype(_MXU_DTYPE)
    return hi, mid, lo


def _inproj_kernel(x_ref, ng_ref, w_ref, qg_ref, kg_ref, bd_ref, cos_ref, sin_ref, eg_ref,
                   q_ref, kc_ref, vc_ref, ksa_ref, vsa_ref, kw_ref, vwa_ref,
                   gc_ref, gs_ref, gw_ref, qsb_ref, ksb_ref, vsb_ref, gsb_ref, *, n_sblk):
    tm = x_ref.shape[0]
    x = x_ref[...]
    ms = jnp.mean(x * x, axis=-1, keepdims=True)
    h = (x * lax.rsqrt(ms + EPS) * ng_ref[...]).astype(_MXU_DTYPE)

    def proj(lo, width):
        return _dot(h, w_ref[:, lo:lo + width])

    lane = lax.broadcasted_iota(jnp.int32, (tm, LANES), 1)
    low_half = lane < HEAD_DIM

    def head_norm_rope(y, gain):
        width = y.shape[1]
        rep = width // LANES
        hi, lo = _split2(y * y)
        ssum = jnp.concatenate(
            [_dot(jnp.concatenate([hi[:, c:c + LANES], lo[:, c:c + LANES]], axis=1), bd_ref[...])
             for c in range(0, width, LANES)], axis=1)
        yn = y * lax.rsqrt(ssum * (1.0 / HEAD_DIM) + EPS) * gain
        cos = jnp.concatenate([cos_ref[...]] * rep, axis=1) if rep > 1 else cos_ref[...]
        sin = jnp.concatenate([sin_ref[...]] * rep, axis=1) if rep > 1 else sin_ref[...]
        fwd = pltpu.roll(yn, ROPE_HALF, axis=1)
        bwd = pltpu.roll(yn, width - ROPE_HALF, axis=1)
        lane_w = lax.broadcasted_iota(jnp.int32, (tm, width), 1)
        partner = jnp.where((lane_w & (HEAD_DIM - 1)) < ROPE_HALF, bwd, fwd)
        return yn * cos + partner * sin

    def head_pair(slab, p):
        chunk = slab[:, p * LANES:(p + 1) * LANES]
        return chunk, pltpu.roll(chunk, HEAD_DIM, axis=1)

    qn = head_norm_rope(proj(C_Q, NSA_WIDTH), qg_ref[...]) * (SCALE * LOG2E)
    for p in range(NSA_HEADS // 2):
        ev, od = head_pair(qn, p)
        q_ref[0, 2 * p] = jnp.where(low_half, ev, 0.0).astype(q_ref.dtype)
        q_ref[0, 2 * p + 1] = jnp.where(low_half, od, 0.0).astype(q_ref.dtype)

    kv_cmp = proj(C_KC, 2 * NSA_KV_WIDTH)
    kv_slc = proj(C_KS, 2 * NSA_KV_WIDTH)
    kv_win = proj(C_KW, 2 * NSA_KV_WIDTH)

    for ref, p in ((kc_ref, 0), (vc_ref, 1)):
        ev, od = head_pair(kv_cmp, p)
        ref[0, 0] = ev[:, :HEAD_DIM]
        ref[0, 1] = od[:, :HEAD_DIM]

    sblk = lax.rem(pl.program_id(0), n_sblk)
    row = lax.broadcasted_iota(jnp.int32, (tm, LANES), 0)
    key_blk = (sblk * tm + row) >> int(math.log2(SLC_LEN))
    onehot = jnp.where(lane - HEAD_DIM == key_blk, 1.0, 0.0)
    k_sw = head_norm_rope(jnp.concatenate([kv_slc[:, :NSA_KV_WIDTH], kv_win[:, :NSA_KV_WIDTH]], axis=1),
                          kg_ref[...])
    ev, od = head_pair(k_sw, 0)
    ksa_ref[0, 0] = jnp.where(low_half, ev, onehot).astype(ksa_ref.dtype)
    ksa_ref[0, 1] = jnp.where(low_half, od, onehot).astype(ksa_ref.dtype)
    ev, od = head_pair(kv_slc, 1)
    vsa_ref[0, 0] = jnp.where(low_half, ev, 1.0).astype(vsa_ref.dtype)
    vsa_ref[0, 1] = jnp.where(low_half, od, 1.0).astype(vsa_ref.dtype)

    ev, od = head_pair(k_sw, 1)
    kw_ref[0, 0] = ev[:, :HEAD_DIM].astype(kw_ref.dtype)
    kw_ref[0, 1] = od[:, :HEAD_DIM].astype(kw_ref.dtype)
    ev, od = head_pair(kv_win, 1)
    vwa_ref[0, 0] = jnp.where(low_half, ev, 1.0).astype(vwa_ref.dtype)
    vwa_ref[0, 1] = jnp.where(low_half, od, 1.0).astype(vwa_ref.dtype)

    gn = proj(C_GN, NSA_WIDTH)
    silu_n = gn * jax.nn.sigmoid(gn)
    gl_split = jnp.concatenate(_split2(proj(C_GL, LANES)), axis=1)
    for br, ref in enumerate((gc_ref, gs_ref, gw_ref)):
        ref[...] = jax.nn.sigmoid(_dot(gl_split, eg_ref[:, br * NSA_WIDTH:(br + 1) * NSA_WIDTH])) * silu_n

    qsb_ref[...] = (proj(C_QSB, SB_WIDTH) * (SCALE * LOG2E)).astype(qsb_ref.dtype)
    ksb_ref[...] = proj(C_KSB, SB_WIDTH).astype(ksb_ref.dtype)
    vsb_ref[...] = proj(C_VSB, SB_WIDTH).astype(vsb_ref.dtype)
    gsb = proj(C_GSB, SB_WIDTH)
    gsb_ref[...] = gsb * jax.nn.sigmoid(gsb)


def _compress_kernel(kc_ref, vc_ref, posk_ref, posv_ref, w1k_ref, b1k_ref, w2k_ref,
                     w1v_ref, b1v_ref, w2v_ref, kg_ref, cos_ref, sin_ref, perm_ref,
                     kcmp_ref, vcmp_ref):
    half = CMP_STRIDE * HEAD_DIM

    def phi(c_ref, pos_ref, w1_ref, b1_ref, w2_ref):
        c = c_ref[0]
        n = c.shape[0]
        top = _dot((c + pos_ref[0:1, :]).astype(_MXU_DTYPE), w1_ref[:half, :])
        bot = _dot((c + pos_ref[1:2, :]).astype(_MXU_DTYPE), w1_ref[half:, :])
        hid = top + pltpu.roll(bot, n - 1, axis=0) + b1_ref[...]
        return _dot((hid * jax.nn.sigmoid(hid)).astype(_MXU_DTYPE), w2_ref[...])

    k = phi(kc_ref, posk_ref, w1k_ref, b1k_ref, w2k_ref)
    ms = jnp.mean(k * k, axis=-1, keepdims=True)
    kn = k * lax.rsqrt(ms + EPS) * kg_ref[...]
    hi, lo = _split2(kn)
    partner = _dot(hi, perm_ref[...]) + _dot(lo, perm_ref[...])
    kcmp_ref[0] = (kn * cos_ref[...] + partner * sin_ref[...]).astype(kcmp_ref.dtype)
    vcmp_ref[0] = phi(vc_ref, posv_ref, w1v_ref, b1v_ref, w2v_ref).astype(vcmp_ref.dtype)


def _nsa_kernel(bound_ref, q_ref, kc_ref, vc_ref, ksa_ref, vsa_ref, kw_ref, vwa_ref, ovl_ref,
                gc_ref, gs_ref, gw_ref, o_ref):
    i = pl.program_id(1)
    groups = range(kc_ref.shape[1])
    qt = q_ref.shape[2]
    kw = 2 * qt
    rq = NSA_GROUP * qt
    n_blk_lanes = LANES - HEAD_DIM
    q_pads = [q_ref[0, g * NSA_GROUP:(g + 1) * NSA_GROUP].reshape(rq, LANES) for g in groups]
    heads = lambda x: jnp.concatenate([x] * NSA_GROUP, axis=0)
    t_tok = lax.broadcasted_iota(jnp.int32, (qt, 1), 0)
    t_col = i * qt + heads(t_tok)
    rel_w = heads(t_tok - lax.broadcasted_iota(jnp.int32, (qt, kw), 1))
    rel = rel_w[:, :qt]
    lane_q = lax.broadcasted_iota(jnp.int32, (qt, LANES), 1)
    lane_r = heads(lane_q)
    low_half = lane_q < HEAD_DIM

    logit_bound = bound_ref[0]
    static_shift = logit_bound <= MAX_STATIC_SHIFT
    bound_rows = lambda: jnp.full((rq, 1), logit_bound, F32)
    row_max = lambda s: jnp.max(s, axis=-1, keepdims=True)

    def window_logits(g):
        q = q_pads[g][:, :HEAD_DIM]
        s_parts, v_parts = [], []
        for back in range(WINDOW // qt + 1):
            start = pl.multiple_of(jnp.maximum(i - back, 0) * qt, qt)
            s = _nt_dot(q, kw_ref[0, g, pl.ds(start, qt), :])
            if back == 0:
                keep = rel >= 0
            else:
                keep = rel < (WINDOW - back * qt) - jnp.where(i >= back, 0, WINDOW)
            s_parts.append(jnp.where(keep, s, MASK_NEG))
            v_parts.append(vwa_ref[0, g, pl.ds(start, qt), :])
        return jnp.concatenate(s_parts, axis=1), jnp.concatenate(v_parts, axis=0)

    def cmp_logits(g):
        kc = kc_ref[0, g]
        cmp_end = lax.broadcasted_iota(jnp.int32, (rq, kc.shape[0]), 1) * CMP_STRIDE + (CMP_LEN - 1)
        valid = cmp_end <= t_col
        return jnp.where(valid, _nt_dot(q_pads[g][:, :HEAD_DIM], kc), MASK_NEG), valid

    pre_shifts = lax.cond(
        static_shift,
        lambda: tuple(bound_rows() for _ in groups for _ in range(2)),
        lambda: tuple(m for g in groups for m in (row_max(window_logits(g)[0]), row_max(cmp_logits(g)[0]))))

    ovl = ovl_ref[...]
    j_idx = lax.broadcasted_iota(jnp.int32, (n_blk_lanes, qt), 0)
    blk_t = (i * qt + lax.broadcasted_iota(jnp.int32, (n_blk_lanes, qt), 1)) >> int(math.log2(SLC_LEN))
    slc_valid = j_idx <= blk_t
    forced = (j_idx == 0) | (j_idx == blk_t) | (j_idx == blk_t - 1)
    row_in_grp = lax.broadcasted_iota(jnp.int32, (SUBLANES, qt), 0)

    def selected_blocks(p):
        p_sum = p[0:qt]
        for r in range(1, NSA_GROUP):
            p_sum = p_sum + p[r * qt:(r + 1) * qt]
        p_slc = sum(_nt_dot(ovl, part) for part in _split3(p_sum))
        score = jnp.where(slc_valid, p_slc + jnp.where(forced, FORCE_BONUS, 0.0), -jnp.inf)
        n_grp = n_blk_lanes // SUBLANES
        grp_rows = [score[c * SUBLANES:(c + 1) * SUBLANES] for c in range(n_grp)]
        grp_rank = [jnp.zeros((SUBLANES, qt), F32) for _ in range(n_grp)]
        for ii in range(n_blk_lanes):
            s_i = score[ii:ii + 1, :]
            for c in range(n_grp):
                rows = grp_rows[c]
                if c * SUBLANES > ii:
                    beats = jnp.where(s_i >= rows, 1.0, 0.0)
                elif (c + 1) * SUBLANES - 1 <= ii:
                    beats = jnp.where(s_i > rows, 1.0, 0.0)
                else:
                    beats = jnp.where(row_in_grp > ii - c * SUBLANES,
                                      jnp.where(s_i >= rows, 1.0, 0.0), jnp.where(s_i > rows, 1.0, 0.0))
                grp_rank[c] = grp_rank[c] + beats
        rank = jnp.concatenate(grp_rank, axis=0)
        sel_t = jnp.where(slc_valid, jnp.where(rank < SLC_TOPN, 1.0, 0.0), 0.0)
        return heads(jnp.concatenate([jnp.ones((HEAD_DIM, qt), F32), sel_t], axis=0).T)

    acc_w, o_cmp, sel_rows = [], [], []
    for g in groups:
        m_w, m_c = pre_shifts[2 * g], pre_shifts[2 * g + 1]
        s_win, v_win = window_logits(g)
        acc_w.append(_dot(jnp.exp2(s_win - m_w).astype(_MXU_DTYPE), v_win))
        lg, valid = cmp_logits(g)
        p = jnp.where(valid, jnp.exp2(lg - m_c), 0.0)
        p = p / jnp.maximum(jnp.sum(p, axis=-1, keepdims=True), 1e-30)
        o_cmp.append(_dot(p.astype(_MXU_DTYPE), vc_ref[0, g]))
        sel_rows.append(selected_blocks(p))

    n_kt = (i + 2) // 2

    def augmented_q(g, shift):
        bias = jnp.where(lane_r >= HEAD_DIM, jnp.where(sel_rows[g] > 0.5, -shift, MASK_NEG), 0.0)
        return q_pads[g] + bias.astype(q_pads[g].dtype)

    def slc_logits(g, q_aug, kt):
        s = _nt_dot(q_aug, ksa_ref[0, g, pl.ds(pl.multiple_of(kt * kw, kw), kw), :])
        return jnp.where(rel_w >= kt * kw - i * qt, s, MASK_NEG)

    def slc_row_max(g):
        q_aug = augmented_q(g, 0.0)
        return lax.fori_loop(0, n_kt, lambda kt, m_run: jnp.maximum(m_run, row_max(slc_logits(g, q_aug, kt))),
                             jnp.full((rq, 1), M_INIT, F32))

    slc_shifts = lax.cond(static_shift, lambda: tuple(bound_rows() for _ in groups),
                          lambda: tuple(slc_row_max(g) for g in groups))
    q_augs = [augmented_q(g, slc_shifts[g]) for g in groups]

    def slc_weights(g, kt):
        return jnp.exp2(slc_logits(g, q_augs[g], kt)).astype(_MXU_DTYPE)

    def slc_accumulate(g, acc, p_kt, kt):
        return acc + _dot(p_kt, vsa_ref[0, g, pl.ds(pl.multiple_of(kt * kw, kw), kw), :])

    def slc_body(kt, carry):
        return tuple((slc_weights(g, kt + 1), slc_accumulate(g, carry[g][1], carry[g][0], kt)) for g in groups)

    carry = lax.fori_loop(0, n_kt - 1, slc_body,
                          tuple((slc_weights(g, 0), jnp.zeros((rq, LANES), F32)) for g in groups))
    acc_s = [slc_accumulate(g, carry[g][1], carry[g][0], n_kt - 1) for g in groups]

    head = lambda a, r: a[r * qt:(r + 1) * qt]

    def token_major(o):
        return jnp.concatenate([jnp.where(low_half, head(o, r), head(o, r + 1))
                                for r in range(0, NSA_GROUP, 2)], axis=1)

    def token_major_normalised(acc):
        inv = 1.0 / jnp.where(lane_r >= HEAD_DIM, acc, 1.0)
        pairs = []
        for r in range(0, NSA_GROUP, 2):
            even = head(acc, r) * pltpu.roll(head(inv, r), HEAD_DIM, axis=1)
            odd = pltpu.roll(head(acc, r + 1), HEAD_DIM, axis=1) * head(inv, r + 1)
            pairs.append(jnp.where(low_half, even, odd))
        return jnp.concatenate(pairs, axis=1)

    gw_cols = NSA_GROUP * HEAD_DIM
    for g in groups:
        cols = slice(g * gw_cols, (g + 1) * gw_cols)
        out = (gc_ref[0, :, cols] * token_major(o_cmp[g]) + gs_ref[0, :, cols] * token_major_normalised(acc_s[g])
               + gw_ref[0, :, cols] * token_major_normalised(acc_w[g]))
        o_ref[0, :, cols] = out.astype(o_ref.dtype)


def _sb_kernel(q_ref, k_ref, v_ref, uu_ref, g_ref, o_ref):
    i = pl.program_id(2)
    rows = q_ref.shape[1]
    band = rows // Q_BLOCK
    n_tiles = (i + 1) * band
    low_half = lax.broadcasted_iota(jnp.int32, (rows, LANES), 1) < HEAD_DIM
    low_half_k = lax.broadcasted_iota(jnp.int32, (Q_BLOCK, LANES), 1) < HEAD_DIM
    q_pair = q_ref[0]
    zero = jnp.zeros_like(q_pair)
    q_heads = (jnp.where(low_half, q_pair, zero), jnp.where(low_half, zero, q_pair))

    tri = (lax.broadcasted_iota(jnp.int32, (Q_BLOCK, LANES), 1)
           < lax.broadcasted_iota(jnp.int32, (Q_BLOCK, LANES), 0))

    def update_rows(x, r0, fn):
        return fn(x) if r0 == 0 else jnp.concatenate([x[:r0], fn(x[r0:])], axis=0)

    def on_diagonal(x, fill):
        masked = jnp.where(tri, x[:Q_BLOCK], fill)
        return masked if x.shape[0] == Q_BLOCK else jnp.concatenate([masked, x[Q_BLOCK:]], axis=0)

    def tile_step(start, carry, r0=0, diagonal=False):
        acc, laters = carry[0], carry[1:]
        k_pair = k_ref[0, pl.ds(start, Q_BLOCK), :]
        v_pair = v_ref[0, pl.ds(start, Q_BLOCK), :]
        zero_v = jnp.zeros_like(v_pair)
        v_bd = jnp.concatenate([jnp.where(low_half_k, v_pair, zero_v),
                                jnp.where(low_half_k, zero_v, v_pair)], axis=0)
        weights, new_laters = [], []
        for q_h, later in zip(q_heads, laters):
            z = _nt_dot(q_h[r0:] if r0 else q_h, k_pair)
            sp = jnp.maximum(z, 0.0) + jnp.log2(1.0 + jnp.exp2(-jnp.abs(z)))
            if diagonal:
                sp = on_diagonal(sp, 0.0)
            hi, lo = _split2(sp)
            r = _dot(jnp.concatenate([hi, lo], axis=1), uu_ref[...])
            after = r[:, :LANES] + (later[r0:] if r0 else later)
            a = jnp.exp2(z - sp - after)
            if diagonal:
                a = on_diagonal(a, 0.0)
            weights.append(a.astype(_MXU_DTYPE))
            new_laters.append(update_rows(later, r0, lambda part, r=r: part + r[:, LANES:]))
        pv = _dot(jnp.concatenate(weights, axis=1), v_bd)
        return (update_rows(acc, r0, lambda part: part + pv), *new_laters)

    zeros = jnp.zeros((rows, LANES), F32)
    carry = (zeros, zeros, zeros)
    for c in reversed(range(band)):
        carry = tile_step(pl.multiple_of(i * rows + c * Q_BLOCK, Q_BLOCK), carry, r0=c * Q_BLOCK, diagonal=True)

    def main_body(kg, carry):
        for u in range(SB_UNROLL):
            first_key = (n_tiles - band - 1 - (kg * SB_UNROLL + u)) * Q_BLOCK
            carry = tile_step(pl.multiple_of(first_key, Q_BLOCK), carry)
        return carry

    n_trips = (n_tiles - band) // SB_UNROLL

    def settled(laters):
        return jnp.min(jnp.minimum(*laters)) >= SB_UNDERFLOW_BITS

    def keep_going(state):
        kg, done, _ = state
        return jnp.logical_and(kg < n_trips, jnp.logical_not(done))

    def main_step(state):
        kg, _, carry = state
        carry = main_body(kg, carry)
        return kg + 1, settled(carry[1:]), carry

    _, _, carry = lax.while_loop(keep_going, main_step, (0, settled(carry[1:]), carry))
    o_ref[0] = (carry[0] * g_ref[0]).astype(o_ref.dtype)


def _outproj_kernel(x_ref, on_ref, os_ref, w_ref, o_ref):
    o_ref[...] = (x_ref[...] + _dot(on_ref[...], w_ref[:NSA_WIDTH, :])
                  + _dot(os_ref[...], w_ref[NSA_WIDTH:, :]))


def _rope_tables(pos, reps):
    inv_freq = jnp.power(ROPE_THETA, -jnp.arange(0, ROPE_DIM, 2, dtype=F32) / ROPE_DIM)
    ang = pos.astype(F32)[:, None] * inv_freq[None, :]
    cos, sin = jnp.cos(ang), jnp.sin(ang)
    n = pos.shape[0]
    rest = HEAD_DIM - ROPE_DIM
    cos_h = jnp.concatenate([cos, cos, jnp.ones((n, rest), F32)], axis=1)
    sin_h = jnp.concatenate([-sin, sin, jnp.zeros((n, rest), F32)], axis=1)
    return jnp.tile(cos_h, (1, reps)), jnp.tile(sin_h, (1, reps))


def _const_spec(shape):
    return pl.BlockSpec(shape, lambda *_: (0,) * len(shape))


def kernel(x, norm_gain, w_in, q_norm_gain, k_norm_cmp, k_norm_slc, k_norm_win,
           cmp_k_pos, cmp_k_w1, cmp_k_b1, cmp_k_w2, cmp_v_pos, cmp_v_w1, cmp_v_b1, cmp_v_w2, w_out):
    B, S, DM = x.shape
    D, G, R = HEAD_DIM, NSA_KV_HEADS, NSA_GROUP
    mxu = _MXU_DTYPE
    n_tok = B * S
    tm = ROW_TILE
    n_sblk = S // tm
    nq = S // Q_BLOCK
    ncp = S // CMP_STRIDE
    n_slc = S // SLC_LEN
    n_cmp = (S - CMP_LEN) // CMP_STRIDE + 1
    assert S % tm == 0 and ncp % LANES == 0 and n_slc <= LANES - D and n_slc >= SLC_TOPN

    n_gl = NSA_HEADS * N_BRANCH
    gl0 = C_GN + n_gl
    w_cat = jnp.concatenate([w_in[:, :C_GN], w_in[:, gl0:], w_in[:, C_GN:gl0],
                             jnp.zeros((DM, LANES - n_gl), w_in.dtype)], axis=1).astype(mxu)
    assert w_cat.shape[1] == N_COLS
    pos = jnp.arange(S, dtype=jnp.int32)
    cos_t, sin_t = _rope_tables(pos, LANES // D)
    cmp_end = jnp.arange(ncp, dtype=jnp.int32) * CMP_STRIDE + (CMP_LEN - 1)
    cos_c, sin_c = _rope_tables(cmp_end, 1)
    lane_i = np.arange(LANES)
    bd = (lane_i[:, None] // D == lane_i[None, :] // D).astype(np.float32)
    bd = jnp.asarray(np.concatenate([bd, bd], axis=0), mxu)
    eg = np.zeros((LANES, N_BRANCH * NSA_WIDTH), np.float32)
    for hh in range(NSA_HEADS):
        for br in range(N_BRANCH):
            eg[hh * N_BRANCH + br, br * NSA_WIDTH + hh * D:br * NSA_WIDTH + (hh + 1) * D] = 1.0
    eg = jnp.asarray(np.concatenate([eg, eg], axis=0), mxu)
    perm = np.zeros((D, D), np.float32)
    for c in range(ROPE_HALF):
        perm[c + ROPE_HALF, c] = 1.0
        perm[c, c + ROPE_HALF] = 1.0
    perm = jnp.asarray(perm, mxu)
    cs = np.arange(ncp) * CMP_STRIDE
    ss = np.arange(LANES - D) * SLC_LEN
    ovl = np.clip(np.minimum(cs[None, :] + CMP_LEN, ss[:, None] + SLC_LEN)
                  - np.maximum(cs[None, :], ss[:, None]), 0, None).astype(np.float32) / CMP_LEN
    ovl[:, n_cmp:] = 0.0
    ovl[n_slc:, :] = 0.0
    ovl = jnp.asarray(ovl, mxu)
    sidx = np.arange(LANES)
    tri = (sidx[:, None] > sidx[None, :]).astype(np.float32)
    uu_half = np.concatenate([tri, np.ones((LANES, LANES), np.float32)], axis=1)
    uu = jnp.asarray(np.concatenate([uu_half, uu_half], axis=0), mxu)

    row = lambda v: v.reshape(1, -1).astype(F32)
    x2 = x.reshape(n_tok, DM)

    tok_spec = lambda w: pl.BlockSpec((tm, w), lambda t: (t, 0))
    head_spec = lambda nh, w: pl.BlockSpec((1, nh, tm, w), lambda t: (t // n_sblk, 0, t % n_sblk, 0))
    tab_spec = pl.BlockSpec((tm, LANES), lambda t: (t % n_sblk, 0))
    sds = jax.ShapeDtypeStruct
    outs = pl.pallas_call(
        functools.partial(_inproj_kernel, n_sblk=n_sblk),
        grid=(n_tok // tm,),
        in_specs=[tok_spec(DM), _const_spec((1, DM)), _const_spec((DM, N_COLS)),
                  _const_spec((1, NSA_WIDTH)), _const_spec((1, 2 * NSA_KV_WIDTH)),
                  _const_spec((2 * LANES, LANES)), tab_spec, tab_spec,
                  _const_spec((2 * LANES, N_BRANCH * NSA_WIDTH))],
        out_specs=[head_spec(NSA_HEADS, LANES), head_spec(G, D), head_spec(G, D),
                   head_spec(G, LANES), head_spec(G, LANES), head_spec(G, D), head_spec(G, LANES),
                   tok_spec(NSA_WIDTH), tok_spec(NSA_WIDTH), tok_spec(NSA_WIDTH),
                   tok_spec(SB_WIDTH), tok_spec(SB_WIDTH), tok_spec(SB_WIDTH), tok_spec(SB_WIDTH)],
        out_shape=[sds((B, NSA_HEADS, S, LANES), mxu), sds((B, G, S, D), F32), sds((B, G, S, D), F32),
                   sds((B, G, S, LANES), mxu), sds((B, G, S, LANES), mxu), sds((B, G, S, D), mxu),
                   sds((B, G, S, LANES), mxu),
                   sds((n_tok, NSA_WIDTH), F32), sds((n_tok, NSA_WIDTH), F32), sds((n_tok, NSA_WIDTH), F32),
                   sds((n_tok, SB_WIDTH), mxu), sds((n_tok, SB_WIDTH), mxu), sds((n_tok, SB_WIDTH), mxu),
                   sds((n_tok, SB_WIDTH), F32)],
        compiler_params=pltpu.CompilerParams(dimension_semantics=("parallel",), vmem_limit_bytes=VMEM_LIMIT),
        name="inproj",
    )(x2, row(norm_gain), w_cat, row(jnp.tile(q_norm_gain, NSA_HEADS)),
      row(jnp.concatenate([jnp.tile(k_norm_slc, G), jnp.tile(k_norm_win, G)])), bd, cos_t, sin_t, eg)
    (q_nsa, kc_raw, vc_raw, ks_aug, vs_aug, k_win, vw_aug, g_cmp, g_slc, g_win,
     q_sb, k_sb, v_sb, g_sb) = outs

    chunk_w = CMP_STRIDE * D
    chunks = lambda a: a.reshape(B * G, ncp, chunk_w)
    bg_spec = lambda r, w: pl.BlockSpec((1, r, w), lambda t: (t, 0, 0))
    k_cmp, v_cmp = pl.pallas_call(
        _compress_kernel,
        grid=(B * G,),
        in_specs=[bg_spec(ncp, chunk_w), bg_spec(ncp, chunk_w),
                  _const_spec((2, chunk_w)), _const_spec((2, chunk_w)),
                  _const_spec((CMP_LEN * D, D)), _const_spec((1, D)), _const_spec((D, D)),
                  _const_spec((CMP_LEN * D, D)), _const_spec((1, D)), _const_spec((D, LANES)),
                  _const_spec((1, D)), _const_spec((ncp, D)), _const_spec((ncp, D)), _const_spec((D, D))],
        out_specs=[bg_spec(ncp, D), bg_spec(ncp, LANES)],
        out_shape=[sds((B * G, ncp, D), mxu), sds((B * G, ncp, LANES), mxu)],
        compiler_params=pltpu.CompilerParams(dimension_semantics=("parallel",), vmem_limit_bytes=VMEM_LIMIT),
        name="compress",
    )(chunks(kc_raw), chunks(vc_raw), cmp_k_pos.reshape(2, chunk_w), cmp_v_pos.reshape(2, chunk_w),
      cmp_k_w1.astype(mxu), row(cmp_k_b1), cmp_k_w2.astype(mxu),
      cmp_v_w1.astype(mxu), row(cmp_v_b1), jnp.tile(cmp_v_w2, (1, LANES // D)).astype(mxu),
      row(k_norm_cmp), cos_c, sin_c, perm)
    k_cmp = k_cmp.reshape(B, G, ncp, D)
    v_cmp = v_cmp.reshape(B, G, ncp, LANES)

    k_gain = jnp.max(jnp.abs(jnp.stack([k_norm_cmp, k_norm_slc, k_norm_win])))
    logit_bound = jnp.max(jnp.abs(q_norm_gain)) * k_gain * (1.02 * D * SCALE * LOG2E)
    logit_bound = logit_bound.astype(F32).reshape(1)
    kv_spec = lambda r, w: pl.BlockSpec((1, G, r, w), lambda b, i: (b, 0, 0, 0))
    gate_spec = pl.BlockSpec((1, NSA_QT, NSA_WIDTH), lambda b, i: (b, i, 0))
    g3 = lambda a: a.reshape(B, S, NSA_WIDTH)
    o_nsa = pl.pallas_call(
        _nsa_kernel,
        grid=(B, S // NSA_QT),
        in_specs=[pl.BlockSpec(memory_space=pltpu.SMEM),
                  pl.BlockSpec((1, NSA_HEADS, NSA_QT, LANES), lambda b, i: (b, 0, i, 0)),
                  kv_spec(ncp, D), kv_spec(ncp, LANES), kv_spec(S, LANES), kv_spec(S, LANES),
                  kv_spec(S, D), kv_spec(S, LANES), _const_spec((LANES - D, ncp)),
                  gate_spec, gate_spec, gate_spec],
        out_specs=gate_spec,
        out_shape=sds((B, S, NSA_WIDTH), mxu),
        compiler_params=pltpu.CompilerParams(dimension_semantics=("parallel", "arbitrary"),
                                             vmem_limit_bytes=VMEM_LIMIT),
        name="nsa",
    )(logit_bound, q_nsa, k_cmp, v_cmp, ks_aug, vs_aug, k_win, vw_aug, ovl, g3(g_cmp), g3(g_slc), g3(g_win))

    sb3 = lambda a: a.reshape(B, S, SB_WIDTH)
    pair_q = pl.BlockSpec((1, SB_ROWS, LANES), lambda b, hp, i: (b, i, hp))
    pair_kv = pl.BlockSpec((1, S, LANES), lambda b, hp, i: (b, 0, hp))
    o_sb = pl.pallas_call(
        _sb_kernel,
        grid=(B, SB_WIDTH // LANES, S // SB_ROWS),
        in_specs=[pair_q, pair_kv, pair_kv, _const_spec((2 * LANES, 2 * LANES)), pair_q],
        out_specs=pair_q,
        out_shape=sds((B, S, SB_WIDTH), mxu),
        compiler_params=pltpu.CompilerParams(dimension_semantics=("parallel", "parallel", "arbitrary"),
                                             vmem_limit_bytes=VMEM_LIMIT),
        name="stickbreak",
    )(sb3(q_sb), sb3(k_sb), sb3(v_sb), uu, sb3(g_sb))

    out = pl.pallas_call(
        _outproj_kernel,
        grid=(n_tok // tm,),
        in_specs=[tok_spec(DM), tok_spec(NSA_WIDTH), tok_spec(SB_WIDTH),
                  _const_spec((NSA_WIDTH + SB_WIDTH, DM))],
        out_specs=tok_spec(DM),
        out_shape=sds((n_tok, DM), x.dtype),
        compiler_params=pltpu.CompilerParams(dimension_semantics=("parallel",), vmem_limit_bytes=VMEM_LIMIT),
        name="outproj",
    )(x2, o_nsa.reshape(n_tok, NSA_WIDTH), o_sb.reshape(n_tok, SB_WIDTH), w_out.astype(mxu))
    return out.reshape(B, S, DM)
```

```python
import functools
import math

import numpy as np
import jax
import jax.numpy as jnp
from jax import lax
from jax.experimental import pallas as pl
from jax.experimental.pallas import tpu as pltpu

HEAD_DIM = 64
NSA_HEADS = 8
NSA_KV_HEADS = 2
NSA_GROUP = NSA_HEADS // NSA_KV_HEADS
SB_HEADS = 8
NSA_WIDTH = NSA_HEADS * HEAD_DIM
SB_WIDTH = SB_HEADS * HEAD_DIM
NSA_KV_WIDTH = NSA_KV_HEADS * HEAD_DIM
N_BRANCH = 3
CMP_LEN = 32
CMP_STRIDE = 16
SLC_LEN = 64
SLC_TOPN = 16
WINDOW = 512
Q_BLOCK = 128
ROPE_DIM = HEAD_DIM // 4
ROPE_HALF = ROPE_DIM // 2
ROPE_THETA = 500000.0
EPS = 1e-6
FORCE_BONUS = 1.0e4
SCALE = 1.0 / math.sqrt(HEAD_DIM)
LOG2E = math.log2(math.e)

LANES = 128
SUBLANES = 8
MASK_NEG = -1.0e30
M_INIT = -3.0e38
ROW_TILE = 512
NSA_QT = 256
SB_ROWS = 512
SB_UNROLL = 2
MAX_STATIC_SHIFT = 60.0
SB_UNDERFLOW_BITS = 160.0
VMEM_LIMIT = 56 * 1024 * 1024

_MXU_DTYPE = jnp.bfloat16
F32 = jnp.float32

C_Q = 0
C_KC = 512
C_KS = 768
C_KW = 1024
C_GN = 1280
C_QSB = 1792
C_KSB = 2304
C_VSB = 2816
C_GSB = 3328
C_GL = 3840
N_COLS = 3968


def _nt_dot(a, b):
    return lax.dot_general(a, b, (((1,), (1,)), ((), ())), preferred_element_type=F32)


def _dot(a, b):
    return jnp.dot(a, b, preferred_element_type=F32)


def _split2(v):
    hi = v.astype(_MXU_DTYPE)
    lo = (v - hi.astype(F32)).astype(_MXU_DTYPE)
    return hi, lo


def _split3(v):
    hi = v.astype(_MXU_DTYPE)
    r1 = v - hi.astype(F32)
    mid = r1.astype(_MXU_DTYPE)
    lo = (r1 - mid.astype(F32)).astype(_MXU_DTYPE)
    return hi, mid, lo


def _inproj_kernel(x_ref, ng_ref, w_ref, qg_ref, kg_ref, bd_ref, cos_ref, sin_ref, eg_ref,
                   q_ref, kc_ref, vc_ref, ksa_ref, vsa_ref, kw_ref, vwa_ref,
                   gc_ref, gs_ref, gw_ref, qsb_ref, ksb_ref, vsb_ref, gsb_ref, *, n_sblk):
    tm = x_ref.shape[0]
    x = x_ref[...]
    ms = jnp.mean(x * x, axis=-1, keepdims=True)
    h = (x * lax.rsqrt(ms + EPS) * ng_ref[...]).astype(_MXU_DTYPE)

    def proj(lo, width):
        return _dot(h, w_ref[:, lo:lo + width])

    lane = lax.broadcasted_iota(jnp.int32, (tm, LANES), 1)
    low_half = lane < HEAD_DIM

    def head_norm_rope(y, gain):
        width = y.shape[1]
        rep = width // LANES
        hi, lo = _split2(y * y)
        ssum = jnp.concatenate(
            [_dot(jnp.concatenate([hi[:, c:c + LANES], lo[:, c:c + LANES]], axis=1), bd_ref[...])
             for c in range(0, width, LANES)], axis=1)
        yn = y * lax.rsqrt(ssum * (1.0 / HEAD_DIM) + EPS) * gain
        cos = jnp.concatenate([cos_ref[...]] * rep, axis=1) if rep > 1 else cos_ref[...]
        sin = jnp.concatenate([sin_ref[...]] * rep, axis=1) if rep > 1 else sin_ref[...]
        fwd = pltpu.roll(yn, ROPE_HALF, axis=1)
        bwd = pltpu.roll(yn, width - ROPE_HALF, axis=1)
        lane_w = lax.broadcasted_iota(jnp.int32, (tm, width), 1)
        partner = jnp.where((lane_w & (HEAD_DIM - 1)) < ROPE_HALF, bwd, fwd)
        return yn * cos + partner * sin

    def head_pair(slab, p):
        chunk = slab[:, p * LANES:(p + 1) * LANES]
        return chunk, pltpu.roll(chunk, HEAD_DIM, axis=1)

    qn = head_norm_rope(proj(C_Q, NSA_WIDTH), qg_ref[...]) * (SCALE * LOG2E)
    for p in range(NSA_HEADS // 2):
        ev, od = head_pair(qn, p)
        q_ref[0, 2 * p] = jnp.where(low_half, ev, 0.0).astype(q_ref.dtype)
        q_ref[0, 2 * p + 1] = jnp.where(low_half, od, 0.0).astype(q_ref.dtype)

    kv_cmp = proj(C_KC, 2 * NSA_KV_WIDTH)
    kv_slc = proj(C_KS, 2 * NSA_KV_WIDTH)
    kv_win = proj(C_KW, 2 * NSA_KV_WIDTH)

    for ref, p in ((kc_ref, 0), (vc_ref, 1)):
        ev, od = head_pair(kv_cmp, p)
        ref[0, 0] = ev[:, :HEAD_DIM]
        ref[0, 1] = od[:, :HEAD_DIM]

    sblk = lax.rem(pl.program_id(0), n_sblk)
    row = lax.broadcasted_iota(jnp.int32, (tm, LANES), 0)
    key_blk = (sblk * tm + row) >> int(math.log2(SLC_LEN))
    onehot = jnp.where(lane - HEAD_DIM == key_blk, 1.0, 0.0)
    k_sw = head_norm_rope(jnp.concatenate([kv_slc[:, :NSA_KV_WIDTH], kv_win[:, :NSA_KV_WIDTH]], axis=1),
                          kg_ref[...])
    ev, od = head_pair(k_sw, 0)
    ksa_ref[0, 0] = jnp.where(low_half, ev, onehot).astype(ksa_ref.dtype)
    ksa_ref[0, 1] = jnp.where(low_half, od, onehot).astype(ksa_ref.dtype)
    ev, od = head_pair(kv_slc, 1)
    vsa_ref[0, 0] = jnp.where(low_half, ev, 1.0).astype(vsa_ref.dtype)
    vsa_ref[0, 1] = jnp.where(low_half, od, 1.0).astype(vsa_ref.dtype)

    ev, od = head_pair(k_sw, 1)
    kw_ref[0, 0] = ev[:, :HEAD_DIM].astype(kw_ref.dtype)
    kw_ref[0, 1] = od[:, :HEAD_DIM].astype(kw_ref.dtype)
    ev, od = head_pair(kv_win, 1)
    vwa_ref[0, 0] = jnp.where(low_half, ev, 1.0).astype(vwa_ref.dtype)
    vwa_ref[0, 1] = jnp.where(low_half, od, 1.0).astype(vwa_ref.dtype)

    gn = proj(C_GN, NSA_WIDTH)
    silu_n = gn * jax.nn.sigmoid(gn)
    gl_split = jnp.concatenate(_split2(proj(C_GL, LANES)), axis=1)
    for br, ref in enumerate((gc_ref, gs_ref, gw_ref)):
        ref[...] = jax.nn.sigmoid(_dot(gl_split, eg_ref[:, br * NSA_WIDTH:(br + 1) * NSA_WIDTH])) * silu_n

    qsb_ref[...] = (proj(C_QSB, SB_WIDTH) * (SCALE * LOG2E)).astype(qsb_ref.dtype)
    ksb_ref[...] = proj(C_KSB, SB_WIDTH).astype(ksb_ref.dtype)
    vsb_ref[...] = proj(C_VSB, SB_WIDTH).astype(vsb_ref.dtype)
    gsb = proj(C_GSB, SB_WIDTH)
    gsb_ref[...] = gsb * jax.nn.sigmoid(gsb)


def _compress_kernel(kc_ref, vc_ref, posk_ref, posv_ref, w1k_ref, b1k_ref, w2k_ref,
                     w1v_ref, b1v_ref, w2v_ref, kg_ref, cos_ref, sin_ref, perm_ref,
                     kcmp_ref, vcmp_ref):
    half = CMP_STRIDE * HEAD_DIM

    def phi(c_ref, pos_ref, w1_ref, b1_ref, w2_ref):
        c = c_ref[0]
        n = c.shape[0]
        top = _dot((c + pos_ref[0:1, :]).astype(_MXU_DTYPE), w1_ref[:half, :])
        bot = _dot((c + pos_ref[1:2, :]).astype(_MXU_DTYPE), w1_ref[half:, :])
        hid = top + pltpu.roll(bot, n - 1, axis=0) + b1_ref[...]
        return _dot((hid * jax.nn.sigmoid(hid)).astype(_MXU_DTYPE), w2_ref[...])

    k = phi(kc_ref, posk_ref, w1k_ref, b1k_ref, w2k_ref)
    ms = jnp.mean(k * k, axis=-1, keepdims=True)
    kn = k * lax.rsqrt(ms + EPS) * kg_ref[...]
    hi, lo = _split2(kn)
    partner = _dot(hi, perm_ref[...]) + _dot(lo, perm_ref[...])
    kcmp_ref[0] = (kn * cos_ref[...] + partner * sin_ref[...]).astype(kcmp_ref.dtype)
    vcmp_ref[0] = phi(vc_ref, posv_ref, w1v_ref, b1v_ref, w2v_ref).astype(vcmp_ref.dtype)


def _nsa_kernel(bound_ref, q_ref, kc_ref, vc_ref, ksa_ref, vsa_ref, kw_ref, vwa_ref, ovl_ref,
                gc_ref, gs_ref, gw_ref, o_ref, p_scr, acc_scr):
    i = pl.program_id(1)
    groups = range(kc_ref.shape[1])
    qt = q_ref.shape[2]
    kw = 2 * qt
    rq = NSA_GROUP * qt
    n_blk_lanes = LANES - HEAD_DIM
    q_pads = [q_ref[0, g * NSA_GROUP:(g + 1) * NSA_GROUP].reshape(rq, LANES) for g in groups]
    heads = lambda x: jnp.concatenate([x] * NSA_GROUP, axis=0)
    t_tok = lax.broadcasted_iota(jnp.int32, (qt, 1), 0)
    t_col = i * qt + heads(t_tok)
    rel_w = heads(t_tok - lax.broadcasted_iota(jnp.int32, (qt, kw), 1))
    rel = rel_w[:, :qt]
    lane_q = lax.broadcasted_iota(jnp.int32, (qt, LANES), 1)
    lane_r = heads(lane_q)
    low_half = lane_q < HEAD_DIM

    logit_bound = bound_ref[0]
    static_shift = logit_bound <= MAX_STATIC_SHIFT
    bound_rows = lambda: jnp.full((rq, 1), logit_bound, F32)
    row_max = lambda s: jnp.max(s, axis=-1, keepdims=True)

    def window_logits(g):
        q = q_pads[g][:, :HEAD_DIM]
        s_parts, v_parts = [], []
        for back in range(WINDOW // qt + 1):
            start = pl.multiple_of(jnp.maximum(i - back, 0) * qt, qt)
            s = _nt_dot(q, kw_ref[0, g, pl.ds(start, qt), :])
            if back == 0:
                keep = rel >= 0
            else:
                keep = rel < (WINDOW - back * qt) - jnp.where(i >= back, 0, WINDOW)
            s_parts.append(jnp.where(keep, s, MASK_NEG))
            v_parts.append(vwa_ref[0, g, pl.ds(start, qt), :])
        return jnp.concatenate(s_parts, axis=1), jnp.concatenate(v_parts, axis=0)

    def cmp_logits(g):
        kc = kc_ref[0, g]
        cmp_end = lax.broadcasted_iota(jnp.int32, (rq, kc.shape[0]), 1) * CMP_STRIDE + (CMP_LEN - 1)
        valid = cmp_end <= t_col
        return jnp.where(valid, _nt_dot(q_pads[g][:, :HEAD_DIM], kc), MASK_NEG), valid

    pre_shifts = lax.cond(
        static_shift,
        lambda: tuple(bound_rows() for _ in groups for _ in range(2)),
        lambda: tuple(m for g in groups for m in (row_max(window_logits(g)[0]), row_max(cmp_logits(g)[0]))))

    ovl = ovl_ref[...]
    j_idx = lax.broadcasted_iota(jnp.int32, (n_blk_lanes, qt), 0)
    blk_t = (i * qt + lax.broadcasted_iota(jnp.int32, (n_blk_lanes, qt), 1)) >> int(math.log2(SLC_LEN))
    slc_valid = j_idx <= blk_t
    forced = (j_idx == 0) | (j_idx == blk_t) | (j_idx == blk_t - 1)
    row_in_grp = lax.broadcasted_iota(jnp.int32, (SUBLANES, qt), 0)

    def selected_blocks(p):
        p_sum = p[0:qt]
        for r in range(1, NSA_GROUP):
            p_sum = p_sum + p[r * qt:(r + 1) * qt]
        p_slc = sum(_nt_dot(ovl, part) for part in _split3(p_sum))
        score = jnp.where(slc_valid, p_slc + jnp.where(forced, FORCE_BONUS, 0.0), -jnp.inf)
        n_grp = n_blk_lanes // SUBLANES
        grp_rows = [score[c * SUBLANES:(c + 1) * SUBLANES] for c in range(n_grp)]
        grp_rank = [jnp.zeros((SUBLANES, qt), F32) for _ in range(n_grp)]
        for ii in range(n_blk_lanes):
            s_i = score[ii:ii + 1, :]
            for c in range(n_grp):
                rows = grp_rows[c]
                if c * SUBLANES > ii:
                    beats = jnp.where(s_i >= rows, 1.0, 0.0)
                elif (c + 1) * SUBLANES - 1 <= ii:
                    beats = jnp.where(s_i > rows, 1.0, 0.0)
                else:
                    beats = jnp.where(row_in_grp > ii - c * SUBLANES,
                                      jnp.where(s_i >= rows, 1.0, 0.0), jnp.where(s_i > rows, 1.0, 0.0))
                grp_rank[c] = grp_rank[c] + beats
        rank = jnp.concatenate(grp_rank, axis=0)
        sel_t = jnp.where(slc_valid, jnp.where(rank < SLC_TOPN, 1.0, 0.0), 0.0)
        return heads(jnp.concatenate([jnp.ones((HEAD_DIM, qt), F32), sel_t], axis=0).T)

    acc_w, o_cmp, sel_rows = [], [], []
    for g in groups:
        m_w, m_c = pre_shifts[2 * g], pre_shifts[2 * g + 1]
        s_win, v_win = window_logits(g)
        acc_w.append(_dot(jnp.exp2(s_win - m_w).astype(_MXU_DTYPE), v_win))
        lg, valid = cmp_logits(g)
        p = jnp.where(valid, jnp.exp2(lg - m_c), 0.0)
        p = p / jnp.maximum(jnp.sum(p, axis=-1, keepdims=True), 1e-30)
        o_cmp.append(_dot(p.astype(_MXU_DTYPE), vc_ref[0, g]))
        sel_rows.append(selected_blocks(p))

    n_kt = (i + 2) // 2

    def augmented_q(g, shift):
        bias = jnp.where(lane_r >= HEAD_DIM, jnp.where(sel_rows[g] > 0.5, -shift, MASK_NEG), 0.0)
        return q_pads[g] + bias.astype(q_pads[g].dtype)

    def slc_logits(g, q_aug, kt):
        s = _nt_dot(q_aug, ksa_ref[0, g, pl.ds(pl.multiple_of(kt * kw, kw), kw), :])
        return jnp.where(rel_w >= kt * kw - i * qt, s, MASK_NEG)

    def slc_row_max(g):
        q_aug = augmented_q(g, 0.0)
        return lax.fori_loop(0, n_kt, lambda kt, m_run: jnp.maximum(m_run, row_max(slc_logits(g, q_aug, kt))),
                             jnp.full((rq, 1), M_INIT, F32))

    slc_shifts = lax.cond(static_shift, lambda: tuple(bound_rows() for _ in groups),
                          lambda: tuple(slc_row_max(g) for g in groups))
    q_augs = [augmented_q(g, slc_shifts[g]) for g in groups]

    def slc_weights(g, kt):
        return jnp.exp2(slc_logits(g, q_augs[g], kt)).astype(_MXU_DTYPE)

    def slc_weighted_values(g, kt):
        return _dot(p_scr[g], vsa_ref[0, g, pl.ds(pl.multiple_of(kt * kw, kw), kw), :])

    for g in groups:
        p_scr[g] = slc_weights(g, 0)
        acc_scr[g] = jnp.zeros((rq, LANES), F32)

    def slc_body(kt, _):
        for g in groups:
            acc_scr[g] += slc_weighted_values(g, kt)
            p_scr[g] = slc_weights(g, kt + 1)
        return 0

    lax.fori_loop(0, n_kt - 1, slc_body, 0)
    acc_s = [acc_scr[g] + slc_weighted_values(g, n_kt - 1) for g in groups]

    head = lambda a, r: a[r * qt:(r + 1) * qt]

    def token_major(o):
        return jnp.concatenate([jnp.where(low_half, head(o, r), head(o, r + 1))
                                for r in range(0, NSA_GROUP, 2)], axis=1)

    def token_major_normalised(acc):
        inv = 1.0 / jnp.where(lane_r >= HEAD_DIM, acc, 1.0)
        pairs = []
        for r in range(0, NSA_GROUP, 2):
            even = head(acc, r) * pltpu.roll(head(inv, r), HEAD_DIM, axis=1)
            odd = pltpu.roll(head(acc, r + 1), HEAD_DIM, axis=1) * head(inv, r + 1)
            pairs.append(jnp.where(low_half, even, odd))
        return jnp.concatenate(pairs, axis=1)

    gw_cols = NSA_GROUP * HEAD_DIM
    for g in groups:
        cols = slice(g * gw_cols, (g + 1) * gw_cols)
        out = (gc_ref[0, :, cols] * token_major(o_cmp[g]) + gs_ref[0, :, cols] * token_major_normalised(acc_s[g])
               + gw_ref[0, :, cols] * token_major_normalised(acc_w[g]))
        o_ref[0, :, cols] = out.astype(o_ref.dtype)


def _sb_kernel(q_ref, k_ref, v_ref, uu_ref, g_ref, o_ref, *state_scr):
    i = pl.program_id(2)
    rows = q_ref.shape[1]
    band = rows // Q_BLOCK
    n_tiles = (i + 1) * band
    low_half = lax.broadcasted_iota(jnp.int32, (rows, LANES), 1) < HEAD_DIM
    low_half_k = lax.broadcasted_iota(jnp.int32, (Q_BLOCK, LANES), 1) < HEAD_DIM
    q_pair = q_ref[0]
    zero = jnp.zeros_like(q_pair)
    q_heads = (jnp.where(low_half, q_pair, zero), jnp.where(low_half, zero, q_pair))

    tri = (lax.broadcasted_iota(jnp.int32, (Q_BLOCK, LANES), 1)
           < lax.broadcasted_iota(jnp.int32, (Q_BLOCK, LANES), 0))

    def update_rows(x, r0, fn):
        return fn(x) if r0 == 0 else jnp.concatenate([x[:r0], fn(x[r0:])], axis=0)

    def on_diagonal(x, fill):
        masked = jnp.where(tri, x[:Q_BLOCK], fill)
        return masked if x.shape[0] == Q_BLOCK else jnp.concatenate([masked, x[Q_BLOCK:]], axis=0)

    def tile_step(start, carry, r0=0, diagonal=False):
        acc, laters = carry[0], carry[1:]
        k_pair = k_ref[0, pl.ds(start, Q_BLOCK), :]
        v_pair = v_ref[0, pl.ds(start, Q_BLOCK), :]
        zero_v = jnp.zeros_like(v_pair)
        v_bd = jnp.concatenate([jnp.where(low_half_k, v_pair, zero_v),
                                jnp.where(low_half_k, zero_v, v_pair)], axis=0)
        weights, new_laters = [], []
        for q_h, later in zip(q_heads, laters):
            z = _nt_dot(q_h[r0:] if r0 else q_h, k_pair)
            sp = jnp.maximum(z, 0.0) + jnp.log2(1.0 + jnp.exp2(-jnp.abs(z)))
            if diagonal:
                sp = on_diagonal(sp, 0.0)
            hi, lo = _split2(sp)
            r = _dot(jnp.concatenate([hi, lo], axis=1), uu_ref[...])
            after = r[:, :LANES] + (later[r0:] if r0 else later)
            a = jnp.exp2(z - sp - after)
            if diagonal:
                a = on_diagonal(a, 0.0)
            weights.append(a.astype(_MXU_DTYPE))
            new_laters.append(update_rows(later, r0, lambda part, r=r: part + r[:, LANES:]))
        pv = _dot(jnp.concatenate(weights, axis=1), v_bd)
        return (update_rows(acc, r0, lambda part: part + pv), *new_laters)

    zeros = jnp.zeros((rows, LANES), F32)
    carry = (zeros, zeros, zeros)
    for c in reversed(range(band)):
        carry = tile_step(pl.multiple_of(i * rows + c * Q_BLOCK, Q_BLOCK), carry, r0=c * Q_BLOCK, diagonal=True)

    def main_body(kg, carry):
        for u in range(SB_UNROLL):
            first_key = (n_tiles - band - 1 - (kg * SB_UNROLL + u)) * Q_BLOCK
            carry = tile_step(pl.multiple_of(first_key, Q_BLOCK), carry)
        return carry

    n_trips = (n_tiles - band) // SB_UNROLL

    def settled(laters):
        return jnp.min(jnp.minimum(*laters)) >= SB_UNDERFLOW_BITS

    def keep_going(state):
        kg, done = state
        return jnp.logical_and(kg < n_trips, jnp.logical_not(done))

    def save(carry):
        for ref, value in zip(state_scr, carry):
            ref[...] = value

    def main_step(state):
        kg, _ = state
        carry = main_body(kg, tuple(ref[...] for ref in state_scr))
        save(carry)
        return kg + 1, settled(carry[1:])

    save(carry)
    lax.while_loop(keep_going, main_step, (0, settled(carry[1:])))
    o_ref[0] = (state_scr[0][...] * g_ref[0]).astype(o_ref.dtype)


def _outproj_kernel(x_ref, on_ref, os_ref, w_ref, o_ref):
    o_ref[...] = (x_ref[...] + _dot(on_ref[...], w_ref[:NSA_WIDTH, :])
                  + _dot(os_ref[...], w_ref[NSA_WIDTH:, :]))


def _rope_tables(pos, reps):
    inv_freq = jnp.power(ROPE_THETA, -jnp.arange(0, ROPE_DIM, 2, dtype=F32) / ROPE_DIM)
    ang = pos.astype(F32)[:, None] * inv_freq[None, :]
    cos, sin = jnp.cos(ang), jnp.sin(ang)
    n = pos.shape[0]
    rest = HEAD_DIM - ROPE_DIM
    cos_h = jnp.concatenate([cos, cos, jnp.ones((n, rest), F32)], axis=1)
    sin_h = jnp.concatenate([-sin, sin, jnp.zeros((n, rest), F32)], axis=1)
    return jnp.tile(cos_h, (1, reps)), jnp.tile(sin_h, (1, reps))


def _const_spec(shape):
    return pl.BlockSpec(shape, lambda *_: (0,) * len(shape))


def kernel(x, norm_gain, w_in, q_norm_gain, k_norm_cmp, k_norm_slc, k_norm_win,
           cmp_k_pos, cmp_k_w1, cmp_k_b1, cmp_k_w2, cmp_v_pos, cmp_v_w1, cmp_v_b1, cmp_v_w2, w_out):
    B, S, DM = x.shape
    D, G, R = HEAD_DIM, NSA_KV_HEADS, NSA_GROUP
    mxu = _MXU_DTYPE
    n_tok = B * S
    tm = ROW_TILE
    n_sblk = S // tm
    nq = S // Q_BLOCK
    ncp = S // CMP_STRIDE
    n_slc = S // SLC_LEN
    n_cmp = (S - CMP_LEN) // CMP_STRIDE + 1
    assert S % tm == 0 and ncp % LANES == 0 and n_slc <= LANES - D and n_slc >= SLC_TOPN

    n_gl = NSA_HEADS * N_BRANCH
    gl0 = C_GN + n_gl
    w_cat = jnp.concatenate([w_in[:, :C_GN], w_in[:, gl0:], w_in[:, C_GN:gl0],
                             jnp.zeros((DM, LANES - n_gl), w_in.dtype)], axis=1).astype(mxu)
    assert w_cat.shape[1] == N_COLS
    pos = jnp.arange(S, dtype=jnp.int32)
    cos_t, sin_t = _rope_tables(pos, LANES // D)
    cmp_end = jnp.arange(ncp, dtype=jnp.int32) * CMP_STRIDE + (CMP_LEN - 1)
    cos_c, sin_c = _rope_tables(cmp_end, 1)
    lane_i = np.arange(LANES)
    bd = (lane_i[:, None] // D == lane_i[None, :] // D).astype(np.float32)
    bd = jnp.asarray(np.concatenate([bd, bd], axis=0), mxu)
    eg = np.zeros((LANES, N_BRANCH * NSA_WIDTH), np.float32)
    for hh in range(NSA_HEADS):
        for br in range(N_BRANCH):
            eg[hh * N_BRANCH + br, br * NSA_WIDTH + hh * D:br * NSA_WIDTH + (hh + 1) * D] = 1.0
    eg = jnp.asarray(np.concatenate([eg, eg], axis=0), mxu)
    perm = np.zeros((D, D), np.float32)
    for c in range(ROPE_HALF):
        perm[c + ROPE_HALF, c] = 1.0
        perm[c, c + ROPE_HALF] = 1.0
    perm = jnp.asarray(perm, mxu)
    cs = np.arange(ncp) * CMP_STRIDE
    ss = np.arange(LANES - D) * SLC_LEN
    ovl = np.clip(np.minimum(cs[None, :] + CMP_LEN, ss[:, None] + SLC_LEN)
                  - np.maximum(cs[None, :], ss[:, None]), 0, None).astype(np.float32) / CMP_LEN
    ovl[:, n_cmp:] = 0.0
    ovl[n_slc:, :] = 0.0
    ovl = jnp.asarray(ovl, mxu)
    sidx = np.arange(LANES)
    tri = (sidx[:, None] > sidx[None, :]).astype(np.float32)
    uu_half = np.concatenate([tri, np.ones((LANES, LANES), np.float32)], axis=1)
    uu = jnp.asarray(np.concatenate([uu_half, uu_half], axis=0), mxu)

    row = lambda v: v.reshape(1, -1).astype(F32)
    x2 = x.reshape(n_tok, DM)

    tok_spec = lambda w: pl.BlockSpec((tm, w), lambda t: (t, 0))
    head_spec = lambda nh, w: pl.BlockSpec((1, nh, tm, w), lambda t: (t // n_sblk, 0, t % n_sblk, 0))
    tab_spec = pl.BlockSpec((tm, LANES), lambda t: (t % n_sblk, 0))
    sds = jax.ShapeDtypeStruct
    outs = pl.pallas_call(
        functools.partial(_inproj_kernel, n_sblk=n_sblk),
        grid=(n_tok // tm,),
        in_specs=[tok_spec(DM), _const_spec((1, DM)), _const_spec((DM, N_COLS)),
                  _const_spec((1, NSA_WIDTH)), _const_spec((1, 2 * NSA_KV_WIDTH)),
                  _const_spec((2 * LANES, LANES)), tab_spec, tab_spec,
                  _const_spec((2 * LANES, N_BRANCH * NSA_WIDTH))],
        out_specs=[head_spec(NSA_HEADS, LANES), head_spec(G, D), head_spec(G, D),
                   head_spec(G, LANES), head_spec(G, LANES), head_spec(G, D), head_spec(G, LANES),
                   tok_spec(NSA_WIDTH), tok_spec(NSA_WIDTH), tok_spec(NSA_WIDTH),
                   tok_spec(SB_WIDTH), tok_spec(SB_WIDTH), tok_spec(SB_WIDTH), tok_spec(SB_WIDTH)],
        out_shape=[sds((B, NSA_HEADS, S, LANES), mxu), sds((B, G, S, D), F32), sds((B, G, S, D), F32),
                   sds((B, G, S, LANES), mxu), sds((B, G, S, LANES), mxu), sds((B, G, S, D), mxu),
                   sds((B, G, S, LANES), mxu),
                   sds((n_tok, NSA_WIDTH), F32), sds((n_tok, NSA_WIDTH), F32), sds((n_tok, NSA_WIDTH), F32),
                   sds((n_tok, SB_WIDTH), mxu), sds((n_tok, SB_WIDTH), mxu), sds((n_tok, SB_WIDTH), mxu),
                   sds((n_tok, SB_WIDTH), F32)],
        compiler_params=pltpu.CompilerParams(dimension_semantics=("parallel",), vmem_limit_bytes=VMEM_LIMIT),
        name="inproj",
    )(x2, row(norm_gain), w_cat, row(jnp.tile(q_norm_gain, NSA_HEADS)),
      row(jnp.concatenate([jnp.tile(k_norm_slc, G), jnp.tile(k_norm_win, G)])), bd, cos_t, sin_t, eg)
    (q_nsa, kc_raw, vc_raw, ks_aug, vs_aug, k_win, vw_aug, g_cmp, g_slc, g_win,
     q_sb, k_sb, v_sb, g_sb) = outs

    chunk_w = CMP_STRIDE * D
    chunks = lambda a: a.reshape(B * G, ncp, chunk_w)
    bg_spec = lambda r, w: pl.BlockSpec((1, r, w), lambda t: (t, 0, 0))
    k_cmp, v_cmp = pl.pallas_call(
        _compress_kernel,
        grid=(B * G,),
        in_specs=[bg_spec(ncp, chunk_w), bg_spec(ncp, chunk_w),
                  _const_spec((2, chunk_w)), _const_spec((2, chunk_w)),
                  _const_spec((CMP_LEN * D, D)), _const_spec((1, D)), _const_spec((D, D)),
                  _const_spec((CMP_LEN * D, D)), _const_spec((1, D)), _const_spec((D, LANES)),
                  _const_spec((1, D)), _const_spec((ncp, D)), _const_spec((ncp, D)), _const_spec((D, D))],
        out_specs=[bg_spec(ncp, D), bg_spec(ncp, LANES)],
        out_shape=[sds((B * G, ncp, D), mxu), sds((B * G, ncp, LANES), mxu)],
        compiler_params=pltpu.CompilerParams(dimension_semantics=("parallel",), vmem_limit_bytes=VMEM_LIMIT),
        name="compress",
    )(chunks(kc_raw), chunks(vc_raw), cmp_k_pos.reshape(2, chunk_w), cmp_v_pos.reshape(2, chunk_w),
      cmp_k_w1.astype(mxu), row(cmp_k_b1), cmp_k_w2.astype(mxu),
      cmp_v_w1.astype(mxu), row(cmp_v_b1), jnp.tile(cmp_v_w2, (1, LANES // D)).astype(mxu),
      row(k_norm_cmp), cos_c, sin_c, perm)
    k_cmp = k_cmp.reshape(B, G, ncp, D)
    v_cmp = v_cmp.reshape(B, G, ncp, LANES)

    k_gain = jnp.max(jnp.abs(jnp.stack([k_norm_cmp, k_norm_slc, k_norm_win])))
    logit_bound = jnp.max(jnp.abs(q_norm_gain)) * k_gain * (1.02 * D * SCALE * LOG2E)
    logit_bound = logit_bound.astype(F32).reshape(1)
    kv_spec = lambda r, w: pl.BlockSpec((1, G, r, w), lambda b, i: (b, 0, 0, 0))
    gate_spec = pl.BlockSpec((1, NSA_QT, NSA_WIDTH), lambda b, i: (b, i, 0))
    g3 = lambda a: a.reshape(B, S, NSA_WIDTH)
    o_nsa = pl.pallas_call(
        _nsa_kernel,
        grid=(B, S // NSA_QT),
        in_specs=[pl.BlockSpec(memory_space=pltpu.SMEM),
                  pl.BlockSpec((1, NSA_HEADS, NSA_QT, LANES), lambda b, i: (b, 0, i, 0)),
                  kv_spec(ncp, D), kv_spec(ncp, LANES), kv_spec(S, LANES), kv_spec(S, LANES),
                  kv_spec(S, D), kv_spec(S, LANES), _const_spec((LANES - D, ncp)),
                  gate_spec, gate_spec, gate_spec],
        out_specs=gate_spec,
        out_shape=sds((B, S, NSA_WIDTH), mxu),
        scratch_shapes=[pltpu.VMEM((G, R * NSA_QT, 2 * NSA_QT), mxu),
                        pltpu.VMEM((G, R * NSA_QT, LANES), F32)],
        compiler_params=pltpu.CompilerParams(dimension_semantics=("parallel", "arbitrary"),
                                             vmem_limit_bytes=VMEM_LIMIT),
        name="nsa",
    )(logit_bound, q_nsa, k_cmp, v_cmp, ks_aug, vs_aug, k_win, vw_aug, ovl, g3(g_cmp), g3(g_slc), g3(g_win))

    sb3 = lambda a: a.reshape(B, S, SB_WIDTH)
    pair_q = pl.BlockSpec((1, SB_ROWS, LANES), lambda b, hp, i: (b, i, hp))
    pair_kv = pl.BlockSpec((1, S, LANES), lambda b, hp, i: (b, 0, hp))
    o_sb = pl.pallas_call(
        _sb_kernel,
        grid=(B, SB_WIDTH // LANES, S // SB_ROWS),
        in_specs=[pair_q, pair_kv, pair_kv, _const_spec((2 * LANES, 2 * LANES)), pair_q],
        out_specs=pair_q,
        out_shape=sds((B, S, SB_WIDTH), mxu),
        scratch_shapes=[pltpu.VMEM((SB_ROWS, LANES), F32)] * 3,
        compiler_params=pltpu.CompilerParams(dimension_semantics=("parallel", "parallel", "arbitrary"),
                                             vmem_limit_bytes=VMEM_LIMIT),
        name="stickbreak",
    )(sb3(q_sb), sb3(k_sb), sb3(v_sb), uu, sb3(g_sb))

    out = pl.pallas_call(
        _outproj_kernel,
        grid=(n_tok // tm,),
        in_specs=[tok_spec(DM), tok_spec(NSA_WIDTH), tok_spec(SB_WIDTH),
                  _const_spec((NSA_WIDTH + SB_WIDTH, DM))],
        out_specs=tok_spec(DM),
        out_shape=sds((n_tok, DM), x.dtype),
        compiler_params=pltpu.CompilerParams(dimension_semantics=("parallel",), vmem_limit_bytes=VMEM_LIMIT),
        name="outproj",
    )(x2, o_nsa.reshape(n_tok, NSA_WIDTH), o_sb.reshape(n_tok, SB_WIDTH), w_out.astype(mxu))
    return out.reshape(B, S, DM)
```

```python
import functools
import math

import numpy as np
import jax
import jax.numpy as jnp
from jax import lax
from jax.experimental import pallas as pl
from jax.experimental.pallas import tpu as pltpu

HEAD_DIM = 64
NSA_HEADS = 8
NSA_KV_HEADS = 2
NSA_GROUP = NSA_HEADS // NSA_KV_HEADS
SB_HEADS = 8
NSA_WIDTH = NSA_HEADS * HEAD_DIM
SB_WIDTH = SB_HEADS * HEAD_DIM
NSA_KV_WIDTH = NSA_KV_HEADS * HEAD_DIM
N_BRANCH = 3
CMP_LEN = 32
CMP_STRIDE = 16
SLC_LEN = 64
SLC_TOPN = 16
WINDOW = 512
Q_BLOCK = 128
ROPE_DIM = HEAD_DIM // 4
ROPE_HALF = ROPE_DIM // 2
ROPE_THETA = 500000.0
EPS = 1e-6
FORCE_BONUS = 1.0e4
SCALE = 1.0 / math.sqrt(HEAD_DIM)
LOG2E = math.log2(math.e)

LANES = 128
SUBLANES = 8
MASK_NEG = -1.0e30
M_INIT = -3.0e38
ROW_TILE = 512
NSA_QT = 256
SB_ROWS = 512
SB_UNROLL = 2
MAX_STATIC_SHIFT = 60.0
SB_UNDERFLOW_BITS = 160.0
VMEM_LIMIT = 56 * 1024 * 1024

_MXU_DTYPE = jnp.bfloat16
F32 = jnp.float32

C_Q = 0
C_KC = 512
C_KS = 768
C_KW = 1024
C_GN = 1280
C_QSB = 1792
C_KSB = 2304
C_VSB = 2816
C_GSB = 3328
C_GL = 3840
N_COLS = 3968


def _nt_dot(a, b):
    return lax.dot_general(a, b, (((1,), (1,)), ((), ())), preferred_element_type=F32)


def _dot(a, b):
    return jnp.dot(a, b, preferred_element_type=F32)


def _split2(v):
    hi = v.astype(_MXU_DTYPE)
    lo = (v - hi.astype(F32)).astype(_MXU_DTYPE)
    return hi, lo


def _split3(v):
    hi = v.astype(_MXU_DTYPE)
    r1 = v - hi.astype(F32)
    mid = r1.astype(_MXU_DTYPE)
    lo = (r1 - mid.astype(F32)).astype(_MXU_DTYPE)
    return hi, mid, lo


def _inproj_kernel(x_ref, ng_ref, w_ref, qg_ref, kg_ref, bd_ref, cos_ref, sin_ref, eg_ref,
                   q_ref, kc_ref, vc_ref, ksa_ref, vsa_ref, kw_ref, vwa_ref,
                   gc_ref, gs_ref, gw_ref, qsb_ref, ksb_ref, vsb_ref, gsb_ref, *, n_sblk):
    tm = x_ref.shape[0]
    x = x_ref[...]
    ms = jnp.mean(x * x, axis=-1, keepdims=True)
    h = (x * lax.rsqrt(ms + EPS) * ng_ref[...]).astype(_MXU_DTYPE)

    def proj(lo, width):
        return _dot(h, w_ref[:, lo:lo + width])

    lane = lax.broadcasted_iota(jnp.int32, (tm, LANES), 1)
    low_half = lane < HEAD_DIM

    def head_norm_rope(y, gain):
        width = y.shape[1]
        rep = width // LANES
        hi, lo = _split2(y * y)
        ssum = jnp.concatenate(
            [_dot(jnp.concatenate([hi[:, c:c + LANES], lo[:, c:c + LANES]], axis=1), bd_ref[...])
             for c in range(0, width, LANES)], axis=1)
        yn = y * lax.rsqrt(ssum * (1.0 / HEAD_DIM) + EPS) * gain
        cos = jnp.concatenate([cos_ref[...]] * rep, axis=1) if rep > 1 else cos_ref[...]
        sin = jnp.concatenate([sin_ref[...]] * rep, axis=1) if rep > 1 else sin_ref[...]
        fwd = pltpu.roll(yn, ROPE_HALF, axis=1)
        bwd = pltpu.roll(yn, width - ROPE_HALF, axis=1)
        lane_w = lax.broadcasted_iota(jnp.int32, (tm, width), 1)
        partner = jnp.where((lane_w & (HEAD_DIM - 1)) < ROPE_HALF, bwd, fwd)
        return yn * cos + partner * sin

    def head_pair(slab, p):
        chunk = slab[:, p * LANES:(p + 1) * LANES]
        return chunk, pltpu.roll(chunk, HEAD_DIM, axis=1)

    qn = head_norm_rope(proj(C_Q, NSA_WIDTH), qg_ref[...]) * (SCALE * LOG2E)
    for p in range(NSA_HEADS // 2):
        ev, od = head_pair(qn, p)
        q_ref[0, 2 * p] = jnp.where(low_half, ev, 0.0).astype(q_ref.dtype)
        q_ref[0, 2 * p + 1] = jnp.where(low_half, od, 0.0).astype(q_ref.dtype)

    kv_cmp = proj(C_KC, 2 * NSA_KV_WIDTH)
    kv_slc = proj(C_KS, 2 * NSA_KV_WIDTH)
    kv_win = proj(C_KW, 2 * NSA_KV_WIDTH)

    for ref, p in ((kc_ref, 0), (vc_ref, 1)):
        ev, od = head_pair(kv_cmp, p)
        ref[0, 0] = ev[:, :HEAD_DIM]
        ref[0, 1] = od[:, :HEAD_DIM]

    sblk = lax.rem(pl.program_id(0), n_sblk)
    row = lax.broadcasted_iota(jnp.int32, (tm, LANES), 0)
    key_blk = (sblk * tm + row) >> int(math.log2(SLC_LEN))
    onehot = jnp.where(lane - HEAD_DIM == key_blk, 1.0, 0.0)
    k_sw = head_norm_rope(jnp.concatenate([kv_slc[:, :NSA_KV_WIDTH], kv_win[:, :NSA_KV_WIDTH]], axis=1),
                          kg_ref[...])
    ev, od = head_pair(k_sw, 0)
    ksa_ref[0, 0] = jnp.where(low_half, ev, onehot).astype(ksa_ref.dtype)
    ksa_ref[0, 1] = jnp.where(low_half, od, onehot).astype(ksa_ref.dtype)
    ev, od = head_pair(kv_slc, 1)
    vsa_ref[0, 0] = jnp.where(low_half, ev, 1.0).astype(vsa_ref.dtype)
    vsa_ref[0, 1] = jnp.where(low_half, od, 1.0).astype(vsa_ref.dtype)

    ev, od = head_pair(k_sw, 1)
    kw_ref[0, 0] = ev[:, :HEAD_DIM].astype(kw_ref.dtype)
    kw_ref[0, 1] = od[:, :HEAD_DIM].astype(kw_ref.dtype)
    ev, od = head_pair(kv_win, 1)
    vwa_ref[0, 0] = jnp.where(low_half, ev, 1.0).astype(vwa_ref.dtype)
    vwa_ref[0, 1] = jnp.where(low_half, od, 1.0).astype(vwa_ref.dtype)

    gn = proj(C_GN, NSA_WIDTH)
    silu_n = gn * jax.nn.sigmoid(gn)
    gl_split = jnp.concatenate(_split2(proj(C_GL, LANES)), axis=1)
    for br, ref in enumerate((gc_ref, gs_ref, gw_ref)):
        ref[...] = jax.nn.sigmoid(_dot(gl_split, eg_ref[:, br * NSA_WIDTH:(br + 1) * NSA_WIDTH])) * silu_n

    qsb_ref[...] = (proj(C_QSB, SB_WIDTH) * (SCALE * LOG2E)).astype(qsb_ref.dtype)
    ksb_ref[...] = proj(C_KSB, SB_WIDTH).astype(ksb_ref.dtype)
    vsb_ref[...] = proj(C_VSB, SB_WIDTH).astype(vsb_ref.dtype)
    gsb = proj(C_GSB, SB_WIDTH)
    gsb_ref[...] = gsb * jax.nn.sigmoid(gsb)


def _compress_kernel(kc_ref, vc_ref, posk_ref, posv_ref, w1k_ref, b1k_ref, w2k_ref,
                     w1v_ref, b1v_ref, w2v_ref, kg_ref, cos_ref, sin_ref, perm_ref,
                     kcmp_ref, vcmp_ref):
    half = CMP_STRIDE * HEAD_DIM

    def phi(c_ref, pos_ref, w1_ref, b1_ref, w2_ref):
        c = c_ref[0]
        n = c.shape[0]
        top = _dot((c + pos_ref[0:1, :]).astype(_MXU_DTYPE), w1_ref[:half, :])
        bot = _dot((c + pos_ref[1:2, :]).astype(_MXU_DTYPE), w1_ref[half:, :])
        hid = top + pltpu.roll(bot, n - 1, axis=0) + b1_ref[...]
        return _dot((hid * jax.nn.sigmoid(hid)).astype(_MXU_DTYPE), w2_ref[...])

    k = phi(kc_ref, posk_ref, w1k_ref, b1k_ref, w2k_ref)
    ms = jnp.mean(k * k, axis=-1, keepdims=True)
    kn = k * lax.rsqrt(ms + EPS) * kg_ref[...]
    hi, lo = _split2(kn)
    partner = _dot(hi, perm_ref[...]) + _dot(lo, perm_ref[...])
    kcmp_ref[0] = (kn * cos_ref[...] + partner * sin_ref[...]).astype(kcmp_ref.dtype)
    vcmp_ref[0] = phi(vc_ref, posv_ref, w1v_ref, b1v_ref, w2v_ref).astype(vcmp_ref.dtype)


def _nsa_kernel(bound_ref, q_ref, kc_ref, vc_ref, ksa_ref, vsa_ref, kw_ref, vwa_ref, ovl_ref,
                gc_ref, gs_ref, gw_ref, o_ref, p_scr, acc_scr):
    qt = NSA_QT
    kw = 2 * qt
    assert q_ref.shape[2] == kw
    chains = [(g, half) for half in range(2) for g in range(kc_ref.shape[1])]
    groups = range(len(chains))
    kv = [g for g, _ in chains]
    blk = [2 * pl.program_id(1) + half for _, half in chains]
    rq = NSA_GROUP * qt
    n_blk_lanes = LANES - HEAD_DIM
    q_pads = [q_ref[0, g * NSA_GROUP:(g + 1) * NSA_GROUP, half * qt:(half + 1) * qt].reshape(rq, LANES)
              for g, half in chains]
    heads = lambda x: jnp.concatenate([x] * NSA_GROUP, axis=0)
    t_tok = lax.broadcasted_iota(jnp.int32, (qt, 1), 0)
    rel_w = heads(t_tok - lax.broadcasted_iota(jnp.int32, (qt, kw), 1))
    rel = rel_w[:, :qt]
    lane_q = lax.broadcasted_iota(jnp.int32, (qt, LANES), 1)
    lane_r = heads(lane_q)
    low_half = lane_q < HEAD_DIM

    logit_bound = bound_ref[0]
    static_shift = logit_bound <= MAX_STATIC_SHIFT
    bound_rows = lambda: jnp.full((rq, 1), logit_bound, F32)
    row_max = lambda s: jnp.max(s, axis=-1, keepdims=True)

    def window_logits(g):
        q = q_pads[g][:, :HEAD_DIM]
        i = blk[g]
        s_parts, v_parts = [], []
        for back in range(WINDOW // qt + 1):
            start = pl.multiple_of(jnp.maximum(i - back, 0) * qt, qt)
            s = _nt_dot(q, kw_ref[0, kv[g], pl.ds(start, qt), :])
            if back == 0:
                keep = rel >= 0
            else:
                keep = rel < (WINDOW - back * qt) - jnp.where(i >= back, 0, WINDOW)
            s_parts.append(jnp.where(keep, s, MASK_NEG))
            v_parts.append(vwa_ref[0, kv[g], pl.ds(start, qt), :])
        return jnp.concatenate(s_parts, axis=1), jnp.concatenate(v_parts, axis=0)

    def cmp_logits(g):
        kc = kc_ref[0, kv[g]]
        cmp_end = lax.broadcasted_iota(jnp.int32, (rq, kc.shape[0]), 1) * CMP_STRIDE + (CMP_LEN - 1)
        valid = cmp_end <= blk[g] * qt + heads(t_tok)
        return jnp.where(valid, _nt_dot(q_pads[g][:, :HEAD_DIM], kc), MASK_NEG), valid

    pre_shifts = lax.cond(
        static_shift,
        lambda: tuple(bound_rows() for _ in groups for _ in range(2)),
        lambda: tuple(m for g in groups for m in (row_max(window_logits(g)[0]), row_max(cmp_logits(g)[0]))))

    ovl = ovl_ref[...]
    j_idx = lax.broadcasted_iota(jnp.int32, (n_blk_lanes, qt), 0)
    row_in_grp = lax.broadcasted_iota(jnp.int32, (SUBLANES, qt), 0)

    def selected_blocks(p, i):
        blk_t = (i * qt + lax.broadcasted_iota(jnp.int32, (n_blk_lanes, qt), 1)) >> int(math.log2(SLC_LEN))
        slc_valid = j_idx <= blk_t
        forced = (j_idx == 0) | (j_idx == blk_t) | (j_idx == blk_t - 1)
        p_sum = p[0:qt]
        for r in range(1, NSA_GROUP):
            p_sum = p_sum + p[r * qt:(r + 1) * qt]
        p_slc = sum(_nt_dot(ovl, part) for part in _split3(p_sum))
        score = jnp.where(slc_valid, p_slc + jnp.where(forced, FORCE_BONUS, 0.0), -jnp.inf)
        n_grp = n_blk_lanes // SUBLANES
        grp_rows = [score[c * SUBLANES:(c + 1) * SUBLANES] for c in range(n_grp)]
        grp_rank = [jnp.zeros((SUBLANES, qt), F32) for _ in range(n_grp)]
        for ii in range(n_blk_lanes):
            s_i = score[ii:ii + 1, :]
            for c in range(n_grp):
                rows = grp_rows[c]
                if c * SUBLANES > ii:
                    beats = jnp.where(s_i >= rows, 1.0, 0.0)
                elif (c + 1) * SUBLANES - 1 <= ii:
                    beats = jnp.where(s_i > rows, 1.0, 0.0)
                else:
                    beats = jnp.where(row_in_grp > ii - c * SUBLANES,
                                      jnp.where(s_i >= rows, 1.0, 0.0), jnp.where(s_i > rows, 1.0, 0.0))
                grp_rank[c] = grp_rank[c] + beats
        rank = jnp.concatenate(grp_rank, axis=0)
        sel_t = jnp.where(slc_valid, jnp.where(rank < SLC_TOPN, 1.0, 0.0), 0.0)
        return heads(jnp.concatenate([jnp.ones((HEAD_DIM, qt), F32), sel_t], axis=0).T)

    acc_w, o_cmp, sel_rows = [], [], []
    for g in groups:
        m_w, m_c = pre_shifts[2 * g], pre_shifts[2 * g + 1]
        s_win, v_win = window_logits(g)
        acc_w.append(_dot(jnp.exp2(s_win - m_w).astype(_MXU_DTYPE), v_win))
        lg, valid = cmp_logits(g)
        p = jnp.where(valid, jnp.exp2(lg - m_c), 0.0)
        p = p / jnp.maximum(jnp.sum(p, axis=-1, keepdims=True), 1e-30)
        o_cmp.append(_dot(p.astype(_MXU_DTYPE), vc_ref[0, kv[g]]))
        sel_rows.append(selected_blocks(p, blk[g]))

    n_kt = pl.program_id(1) + 1

    def augmented_q(g, shift):
        bias = jnp.where(lane_r >= HEAD_DIM, jnp.where(sel_rows[g] > 0.5, -shift, MASK_NEG), 0.0)
        return q_pads[g] + bias.astype(q_pads[g].dtype)

    def slc_logits(g, q_aug, kt):
        s = _nt_dot(q_aug, ksa_ref[0, kv[g], pl.ds(pl.multiple_of(kt * kw, kw), kw), :])
        return jnp.where(rel_w >= kt * kw - blk[g] * qt, s, MASK_NEG)

    def slc_row_max(g):
        q_aug = augmented_q(g, 0.0)
        return lax.fori_loop(0, n_kt, lambda kt, m_run: jnp.maximum(m_run, row_max(slc_logits(g, q_aug, kt))),
                             jnp.full((rq, 1), M_INIT, F32))

    slc_shifts = lax.cond(static_shift, lambda: tuple(bound_rows() for _ in groups),
                          lambda: tuple(slc_row_max(g) for g in groups))
    q_augs = [augmented_q(g, slc_shifts[g]) for g in groups]

    def slc_weights(g, kt):
        return jnp.exp2(slc_logits(g, q_augs[g], kt)).astype(_MXU_DTYPE)

    def slc_weighted_values(g, kt):
        return _dot(p_scr[g], vsa_ref[0, kv[g], pl.ds(pl.multiple_of(kt * kw, kw), kw), :])

    for g in groups:
        p_scr[g] = slc_weights(g, 0)
        acc_scr[g] = jnp.zeros((rq, LANES), F32)

    def slc_body(kt, _):
        for g in groups:
            acc_scr[g] += slc_weighted_values(g, kt)
            p_scr[g] = slc_weights(g, kt + 1)
        return 0

    lax.fori_loop(0, n_kt - 1, slc_body, 0)
    acc_s = [acc_scr[g] + slc_weighted_values(g, n_kt - 1) for g in groups]

    head = lambda a, r: a[r * qt:(r + 1) * qt]

    def token_major(o):
        return jnp.concatenate([jnp.where(low_half, head(o, r), head(o, r + 1))
                                for r in range(0, NSA_GROUP, 2)], axis=1)

    def token_major_normalised(acc):
        inv = 1.0 / jnp.where(lane_r >= HEAD_DIM, acc, 1.0)
        pairs = []
        for r in range(0, NSA_GROUP, 2):
            even = head(acc, r) * pltpu.roll(head(inv, r), HEAD_DIM, axis=1)
            odd = pltpu.roll(head(acc, r + 1), HEAD_DIM, axis=1) * head(inv, r + 1)
            pairs.append(jnp.where(low_half, even, odd))
        return jnp.concatenate(pairs, axis=1)

    gw_cols = NSA_GROUP * HEAD_DIM
    for g, (kv_head, half) in enumerate(chains):
        rows = slice(half * qt, (half + 1) * qt)
        cols = slice(kv_head * gw_cols, (kv_head + 1) * gw_cols)
        out = (gc_ref[0, rows, cols] * token_major(o_cmp[g])
               + gs_ref[0, rows, cols] * token_major_normalised(acc_s[g])
               + gw_ref[0, rows, cols] * token_major_normalised(acc_w[g]))
        o_ref[0, rows, cols] = out.astype(o_ref.dtype)


def _sb_kernel(q_ref, k_ref, v_ref, uu_ref, g_ref, o_ref, *state_scr):
    i = pl.program_id(2)
    rows = q_ref.shape[1]
    band = rows // Q_BLOCK
    n_tiles = (i + 1) * band
    low_half = lax.broadcasted_iota(jnp.int32, (rows, LANES), 1) < HEAD_DIM
    low_half_k = lax.broadcasted_iota(jnp.int32, (Q_BLOCK, LANES), 1) < HEAD_DIM
    q_pair = q_ref[0]
    zero = jnp.zeros_like(q_pair)
    q_heads = (jnp.where(low_half, q_pair, zero), jnp.where(low_half, zero, q_pair))

    tri = (lax.broadcasted_iota(jnp.int32, (Q_BLOCK, LANES), 1)
           < lax.broadcasted_iota(jnp.int32, (Q_BLOCK, LANES), 0))

    def update_rows(x, r0, fn):
        return fn(x) if r0 == 0 else jnp.concatenate([x[:r0], fn(x[r0:])], axis=0)

    def on_diagonal(x, fill):
        masked = jnp.where(tri, x[:Q_BLOCK], fill)
        return masked if x.shape[0] == Q_BLOCK else jnp.concatenate([masked, x[Q_BLOCK:]], axis=0)

    def tile_step(start, carry, r0=0, diagonal=False):
        acc, laters = carry[0], carry[1:]
        k_pair = k_ref[0, pl.ds(start, Q_BLOCK), :]
        v_pair = v_ref[0, pl.ds(start, Q_BLOCK), :]
        zero_v = jnp.zeros_like(v_pair)
        v_bd = jnp.concatenate([jnp.where(low_half_k, v_pair, zero_v),
                                jnp.where(low_half_k, zero_v, v_pair)], axis=0)
        weights, new_laters = [], []
        for q_h, later in zip(q_heads, laters):
            z = _nt_dot(q_h[r0:] if r0 else q_h, k_pair)
            sp = jnp.maximum(z, 0.0) + jnp.log2(1.0 + jnp.exp2(-jnp.abs(z)))
            if diagonal:
                sp = on_diagonal(sp, 0.0)
            hi, lo = _split2(sp)
            r = _dot(jnp.concatenate([hi, lo], axis=1), uu_ref[...])
            after = r[:, :LANES] + (later[r0:] if r0 else later)
            a = jnp.exp2(z - sp - after)
            if diagonal:
                a = on_diagonal(a, 0.0)
            weights.append(a.astype(_MXU_DTYPE))
            new_laters.append(update_rows(later, r0, lambda part, r=r: part + r[:, LANES:]))
        pv = _dot(jnp.concatenate(weights, axis=1), v_bd)
        return (update_rows(acc, r0, lambda part: part + pv), *new_laters)

    zeros = jnp.zeros((rows, LANES), F32)
    carry = (zeros, zeros, zeros)
    for c in reversed(range(band)):
        carry = tile_step(pl.multiple_of(i * rows + c * Q_BLOCK, Q_BLOCK), carry, r0=c * Q_BLOCK, diagonal=True)

    def main_body(kg, carry):
        for u in range(SB_UNROLL):
            first_key = (n_tiles - band - 1 - (kg * SB_UNROLL + u)) * Q_BLOCK
            carry = tile_step(pl.multiple_of(first_key, Q_BLOCK), carry)
        return carry

    n_trips = (n_tiles - band) // SB_UNROLL

    def settled(laters):
        return jnp.min(jnp.minimum(*laters)) >= SB_UNDERFLOW_BITS

    def keep_going(state):
        kg, done = state
        return jnp.logical_and(kg < n_trips, jnp.logical_not(done))

    def save(carry):
        for ref, value in zip(state_scr, carry):
            ref[...] = value

    def main_step(state):
        kg, _ = state
        carry = main_body(kg, tuple(ref[...] for ref in state_scr))
        save(carry)
        return kg + 1, settled(carry[1:])

    save(carry)
    lax.while_loop(keep_going, main_step, (0, settled(carry[1:])))
    o_ref[0] = (state_scr[0][...] * g_ref[0]).astype(o_ref.dtype)


def _outproj_kernel(x_ref, on_ref, os_ref, w_ref, o_ref):
    o_ref[...] = (x_ref[...] + _dot(on_ref[...], w_ref[:NSA_WIDTH, :])
                  + _dot(os_ref[...], w_ref[NSA_WIDTH:, :]))


def _rope_tables(pos, reps):
    inv_freq = jnp.power(ROPE_THETA, -jnp.arange(0, ROPE_DIM, 2, dtype=F32) / ROPE_DIM)
    ang = pos.astype(F32)[:, None] * inv_freq[None, :]
    cos, sin = jnp.cos(ang), jnp.sin(ang)
    n = pos.shape[0]
    rest = HEAD_DIM - ROPE_DIM
    cos_h = jnp.concatenate([cos, cos, jnp.ones((n, rest), F32)], axis=1)
    sin_h = jnp.concatenate([-sin, sin, jnp.zeros((n, rest), F32)], axis=1)
    return jnp.tile(cos_h, (1, reps)), jnp.tile(sin_h, (1, reps))


def _const_spec(shape):
    return pl.BlockSpec(shape, lambda *_: (0,) * len(shape))


def kernel(x, norm_gain, w_in, q_norm_gain, k_norm_cmp, k_norm_slc, k_norm_win,
           cmp_k_pos, cmp_k_w1, cmp_k_b1, cmp_k_w2, cmp_v_pos, cmp_v_w1, cmp_v_b1, cmp_v_w2, w_out):
    B, S, DM = x.shape
    D, G, R = HEAD_DIM, NSA_KV_HEADS, NSA_GROUP
    mxu = _MXU_DTYPE
    n_tok = B * S
    tm = ROW_TILE
    n_sblk = S // tm
    nq = S // Q_BLOCK
    ncp = S // CMP_STRIDE
    n_slc = S // SLC_LEN
    n_cmp = (S - CMP_LEN) // CMP_STRIDE + 1
    assert S % tm == 0 and ncp % LANES == 0 and n_slc <= LANES - D and n_slc >= SLC_TOPN

    n_gl = NSA_HEADS * N_BRANCH
    gl0 = C_GN + n_gl
    w_cat = jnp.concatenate([w_in[:, :C_GN], w_in[:, gl0:], w_in[:, C_GN:gl0],
                             jnp.zeros((DM, LANES - n_gl), w_in.dtype)], axis=1).astype(mxu)
    assert w_cat.shape[1] == N_COLS
    pos = jnp.arange(S, dtype=jnp.int32)
    cos_t, sin_t = _rope_tables(pos, LANES // D)
    cmp_end = jnp.arange(ncp, dtype=jnp.int32) * CMP_STRIDE + (CMP_LEN - 1)
    cos_c, sin_c = _rope_tables(cmp_end, 1)
    lane_i = np.arange(LANES)
    bd = (lane_i[:, None] // D == lane_i[None, :] // D).astype(np.float32)
    bd = jnp.asarray(np.concatenate([bd, bd], axis=0), mxu)
    eg = np.zeros((LANES, N_BRANCH * NSA_WIDTH), np.float32)
    for hh in range(NSA_HEADS):
        for br in range(N_BRANCH):
            eg[hh * N_BRANCH + br, br * NSA_WIDTH + hh * D:br * NSA_WIDTH + (hh + 1) * D] = 1.0
    eg = jnp.asarray(np.concatenate([eg, eg], axis=0), mxu)
    perm = np.zeros((D, D), np.float32)
    for c in range(ROPE_HALF):
        perm[c + ROPE_HALF, c] = 1.0
        perm[c, c + ROPE_HALF] = 1.0
    perm = jnp.asarray(perm, mxu)
    cs = np.arange(ncp) * CMP_STRIDE
    ss = np.arange(LANES - D) * SLC_LEN
    ovl = np.clip(np.minimum(cs[None, :] + CMP_LEN, ss[:, None] + SLC_LEN)
                  - np.maximum(cs[None, :], ss[:, None]), 0, None).astype(np.float32) / CMP_LEN
    ovl[:, n_cmp:] = 0.0
    ovl[n_slc:, :] = 0.0
    ovl = jnp.asarray(ovl, mxu)
    sidx = np.arange(LANES)
    tri = (sidx[:, None] > sidx[None, :]).astype(np.float32)
    uu_half = np.concatenate([tri, np.ones((LANES, LANES), np.float32)], axis=1)
    uu = jnp.asarray(np.concatenate([uu_half, uu_half], axis=0), mxu)

    row = lambda v: v.reshape(1, -1).astype(F32)
    x2 = x.reshape(n_tok, DM)

    tok_spec = lambda w: pl.BlockSpec((tm, w), lambda t: (t, 0))
    head_spec = lambda nh, w: pl.BlockSpec((1, nh, tm, w), lambda t: (t // n_sblk, 0, t % n_sblk, 0))
    tab_spec = pl.BlockSpec((tm, LANES), lambda t: (t % n_sblk, 0))
    sds = jax.ShapeDtypeStruct
    outs = pl.pallas_call(
        functools.partial(_inproj_kernel, n_sblk=n_sblk),
        grid=(n_tok // tm,),
        in_specs=[tok_spec(DM), _const_spec((1, DM)), _const_spec((DM, N_COLS)),
                  _const_spec((1, NSA_WIDTH)), _const_spec((1, 2 * NSA_KV_WIDTH)),
                  _const_spec((2 * LANES, LANES)), tab_spec, tab_spec,
                  _const_spec((2 * LANES, N_BRANCH * NSA_WIDTH))],
        out_specs=[head_spec(NSA_HEADS, LANES), head_spec(G, D), head_spec(G, D),
                   head_spec(G, LANES), head_spec(G, LANES), head_spec(G, D), head_spec(G, LANES),
                   tok_spec(NSA_WIDTH), tok_spec(NSA_WIDTH), tok_spec(NSA_WIDTH),
                   tok_spec(SB_WIDTH), tok_spec(SB_WIDTH), tok_spec(SB_WIDTH), tok_spec(SB_WIDTH)],
        out_shape=[sds((B, NSA_HEADS, S, LANES), mxu), sds((B, G, S, D), F32), sds((B, G, S, D), F32),
                   sds((B, G, S, LANES), mxu), sds((B, G, S, LANES), mxu), sds((B, G, S, D), mxu),
                   sds((B, G, S, LANES), mxu),
                   sds((n_tok, NSA_WIDTH), F32), sds((n_tok, NSA_WIDTH), F32), sds((n_tok, NSA_WIDTH), F32),
                   sds((n_tok, SB_WIDTH), mxu), sds((n_tok, SB_WIDTH), mxu), sds((n_tok, SB_WIDTH), mxu),
                   sds((n_tok, SB_WIDTH), F32)],
        compiler_params=pltpu.CompilerParams(dimension_semantics=("parallel",), vmem_limit_bytes=VMEM_LIMIT),
        name="inproj",
    )(x2, row(norm_gain), w_cat, row(jnp.tile(q_norm_gain, NSA_HEADS)),
      row(jnp.concatenate([jnp.tile(k_norm_slc, G), jnp.tile(k_norm_win, G)])), bd, cos_t, sin_t, eg)
    (q_nsa, kc_raw, vc_raw, ks_aug, vs_aug, k_win, vw_aug, g_cmp, g_slc, g_win,
     q_sb, k_sb, v_sb, g_sb) = outs

    chunk_w = CMP_STRIDE * D
    chunks = lambda a: a.reshape(B * G, ncp, chunk_w)
    bg_spec = lambda r, w: pl.BlockSpec((1, r, w), lambda t: (t, 0, 0))
    k_cmp, v_cmp = pl.pallas_call(
        _compress_kernel,
        grid=(B * G,),
        in_specs=[bg_spec(ncp, chunk_w), bg_spec(ncp, chunk_w),
                  _const_spec((2, chunk_w)), _const_spec((2, chunk_w)),
                  _const_spec((CMP_LEN * D, D)), _const_spec((1, D)), _const_spec((D, D)),
                  _const_spec((CMP_LEN * D, D)), _const_spec((1, D)), _const_spec((D, LANES)),
                  _const_spec((1, D)), _const_spec((ncp, D)), _const_spec((ncp, D)), _const_spec((D, D))],
        out_specs=[bg_spec(ncp, D), bg_spec(ncp, LANES)],
        out_shape=[sds((B * G, ncp, D), mxu), sds((B * G, ncp, LANES), mxu)],
        compiler_params=pltpu.CompilerParams(dimension_semantics=("parallel",), vmem_limit_bytes=VMEM_LIMIT),
        name="compress",
    )(chunks(kc_raw), chunks(vc_raw), cmp_k_pos.reshape(2, chunk_w), cmp_v_pos.reshape(2, chunk_w),
      cmp_k_w1.astype(mxu), row(cmp_k_b1), cmp_k_w2.astype(mxu),
      cmp_v_w1.astype(mxu), row(cmp_v_b1), jnp.tile(cmp_v_w2, (1, LANES // D)).astype(mxu),
      row(k_norm_cmp), cos_c, sin_c, perm)
    k_cmp = k_cmp.reshape(B, G, ncp, D)
    v_cmp = v_cmp.reshape(B, G, ncp, LANES)

    k_gain = jnp.max(jnp.abs(jnp.stack([k_norm_cmp, k_norm_slc, k_norm_win])))
    logit_bound = jnp.max(jnp.abs(q_norm_gain)) * k_gain * (1.02 * D * SCALE * LOG2E)
    logit_bound = logit_bound.astype(F32).reshape(1)
    kv_spec = lambda r, w: pl.BlockSpec((1, G, r, w), lambda b, i: (b, 0, 0, 0))
    gate_spec = pl.BlockSpec((1, 2 * NSA_QT, NSA_WIDTH), lambda b, i: (b, i, 0))
    g3 = lambda a: a.reshape(B, S, NSA_WIDTH)
    o_nsa = pl.pallas_call(
        _nsa_kernel,
        grid=(B, S // (2 * NSA_QT)),
        in_specs=[pl.BlockSpec(memory_space=pltpu.SMEM),
                  pl.BlockSpec((1, NSA_HEADS, 2 * NSA_QT, LANES), lambda b, i: (b, 0, i, 0)),
                  kv_spec(ncp, D), kv_spec(ncp, LANES), kv_spec(S, LANES), kv_spec(S, LANES),
                  kv_spec(S, D), kv_spec(S, LANES), _const_spec((LANES - D, ncp)),
                  gate_spec, gate_spec, gate_spec],
        out_specs=gate_spec,
        out_shape=sds((B, S, NSA_WIDTH), mxu),
        scratch_shapes=[pltpu.VMEM((2 * G, R * NSA_QT, 2 * NSA_QT), mxu),
                        pltpu.VMEM((2 * G, R * NSA_QT, LANES), F32)],
        compiler_params=pltpu.CompilerParams(dimension_semantics=("parallel", "arbitrary"),
                                             vmem_limit_bytes=VMEM_LIMIT),
        name="nsa",
    )(logit_bound, q_nsa, k_cmp, v_cmp, ks_aug, vs_aug, k_win, vw_aug, ovl, g3(g_cmp), g3(g_slc), g3(g_win))

    sb3 = lambda a: a.reshape(B, S, SB_WIDTH)
    pair_q = pl.BlockSpec((1, SB_ROWS, LANES), lambda b, hp, i: (b, i, hp))
    pair_kv = pl.BlockSpec((1, S, LANES), lambda b, hp, i: (b, 0, hp))
    o_sb = pl.pallas_call(
        _sb_kernel,
        grid=(B, SB_WIDTH // LANES, S // SB_ROWS),
        in_specs=[pair_q, pair_kv, pair_kv, _const_spec((2 * LANES, 2 * LANES)), pair_q],
        out_specs=pair_q,
        out_shape=sds((B, S, SB_WIDTH), mxu),
        scratch_shapes=[pltpu.VMEM((SB_ROWS, LANES), F32)] * 3,
        compiler_params=pltpu.CompilerParams(dimension_semantics=("parallel", "parallel", "arbitrary"),
                                             vmem_limit_bytes=VMEM_LIMIT),
        name="stickbreak",
    )(sb3(q_sb), sb3(k_sb), sb3(v_sb), uu, sb3(g_sb))

    out = pl.pallas_call(
        _outproj_kernel,
        grid=(n_tok // tm,),
        in_specs=[tok_spec(DM), tok_spec(NSA_WIDTH), tok_spec(SB_WIDTH),
                  _const_spec((NSA_WIDTH + SB_WIDTH, DM))],
        out_specs=tok_spec(DM),
        out_shape=sds((n_tok, DM), x.dtype),
        compiler_params=pltpu.CompilerParams(dimension_semantics=("parallel",), vmem_limit_bytes=VMEM_LIMIT),
        name="outproj",
    )(x2, o_nsa.reshape(n_tok, NSA_WIDTH), o_sb.reshape(n_tok, SB_WIDTH), w_out.astype(mxu))
    return out.reshape(B, S, DM)
```

```python
import functools
import math

import numpy as np
import jax
import jax.numpy as jnp
from jax import lax
from jax.experimental import pallas as pl
from jax.experimental.pallas import tpu as pltpu

HEAD_DIM = 64
NSA_HEADS = 8
NSA_KV_HEADS = 2
NSA_GROUP = NSA_HEADS // NSA_KV_HEADS
SB_HEADS = 8
NSA_WIDTH = NSA_HEADS * HEAD_DIM
SB_WIDTH = SB_HEADS * HEAD_DIM
NSA_KV_WIDTH = NSA_KV_HEADS * HEAD_DIM
N_BRANCH = 3
CMP_LEN = 32
CMP_STRIDE = 16
SLC_LEN = 64
SLC_TOPN = 16
WINDOW = 512
Q_BLOCK = 128
ROPE_DIM = HEAD_DIM // 4
ROPE_HALF = ROPE_DIM // 2
ROPE_THETA = 500000.0
EPS = 1e-6
FORCE_BONUS = 1.0e4
SCALE = 1.0 / math.sqrt(HEAD_DIM)
LOG2E = math.log2(math.e)

LANES = 128
SUBLANES = 8
MASK_NEG = -1.0e30
M_INIT = -3.0e38
ROW_TILE = 512
NSA_QT = 256
SB_ROWS = 512
SB_UNROLL = 2
MAX_STATIC_SHIFT = 60.0
SB_UNDERFLOW_BITS = 160.0
VMEM_LIMIT = 56 * 1024 * 1024

_MXU_DTYPE = jnp.bfloat16
F32 = jnp.float32

C_Q = 0
C_KC = 512
C_KS = 768
C_KW = 1024
C_GN = 1280
C_QSB = 1792
C_KSB = 2304
C_VSB = 2816
C_GSB = 3328
C_GL = 3840
N_COLS = 3968


def _nt_dot(a, b):
    return lax.dot_general(a, b, (((1,), (1,)), ((), ())), preferred_element_type=F32)


def _dot(a, b):
    return jnp.dot(a, b, preferred_element_type=F32)


def _split2(v):
    hi = v.astype(_MXU_DTYPE)
    lo = (v - hi.astype(F32)).astype(_MXU_DTYPE)
    return hi, lo


def _split3(v):
    hi = v.astype(_MXU_DTYPE)
    r1 = v - hi.astype(F32)
    mid = r1.astype(_MXU_DTYPE)
    lo = (r1 - mid.astype(F32)).astype(_MXU_DTYPE)
    return hi, mid, lo


def _inproj_kernel(x_ref, ng_ref, w_ref, qg_ref, kg_ref, bd_ref, cos_ref, sin_ref, eg_ref,
                   q_ref, kc_ref, vc_ref, ksa_ref, vsa_ref, kw_ref, vwa_ref,
                   gc_ref, gs_ref, gw_ref, qsb_ref, ksb_ref, vsb_ref, gsb_ref, *, n_sblk):
    tm = x_ref.shape[0]
    x = x_ref[...]
    ms = jnp.mean(x * x, axis=-1, keepdims=True)
    h = (x * lax.rsqrt(ms + EPS) * ng_ref[...]).astype(_MXU_DTYPE)

    def proj(lo, width):
        return _dot(h, w_ref[:, lo:lo + width])

    lane = lax.broadcasted_iota(jnp.int32, (tm, LANES), 1)
    low_half = lane < HEAD_DIM

    def head_norm_rope(y, gain):
        width = y.shape[1]
        rep = width // LANES
        hi, lo = _split2(y * y)
        ssum = jnp.concatenate(
            [_dot(jnp.concatenate([hi[:, c:c + LANES], lo[:, c:c + LANES]], axis=1), bd_ref[...])
             for c in range(0, width, LANES)], axis=1)
        yn = y * lax.rsqrt(ssum * (1.0 / HEAD_DIM) + EPS) * gain
        cos = jnp.concatenate([cos_ref[...]] * rep, axis=1) if rep > 1 else cos_ref[...]
        sin = jnp.concatenate([sin_ref[...]] * rep, axis=1) if rep > 1 else sin_ref[...]
        fwd = pltpu.roll(yn, ROPE_HALF, axis=1)
        bwd = pltpu.roll(yn, width - ROPE_HALF, axis=1)
        lane_w = lax.broadcasted_iota(jnp.int32, (tm, width), 1)
        partner = jnp.where((lane_w & (HEAD_DIM - 1)) < ROPE_HALF, bwd, fwd)
        return yn * cos + partner * sin

    def head_pair(slab, p):
        chunk = slab[:, p * LANES:(p + 1) * LANES]
        return chunk, pltpu.roll(chunk, HEAD_DIM, axis=1)

    qn = head_norm_rope(proj(C_Q, NSA_WIDTH), qg_ref[...]) * (SCALE * LOG2E)
    for p in range(NSA_HEADS // 2):
        ev, od = head_pair(qn, p)
        q_ref[0, 2 * p] = jnp.where(low_half, ev, 0.0).astype(q_ref.dtype)
        q_ref[0, 2 * p + 1] = jnp.where(low_half, od, 0.0).astype(q_ref.dtype)

    kv_cmp = proj(C_KC, 2 * NSA_KV_WIDTH)
    kv_slc = proj(C_KS, 2 * NSA_KV_WIDTH)
    kv_win = proj(C_KW, 2 * NSA_KV_WIDTH)

    for ref, p in ((kc_ref, 0), (vc_ref, 1)):
        ev, od = head_pair(kv_cmp, p)
        ref[0, 0] = ev[:, :HEAD_DIM]
        ref[0, 1] = od[:, :HEAD_DIM]

    sblk = lax.rem(pl.program_id(0), n_sblk)
    row = lax.broadcasted_iota(jnp.int32, (tm, LANES), 0)
    key_blk = (sblk * tm + row) >> int(math.log2(SLC_LEN))
    onehot = jnp.where(lane - HEAD_DIM == key_blk, 1.0, 0.0)
    k_sw = head_norm_rope(jnp.concatenate([kv_slc[:, :NSA_KV_WIDTH], kv_win[:, :NSA_KV_WIDTH]], axis=1),
                          kg_ref[...])
    ev, od = head_pair(k_sw, 0)
    ksa_ref[0, 0] = jnp.where(low_half, ev, onehot).astype(ksa_ref.dtype)
    ksa_ref[0, 1] = jnp.where(low_half, od, onehot).astype(ksa_ref.dtype)
    ev, od = head_pair(kv_slc, 1)
    vsa_ref[0, 0] = jnp.where(low_half, ev, 1.0).astype(vsa_ref.dtype)
    vsa_ref[0, 1] = jnp.where(low_half, od, 1.0).astype(vsa_ref.dtype)

    one_lane = jnp.where(lane == HEAD_DIM, 1.0, 0.0)
    ev, od = head_pair(k_sw, 1)
    kw_ref[0, 0] = jnp.where(low_half, ev, one_lane).astype(kw_ref.dtype)
    kw_ref[0, 1] = jnp.where(low_half, od, one_lane).astype(kw_ref.dtype)
    ev, od = head_pair(kv_win, 1)
    vwa_ref[0, 0] = jnp.where(low_half, ev, 1.0).astype(vwa_ref.dtype)
    vwa_ref[0, 1] = jnp.where(low_half, od, 1.0).astype(vwa_ref.dtype)

    gn = proj(C_GN, NSA_WIDTH)
    silu_n = gn * jax.nn.sigmoid(gn)
    gl_split = jnp.concatenate(_split2(proj(C_GL, LANES)), axis=1)
    for br, ref in enumerate((gc_ref, gs_ref, gw_ref)):
        ref[...] = jax.nn.sigmoid(_dot(gl_split, eg_ref[:, br * NSA_WIDTH:(br + 1) * NSA_WIDTH])) * silu_n

    qsb_ref[...] = (proj(C_QSB, SB_WIDTH) * (SCALE * LOG2E)).astype(qsb_ref.dtype)
    ksb_ref[...] = proj(C_KSB, SB_WIDTH).astype(ksb_ref.dtype)
    vsb_ref[...] = proj(C_VSB, SB_WIDTH).astype(vsb_ref.dtype)
    gsb = proj(C_GSB, SB_WIDTH)
    gsb_ref[...] = gsb * jax.nn.sigmoid(gsb)


def _compress_kernel(kc_ref, vc_ref, posk_ref, posv_ref, w1k_ref, b1k_ref, w2k_ref,
                     w1v_ref, b1v_ref, w2v_ref, kg_ref, cos_ref, sin_ref, perm_ref,
                     kcmp_ref, vcmp_ref):
    half = CMP_STRIDE * HEAD_DIM

    def phi(c_ref, pos_ref, w1_ref, b1_ref, w2_ref):
        c = c_ref[0]
        n = c.shape[0]
        top = _dot((c + pos_ref[0:1, :]).astype(_MXU_DTYPE), w1_ref[:half, :])
        bot = _dot((c + pos_ref[1:2, :]).astype(_MXU_DTYPE), w1_ref[half:, :])
        hid = top + pltpu.roll(bot, n - 1, axis=0) + b1_ref[...]
        return _dot((hid * jax.nn.sigmoid(hid)).astype(_MXU_DTYPE), w2_ref[...])

    k = phi(kc_ref, posk_ref, w1k_ref, b1k_ref, w2k_ref)
    ms = jnp.sum(k * k, axis=-1, keepdims=True) * (1.0 / HEAD_DIM)
    kn = k * lax.rsqrt(ms + EPS) * kg_ref[...]
    hi, lo = _split2(kn)
    partner = _dot(hi, perm_ref[...]) + _dot(lo, perm_ref[...])
    one_lane = jnp.where(lax.broadcasted_iota(jnp.int32, k.shape, 1) == HEAD_DIM, 1.0, 0.0)
    kcmp_ref[0] = (kn * cos_ref[...] + partner * sin_ref[...] + one_lane).astype(kcmp_ref.dtype)
    vcmp_ref[0] = phi(vc_ref, posv_ref, w1v_ref, b1v_ref, w2v_ref).astype(vcmp_ref.dtype)


def _nsa_kernel(bound_ref, q_ref, kc_ref, vc_ref, ksa_ref, vsa_ref, kw_ref, vwa_ref, ovl_ref,
                gc_ref, gs_ref, gw_ref, o_ref, p_scr, acc_scr):
    qt = NSA_QT
    kw = 2 * qt
    assert q_ref.shape[2] == kw
    chains = [(g, half) for half in range(2) for g in range(kc_ref.shape[1])]
    groups = range(len(chains))
    kv = [g for g, _ in chains]
    blk = [2 * pl.program_id(1) + half for _, half in chains]
    rq = NSA_GROUP * qt
    n_blk_lanes = LANES - HEAD_DIM
    q_pads = [q_ref[0, g * NSA_GROUP:(g + 1) * NSA_GROUP, half * qt:(half + 1) * qt].reshape(rq, LANES)
              for g, half in chains]
    heads = lambda x: jnp.concatenate([x] * NSA_GROUP, axis=0)
    t_tok = lax.broadcasted_iota(jnp.int32, (qt, 1), 0)
    rel_w = heads(t_tok - lax.broadcasted_iota(jnp.int32, (qt, kw), 1))
    rel = rel_w[:, :qt]
    lane_q = lax.broadcasted_iota(jnp.int32, (qt, LANES), 1)
    lane_r = heads(lane_q)
    low_half = lane_q < HEAD_DIM

    logit_bound = bound_ref[0]
    static_shift = logit_bound <= MAX_STATIC_SHIFT
    bound_rows = lambda: jnp.full((rq, 1), logit_bound, F32)
    row_max = lambda s: jnp.max(s, axis=-1, keepdims=True)

    def shift_lane(shift):
        return jnp.where(lane_r == HEAD_DIM, -shift, 0.0).astype(q_pads[0].dtype)

    def window_logits(g, q_lhs):
        i = blk[g]
        s_parts, v_parts = [], []
        for back in range(WINDOW // qt + 1):
            start = pl.multiple_of(jnp.maximum(i - back, 0) * qt, qt)
            s = _nt_dot(q_lhs, kw_ref[0, kv[g], pl.ds(start, qt), :])
            v = vwa_ref[0, kv[g], pl.ds(start, qt), :]
            if back == 0:
                s = jnp.where(rel >= 0, s, MASK_NEG)
            elif (back + 1) * qt <= WINDOW:
                v = v * jnp.where(i >= back, 1.0, 0.0).astype(v.dtype)
            else:
                s = jnp.where(rel < (WINDOW - back * qt) - jnp.where(i >= back, 0, WINDOW), s, MASK_NEG)
            s_parts.append(s)
            v_parts.append(v)
        return jnp.concatenate(s_parts, axis=1), jnp.concatenate(v_parts, axis=0)

    def cmp_logits(g, q_lhs):
        kc = kc_ref[0, kv[g]]
        cmp_end = lax.broadcasted_iota(jnp.int32, (rq, kc.shape[0]), 1) * CMP_STRIDE + (CMP_LEN - 1)
        return jnp.where(cmp_end <= blk[g] * qt + heads(t_tok), _nt_dot(q_lhs, kc), MASK_NEG)

    pre_shifts = lax.cond(
        static_shift,
        lambda: tuple(shift_lane(logit_bound) for _ in groups for _ in range(2)),
        lambda: tuple(shift_lane(m) for g in groups
                      for m in (row_max(window_logits(g, q_pads[g])[0]), row_max(cmp_logits(g, q_pads[g])))))

    ovl = ovl_ref[...]
    j_idx = lax.broadcasted_iota(jnp.int32, (n_blk_lanes, qt), 0)
    row_in_grp = lax.broadcasted_iota(jnp.int32, (SUBLANES, qt), 0)

    def selected_blocks(p, i):
        blk_t = (i * qt + lax.broadcasted_iota(jnp.int32, (n_blk_lanes, qt), 1)) >> int(math.log2(SLC_LEN))
        slc_valid = j_idx <= blk_t
        forced = (j_idx == 0) | (j_idx == blk_t) | (j_idx == blk_t - 1)
        p_sum = p[0:qt]
        for r in range(1, NSA_GROUP):
            p_sum = p_sum + p[r * qt:(r + 1) * qt]
        p_slc = sum(_nt_dot(ovl, part) for part in _split3(p_sum))
        score = jnp.where(slc_valid, p_slc + jnp.where(forced, FORCE_BONUS, 0.0), -jnp.inf)
        n_grp = n_blk_lanes // SUBLANES
        grp_rows = [score[c * SUBLANES:(c + 1) * SUBLANES] for c in range(n_grp)]
        grp_rank = [jnp.zeros((SUBLANES, qt), F32) for _ in range(n_grp)]
        for ii in range(n_blk_lanes):
            s_i = score[ii:ii + 1, :]
            for c in range(n_grp):
                rows = grp_rows[c]
                if c * SUBLANES > ii:
                    beats = jnp.where(s_i >= rows, 1.0, 0.0)
                elif (c + 1) * SUBLANES - 1 <= ii:
                    beats = jnp.where(s_i > rows, 1.0, 0.0)
                else:
                    beats = jnp.where(row_in_grp > ii - c * SUBLANES,
                                      jnp.where(s_i >= rows, 1.0, 0.0), jnp.where(s_i > rows, 1.0, 0.0))
                grp_rank[c] = grp_rank[c] + beats
        rank = jnp.concatenate(grp_rank, axis=0)
        sel_t = jnp.where(slc_valid, jnp.where(rank < SLC_TOPN, 1.0, 0.0), 0.0)
        return heads(jnp.concatenate([jnp.ones((HEAD_DIM, qt), F32), sel_t], axis=0).T)

    acc_w, o_cmp, sel_rows = [], [], []
    for g in groups:
        s_win, v_win = window_logits(g, q_pads[g] + pre_shifts[2 * g])
        acc_w.append(_dot(jnp.exp2(s_win).astype(_MXU_DTYPE), v_win))
        p = jnp.exp2(cmp_logits(g, q_pads[g] + pre_shifts[2 * g + 1]))
        p = p / jnp.maximum(jnp.sum(p, axis=-1, keepdims=True), 1e-30)
        o_cmp.append(_dot(p.astype(_MXU_DTYPE), vc_ref[0, kv[g]]))
        sel_rows.append(selected_blocks(p, blk[g]))

    n_kt = pl.program_id(1) + 1

    def augmented_q(g, shift):
        bias = jnp.where(lane_r >= HEAD_DIM, jnp.where(sel_rows[g] > 0.5, -shift, MASK_NEG), 0.0)
        return q_pads[g] + bias.astype(q_pads[g].dtype)

    def slc_logits(g, q_aug, kt):
        s = _nt_dot(q_aug, ksa_ref[0, kv[g], pl.ds(pl.multiple_of(kt * kw, kw), kw), :])
        return jnp.where(rel_w >= kt * kw - blk[g] * qt, s, MASK_NEG)

    def slc_row_max(g):
        q_aug = augmented_q(g, 0.0)
        return lax.fori_loop(0, n_kt, lambda kt, m_run: jnp.maximum(m_run, row_max(slc_logits(g, q_aug, kt))),
                             jnp.full((rq, 1), M_INIT, F32))

    slc_shifts = lax.cond(static_shift, lambda: tuple(bound_rows() for _ in groups),
                          lambda: tuple(slc_row_max(g) for g in groups))
    q_augs = [augmented_q(g, slc_shifts[g]) for g in groups]

    def slc_weights(g, kt):
        return jnp.exp2(slc_logits(g, q_augs[g], kt)).astype(_MXU_DTYPE)

    def slc_weighted_values(g, kt):
        return _dot(p_scr[g], vsa_ref[0, kv[g], pl.ds(pl.multiple_of(kt * kw, kw), kw), :])

    for g in groups:
        p_scr[g] = slc_weights(g, 0)
        acc_scr[g] = jnp.zeros((rq, LANES), F32)

    def slc_body(kt, _):
        for g in groups:
            acc_scr[g] += slc_weighted_values(g, kt)
            p_scr[g] = slc_weights(g, kt + 1)
        return 0

    lax.fori_loop(0, n_kt - 1, slc_body, 0)
    acc_s = [acc_scr[g] + slc_weighted_values(g, n_kt - 1) for g in groups]

    head = lambda a, r: a[r * qt:(r + 1) * qt]

    def token_major(o):
        return jnp.concatenate([jnp.where(low_half, head(o, r), head(o, r + 1))
                                for r in range(0, NSA_GROUP, 2)], axis=1)

    def token_major_normalised(acc):
        inv = 1.0 / jnp.where(lane_r >= HEAD_DIM, acc, 1.0)
        pairs = []
        for r in range(0, NSA_GROUP, 2):
            even = head(acc, r) * pltpu.roll(head(inv, r), HEAD_DIM, axis=1)
            odd = pltpu.roll(head(acc, r + 1), HEAD_DIM, axis=1) * head(inv, r + 1)
            pairs.append(jnp.where(low_half, even, odd))
        return jnp.concatenate(pairs, axis=1)

    gw_cols = NSA_GROUP * HEAD_DIM
    for g, (kv_head, half) in enumerate(chains):
        rows = slice(half * qt, (half + 1) * qt)
        cols = slice(kv_head * gw_cols, (kv_head + 1) * gw_cols)
        out = (gc_ref[0, rows, cols] * token_major(o_cmp[g])
               + gs_ref[0, rows, cols] * token_major_normalised(acc_s[g])
               + gw_ref[0, rows, cols] * token_major_normalised(acc_w[g]))
        o_ref[0, rows, cols] = out.astype(o_ref.dtype)


def _sb_kernel(q_ref, k_ref, v_ref, uu_ref, g_ref, o_ref, *state_scr):
    i = pl.program_id(2)
    rows = q_ref.shape[1]
    band = rows // Q_BLOCK
    n_tiles = (i + 1) * band
    low_half = lax.broadcasted_iota(jnp.int32, (rows, LANES), 1) < HEAD_DIM
    low_half_k = lax.broadcasted_iota(jnp.int32, (Q_BLOCK, LANES), 1) < HEAD_DIM
    q_pair = q_ref[0]
    zero = jnp.zeros_like(q_pair)
    q_heads = (jnp.where(low_half, q_pair, zero), jnp.where(low_half, zero, q_pair))

    tri = (lax.broadcasted_iota(jnp.int32, (Q_BLOCK, LANES), 1)
           < lax.broadcasted_iota(jnp.int32, (Q_BLOCK, LANES), 0))

    def update_rows(x, r0, fn):
        return fn(x) if r0 == 0 else jnp.concatenate([x[:r0], fn(x[r0:])], axis=0)

    def on_diagonal(x, fill):
        masked = jnp.where(tri, x[:Q_BLOCK], fill)
        return masked if x.shape[0] == Q_BLOCK else jnp.concatenate([masked, x[Q_BLOCK:]], axis=0)

    def tile_step(start, carry, r0=0, diagonal=False):
        acc, laters = carry[0], carry[1:]
        k_pair = k_ref[0, pl.ds(start, Q_BLOCK), :]
        v_pair = v_ref[0, pl.ds(start, Q_BLOCK), :]
        zero_v = jnp.zeros_like(v_pair)
        v_bd = jnp.concatenate([jnp.where(low_half_k, v_pair, zero_v),
                                jnp.where(low_half_k, zero_v, v_pair)], axis=0)
        weights, new_laters = [], []
        for q_h, later in zip(q_heads, laters):
            z = _nt_dot(q_h[r0:] if r0 else q_h, k_pair)
            sp = jnp.maximum(z, 0.0) + jnp.log2(1.0 + jnp.exp2(-jnp.abs(z)))
            if diagonal:
                sp = on_diagonal(sp, 0.0)
            hi, lo = _split2(sp)
            r = _dot(jnp.concatenate([hi, lo], axis=1), uu_ref[...])
            after = r[:, :LANES] + (later[r0:] if r0 else later)
            a = jnp.exp2(z - sp - after)
            if diagonal:
                a = on_diagonal(a, 0.0)
            weights.append(a.astype(_MXU_DTYPE))
            new_laters.append(update_rows(later, r0, lambda part, r=r: part + r[:, LANES:]))
        pv = _dot(jnp.concatenate(weights, axis=1), v_bd)
        return (update_rows(acc, r0, lambda part: part + pv), *new_laters)

    zeros = jnp.zeros((rows, LANES), F32)
    carry = (zeros, zeros, zeros)
    for c in reversed(range(band)):
        carry = tile_step(pl.multiple_of(i * rows + c * Q_BLOCK, Q_BLOCK), carry, r0=c * Q_BLOCK, diagonal=True)

    def main_body(kg, carry):
        for u in range(SB_UNROLL):
            first_key = (n_tiles - band - 1 - (kg * SB_UNROLL + u)) * Q_BLOCK
            carry = tile_step(pl.multiple_of(first_key, Q_BLOCK), carry)
        return carry

    n_trips = (n_tiles - band) // SB_UNROLL

    def settled(laters):
        return jnp.min(jnp.minimum(*laters)) >= SB_UNDERFLOW_BITS

    def keep_going(state):
        kg, done = state
        return jnp.logical_and(kg < n_trips, jnp.logical_not(done))

    def save(carry):
        for ref, value in zip(state_scr, carry):
            ref[...] = value

    def main_step(state):
        kg, _ = state
        carry = main_body(kg, tuple(ref[...] for ref in state_scr))
        save(carry)
        return kg + 1, settled(carry[1:])

    save(carry)
    lax.while_loop(keep_going, main_step, (0, settled(carry[1:])))
    o_ref[0] = (state_scr[0][...] * g_ref[0]).astype(o_ref.dtype)


def _outproj_kernel(x_ref, on_ref, os_ref, w_ref, o_ref):
    o_ref[...] = (x_ref[...] + _dot(on_ref[...], w_ref[:NSA_WIDTH, :])
                  + _dot(os_ref[...], w_ref[NSA_WIDTH:, :]))


def _rope_tables(pos, reps):
    inv_freq = jnp.power(ROPE_THETA, -jnp.arange(0, ROPE_DIM, 2, dtype=F32) / ROPE_DIM)
    ang = pos.astype(F32)[:, None] * inv_freq[None, :]
    cos, sin = jnp.cos(ang), jnp.sin(ang)
    n = pos.shape[0]
    rest = HEAD_DIM - ROPE_DIM
    cos_h = jnp.concatenate([cos, cos, jnp.ones((n, rest), F32)], axis=1)
    sin_h = jnp.concatenate([-sin, sin, jnp.zeros((n, rest), F32)], axis=1)
    return jnp.tile(cos_h, (1, reps)), jnp.tile(sin_h, (1, reps))


def _const_spec(shape):
    return pl.BlockSpec(shape, lambda *_: (0,) * len(shape))


def kernel(x, norm_gain, w_in, q_norm_gain, k_norm_cmp, k_norm_slc, k_norm_win,
           cmp_k_pos, cmp_k_w1, cmp_k_b1, cmp_k_w2, cmp_v_pos, cmp_v_w1, cmp_v_b1, cmp_v_w2, w_out):
    B, S, DM = x.shape
    D, G, R = HEAD_DIM, NSA_KV_HEADS, NSA_GROUP
    mxu = _MXU_DTYPE
    n_tok = B * S
    tm = ROW_TILE
    n_sblk = S // tm
    nq = S // Q_BLOCK
    ncp = S // CMP_STRIDE
    n_slc = S // SLC_LEN
    n_cmp = (S - CMP_LEN) // CMP_STRIDE + 1
    assert S % tm == 0 and ncp % LANES == 0 and n_slc <= LANES - D and n_slc >= SLC_TOPN

    n_gl = NSA_HEADS * N_BRANCH
    gl0 = C_GN + n_gl
    w_cat = jnp.concatenate([w_in[:, :C_GN], w_in[:, gl0:], w_in[:, C_GN:gl0],
                             jnp.zeros((DM, LANES - n_gl), w_in.dtype)], axis=1).astype(mxu)
    assert w_cat.shape[1] == N_COLS
    pos = jnp.arange(S, dtype=jnp.int32)
    cos_t, sin_t = _rope_tables(pos, LANES // D)
    cmp_end = jnp.arange(ncp, dtype=jnp.int32) * CMP_STRIDE + (CMP_LEN - 1)
    cos_c, sin_c = _rope_tables(cmp_end, LANES // D)
    lane_i = np.arange(LANES)
    bd = (lane_i[:, None] // D == lane_i[None, :] // D).astype(np.float32)
    bd = jnp.asarray(np.concatenate([bd, bd], axis=0), mxu)
    eg = np.zeros((LANES, N_BRANCH * NSA_WIDTH), np.float32)
    for hh in range(NSA_HEADS):
        for br in range(N_BRANCH):
            eg[hh * N_BRANCH + br, br * NSA_WIDTH + hh * D:br * NSA_WIDTH + (hh + 1) * D] = 1.0
    eg = jnp.asarray(np.concatenate([eg, eg], axis=0), mxu)
    perm = np.zeros((LANES, LANES), np.float32)
    for c in range(ROPE_HALF):
        perm[c + ROPE_HALF, c] = 1.0
        perm[c, c + ROPE_HALF] = 1.0
    perm = jnp.asarray(perm, mxu)
    cs = np.arange(ncp) * CMP_STRIDE
    ss = np.arange(LANES - D) * SLC_LEN
    ovl = np.clip(np.minimum(cs[None, :] + CMP_LEN, ss[:, None] + SLC_LEN)
                  - np.maximum(cs[None, :], ss[:, None]), 0, None).astype(np.float32) / CMP_LEN
    ovl[:, n_cmp:] = 0.0
    ovl[n_slc:, :] = 0.0
    ovl = jnp.asarray(ovl, mxu)
    sidx = np.arange(LANES)
    tri = (sidx[:, None] > sidx[None, :]).astype(np.float32)
    uu_half = np.concatenate([tri, np.ones((LANES, LANES), np.float32)], axis=1)
    uu = jnp.asarray(np.concatenate([uu_half, uu_half], axis=0), mxu)

    row = lambda v: v.reshape(1, -1).astype(F32)
    x2 = x.reshape(n_tok, DM)

    tok_spec = lambda w: pl.BlockSpec((tm, w), lambda t: (t, 0))
    head_spec = lambda nh, w: pl.BlockSpec((1, nh, tm, w), lambda t: (t // n_sblk, 0, t % n_sblk, 0))
    tab_spec = pl.BlockSpec((tm, LANES), lambda t: (t % n_sblk, 0))
    sds = jax.ShapeDtypeStruct
    outs = pl.pallas_call(
        functools.partial(_inproj_kernel, n_sblk=n_sblk),
        grid=(n_tok // tm,),
        in_specs=[tok_spec(DM), _const_spec((1, DM)), _const_spec((DM, N_COLS)),
                  _const_spec((1, NSA_WIDTH)), _const_spec((1, 2 * NSA_KV_WIDTH)),
                  _const_spec((2 * LANES, LANES)), tab_spec, tab_spec,
                  _const_spec((2 * LANES, N_BRANCH * NSA_WIDTH))],
        out_specs=[head_spec(NSA_HEADS, LANES), head_spec(G, D), head_spec(G, D),
                   head_spec(G, LANES), head_spec(G, LANES), head_spec(G, LANES), head_spec(G, LANES),
                   tok_spec(NSA_WIDTH), tok_spec(NSA_WIDTH), tok_spec(NSA_WIDTH),
                   tok_spec(SB_WIDTH), tok_spec(SB_WIDTH), tok_spec(SB_WIDTH), tok_spec(SB_WIDTH)],
        out_shape=[sds((B, NSA_HEADS, S, LANES), mxu), sds((B, G, S, D), F32), sds((B, G, S, D), F32),
                   sds((B, G, S, LANES), mxu), sds((B, G, S, LANES), mxu), sds((B, G, S, LANES), mxu),
                   sds((B, G, S, LANES), mxu),
                   sds((n_tok, NSA_WIDTH), F32), sds((n_tok, NSA_WIDTH), F32), sds((n_tok, NSA_WIDTH), F32),
                   sds((n_tok, SB_WIDTH), mxu), sds((n_tok, SB_WIDTH), mxu), sds((n_tok, SB_WIDTH), mxu),
                   sds((n_tok, SB_WIDTH), F32)],
        compiler_params=pltpu.CompilerParams(dimension_semantics=("parallel",), vmem_limit_bytes=VMEM_LIMIT),
        name="inproj",
    )(x2, row(norm_gain), w_cat, row(jnp.tile(q_norm_gain, NSA_HEADS)),
      row(jnp.concatenate([jnp.tile(k_norm_slc, G), jnp.tile(k_norm_win, G)])), bd, cos_t, sin_t, eg)
    (q_nsa, kc_raw, vc_raw, ks_aug, vs_aug, k_win, vw_aug, g_cmp, g_slc, g_win,
     q_sb, k_sb, v_sb, g_sb) = outs

    chunk_w = CMP_STRIDE * D
    chunks = lambda a: a.reshape(B * G, ncp, chunk_w)
    bg_spec = lambda r, w: pl.BlockSpec((1, r, w), lambda t: (t, 0, 0))
    k_cmp, v_cmp = pl.pallas_call(
        _compress_kernel,
        grid=(B * G,),
        in_specs=[bg_spec(ncp, chunk_w), bg_spec(ncp, chunk_w),
                  _const_spec((2, chunk_w)), _const_spec((2, chunk_w)),
                  _const_spec((CMP_LEN * D, D)), _const_spec((1, D)), _const_spec((D, LANES)),
                  _const_spec((CMP_LEN * D, D)), _const_spec((1, D)), _const_spec((D, LANES)),
                  _const_spec((1, LANES)), _const_spec((ncp, LANES)), _const_spec((ncp, LANES)),
                  _const_spec((LANES, LANES))],
        out_specs=[bg_spec(ncp, LANES), bg_spec(ncp, LANES)],
        out_shape=[sds((B * G, ncp, LANES), mxu), sds((B * G, ncp, LANES), mxu)],
        compiler_params=pltpu.CompilerParams(dimension_semantics=("parallel",), vmem_limit_bytes=VMEM_LIMIT),
        name="compress",
    )(chunks(kc_raw), chunks(vc_raw), cmp_k_pos.reshape(2, chunk_w), cmp_v_pos.reshape(2, chunk_w),
      cmp_k_w1.astype(mxu), row(cmp_k_b1), jnp.pad(cmp_k_w2, ((0, 0), (0, LANES - D))).astype(mxu),
      cmp_v_w1.astype(mxu), row(cmp_v_b1), jnp.tile(cmp_v_w2, (1, LANES // D)).astype(mxu),
      row(jnp.pad(k_norm_cmp, (0, LANES - D))), cos_c, sin_c, perm)
    k_cmp = k_cmp.reshape(B, G, ncp, LANES)
    v_cmp = v_cmp.reshape(B, G, ncp, LANES)

    k_gain = jnp.max(jnp.abs(jnp.stack([k_norm_cmp, k_norm_slc, k_norm_win])))
    logit_bound = jnp.max(jnp.abs(q_norm_gain)) * k_gain * (1.02 * D * SCALE * LOG2E)
    logit_bound = logit_bound.astype(F32).reshape(1)
    kv_spec = lambda r, w: pl.BlockSpec((1, G, r, w), lambda b, i: (b, 0, 0, 0))
    gate_spec = pl.BlockSpec((1, 2 * NSA_QT, NSA_WIDTH), lambda b, i: (b, i, 0))
    g3 = lambda a: a.reshape(B, S, NSA_WIDTH)
    o_nsa = pl.pallas_call(
        _nsa_kernel,
        grid=(B, S // (2 * NSA_QT)),
        in_specs=[pl.BlockSpec(memory_space=pltpu.SMEM),
                  pl.BlockSpec((1, NSA_HEADS, 2 * NSA_QT, LANES), lambda b, i: (b, 0, i, 0)),
                  kv_spec(ncp, LANES), kv_spec(ncp, LANES), kv_spec(S, LANES), kv_spec(S, LANES),
                  kv_spec(S, LANES), kv_spec(S, LANES), _const_spec((LANES - D, ncp)),
                  gate_spec, gate_spec, gate_spec],
        out_specs=gate_spec,
        out_shape=sds((B, S, NSA_WIDTH), mxu),
        scratch_shapes=[pltpu.VMEM((2 * G, R * NSA_QT, 2 * NSA_QT), mxu),
                        pltpu.VMEM((2 * G, R * NSA_QT, LANES), F32)],
        compiler_params=pltpu.CompilerParams(dimension_semantics=("parallel", "arbitrary"),
                                             vmem_limit_bytes=VMEM_LIMIT),
        name="nsa",
    )(logit_bound, q_nsa, k_cmp, v_cmp, ks_aug, vs_aug, k_win, vw_aug, ovl, g3(g_cmp), g3(g_slc), g3(g_win))

    sb3 = lambda a: a.reshape(B, S, SB_WIDTH)
    pair_q = pl.BlockSpec((1, SB_ROWS, LANES), lambda b, hp, i: (b, i, hp))
    pair_kv = pl.BlockSpec((1, S, LANES), lambda b, hp, i: (b, 0, hp))
    o_sb = pl.pallas_call(
        _sb_kernel,
        grid=(B, SB_WIDTH // LANES, S // SB_ROWS),
        in_specs=[pair_q, pair_kv, pair_kv, _const_spec((2 * LANES, 2 * LANES)), pair_q],
        out_specs=pair_q,
        out_shape=sds((B, S, SB_WIDTH), mxu),
        scratch_shapes=[pltpu.VMEM((SB_ROWS, LANES), F32)] * 3,
        compiler_params=pltpu.CompilerParams(dimension_semantics=("parallel", "parallel", "arbitrary"),
                                             vmem_limit_bytes=VMEM_LIMIT),
        name="stickbreak",
    )(sb3(q_sb), sb3(k_sb), sb3(v_sb), uu, sb3(g_sb))

    out = pl.pallas_call(
        _outproj_kernel,
        grid=(n_tok // tm,),
        in_specs=[tok_spec(DM), tok_spec(NSA_WIDTH), tok_spec(SB_WIDTH),
                  _const_spec((NSA_WIDTH + SB_WIDTH, DM))],
        out_specs=tok_spec(DM),
        out_shape=sds((n_tok, DM), x.dtype),
        compiler_params=pltpu.CompilerParams(dimension_semantics=("parallel",), vmem_limit_bytes=VMEM_LIMIT),
        name="outproj",
    )(x2, o_nsa.reshape(n_tok, NSA_WIDTH), o_sb.reshape(n_tok, SB_WIDTH), w_out.astype(mxu))
    return out.reshape(B, S, DM)
```

```python
import functools
import math

import numpy as np
import jax
import jax.numpy as jnp
from jax import lax
from jax.experimental import pallas as pl
from jax.experimental.pallas import tpu as pltpu

HEAD_DIM = 64
NSA_HEADS = 8
NSA_KV_HEADS = 2
NSA_GROUP = NSA_HEADS // NSA_KV_HEADS
SB_HEADS = 8
NSA_WIDTH = NSA_HEADS * HEAD_DIM
SB_WIDTH = SB_HEADS * HEAD_DIM
NSA_KV_WIDTH = NSA_KV_HEADS * HEAD_DIM
N_BRANCH = 3
CMP_LEN = 32
CMP_STRIDE = 16
SLC_LEN = 64
SLC_TOPN = 16
WINDOW = 512
Q_BLOCK = 128
ROPE_DIM = HEAD_DIM // 4
ROPE_HALF = ROPE_DIM // 2
ROPE_THETA = 500000.0
EPS = 1e-6
FORCE_BONUS = 1.0e4
SCALE = 1.0 / math.sqrt(HEAD_DIM)
LOG2E = math.log2(math.e)

LANES = 128
SUBLANES = 8
MASK_NEG = -1.0e30
M_INIT = -3.0e38
ROW_TILE = 512
NSA_QT = 256
SB_ROWS = 512
SB_UNROLL = 2
SB_PAIRS = 4
MAX_STATIC_SHIFT = 60.0
SB_UNDERFLOW_BITS = 160.0
VMEM_LIMIT = 56 * 1024 * 1024

_MXU_DTYPE = jnp.bfloat16
F32 = jnp.float32

C_Q = 0
C_KC = 512
C_KS = 768
C_KW = 1024
C_GN = 1280
C_QSB = 1792
C_KSB = 2304
C_VSB = 2816
C_GSB = 3328
C_GL = 3840
N_COLS = 3968


def _nt_dot(a, b):
    return lax.dot_general(a, b, (((1,), (1,)), ((), ())), preferred_element_type=F32)


def _dot(a, b):
    return jnp.dot(a, b, preferred_element_type=F32)


def _split2(v):
    hi = v.astype(_MXU_DTYPE)
    lo = (v - hi.astype(F32)).astype(_MXU_DTYPE)
    return hi, lo


def _split3(v):
    hi = v.astype(_MXU_DTYPE)
    r1 = v - hi.astype(F32)
    mid = r1.astype(_MXU_DTYPE)
    lo = (r1 - mid.astype(F32)).astype(_MXU_DTYPE)
    return hi, mid, lo


def _inproj_kernel(x_ref, ng_ref, w_ref, qg_ref, kg_ref, bd_ref, cos_ref, sin_ref, eg_ref,
                   q_ref, kc_ref, vc_ref, ksa_ref, vsa_ref, kw_ref, vwa_ref,
                   gc_ref, gs_ref, gw_ref, qsb_ref, ksb_ref, vsb_ref, gsb_ref, *, n_sblk):
    tm = x_ref.shape[0]
    x = x_ref[...]
    ms = jnp.mean(x * x, axis=-1, keepdims=True)
    h = (x * lax.rsqrt(ms + EPS) * ng_ref[...]).astype(_MXU_DTYPE)

    def proj(lo, width):
        return _dot(h, w_ref[:, lo:lo + width])

    lane = lax.broadcasted_iota(jnp.int32, (tm, LANES), 1)
    low_half = lane < HEAD_DIM

    def head_norm_rope(y, gain):
        width = y.shape[1]
        rep = width // LANES
        hi, lo = _split2(y * y)
        ssum = jnp.concatenate(
            [_dot(jnp.concatenate([hi[:, c:c + LANES], lo[:, c:c + LANES]], axis=1), bd_ref[...])
             for c in range(0, width, LANES)], axis=1)
        yn = y * lax.rsqrt(ssum * (1.0 / HEAD_DIM) + EPS) * gain
        cos = jnp.concatenate([cos_ref[...]] * rep, axis=1) if rep > 1 else cos_ref[...]
        sin = jnp.concatenate([sin_ref[...]] * rep, axis=1) if rep > 1 else sin_ref[...]
        fwd = pltpu.roll(yn, ROPE_HALF, axis=1)
        bwd = pltpu.roll(yn, width - ROPE_HALF, axis=1)
        lane_w = lax.broadcasted_iota(jnp.int32, (tm, width), 1)
        partner = jnp.where((lane_w & (HEAD_DIM - 1)) < ROPE_HALF, bwd, fwd)
        return yn * cos + partner * sin

    def head_pair(slab, p):
        chunk = slab[:, p * LANES:(p + 1) * LANES]
        return chunk, pltpu.roll(chunk, HEAD_DIM, axis=1)

    qn = head_norm_rope(proj(C_Q, NSA_WIDTH), qg_ref[...]) * (SCALE * LOG2E)
    for p in range(NSA_HEADS // 2):
        ev, od = head_pair(qn, p)
        q_ref[0, 2 * p] = jnp.where(low_half, ev, 0.0).astype(q_ref.dtype)
        q_ref[0, 2 * p + 1] = jnp.where(low_half, od, 0.0).astype(q_ref.dtype)

    kv_cmp = proj(C_KC, 2 * NSA_KV_WIDTH)
    kv_slc = proj(C_KS, 2 * NSA_KV_WIDTH)
    kv_win = proj(C_KW, 2 * NSA_KV_WIDTH)

    for ref, p in ((kc_ref, 0), (vc_ref, 1)):
        ev, od = head_pair(kv_cmp, p)
        ref[0, 0] = ev[:, :HEAD_DIM]
        ref[0, 1] = od[:, :HEAD_DIM]

    sblk = lax.rem(pl.program_id(0), n_sblk)
    row = lax.broadcasted_iota(jnp.int32, (tm, LANES), 0)
    key_blk = (sblk * tm + row) >> int(math.log2(SLC_LEN))
    onehot = jnp.where(lane - HEAD_DIM == key_blk, 1.0, 0.0)
    k_sw = head_norm_rope(jnp.concatenate([kv_slc[:, :NSA_KV_WIDTH], kv_win[:, :NSA_KV_WIDTH]], axis=1),
                          kg_ref[...])
    ev, od = head_pair(k_sw, 0)
    ksa_ref[0, 0] = jnp.where(low_half, ev, onehot).astype(ksa_ref.dtype)
    ksa_ref[0, 1] = jnp.where(low_half, od, onehot).astype(ksa_ref.dtype)
    ev, od = head_pair(kv_slc, 1)
    vsa_ref[0, 0] = jnp.where(low_half, ev, 1.0).astype(vsa_ref.dtype)
    vsa_ref[0, 1] = jnp.where(low_half, od, 1.0).astype(vsa_ref.dtype)

    one_lane = jnp.where(lane == HEAD_DIM, 1.0, 0.0)
    ev, od = head_pair(k_sw, 1)
    kw_ref[0, 0] = jnp.where(low_half, ev, one_lane).astype(kw_ref.dtype)
    kw_ref[0, 1] = jnp.where(low_half, od, one_lane).astype(kw_ref.dtype)
    ev, od = head_pair(kv_win, 1)
    vwa_ref[0, 0] = jnp.where(low_half, ev, 1.0).astype(vwa_ref.dtype)
    vwa_ref[0, 1] = jnp.where(low_half, od, 1.0).astype(vwa_ref.dtype)

    gn = proj(C_GN, NSA_WIDTH)
    silu_n = gn * jax.nn.sigmoid(gn)
    gl_split = jnp.concatenate(_split2(proj(C_GL, LANES)), axis=1)
    for br, ref in enumerate((gc_ref, gs_ref, gw_ref)):
        ref[...] = jax.nn.sigmoid(_dot(gl_split, eg_ref[:, br * NSA_WIDTH:(br + 1) * NSA_WIDTH])) * silu_n

    qsb_ref[...] = (proj(C_QSB, SB_WIDTH) * (SCALE * LOG2E)).astype(qsb_ref.dtype)
    ksb_ref[...] = proj(C_KSB, SB_WIDTH).astype(ksb_ref.dtype)
    vsb_ref[...] = proj(C_VSB, SB_WIDTH).astype(vsb_ref.dtype)
    gsb = proj(C_GSB, SB_WIDTH)
    gsb_ref[...] = gsb * jax.nn.sigmoid(gsb)


def _compress_kernel(kc_ref, vc_ref, posk_ref, posv_ref, w1k_ref, b1k_ref, w2k_ref,
                     w1v_ref, b1v_ref, w2v_ref, kg_ref, cos_ref, sin_ref, perm_ref,
                     kcmp_ref, vcmp_ref):
    half = CMP_STRIDE * HEAD_DIM

    def phi(c_ref, pos_ref, w1_ref, b1_ref, w2_ref):
        c = c_ref[0]
        n = c.shape[0]
        top = _dot((c + pos_ref[0:1, :]).astype(_MXU_DTYPE), w1_ref[:half, :])
        bot = _dot((c + pos_ref[1:2, :]).astype(_MXU_DTYPE), w1_ref[half:, :])
        hid = top + pltpu.roll(bot, n - 1, axis=0) + b1_ref[...]
        return _dot((hid * jax.nn.sigmoid(hid)).astype(_MXU_DTYPE), w2_ref[...])

    k = phi(kc_ref, posk_ref, w1k_ref, b1k_ref, w2k_ref)
    ms = jnp.sum(k * k, axis=-1, keepdims=True) * (1.0 / HEAD_DIM)
    kn = k * lax.rsqrt(ms + EPS) * kg_ref[...]
    hi, lo = _split2(kn)
    partner = _dot(hi, perm_ref[...]) + _dot(lo, perm_ref[...])
    one_lane = jnp.where(lax.broadcasted_iota(jnp.int32, k.shape, 1) == HEAD_DIM, 1.0, 0.0)
    kcmp_ref[0] = (kn * cos_ref[...] + partner * sin_ref[...] + one_lane).astype(kcmp_ref.dtype)
    vcmp_ref[0] = phi(vc_ref, posv_ref, w1v_ref, b1v_ref, w2v_ref).astype(vcmp_ref.dtype)


def _nsa_kernel(bound_ref, q_ref, kc_ref, vc_ref, ksa_ref, vsa_ref, kw_ref, vwa_ref, ovl_ref,
                gc_ref, gs_ref, gw_ref, o_ref, p_scr, acc_scr):
    qt = NSA_QT
    kw = 2 * qt
    assert q_ref.shape[2] == kw
    chains = [(g, half) for half in range(2) for g in range(kc_ref.shape[1])]
    groups = range(len(chains))
    kv = [g for g, _ in chains]
    blk = [2 * pl.program_id(1) + half for _, half in chains]
    rq = NSA_GROUP * qt
    n_blk_lanes = LANES - HEAD_DIM
    q_pads = [q_ref[0, g * NSA_GROUP:(g + 1) * NSA_GROUP, half * qt:(half + 1) * qt].reshape(rq, LANES)
              for g, half in chains]
    heads = lambda x: jnp.concatenate([x] * NSA_GROUP, axis=0)
    t_tok = lax.broadcasted_iota(jnp.int32, (qt, 1), 0)
    rel_w = heads(t_tok - lax.broadcasted_iota(jnp.int32, (qt, kw), 1))
    rel = rel_w[:, :qt]
    lane_q = lax.broadcasted_iota(jnp.int32, (qt, LANES), 1)
    lane_r = heads(lane_q)
    low_half = lane_q < HEAD_DIM

    logit_bound = bound_ref[0]
    static_shift = logit_bound <= MAX_STATIC_SHIFT
    bound_rows = lambda: jnp.full((rq, 1), logit_bound, F32)
    row_max = lambda s: jnp.max(s, axis=-1, keepdims=True)

    def shift_lane(shift):
        return jnp.where(lane_r == HEAD_DIM, -shift, 0.0).astype(q_pads[0].dtype)

    def window_logits(g, q_lhs):
        i = blk[g]
        s_parts, v_parts = [], []
        for back in range(WINDOW // qt + 1):
            start = pl.multiple_of(jnp.maximum(i - back, 0) * qt, qt)
            s = _nt_dot(q_lhs, kw_ref[0, kv[g], pl.ds(start, qt), :])
            v = vwa_ref[0, kv[g], pl.ds(start, qt), :]
            if back == 0:
                s = jnp.where(rel >= 0, s, MASK_NEG)
            elif (back + 1) * qt <= WINDOW:
                v = v * jnp.where(i >= back, 1.0, 0.0).astype(v.dtype)
            else:
                s = jnp.where(rel < (WINDOW - back * qt) - jnp.where(i >= back, 0, WINDOW), s, MASK_NEG)
            s_parts.append(s)
            v_parts.append(v)
        return jnp.concatenate(s_parts, axis=1), jnp.concatenate(v_parts, axis=0)

    def cmp_logits(g, q_lhs):
        kc = kc_ref[0, kv[g]]
        cmp_end = lax.broadcasted_iota(jnp.int32, (rq, kc.shape[0]), 1) * CMP_STRIDE + (CMP_LEN - 1)
        return jnp.where(cmp_end <= blk[g] * qt + heads(t_tok), _nt_dot(q_lhs, kc), MASK_NEG)

    pre_shifts = lax.cond(
        static_shift,
        lambda: tuple(shift_lane(logit_bound) for _ in groups for _ in range(2)),
        lambda: tuple(shift_lane(m) for g in groups
                      for m in (row_max(window_logits(g, q_pads[g])[0]), row_max(cmp_logits(g, q_pads[g])))))

    ovl = ovl_ref[...]
    j_idx = lax.broadcasted_iota(jnp.int32, (n_blk_lanes, qt), 0)
    row_in_grp = lax.broadcasted_iota(jnp.int32, (SUBLANES, qt), 0)

    def selected_blocks(p, i):
        blk_t = (i * qt + lax.broadcasted_iota(jnp.int32, (n_blk_lanes, qt), 1)) >> int(math.log2(SLC_LEN))
        slc_valid = j_idx <= blk_t
        forced = (j_idx == 0) | (j_idx == blk_t) | (j_idx == blk_t - 1)
        p_sum = p[0:qt]
        for r in range(1, NSA_GROUP):
            p_sum = p_sum + p[r * qt:(r + 1) * qt]
        p_slc = sum(_nt_dot(ovl, part) for part in _split3(p_sum))
        score = jnp.where(slc_valid, p_slc + jnp.where(forced, FORCE_BONUS, 0.0), -jnp.inf)
        n_grp = n_blk_lanes // SUBLANES
        grp_rows = [score[c * SUBLANES:(c + 1) * SUBLANES] for c in range(n_grp)]
        grp_rank = [jnp.zeros((SUBLANES, qt), F32) for _ in range(n_grp)]
        for ii in range(n_blk_lanes):
            s_i = score[ii:ii + 1, :]
            for c in range(n_grp):
                rows = grp_rows[c]
                if c * SUBLANES > ii:
                    beats = jnp.where(s_i >= rows, 1.0, 0.0)
                elif (c + 1) * SUBLANES - 1 <= ii:
                    beats = jnp.where(s_i > rows, 1.0, 0.0)
                else:
                    beats = jnp.where(row_in_grp > ii - c * SUBLANES,
                                      jnp.where(s_i >= rows, 1.0, 0.0), jnp.where(s_i > rows, 1.0, 0.0))
                grp_rank[c] = grp_rank[c] + beats
        rank = jnp.concatenate(grp_rank, axis=0)
        sel_t = jnp.where(slc_valid, jnp.where(rank < SLC_TOPN, 1.0, 0.0), 0.0)
        return heads(jnp.concatenate([jnp.ones((HEAD_DIM, qt), F32), sel_t], axis=0).T)

    acc_w, o_cmp, sel_rows = [], [], []
    for g in groups:
        s_win, v_win = window_logits(g, q_pads[g] + pre_shifts[2 * g])
        acc_w.append(_dot(jnp.exp2(s_win).astype(_MXU_DTYPE), v_win))
        p = jnp.exp2(cmp_logits(g, q_pads[g] + pre_shifts[2 * g + 1]))
        p = p / jnp.maximum(jnp.sum(p, axis=-1, keepdims=True), 1e-30)
        o_cmp.append(_dot(p.astype(_MXU_DTYPE), vc_ref[0, kv[g]]))
        sel_rows.append(selected_blocks(p, blk[g]))

    n_kt = pl.program_id(1) + 1

    def augmented_q(g, shift):
        bias = jnp.where(lane_r >= HEAD_DIM, jnp.where(sel_rows[g] > 0.5, -shift, MASK_NEG), 0.0)
        return q_pads[g] + bias.astype(q_pads[g].dtype)

    def slc_logits(g, q_aug, kt):
        s = _nt_dot(q_aug, ksa_ref[0, kv[g], pl.ds(pl.multiple_of(kt * kw, kw), kw), :])
        return jnp.where(rel_w >= kt * kw - blk[g] * qt, s, MASK_NEG)

    def slc_row_max(g):
        q_aug = augmented_q(g, 0.0)
        return lax.fori_loop(0, n_kt, lambda kt, m_run: jnp.maximum(m_run, row_max(slc_logits(g, q_aug, kt))),
                             jnp.full((rq, 1), M_INIT, F32))

    slc_shifts = lax.cond(static_shift, lambda: tuple(bound_rows() for _ in groups),
                          lambda: tuple(slc_row_max(g) for g in groups))
    q_augs = [augmented_q(g, slc_shifts[g]) for g in groups]

    def slc_weights(g, kt):
        return jnp.exp2(slc_logits(g, q_augs[g], kt)).astype(_MXU_DTYPE)

    def slc_weighted_values(g, kt):
        return _dot(p_scr[g], vsa_ref[0, kv[g], pl.ds(pl.multiple_of(kt * kw, kw), kw), :])

    for g in groups:
        p_scr[g] = slc_weights(g, 0)
        acc_scr[g] = jnp.zeros((rq, LANES), F32)

    def slc_body(kt, _):
        for g in groups:
            acc_scr[g] += slc_weighted_values(g, kt)
            p_scr[g] = slc_weights(g, kt + 1)
        return 0

    lax.fori_loop(0, n_kt - 1, slc_body, 0)
    acc_s = [acc_scr[g] + slc_weighted_values(g, n_kt - 1) for g in groups]

    head = lambda a, r: a[r * qt:(r + 1) * qt]

    def token_major(o):
        return jnp.concatenate([jnp.where(low_half, head(o, r), head(o, r + 1))
                                for r in range(0, NSA_GROUP, 2)], axis=1)

    def token_major_normalised(acc):
        inv = 1.0 / jnp.where(lane_r >= HEAD_DIM, acc, 1.0)
        pairs = []
        for r in range(0, NSA_GROUP, 2):
            even = head(acc, r) * pltpu.roll(head(inv, r), HEAD_DIM, axis=1)
            odd = pltpu.roll(head(acc, r + 1), HEAD_DIM, axis=1) * head(inv, r + 1)
            pairs.append(jnp.where(low_half, even, odd))
        return jnp.concatenate(pairs, axis=1)

    gw_cols = NSA_GROUP * HEAD_DIM
    for g, (kv_head, half) in enumerate(chains):
        rows = slice(half * qt, (half + 1) * qt)
        cols = slice(kv_head * gw_cols, (kv_head + 1) * gw_cols)
        out = (gc_ref[0, rows, cols] * token_major(o_cmp[g])
               + gs_ref[0, rows, cols] * token_major_normalised(acc_s[g])
               + gw_ref[0, rows, cols] * token_major_normalised(acc_w[g]))
        o_ref[0, rows, cols] = out.astype(o_ref.dtype)


def _sb_kernel(q_ref, k_ref, v_ref, uu_ref, g_ref, o_ref, *state_scr):
    i = pl.program_id(2)
    rows = q_ref.shape[1]
    pair_lanes = [slice(pr * LANES, (pr + 1) * LANES) for pr in range(q_ref.shape[2] // LANES)]
    band = rows // Q_BLOCK
    n_tiles = (i + 1) * band
    low_half = lax.broadcasted_iota(jnp.int32, (rows, LANES), 1) < HEAD_DIM
    low_half_k = lax.broadcasted_iota(jnp.int32, (Q_BLOCK, LANES), 1) < HEAD_DIM
    q_heads = []
    for lanes in pair_lanes:
        q_pair = q_ref[0, :, lanes]
        zero = jnp.zeros_like(q_pair)
        q_heads.append((jnp.where(low_half, q_pair, zero), jnp.where(low_half, zero, q_pair)))

    tri = (lax.broadcasted_iota(jnp.int32, (Q_BLOCK, LANES), 1)
           < lax.broadcasted_iota(jnp.int32, (Q_BLOCK, LANES), 0))

    def update_rows(x, r0, fn):
        return fn(x) if r0 == 0 else jnp.concatenate([x[:r0], fn(x[r0:])], axis=0)

    def on_diagonal(x, fill):
        masked = jnp.where(tri, x[:Q_BLOCK], fill)
        return masked if x.shape[0] == Q_BLOCK else jnp.concatenate([masked, x[Q_BLOCK:]], axis=0)

    def tile_step(start, carry, r0=0, diagonal=False):
        new_carry = []
        for pr, lanes in enumerate(pair_lanes):
            new_carry.extend(pair_tile_step(pr, lanes, start, carry[3 * pr:3 * pr + 3], r0, diagonal))
        return tuple(new_carry)

    def pair_tile_step(pr, lanes, start, carry, r0, diagonal):
        acc, laters = carry[0], carry[1:]
        k_pair = k_ref[0, pl.ds(start, Q_BLOCK), lanes]
        v_pair = v_ref[0, pl.ds(start, Q_BLOCK), lanes]
        zero_v = jnp.zeros_like(v_pair)
        v_bd = jnp.concatenate([jnp.where(low_half_k, v_pair, zero_v),
                                jnp.where(low_half_k, zero_v, v_pair)], axis=0)
        weights, new_laters = [], []
        for q_h, later in zip(q_heads[pr], laters):
            z = _nt_dot(q_h[r0:] if r0 else q_h, k_pair)
            sp = jnp.maximum(z, 0.0) + jnp.log2(1.0 + jnp.exp2(-jnp.abs(z)))
            if diagonal:
                sp = on_diagonal(sp, 0.0)
            hi, lo = _split2(sp)
            r = _dot(jnp.concatenate([hi, lo], axis=1), uu_ref[...])
            after = r[:, :LANES] + (later[r0:] if r0 else later)
            a = jnp.exp2(z - sp - after)
            if diagonal:
                a = on_diagonal(a, 0.0)
            weights.append(a.astype(_MXU_DTYPE))
            new_laters.append(update_rows(later, r0, lambda part, r=r: part + r[:, LANES:]))
        pv = _dot(jnp.concatenate(weights, axis=1), v_bd)
        return (update_rows(acc, r0, lambda part: part + pv), *new_laters)

    zeros = jnp.zeros((rows, LANES), F32)
    carry = (zeros,) * (3 * len(pair_lanes))
    for c in reversed(range(band)):
        carry = tile_step(pl.multiple_of(i * rows + c * Q_BLOCK, Q_BLOCK), carry, r0=c * Q_BLOCK, diagonal=True)

    def main_body(kg, carry):
        for u in range(SB_UNROLL):
            first_key = (n_tiles - band - 1 - (kg * SB_UNROLL + u)) * Q_BLOCK
            carry = tile_step(pl.multiple_of(first_key, Q_BLOCK), carry)
        return carry

    n_trips = (n_tiles - band) // SB_UNROLL

    def settled(carry):
        laters = [x for n, x in enumerate(carry) if n % 3]
        return jnp.min(functools.reduce(jnp.minimum, laters)) >= SB_UNDERFLOW_BITS

    def keep_going(state):
        kg, done = state
        return jnp.logical_and(kg < n_trips, jnp.logical_not(done))

    def save(carry):
        for ref, value in zip(state_scr, carry):
            ref[...] = value

    def main_step(state):
        kg, _ = state
        carry = main_body(kg, tuple(ref[...] for ref in state_scr))
        save(carry)
        return kg + 1, settled(carry)

    save(carry)
    lax.while_loop(keep_going, main_step, (0, settled(carry)))
    for pr, lanes in enumerate(pair_lanes):
        o_ref[0, :, lanes] = (state_scr[3 * pr][...] * g_ref[0, :, lanes]).astype(o_ref.dtype)


def _outproj_kernel(x_ref, on_ref, os_ref, w_ref, o_ref):
    o_ref[...] = (x_ref[...] + _dot(on_ref[...], w_ref[:NSA_WIDTH, :])
                  + _dot(os_ref[...], w_ref[NSA_WIDTH:, :]))


def _rope_tables(pos, reps):
    inv_freq = jnp.power(ROPE_THETA, -jnp.arange(0, ROPE_DIM, 2, dtype=F32) / ROPE_DIM)
    ang = pos.astype(F32)[:, None] * inv_freq[None, :]
    cos, sin = jnp.cos(ang), jnp.sin(ang)
    n = pos.shape[0]
    rest = HEAD_DIM - ROPE_DIM
    cos_h = jnp.concatenate([cos, cos, jnp.ones((n, rest), F32)], axis=1)
    sin_h = jnp.concatenate([-sin, sin, jnp.zeros((n, rest), F32)], axis=1)
    return jnp.tile(cos_h, (1, reps)), jnp.tile(sin_h, (1, reps))


def _const_spec(shape):
    return pl.BlockSpec(shape, lambda *_: (0,) * len(shape))


def kernel(x, norm_gain, w_in, q_norm_gain, k_norm_cmp, k_norm_slc, k_norm_win,
           cmp_k_pos, cmp_k_w1, cmp_k_b1, cmp_k_w2, cmp_v_pos, cmp_v_w1, cmp_v_b1, cmp_v_w2, w_out):
    B, S, DM = x.shape
    D, G, R = HEAD_DIM, NSA_KV_HEADS, NSA_GROUP
    mxu = _MXU_DTYPE
    n_tok = B * S
    tm = ROW_TILE
    n_sblk = S // tm
    nq = S // Q_BLOCK
    ncp = S // CMP_STRIDE
    n_slc = S // SLC_LEN
    n_cmp = (S - CMP_LEN) // CMP_STRIDE + 1
    assert S % tm == 0 and ncp % LANES == 0 and n_slc <= LANES - D and n_slc >= SLC_TOPN

    n_gl = NSA_HEADS * N_BRANCH
    gl0 = C_GN + n_gl
    w_cat = jnp.concatenate([w_in[:, :C_GN], w_in[:, gl0:], w_in[:, C_GN:gl0],
                             jnp.zeros((DM, LANES - n_gl), w_in.dtype)], axis=1).astype(mxu)
    assert w_cat.shape[1] == N_COLS
    pos = jnp.arange(S, dtype=jnp.int32)
    cos_t, sin_t = _rope_tables(pos, LANES // D)
    cmp_end = jnp.arange(ncp, dtype=jnp.int32) * CMP_STRIDE + (CMP_LEN - 1)
    cos_c, sin_c = _rope_tables(cmp_end, LANES // D)
    lane_i = np.arange(LANES)
    bd = (lane_i[:, None] // D == lane_i[None, :] // D).astype(np.float32)
    bd = jnp.asarray(np.concatenate([bd, bd], axis=0), mxu)
    eg = np.zeros((LANES, N_BRANCH * NSA_WIDTH), np.float32)
    for hh in range(NSA_HEADS):
        for br in range(N_BRANCH):
            eg[hh * N_BRANCH + br, br * NSA_WIDTH + hh * D:br * NSA_WIDTH + (hh + 1) * D] = 1.0
    eg = jnp.asarray(np.concatenate([eg, eg], axis=0), mxu)
    perm = np.zeros((LANES, LANES), np.float32)
    for c in range(ROPE_HALF):
        perm[c + ROPE_HALF, c] = 1.0
        perm[c, c + ROPE_HALF] = 1.0
    perm = jnp.asarray(perm, mxu)
    cs = np.arange(ncp) * CMP_STRIDE
    ss = np.arange(LANES - D) * SLC_LEN
    ovl = np.clip(np.minimum(cs[None, :] + CMP_LEN, ss[:, None] + SLC_LEN)
                  - np.maximum(cs[None, :], ss[:, None]), 0, None).astype(np.float32) / CMP_LEN
    ovl[:, n_cmp:] = 0.0
    ovl[n_slc:, :] = 0.0
    ovl = jnp.asarray(ovl, mxu)
    sidx = np.arange(LANES)
    tri = (sidx[:, None] > sidx[None, :]).astype(np.float32)
    uu_half = np.concatenate([tri, np.ones((LANES, LANES), np.float32)], axis=1)
    uu = jnp.asarray(np.concatenate([uu_half, uu_half], axis=0), mxu)

    row = lambda v: v.reshape(1, -1).astype(F32)
    x2 = x.reshape(n_tok, DM)

    tok_spec = lambda w: pl.BlockSpec((tm, w), lambda t: (t, 0))
    head_spec = lambda nh, w: pl.BlockSpec((1, nh, tm, w), lambda t: (t // n_sblk, 0, t % n_sblk, 0))
    tab_spec = pl.BlockSpec((tm, LANES), lambda t: (t % n_sblk, 0))
    sds = jax.ShapeDtypeStruct
    outs = pl.pallas_call(
        functools.partial(_inproj_kernel, n_sblk=n_sblk),
        grid=(n_tok // tm,),
        in_specs=[tok_spec(DM), _const_spec((1, DM)), _const_spec((DM, N_COLS)),
                  _const_spec((1, NSA_WIDTH)), _const_spec((1, 2 * NSA_KV_WIDTH)),
                  _const_spec((2 * LANES, LANES)), tab_spec, tab_spec,
                  _const_spec((2 * LANES, N_BRANCH * NSA_WIDTH))],
        out_specs=[head_spec(NSA_HEADS, LANES), head_spec(G, D), head_spec(G, D),
                   head_spec(G, LANES), head_spec(G, LANES), head_spec(G, LANES), head_spec(G, LANES),
                   tok_spec(NSA_WIDTH), tok_spec(NSA_WIDTH), tok_spec(NSA_WIDTH),
                   tok_spec(SB_WIDTH), tok_spec(SB_WIDTH), tok_spec(SB_WIDTH), tok_spec(SB_WIDTH)],
        out_shape=[sds((B, NSA_HEADS, S, LANES), mxu), sds((B, G, S, D), F32), sds((B, G, S, D), F32),
                   sds((B, G, S, LANES), mxu), sds((B, G, S, LANES), mxu), sds((B, G, S, LANES), mxu),
                   sds((B, G, S, LANES), mxu),
                   sds((n_tok, NSA_WIDTH), F32), sds((n_tok, NSA_WIDTH), F32), sds((n_tok, NSA_WIDTH), F32),
                   sds((n_tok, SB_WIDTH), mxu), sds((n_tok, SB_WIDTH), mxu), sds((n_tok, SB_WIDTH), mxu),
                   sds((n_tok, SB_WIDTH), F32)],
        compiler_params=pltpu.CompilerParams(dimension_semantics=("parallel",), vmem_limit_bytes=VMEM_LIMIT),
        name="inproj",
    )(x2, row(norm_gain), w_cat, row(jnp.tile(q_norm_gain, NSA_HEADS)),
      row(jnp.concatenate([jnp.tile(k_norm_slc, G), jnp.tile(k_norm_win, G)])), bd, cos_t, sin_t, eg)
    (q_nsa, kc_raw, vc_raw, ks_aug, vs_aug, k_win, vw_aug, g_cmp, g_slc, g_win,
     q_sb, k_sb, v_sb, g_sb) = outs

    chunk_w = CMP_STRIDE * D
    chunks = lambda a: a.reshape(B * G, ncp, chunk_w)
    bg_spec = lambda r, w: pl.BlockSpec((1, r, w), lambda t: (t, 0, 0))
    k_cmp, v_cmp = pl.pallas_call(
        _compress_kernel,
        grid=(B * G,),
        in_specs=[bg_spec(ncp, chunk_w), bg_spec(ncp, chunk_w),
                  _const_spec((2, chunk_w)), _const_spec((2, chunk_w)),
                  _const_spec((CMP_LEN * D, D)), _const_spec((1, D)), _const_spec((D, LANES)),
                  _const_spec((CMP_LEN * D, D)), _const_spec((1, D)), _const_spec((D, LANES)),
                  _const_spec((1, LANES)), _const_spec((ncp, LANES)), _const_spec((ncp, LANES)),
                  _const_spec((LANES, LANES))],
        out_specs=[bg_spec(ncp, LANES), bg_spec(ncp, LANES)],
        out_shape=[sds((B * G, ncp, LANES), mxu), sds((B * G, ncp, LANES), mxu)],
        compiler_params=pltpu.CompilerParams(dimension_semantics=("parallel",), vmem_limit_bytes=VMEM_LIMIT),
        name="compress",
    )(chunks(kc_raw), chunks(vc_raw), cmp_k_pos.reshape(2, chunk_w), cmp_v_pos.reshape(2, chunk_w),
      cmp_k_w1.astype(mxu), row(cmp_k_b1), jnp.pad(cmp_k_w2, ((0, 0), (0, LANES - D))).astype(mxu),
      cmp_v_w1.astype(mxu), row(cmp_v_b1), jnp.tile(cmp_v_w2, (1, LANES // D)).astype(mxu),
      row(jnp.pad(k_norm_cmp, (0, LANES - D))), cos_c, sin_c, perm)
    k_cmp = k_cmp.reshape(B, G, ncp, LANES)
    v_cmp = v_cmp.reshape(B, G, ncp, LANES)

    k_gain = jnp.max(jnp.abs(jnp.stack([k_norm_cmp, k_norm_slc, k_norm_win])))
    logit_bound = jnp.max(jnp.abs(q_norm_gain)) * k_gain * (1.02 * D * SCALE * LOG2E)
    logit_bound = logit_bound.astype(F32).reshape(1)
    kv_spec = lambda r, w: pl.BlockSpec((1, G, r, w), lambda b, i: (b, 0, 0, 0))
    gate_spec = pl.BlockSpec((1, 2 * NSA_QT, NSA_WIDTH), lambda b, i: (b, i, 0))
    g3 = lambda a: a.reshape(B, S, NSA_WIDTH)
    o_nsa = pl.pallas_call(
        _nsa_kernel,
        grid=(B, S // (2 * NSA_QT)),
        in_specs=[pl.BlockSpec(memory_space=pltpu.SMEM),
                  pl.BlockSpec((1, NSA_HEADS, 2 * NSA_QT, LANES), lambda b, i: (b, 0, i, 0)),
                  kv_spec(ncp, LANES), kv_spec(ncp, LANES), kv_spec(S, LANES), kv_spec(S, LANES),
                  kv_spec(S, LANES), kv_spec(S, LANES), _const_spec((LANES - D, ncp)),
                  gate_spec, gate_spec, gate_spec],
        out_specs=gate_spec,
        out_shape=sds((B, S, NSA_WIDTH), mxu),
        scratch_shapes=[pltpu.VMEM((2 * G, R * NSA_QT, 2 * NSA_QT), mxu),
                        pltpu.VMEM((2 * G, R * NSA_QT, LANES), F32)],
        compiler_params=pltpu.CompilerParams(dimension_semantics=("parallel", "arbitrary"),
                                             vmem_limit_bytes=VMEM_LIMIT),
        name="nsa",
    )(logit_bound, q_nsa, k_cmp, v_cmp, ks_aug, vs_aug, k_win, vw_aug, ovl, g3(g_cmp), g3(g_slc), g3(g_win))

    sb3 = lambda a: a.reshape(B, S, SB_WIDTH)
    pair_q = pl.BlockSpec((1, SB_ROWS, SB_PAIRS * LANES), lambda b, hp, i: (b, i, hp))
    pair_kv = pl.BlockSpec((1, S, SB_PAIRS * LANES), lambda b, hp, i: (b, 0, hp))
    o_sb = pl.pallas_call(
        _sb_kernel,
        grid=(B, SB_WIDTH // (SB_PAIRS * LANES), S // SB_ROWS),
        in_specs=[pair_q, pair_kv, pair_kv, _const_spec((2 * LANES, 2 * LANES)), pair_q],
        out_specs=pair_q,
        out_shape=sds((B, S, SB_WIDTH), mxu),
        scratch_shapes=[pltpu.VMEM((SB_ROWS, LANES), F32)] * (3 * SB_PAIRS),
        compiler_params=pltpu.CompilerParams(dimension_semantics=("parallel", "parallel", "arbitrary"),
                                             vmem_limit_bytes=VMEM_LIMIT),
        name="stickbreak",
    )(sb3(q_sb), sb3(k_sb), sb3(v_sb), uu, sb3(g_sb))

    out = pl.pallas_call(
        _outproj_kernel,
        grid=(n_tok // tm,),
        in_specs=[tok_spec(DM), tok_spec(NSA_WIDTH), tok_spec(SB_WIDTH),
                  _const_spec((NSA_WIDTH + SB_WIDTH, DM))],
        out_specs=tok_spec(DM),
        out_shape=sds((n_tok, DM), x.dtype),
        compiler_params=pltpu.CompilerParams(dimension_semantics=("parallel",), vmem_limit_bytes=VMEM_LIMIT),
        name="outproj",
    )(x2, o_nsa.reshape(n_tok, NSA_WIDTH), o_sb.reshape(n_tok, SB_WIDTH), w_out.astype(mxu))
    return out.reshape(B, S, DM)
```

```python
import functools
import math

import numpy as np
import jax
import jax.numpy as jnp
from jax import lax
from jax.experimental import pallas as pl
from jax.experimental.pallas import tpu as pltpu

HEAD_DIM = 64
NSA_HEADS = 8
NSA_KV_HEADS = 2
NSA_GROUP = NSA_HEADS // NSA_KV_HEADS
SB_HEADS = 8
NSA_WIDTH = NSA_HEADS * HEAD_DIM
SB_WIDTH = SB_HEADS * HEAD_DIM
NSA_KV_WIDTH = NSA_KV_HEADS * HEAD_DIM
N_BRANCH = 3
CMP_LEN = 32
CMP_STRIDE = 16
SLC_LEN = 64
SLC_TOPN = 16
WINDOW = 512
Q_BLOCK = 128
ROPE_DIM = HEAD_DIM // 4
ROPE_HALF = ROPE_DIM // 2
ROPE_THETA = 500000.0
EPS = 1e-6
FORCE_BONUS = 1.0e4
SCALE = 1.0 / math.sqrt(HEAD_DIM)
LOG2E = math.log2(math.e)

LANES = 128
SUBLANES = 8
MASK_NEG = -1.0e30
M_INIT = -3.0e38
ROW_TILE = 512
NSA_QT = 256
SB_ROWS = 512
SB_UNROLL = 2
SB_PAIRS = 4
MAX_STATIC_SHIFT = 60.0
SB_UNDERFLOW_BITS = 160.0
VMEM_LIMIT = 56 * 1024 * 1024

_MXU_DTYPE = jnp.bfloat16
F32 = jnp.float32

C_Q = 0
C_KC = 512
C_KS = 768
C_KW = 1024
C_GN = 1280
C_QSB = 1792
C_KSB = 2304
C_VSB = 2816
C_GSB = 3328
C_GL = 3840
N_COLS = 3968


def _nt_dot(a, b):
    return lax.dot_general(a, b, (((1,), (1,)), ((), ())), preferred_element_type=F32)


def _dot(a, b):
    return jnp.dot(a, b, preferred_element_type=F32)


def _split2(v):
    hi = v.astype(_MXU_DTYPE)
    lo = (v - hi.astype(F32)).astype(_MXU_DTYPE)
    return hi, lo


def _split3(v):
    hi = v.astype(_MXU_DTYPE)
    r1 = v - hi.astype(F32)
    mid = r1.astype(_MXU_DTYPE)
    lo = (r1 - mid.astype(F32)).astype(_MXU_DTYPE)
    return hi, mid, lo


def _inproj_kernel(x_ref, ng_ref, w_ref, qg_ref, kg_ref, bd_ref, cos_ref, sin_ref, eg_ref,
                   q_ref, kc_ref, vc_ref, ksa_ref, vsa_ref, kw_ref, vwa_ref,
                   gc_ref, gs_ref, gw_ref, qsb_ref, ksb_ref, vsb_ref, gsb_ref, cmp_scr, *, n_sblk):
    tm = x_ref.shape[0]
    x = x_ref[...]
    ms = jnp.mean(x * x, axis=-1, keepdims=True)
    h = (x * lax.rsqrt(ms + EPS) * ng_ref[...]).astype(_MXU_DTYPE)

    def proj(lo, width):
        return _dot(h, w_ref[:, lo:lo + width])

    lane = lax.broadcasted_iota(jnp.int32, (tm, LANES), 1)
    low_half = lane < HEAD_DIM

    def head_norm_rope(y, gain):
        width = y.shape[1]
        rep = width // LANES
        hi, lo = _split2(y * y)
        ssum = jnp.concatenate(
            [_dot(jnp.concatenate([hi[:, c:c + LANES], lo[:, c:c + LANES]], axis=1), bd_ref[...])
             for c in range(0, width, LANES)], axis=1)
        yn = y * lax.rsqrt(ssum * (1.0 / HEAD_DIM) + EPS) * gain
        cos = jnp.concatenate([cos_ref[...]] * rep, axis=1) if rep > 1 else cos_ref[...]
        sin = jnp.concatenate([sin_ref[...]] * rep, axis=1) if rep > 1 else sin_ref[...]
        fwd = pltpu.roll(yn, ROPE_HALF, axis=1)
        bwd = pltpu.roll(yn, width - ROPE_HALF, axis=1)
        lane_w = lax.broadcasted_iota(jnp.int32, (tm, width), 1)
        partner = jnp.where((lane_w & (HEAD_DIM - 1)) < ROPE_HALF, bwd, fwd)
        return yn * cos + partner * sin

    def head_pair(slab, p):
        chunk = slab[:, p * LANES:(p + 1) * LANES]
        return chunk, pltpu.roll(chunk, HEAD_DIM, axis=1)

    qn = head_norm_rope(proj(C_Q, NSA_WIDTH), qg_ref[...]) * (SCALE * LOG2E)
    for p in range(NSA_HEADS // 2):
        ev, od = head_pair(qn, p)
        q_ref[0, 2 * p] = jnp.where(low_half, ev, 0.0).astype(q_ref.dtype)
        q_ref[0, 2 * p + 1] = jnp.where(low_half, od, 0.0).astype(q_ref.dtype)

    kv_cmp = proj(C_KC, 2 * NSA_KV_WIDTH)
    kv_slc = proj(C_KS, 2 * NSA_KV_WIDTH)
    kv_win = proj(C_KW, 2 * NSA_KV_WIDTH)

    n_rows = tm // CMP_STRIDE
    low_half_c = lax.broadcasted_iota(jnp.int32, (n_rows, LANES), 1) < HEAD_DIM
    for ref, p in ((kc_ref, 0), (vc_ref, 1)):
        cmp_scr[p] = kv_cmp[:, p * LANES:(p + 1) * LANES]
        for j in range(CMP_STRIDE // 2):
            a = cmp_scr[p, pl.ds(2 * j, n_rows, stride=CMP_STRIDE), :]
            b = cmp_scr[p, pl.ds(2 * j + 1, n_rows, stride=CMP_STRIDE), :]
            ref[0, 0, :, j * LANES:(j + 1) * LANES] = jnp.where(low_half_c, a, pltpu.roll(b, HEAD_DIM, axis=1))
            ref[0, 1, :, j * LANES:(j + 1) * LANES] = jnp.where(low_half_c, pltpu.roll(a, HEAD_DIM, axis=1), b)

    sblk = lax.rem(pl.program_id(0), n_sblk)
    row = lax.broadcasted_iota(jnp.int32, (tm, LANES), 0)
    key_blk = (sblk * tm + row) >> int(math.log2(SLC_LEN))
    onehot = jnp.where(lane - HEAD_DIM == key_blk, 1.0, 0.0)
    k_sw = head_norm_rope(jnp.concatenate([kv_slc[:, :NSA_KV_WIDTH], kv_win[:, :NSA_KV_WIDTH]], axis=1),
                          kg_ref[...])
    ev, od = head_pair(k_sw, 0)
    ksa_ref[0, 0] = jnp.where(low_half, ev, onehot).astype(ksa_ref.dtype)
    ksa_ref[0, 1] = jnp.where(low_half, od, onehot).astype(ksa_ref.dtype)
    ev, od = head_pair(kv_slc, 1)
    vsa_ref[0, 0] = jnp.where(low_half, ev, 1.0).astype(vsa_ref.dtype)
    vsa_ref[0, 1] = jnp.where(low_half, od, 1.0).astype(vsa_ref.dtype)

    one_lane = jnp.where(lane == HEAD_DIM, 1.0, 0.0)
    ev, od = head_pair(k_sw, 1)
    kw_ref[0, 0] = jnp.where(low_half, ev, one_lane).astype(kw_ref.dtype)
    kw_ref[0, 1] = jnp.where(low_half, od, one_lane).astype(kw_ref.dtype)
    ev, od = head_pair(kv_win, 1)
    vwa_ref[0, 0] = jnp.where(low_half, ev, 1.0).astype(vwa_ref.dtype)
    vwa_ref[0, 1] = jnp.where(low_half, od, 1.0).astype(vwa_ref.dtype)

    gn = proj(C_GN, NSA_WIDTH)
    silu_n = gn * jax.nn.sigmoid(gn)
    gl_split = jnp.concatenate(_split2(proj(C_GL, LANES)), axis=1)
    for br, ref in enumerate((gc_ref, gs_ref, gw_ref)):
        ref[...] = jax.nn.sigmoid(_dot(gl_split, eg_ref[:, br * NSA_WIDTH:(br + 1) * NSA_WIDTH])) * silu_n

    qsb_ref[...] = (proj(C_QSB, SB_WIDTH) * (SCALE * LOG2E)).astype(qsb_ref.dtype)
    ksb_ref[...] = proj(C_KSB, SB_WIDTH).astype(ksb_ref.dtype)
    vsb_ref[...] = proj(C_VSB, SB_WIDTH).astype(vsb_ref.dtype)
    gsb = proj(C_GSB, SB_WIDTH)
    gsb_ref[...] = gsb * jax.nn.sigmoid(gsb)


def _compress_kernel(kc_ref, vc_ref, posk_ref, posv_ref, w1k_ref, b1k_ref, w2k_ref,
                     w1v_ref, b1v_ref, w2v_ref, kg_ref, cos_ref, sin_ref, perm_ref,
                     kcmp_ref, vcmp_ref):
    half = CMP_STRIDE * HEAD_DIM

    def phi(c_ref, pos_ref, w1_ref, b1_ref, w2_ref):
        c = c_ref[0]
        n = c.shape[0]
        top = _dot((c + pos_ref[0:1, :]).astype(_MXU_DTYPE), w1_ref[:half, :])
        bot = _dot((c + pos_ref[1:2, :]).astype(_MXU_DTYPE), w1_ref[half:, :])
        hid = top + pltpu.roll(bot, n - 1, axis=0) + b1_ref[...]
        return _dot((hid * jax.nn.sigmoid(hid)).astype(_MXU_DTYPE), w2_ref[...])

    k = phi(kc_ref, posk_ref, w1k_ref, b1k_ref, w2k_ref)
    ms = jnp.sum(k * k, axis=-1, keepdims=True) * (1.0 / HEAD_DIM)
    kn = k * lax.rsqrt(ms + EPS) * kg_ref[...]
    hi, lo = _split2(kn)
    partner = _dot(hi, perm_ref[...]) + _dot(lo, perm_ref[...])
    one_lane = jnp.where(lax.broadcasted_iota(jnp.int32, k.shape, 1) == HEAD_DIM, 1.0, 0.0)
    kcmp_ref[0] = (kn * cos_ref[...] + partner * sin_ref[...] + one_lane).astype(kcmp_ref.dtype)
    vcmp_ref[0] = phi(vc_ref, posv_ref, w1v_ref, b1v_ref, w2v_ref).astype(vcmp_ref.dtype)


def _nsa_kernel(bound_ref, q_ref, kc_ref, vc_ref, ksa_ref, vsa_ref, kw_ref, vwa_ref, ovl_ref,
                gc_ref, gs_ref, gw_ref, o_ref, p_scr, acc_scr):
    qt = NSA_QT
    kw = 2 * qt
    assert q_ref.shape[2] == kw
    chains = [(g, half) for half in range(2) for g in range(kc_ref.shape[1])]
    groups = range(len(chains))
    kv = [g for g, _ in chains]
    blk = [2 * pl.program_id(1) + half for _, half in chains]
    rq = NSA_GROUP * qt
    n_blk_lanes = LANES - HEAD_DIM
    q_pads = [q_ref[0, g * NSA_GROUP:(g + 1) * NSA_GROUP, half * qt:(half + 1) * qt].reshape(rq, LANES)
              for g, half in chains]
    heads = lambda x: jnp.concatenate([x] * NSA_GROUP, axis=0)
    t_tok = lax.broadcasted_iota(jnp.int32, (qt, 1), 0)
    rel_w = heads(t_tok - lax.broadcasted_iota(jnp.int32, (qt, kw), 1))
    rel = rel_w[:, :qt]
    lane_q = lax.broadcasted_iota(jnp.int32, (qt, LANES), 1)
    lane_r = heads(lane_q)
    low_half = lane_q < HEAD_DIM

    logit_bound = bound_ref[0]
    static_shift = logit_bound <= MAX_STATIC_SHIFT
    bound_rows = lambda: jnp.full((rq, 1), logit_bound, F32)
    row_max = lambda s: jnp.max(s, axis=-1, keepdims=True)

    def shift_lane(shift):
        return jnp.where(lane_r == HEAD_DIM, -shift, 0.0).astype(q_pads[0].dtype)

    def window_logits(g, q_lhs):
        i = blk[g]
        s_parts, v_parts = [], []
        for back in range(WINDOW // qt + 1):
            start = pl.multiple_of(jnp.maximum(i - back, 0) * qt, qt)
            s = _nt_dot(q_lhs, kw_ref[0, kv[g], pl.ds(start, qt), :])
            v = vwa_ref[0, kv[g], pl.ds(start, qt), :]
            if back == 0:
                s = jnp.where(rel >= 0, s, MASK_NEG)
            elif (back + 1) * qt <= WINDOW:
                v = v * jnp.where(i >= back, 1.0, 0.0).astype(v.dtype)
            else:
                s = jnp.where(rel < (WINDOW - back * qt) - jnp.where(i >= back, 0, WINDOW), s, MASK_NEG)
            s_parts.append(s)
            v_parts.append(v)
        return jnp.concatenate(s_parts, axis=1), jnp.concatenate(v_parts, axis=0)

    def cmp_logits(g, q_lhs):
        kc = kc_ref[0, kv[g]]
        cmp_end = lax.broadcasted_iota(jnp.int32, (rq, kc.shape[0]), 1) * CMP_STRIDE + (CMP_LEN - 1)
        return jnp.where(cmp_end <= blk[g] * qt + heads(t_tok), _nt_dot(q_lhs, kc), MASK_NEG)

    pre_shifts = lax.cond(
        static_shift,
        lambda: tuple(shift_lane(logit_bound) for _ in groups for _ in range(2)),
        lambda: tuple(shift_lane(m) for g in groups
                      for m in (row_max(window_logits(g, q_pads[g])[0]), row_max(cmp_logits(g, q_pads[g])))))

    ovl = ovl_ref[...]
    j_idx = lax.broadcasted_iota(jnp.int32, (n_blk_lanes, qt), 0)
    row_in_grp = lax.broadcasted_iota(jnp.int32, (SUBLANES, qt), 0)

    def selected_blocks(p, i):
        blk_t = (i * qt + lax.broadcasted_iota(jnp.int32, (n_blk_lanes, qt), 1)) >> int(math.log2(SLC_LEN))
        slc_valid = j_idx <= blk_t
        forced = (j_idx == 0) | (j_idx == blk_t) | (j_idx == blk_t - 1)
        p_sum = p[0:qt]
        for r in range(1, NSA_GROUP):
            p_sum = p_sum + p[r * qt:(r + 1) * qt]
        p_slc = sum(_nt_dot(ovl, part) for part in _split3(p_sum))
        score = jnp.where(slc_valid, p_slc + jnp.where(forced, FORCE_BONUS, 0.0), -jnp.inf)
        n_grp = n_blk_lanes // SUBLANES
        grp_rows = [score[c * SUBLANES:(c + 1) * SUBLANES] for c in range(n_grp)]
        grp_rank = [jnp.zeros((SUBLANES, qt), F32) for _ in range(n_grp)]
        for ii in range(n_blk_lanes):
            s_i = score[ii:ii + 1, :]
            for c in range(n_grp):
                rows = grp_rows[c]
                if c * SUBLANES > ii:
                    beats = jnp.where(s_i >= rows, 1.0, 0.0)
                elif (c + 1) * SUBLANES - 1 <= ii:
                    beats = jnp.where(s_i > rows, 1.0, 0.0)
                else:
                    beats = jnp.where(row_in_grp > ii - c * SUBLANES,
                                      jnp.where(s_i >= rows, 1.0, 0.0), jnp.where(s_i > rows, 1.0, 0.0))
                grp_rank[c] = grp_rank[c] + beats
        rank = jnp.concatenate(grp_rank, axis=0)
        sel_t = jnp.where(slc_valid, jnp.where(rank < SLC_TOPN, 1.0, 0.0), 0.0)
        return heads(jnp.concatenate([jnp.ones((HEAD_DIM, qt), F32), sel_t], axis=0).T)

    acc_w, o_cmp, sel_rows = [], [], []
    for g in groups:
        s_win, v_win = window_logits(g, q_pads[g] + pre_shifts[2 * g])
        acc_w.append(_dot(jnp.exp2(s_win).astype(_MXU_DTYPE), v_win))
        p = jnp.exp2(cmp_logits(g, q_pads[g] + pre_shifts[2 * g + 1]))
        p = p / jnp.maximum(jnp.sum(p, axis=-1, keepdims=True), 1e-30)
        o_cmp.append(_dot(p.astype(_MXU_DTYPE), vc_ref[0, kv[g]]))
        sel_rows.append(selected_blocks(p, blk[g]))

    n_kt = pl.program_id(1) + 1

    def augmented_q(g, shift):
        bias = jnp.where(lane_r >= HEAD_DIM, jnp.where(sel_rows[g] > 0.5, -shift, MASK_NEG), 0.0)
        return q_pads[g] + bias.astype(q_pads[g].dtype)

    def slc_logits(g, q_aug, kt):
        s = _nt_dot(q_aug, ksa_ref[0, kv[g], pl.ds(pl.multiple_of(kt * kw, kw), kw), :])
        return jnp.where(rel_w >= kt * kw - blk[g] * qt, s, MASK_NEG)

    def slc_row_max(g):
        q_aug = augmented_q(g, 0.0)
        return lax.fori_loop(0, n_kt, lambda kt, m_run: jnp.maximum(m_run, row_max(slc_logits(g, q_aug, kt))),
                             jnp.full((rq, 1), M_INIT, F32))

    slc_shifts = lax.cond(static_shift, lambda: tuple(bound_rows() for _ in groups),
                          lambda: tuple(slc_row_max(g) for g in groups))
    q_augs = [augmented_q(g, slc_shifts[g]) for g in groups]

    def slc_weights(g, kt):
        return jnp.exp2(slc_logits(g, q_augs[g], kt)).astype(_MXU_DTYPE)

    def slc_weighted_values(g, kt):
        return _dot(p_scr[g], vsa_ref[0, kv[g], pl.ds(pl.multiple_of(kt * kw, kw), kw), :])

    for g in groups:
        p_scr[g] = slc_weights(g, 0)
        acc_scr[g] = jnp.zeros((rq, LANES), F32)

    def slc_body(kt, _):
        for g in groups:
            acc_scr[g] += slc_weighted_values(g, kt)
            p_scr[g] = slc_weights(g, kt + 1)
        return 0

    lax.fori_loop(0, n_kt - 1, slc_body, 0)
    acc_s = [acc_scr[g] + slc_weighted_values(g, n_kt - 1) for g in groups]

    head = lambda a, r: a[r * qt:(r + 1) * qt]

    def token_major(o):
        return jnp.concatenate([jnp.where(low_half, head(o, r), head(o, r + 1))
                                for r in range(0, NSA_GROUP, 2)], axis=1)

    def token_major_normalised(acc):
        inv = 1.0 / jnp.where(lane_r >= HEAD_DIM, acc, 1.0)
        pairs = []
        for r in range(0, NSA_GROUP, 2):
            even = head(acc, r) * pltpu.roll(head(inv, r), HEAD_DIM, axis=1)
            odd = pltpu.roll(head(acc, r + 1), HEAD_DIM, axis=1) * head(inv, r + 1)
            pairs.append(jnp.where(low_half, even, odd))
        return jnp.concatenate(pairs, axis=1)

    gw_cols = NSA_GROUP * HEAD_DIM
    for g, (kv_head, half) in enumerate(chains):
        rows = slice(half * qt, (half + 1) * qt)
        cols = slice(kv_head * gw_cols, (kv_head + 1) * gw_cols)
        out = (gc_ref[0, rows, cols] * token_major(o_cmp[g])
               + gs_ref[0, rows, cols] * token_major_normalised(acc_s[g])
               + gw_ref[0, rows, cols] * token_major_normalised(acc_w[g]))
        o_ref[0, rows, cols] = out.astype(o_ref.dtype)


def _sb_kernel(q_ref, k_ref, v_ref, uu_ref, g_ref, o_ref, *state_scr):
    i = pl.program_id(2)
    rows = q_ref.shape[1]
    pair_lanes = [slice(pr * LANES, (pr + 1) * LANES) for pr in range(q_ref.shape[2] // LANES)]
    band = rows // Q_BLOCK
    n_tiles = (i + 1) * band
    low_half = lax.broadcasted_iota(jnp.int32, (rows, LANES), 1) < HEAD_DIM
    low_half_k = lax.broadcasted_iota(jnp.int32, (Q_BLOCK, LANES), 1) < HEAD_DIM
    q_heads = []
    for lanes in pair_lanes:
        q_pair = q_ref[0, :, lanes]
        zero = jnp.zeros_like(q_pair)
        q_heads.append((jnp.where(low_half, q_pair, zero), jnp.where(low_half, zero, q_pair)))

    tri = (lax.broadcasted_iota(jnp.int32, (Q_BLOCK, LANES), 1)
           < lax.broadcasted_iota(jnp.int32, (Q_BLOCK, LANES), 0))

    def update_rows(x, r0, fn):
        return fn(x) if r0 == 0 else jnp.concatenate([x[:r0], fn(x[r0:])], axis=0)

    def on_diagonal(x, fill):
        masked = jnp.where(tri, x[:Q_BLOCK], fill)
        return masked if x.shape[0] == Q_BLOCK else jnp.concatenate([masked, x[Q_BLOCK:]], axis=0)

    def tile_step(start, carry, r0=0, diagonal=False):
        new_carry = []
        for pr, lanes in enumerate(pair_lanes):
            new_carry.extend(pair_tile_step(pr, lanes, start, carry[3 * pr:3 * pr + 3], r0, diagonal))
        return tuple(new_carry)

    def pair_tile_step(pr, lanes, start, carry, r0, diagonal):
        acc, laters = carry[0], carry[1:]
        k_pair = k_ref[0, pl.ds(start, Q_BLOCK), lanes]
        v_pair = v_ref[0, pl.ds(start, Q_BLOCK), lanes]
        zero_v = jnp.zeros_like(v_pair)
        v_bd = jnp.concatenate([jnp.where(low_half_k, v_pair, zero_v),
                                jnp.where(low_half_k, zero_v, v_pair)], axis=0)
        weights, new_laters = [], []
        for q_h, later in zip(q_heads[pr], laters):
            z = _nt_dot(q_h[r0:] if r0 else q_h, k_pair)
            sp = jnp.maximum(z, 0.0) + jnp.log2(1.0 + jnp.exp2(-jnp.abs(z)))
            if diagonal:
                sp = on_diagonal(sp, 0.0)
            hi, lo = _split2(sp)
            r = _dot(jnp.concatenate([hi, lo], axis=1), uu_ref[...])
            after = r[:, :LANES] + (later[r0:] if r0 else later)
            a = jnp.exp2(z - sp - after)
            if diagonal:
                a = on_diagonal(a, 0.0)
            weights.append(a.astype(_MXU_DTYPE))
            new_laters.append(update_rows(later, r0, lambda part, r=r: part + r[:, LANES:]))
        pv = _dot(jnp.concatenate(weights, axis=1), v_bd)
        return (update_rows(acc, r0, lambda part: part + pv), *new_laters)

    zeros = jnp.zeros((rows, LANES), F32)
    carry = (zeros,) * (3 * len(pair_lanes))
    for c in reversed(range(band)):
        carry = tile_step(pl.multiple_of(i * rows + c * Q_BLOCK, Q_BLOCK), carry, r0=c * Q_BLOCK, diagonal=True)

    def main_body(kg, carry):
        for u in range(SB_UNROLL):
            first_key = (n_tiles - band - 1 - (kg * SB_UNROLL + u)) * Q_BLOCK
            carry = tile_step(pl.multiple_of(first_key, Q_BLOCK), carry)
        return carry

    n_trips = (n_tiles - band) // SB_UNROLL

    def settled(carry):
        laters = [x for n, x in enumerate(carry) if n % 3]
        return jnp.min(functools.reduce(jnp.minimum, laters)) >= SB_UNDERFLOW_BITS

    def keep_going(state):
        kg, done = state
        return jnp.logical_and(kg < n_trips, jnp.logical_not(done))

    def save(carry):
        for ref, value in zip(state_scr, carry):
            ref[...] = value

    def main_step(state):
        kg, _ = state
        carry = main_body(kg, tuple(ref[...] for ref in state_scr))
        save(carry)
        return kg + 1, settled(carry)

    save(carry)
    lax.while_loop(keep_going, main_step, (0, settled(carry)))
    for pr, lanes in enumerate(pair_lanes):
        o_ref[0, :, lanes] = (state_scr[3 * pr][...] * g_ref[0, :, lanes]).astype(o_ref.dtype)


def _outproj_kernel(x_ref, on_ref, os_ref, w_ref, o_ref):
    o_ref[...] = (x_ref[...] + _dot(on_ref[...], w_ref[:NSA_WIDTH, :])
                  + _dot(os_ref[...], w_ref[NSA_WIDTH:, :]))


def _rope_tables(pos, reps):
    inv_freq = jnp.power(ROPE_THETA, -jnp.arange(0, ROPE_DIM, 2, dtype=F32) / ROPE_DIM)
    ang = pos.astype(F32)[:, None] * inv_freq[None, :]
    cos, sin = jnp.cos(ang), jnp.sin(ang)
    n = pos.shape[0]
    rest = HEAD_DIM - ROPE_DIM
    cos_h = jnp.concatenate([cos, cos, jnp.ones((n, rest), F32)], axis=1)
    sin_h = jnp.concatenate([-sin, sin, jnp.zeros((n, rest), F32)], axis=1)
    return jnp.tile(cos_h, (1, reps)), jnp.tile(sin_h, (1, reps))


def _const_spec(shape):
    return pl.BlockSpec(shape, lambda *_: (0,) * len(shape))


def kernel(x, norm_gain, w_in, q_norm_gain, k_norm_cmp, k_norm_slc, k_norm_win,
           cmp_k_pos, cmp_k_w1, cmp_k_b1, cmp_k_w2, cmp_v_pos, cmp_v_w1, cmp_v_b1, cmp_v_w2, w_out):
    B, S, DM = x.shape
    D, G, R = HEAD_DIM, NSA_KV_HEADS, NSA_GROUP
    mxu = _MXU_DTYPE
    n_tok = B * S
    tm = ROW_TILE
    n_sblk = S // tm
    nq = S // Q_BLOCK
    ncp = S // CMP_STRIDE
    n_slc = S // SLC_LEN
    n_cmp = (S - CMP_LEN) // CMP_STRIDE + 1
    assert S % tm == 0 and ncp % LANES == 0 and n_slc <= LANES - D and n_slc >= SLC_TOPN

    n_gl = NSA_HEADS * N_BRANCH
    gl0 = C_GN + n_gl
    w_mxu = w_in.astype(mxu)
    w_cat = jnp.concatenate([w_mxu[:, :C_GN], w_mxu[:, gl0:], w_mxu[:, C_GN:gl0],
                             jnp.zeros((DM, LANES - n_gl), mxu)], axis=1)
    assert w_cat.shape[1] == N_COLS
    pos = jnp.arange(S, dtype=jnp.int32)
    cos_t, sin_t = _rope_tables(pos, LANES // D)
    cmp_end = jnp.arange(ncp, dtype=jnp.int32) * CMP_STRIDE + (CMP_LEN - 1)
    cos_c, sin_c = _rope_tables(cmp_end, LANES // D)
    lane_i = np.arange(LANES)
    bd = (lane_i[:, None] // D == lane_i[None, :] // D).astype(np.float32)
    bd = jnp.asarray(np.concatenate([bd, bd], axis=0), mxu)
    eg = np.zeros((LANES, N_BRANCH * NSA_WIDTH), np.float32)
    for hh in range(NSA_HEADS):
        for br in range(N_BRANCH):
            eg[hh * N_BRANCH + br, br * NSA_WIDTH + hh * D:br * NSA_WIDTH + (hh + 1) * D] = 1.0
    eg = jnp.asarray(np.concatenate([eg, eg], axis=0), mxu)
    perm = np.zeros((LANES, LANES), np.float32)
    for c in range(ROPE_HALF):
        perm[c + ROPE_HALF, c] = 1.0
        perm[c, c + ROPE_HALF] = 1.0
    perm = jnp.asarray(perm, mxu)
    cs = np.arange(ncp) * CMP_STRIDE
    ss = np.arange(LANES - D) * SLC_LEN
    ovl = np.clip(np.minimum(cs[None, :] + CMP_LEN, ss[:, None] + SLC_LEN)
                  - np.maximum(cs[None, :], ss[:, None]), 0, None).astype(np.float32) / CMP_LEN
    ovl[:, n_cmp:] = 0.0
    ovl[n_slc:, :] = 0.0
    ovl = jnp.asarray(ovl, mxu)
    sidx = np.arange(LANES)
    tri = (sidx[:, None] > sidx[None, :]).astype(np.float32)
    uu_half = np.concatenate([tri, np.ones((LANES, LANES), np.float32)], axis=1)
    uu = jnp.asarray(np.concatenate([uu_half, uu_half], axis=0), mxu)

    row = lambda v: v.reshape(1, -1).astype(F32)
    x2 = x.reshape(n_tok, DM)

    tok_spec = lambda w: pl.BlockSpec((tm, w), lambda t: (t, 0))
    head_spec = lambda nh, w: pl.BlockSpec((1, nh, tm, w), lambda t: (t // n_sblk, 0, t % n_sblk, 0))
    tab_spec = pl.BlockSpec((tm, LANES), lambda t: (t % n_sblk, 0))
    chunk_w = CMP_STRIDE * D
    chunk_spec = pl.BlockSpec((1, G, tm // CMP_STRIDE, chunk_w), lambda t: (t // n_sblk, 0, t % n_sblk, 0))
    sds = jax.ShapeDtypeStruct
    outs = pl.pallas_call(
        functools.partial(_inproj_kernel, n_sblk=n_sblk),
        grid=(n_tok // tm,),
        in_specs=[tok_spec(DM), _const_spec((1, DM)), _const_spec((DM, N_COLS)),
                  _const_spec((1, NSA_WIDTH)), _const_spec((1, 2 * NSA_KV_WIDTH)),
                  _const_spec((2 * LANES, LANES)), tab_spec, tab_spec,
                  _const_spec((2 * LANES, N_BRANCH * NSA_WIDTH))],
        out_specs=[head_spec(NSA_HEADS, LANES), chunk_spec, chunk_spec,
                   head_spec(G, LANES), head_spec(G, LANES), head_spec(G, LANES), head_spec(G, LANES),
                   tok_spec(NSA_WIDTH), tok_spec(NSA_WIDTH), tok_spec(NSA_WIDTH),
                   tok_spec(SB_WIDTH), tok_spec(SB_WIDTH), tok_spec(SB_WIDTH), tok_spec(SB_WIDTH)],
        out_shape=[sds((B, NSA_HEADS, S, LANES), mxu), sds((B, G, ncp, chunk_w), F32), sds((B, G, ncp, chunk_w), F32),
                   sds((B, G, S, LANES), mxu), sds((B, G, S, LANES), mxu), sds((B, G, S, LANES), mxu),
                   sds((B, G, S, LANES), mxu),
                   sds((n_tok, NSA_WIDTH), F32), sds((n_tok, NSA_WIDTH), F32), sds((n_tok, NSA_WIDTH), F32),
                   sds((n_tok, SB_WIDTH), mxu), sds((n_tok, SB_WIDTH), mxu), sds((n_tok, SB_WIDTH), mxu),
                   sds((n_tok, SB_WIDTH), F32)],
        scratch_shapes=[pltpu.VMEM((2, tm, NSA_KV_WIDTH), F32)],
        compiler_params=pltpu.CompilerParams(dimension_semantics=("parallel",), vmem_limit_bytes=VMEM_LIMIT),
        name="inproj",
    )(x2, row(norm_gain), w_cat, row(jnp.tile(q_norm_gain, NSA_HEADS)),
      row(jnp.concatenate([jnp.tile(k_norm_slc, G), jnp.tile(k_norm_win, G)])), bd, cos_t, sin_t, eg)
    (q_nsa, kc_raw, vc_raw, ks_aug, vs_aug, k_win, vw_aug, g_cmp, g_slc, g_win,
     q_sb, k_sb, v_sb, g_sb) = outs

    chunks = lambda a: a.reshape(B * G, ncp, chunk_w)
    bg_spec = lambda r, w: pl.BlockSpec((1, r, w), lambda t: (t, 0, 0))
    k_cmp, v_cmp = pl.pallas_call(
        _compress_kernel,
        grid=(B * G,),
        in_specs=[bg_spec(ncp, chunk_w), bg_spec(ncp, chunk_w),
                  _const_spec((2, chunk_w)), _const_spec((2, chunk_w)),
                  _const_spec((CMP_LEN * D, D)), _const_spec((1, D)), _const_spec((D, LANES)),
                  _const_spec((CMP_LEN * D, D)), _const_spec((1, D)), _const_spec((D, LANES)),
                  _const_spec((1, LANES)), _const_spec((ncp, LANES)), _const_spec((ncp, LANES)),
                  _const_spec((LANES, LANES))],
        out_specs=[bg_spec(ncp, LANES), bg_spec(ncp, LANES)],
        out_shape=[sds((B * G, ncp, LANES), mxu), sds((B * G, ncp, LANES), mxu)],
        compiler_params=pltpu.CompilerParams(dimension_semantics=("parallel",), vmem_limit_bytes=VMEM_LIMIT),
        name="compress",
    )(chunks(kc_raw), chunks(vc_raw), cmp_k_pos.reshape(2, chunk_w), cmp_v_pos.reshape(2, chunk_w),
      cmp_k_w1.astype(mxu), row(cmp_k_b1), jnp.pad(cmp_k_w2, ((0, 0), (0, LANES - D))).astype(mxu),
      cmp_v_w1.astype(mxu), row(cmp_v_b1), jnp.tile(cmp_v_w2, (1, LANES // D)).astype(mxu),
      row(jnp.pad(k_norm_cmp, (0, LANES - D))), cos_c, sin_c, perm)
    k_cmp = k_cmp.reshape(B, G, ncp, LANES)
    v_cmp = v_cmp.reshape(B, G, ncp, LANES)

    k_gain = jnp.max(jnp.abs(jnp.stack([k_norm_cmp, k_norm_slc, k_norm_win])))
    logit_bound = jnp.max(jnp.abs(q_norm_gain)) * k_gain * (1.02 * D * SCALE * LOG2E)
    logit_bound = logit_bound.astype(F32).reshape(1)
    kv_spec = lambda r, w: pl.BlockSpec((1, G, r, w), lambda b, i: (b, 0, 0, 0))
    gate_spec = pl.BlockSpec((1, 2 * NSA_QT, NSA_WIDTH), lambda b, i: (b, i, 0))
    g3 = lambda a: a.reshape(B, S, NSA_WIDTH)
    o_nsa = pl.pallas_call(
        _nsa_kernel,
        grid=(B, S // (2 * NSA_QT)),
        in_specs=[pl.BlockSpec(memory_space=pltpu.SMEM),
                  pl.BlockSpec((1, NSA_HEADS, 2 * NSA_QT, LANES), lambda b, i: (b, 0, i, 0)),
                  kv_spec(ncp, LANES), kv_spec(ncp, LANES), kv_spec(S, LANES), kv_spec(S, LANES),
                  kv_spec(S, LANES), kv_spec(S, LANES), _const_spec((LANES - D, ncp)),
                  gate_spec, gate_spec, gate_spec],
        out_specs=gate_spec,
        out_shape=sds((B, S, NSA_WIDTH), mxu),
        scratch_shapes=[pltpu.VMEM((2 * G, R * NSA_QT, 2 * NSA_QT), mxu),
                        pltpu.VMEM((2 * G, R * NSA_QT, LANES), F32)],
        compiler_params=pltpu.CompilerParams(dimension_semantics=("parallel", "arbitrary"),
                                             vmem_limit_bytes=VMEM_LIMIT),
        name="nsa",
    )(logit_bound, q_nsa, k_cmp, v_cmp, ks_aug, vs_aug, k_win, vw_aug, ovl, g3(g_cmp), g3(g_slc), g3(g_win))

    sb3 = lambda a: a.reshape(B, S, SB_WIDTH)
    pair_q = pl.BlockSpec((1, SB_ROWS, SB_PAIRS * LANES), lambda b, hp, i: (b, i, hp))
    pair_kv = pl.BlockSpec((1, S, SB_PAIRS * LANES), lambda b, hp, i: (b, 0, hp))
    o_sb = pl.pallas_call(
        _sb_kernel,
        grid=(B, SB_WIDTH // (SB_PAIRS * LANES), S // SB_ROWS),
        in_specs=[pair_q, pair_kv, pair_kv, _const_spec((2 * LANES, 2 * LANES)), pair_q],
        out_specs=pair_q,
        out_shape=sds((B, S, SB_WIDTH), mxu),
        scratch_shapes=[pltpu.VMEM((SB_ROWS, LANES), F32)] * (3 * SB_PAIRS),
        compiler_params=pltpu.CompilerParams(dimension_semantics=("parallel", "parallel", "arbitrary"),
                                             vmem_limit_bytes=VMEM_LIMIT),
        name="stickbreak",
    )(sb3(q_sb), sb3(k_sb), sb3(v_sb), uu, sb3(g_sb))

    out = pl.pallas_call(
        _outproj_kernel,
        grid=(n_tok // tm,),
        in_specs=[tok_spec(DM), tok_spec(NSA_WIDTH), tok_spec(SB_WIDTH),
                  _const_spec((NSA_WIDTH + SB_WIDTH, DM))],
        out_specs=tok_spec(DM),
        out_shape=sds((n_tok, DM), x.dtype),
        compiler_params=pltpu.CompilerParams(dimension_semantics=("parallel",), vmem_limit_bytes=VMEM_LIMIT),
        name="outproj",
    )(x2, o_nsa.reshape(n_tok, NSA_WIDTH), o_sb.reshape(n_tok, SB_WIDTH), w_out.astype(mxu))
    return out.reshape(B, S, DM)
```

```python
import functools
import math

import numpy as np
import jax
import jax.numpy as jnp
from jax import lax
from jax.experimental import pallas as pl
from jax.experimental.pallas import tpu as pltpu

HEAD_DIM = 64
NSA_HEADS = 8
NSA_KV_HEADS = 2
NSA_GROUP = NSA_HEADS // NSA_KV_HEADS
SB_HEADS = 8
NSA_WIDTH = NSA_HEADS * HEAD_DIM
SB_WIDTH = SB_HEADS * HEAD_DIM
NSA_KV_WIDTH = NSA_KV_HEADS * HEAD_DIM
N_BRANCH = 3
CMP_LEN = 32
CMP_STRIDE = 16
SLC_LEN = 64
SLC_TOPN = 16
WINDOW = 512
Q_BLOCK = 128
ROPE_DIM = HEAD_DIM // 4
ROPE_HALF = ROPE_DIM // 2
ROPE_THETA = 500000.0
EPS = 1e-6
FORCE_BONUS = 1.0e4
SCALE = 1.0 / math.sqrt(HEAD_DIM)
LOG2E = math.log2(math.e)

LANES = 128
SUBLANES = 8
MASK_NEG = -1.0e30
M_INIT = -3.0e38
ROW_TILE = 512
NSA_QT = 256
SB_ROWS = 512
SB_UNROLL = 2
SB_PAIRS = 4
MAX_STATIC_SHIFT = 60.0
SB_UNDERFLOW_BITS = 160.0
VMEM_LIMIT = 56 * 1024 * 1024

_MXU_DTYPE = jnp.bfloat16
F32 = jnp.float32

C_Q = 0
C_KC = 512
C_KS = 768
C_KW = 1024
C_GN = 1280
C_QSB = 1792
C_KSB = 2304
C_VSB = 2816
C_GSB = 3328
C_GL = 3840
N_COLS = 3968


def _nt_dot(a, b):
    return lax.dot_general(a, b, (((1,), (1,)), ((), ())), preferred_element_type=F32)


def _dot(a, b):
    return jnp.dot(a, b, preferred_element_type=F32)


def _split2(v):
    hi = v.astype(_MXU_DTYPE)
    lo = (v - hi.astype(F32)).astype(_MXU_DTYPE)
    return hi, lo


def _split3(v):
    hi = v.astype(_MXU_DTYPE)
    r1 = v - hi.astype(F32)
    mid = r1.astype(_MXU_DTYPE)
    lo = (r1 - mid.astype(F32)).astype(_MXU_DTYPE)
    return hi, mid, lo


def _inproj_kernel(x_ref, ng_ref, w_ref, qg_ref, kg_ref, bd_ref, cos_ref, sin_ref, eg_ref,
                   q_ref, kc_ref, vc_ref, ksa_ref, vsa_ref, kw_ref, vwa_ref,
                   gc_ref, gs_ref, gw_ref, qsb_ref, ksb_ref, vsb_ref, gsb_ref, cmp_scr, *, n_sblk):
    tm = x_ref.shape[0]
    x = x_ref[...]
    ms = jnp.mean(x * x, axis=-1, keepdims=True)
    h = (x * lax.rsqrt(ms + EPS) * ng_ref[...]).astype(_MXU_DTYPE)

    def proj(lo, width):
        return _dot(h, w_ref[:, lo:lo + width])

    lane = lax.broadcasted_iota(jnp.int32, (tm, LANES), 1)
    low_half = lane < HEAD_DIM

    def head_norm_rope(y, gain):
        width = y.shape[1]
        rep = width // LANES
        hi, lo = _split2(y * y)
        ssum = jnp.concatenate(
            [_dot(jnp.concatenate([hi[:, c:c + LANES], lo[:, c:c + LANES]], axis=1), bd_ref[...])
             for c in range(0, width, LANES)], axis=1)
        yn = y * lax.rsqrt(ssum * (1.0 / HEAD_DIM) + EPS) * gain
        cos = jnp.concatenate([cos_ref[...]] * rep, axis=1) if rep > 1 else cos_ref[...]
        sin = jnp.concatenate([sin_ref[...]] * rep, axis=1) if rep > 1 else sin_ref[...]
        fwd = pltpu.roll(yn, ROPE_HALF, axis=1)
        bwd = pltpu.roll(yn, width - ROPE_HALF, axis=1)
        lane_w = lax.broadcasted_iota(jnp.int32, (tm, width), 1)
        partner = jnp.where((lane_w & (HEAD_DIM - 1)) < ROPE_HALF, bwd, fwd)
        return yn * cos + partner * sin

    def head_pair(slab, p):
        chunk = slab[:, p * LANES:(p + 1) * LANES]
        return chunk, pltpu.roll(chunk, HEAD_DIM, axis=1)

    qn = head_norm_rope(proj(C_Q, NSA_WIDTH), qg_ref[...]) * (SCALE * LOG2E)
    for p in range(NSA_HEADS // 2):
        ev, od = head_pair(qn, p)
        q_ref[0, 2 * p] = jnp.where(low_half, ev, 0.0).astype(q_ref.dtype)
        q_ref[0, 2 * p + 1] = jnp.where(low_half, od, 0.0).astype(q_ref.dtype)

    kv_cmp = proj(C_KC, 2 * NSA_KV_WIDTH)
    kv_slc = proj(C_KS, 2 * NSA_KV_WIDTH)
    kv_win = proj(C_KW, 2 * NSA_KV_WIDTH)

    n_rows = tm // CMP_STRIDE
    low_half_c = lax.broadcasted_iota(jnp.int32, (n_rows, LANES), 1) < HEAD_DIM
    for ref, p in ((kc_ref, 0), (vc_ref, 1)):
        cmp_scr[p] = kv_cmp[:, p * LANES:(p + 1) * LANES]
        for j in range(CMP_STRIDE // 2):
            a = cmp_scr[p, pl.ds(2 * j, n_rows, stride=CMP_STRIDE), :]
            b = cmp_scr[p, pl.ds(2 * j + 1, n_rows, stride=CMP_STRIDE), :]
            ref[0, 0, :, j * LANES:(j + 1) * LANES] = jnp.where(low_half_c, a, pltpu.roll(b, HEAD_DIM, axis=1))
            ref[0, 1, :, j * LANES:(j + 1) * LANES] = jnp.where(low_half_c, pltpu.roll(a, HEAD_DIM, axis=1), b)

    sblk = lax.rem(pl.program_id(0), n_sblk)
    row = lax.broadcasted_iota(jnp.int32, (tm, LANES), 0)
    key_blk = (sblk * tm + row) >> int(math.log2(SLC_LEN))
    onehot = jnp.where(lane - HEAD_DIM == key_blk, 1.0, 0.0)
    k_sw = head_norm_rope(jnp.concatenate([kv_slc[:, :NSA_KV_WIDTH], kv_win[:, :NSA_KV_WIDTH]], axis=1),
                          kg_ref[...])
    ev, od = head_pair(k_sw, 0)
    ksa_ref[0, 0] = jnp.where(low_half, ev, onehot).astype(ksa_ref.dtype)
    ksa_ref[0, 1] = jnp.where(low_half, od, onehot).astype(ksa_ref.dtype)
    ev, od = head_pair(kv_slc, 1)
    vsa_ref[0, 0] = jnp.where(low_half, ev, 1.0).astype(vsa_ref.dtype)
    vsa_ref[0, 1] = jnp.where(low_half, od, 1.0).astype(vsa_ref.dtype)

    one_lane = jnp.where(lane == HEAD_DIM, 1.0, 0.0)
    ev, od = head_pair(k_sw, 1)
    kw_ref[0, 0] = jnp.where(low_half, ev, one_lane).astype(kw_ref.dtype)
    kw_ref[0, 1] = jnp.where(low_half, od, one_lane).astype(kw_ref.dtype)
    ev, od = head_pair(kv_win, 1)
    vwa_ref[0, 0] = jnp.where(low_half, ev, 1.0).astype(vwa_ref.dtype)
    vwa_ref[0, 1] = jnp.where(low_half, od, 1.0).astype(vwa_ref.dtype)

    gn = proj(C_GN, NSA_WIDTH)
    silu_n = gn * jax.nn.sigmoid(gn)
    gl_split = jnp.concatenate(_split2(proj(C_GL, LANES)), axis=1)
    for br, ref in enumerate((gc_ref, gs_ref, gw_ref)):
        ref[...] = jax.nn.sigmoid(_dot(gl_split, eg_ref[:, br * NSA_WIDTH:(br + 1) * NSA_WIDTH])) * silu_n

    qsb_ref[...] = (proj(C_QSB, SB_WIDTH) * (SCALE * LOG2E)).astype(qsb_ref.dtype)
    ksb_ref[...] = proj(C_KSB, SB_WIDTH).astype(ksb_ref.dtype)
    vsb_ref[...] = proj(C_VSB, SB_WIDTH).astype(vsb_ref.dtype)
    gsb = proj(C_GSB, SB_WIDTH)
    gsb_ref[...] = gsb * jax.nn.sigmoid(gsb)


def _compress_kernel(kc_ref, vc_ref, posk_ref, posv_ref, w1k_ref, b1k_ref, w2k_ref,
                     w1v_ref, b1v_ref, w2v_ref, kg_ref, cos_ref, sin_ref, perm_ref,
                     kcmp_ref, vcmp_ref):
    half = CMP_STRIDE * HEAD_DIM

    def phi(c_ref, pos_ref, w1_ref, b1_ref, w2_ref):
        c = c_ref[0]
        n = c.shape[0]
        top = _dot((c + pos_ref[0:1, :]).astype(_MXU_DTYPE), w1_ref[:half, :])
        bot = _dot((c + pos_ref[1:2, :]).astype(_MXU_DTYPE), w1_ref[half:, :])
        hid = top + pltpu.roll(bot, n - 1, axis=0) + b1_ref[...]
        return _dot((hid * jax.nn.sigmoid(hid)).astype(_MXU_DTYPE), w2_ref[...])

    k = phi(kc_ref, posk_ref, w1k_ref, b1k_ref, w2k_ref)
    ms = jnp.sum(k * k, axis=-1, keepdims=True) * (1.0 / HEAD_DIM)
    kn = k * lax.rsqrt(ms + EPS) * kg_ref[...]
    hi, lo = _split2(kn)
    partner = _dot(hi, perm_ref[...]) + _dot(lo, perm_ref[...])
    one_lane = jnp.where(lax.broadcasted_iota(jnp.int32, k.shape, 1) == HEAD_DIM, 1.0, 0.0)
    kcmp_ref[0] = (kn * cos_ref[...] + partner * sin_ref[...] + one_lane).astype(kcmp_ref.dtype)
    vcmp_ref[0] = phi(vc_ref, posv_ref, w1v_ref, b1v_ref, w2v_ref).astype(vcmp_ref.dtype)


def _nsa_kernel(bound_ref, q_ref, kc_ref, vc_ref, ksa_ref, vsa_ref, kw_ref, vwa_ref, ovl_ref,
                gc_ref, gs_ref, gw_ref, o_ref, p_scr, acc_scr, *, exact_shift):
    qt = NSA_QT
    kw = 2 * qt
    assert q_ref.shape[2] == kw
    chains = [(g, half) for half in range(2) for g in range(kc_ref.shape[1])]
    groups = range(len(chains))
    kv = [g for g, _ in chains]
    blk = [2 * pl.program_id(1) + half for _, half in chains]
    rq = NSA_GROUP * qt
    n_blk_lanes = LANES - HEAD_DIM
    q_pads = [q_ref[0, g * NSA_GROUP:(g + 1) * NSA_GROUP, half * qt:(half + 1) * qt].reshape(rq, LANES)
              for g, half in chains]
    heads = lambda x: jnp.concatenate([x] * NSA_GROUP, axis=0)
    t_tok = lax.broadcasted_iota(jnp.int32, (qt, 1), 0)
    rel_w = heads(t_tok - lax.broadcasted_iota(jnp.int32, (qt, kw), 1))
    rel = rel_w[:, :qt]
    lane_q = lax.broadcasted_iota(jnp.int32, (qt, LANES), 1)
    lane_r = heads(lane_q)
    low_half = lane_q < HEAD_DIM

    logit_bound = bound_ref[0]
    row_max = lambda s: jnp.max(s, axis=-1, keepdims=True)

    def shift_lane(shift):
        return jnp.where(lane_r == HEAD_DIM, -shift, 0.0).astype(q_pads[0].dtype)

    def window_logits(g, q_lhs):
        i = blk[g]
        s_parts, v_parts = [], []
        for back in range(WINDOW // qt + 1):
            start = pl.multiple_of(jnp.maximum(i - back, 0) * qt, qt)
            s = _nt_dot(q_lhs, kw_ref[0, kv[g], pl.ds(start, qt), :])
            v = vwa_ref[0, kv[g], pl.ds(start, qt), :]
            if back == 0:
                s = jnp.where(rel >= 0, s, MASK_NEG)
            elif (back + 1) * qt <= WINDOW:
                v = v * jnp.where(i >= back, 1.0, 0.0).astype(v.dtype)
            else:
                s = jnp.where(rel < (WINDOW - back * qt) - jnp.where(i >= back, 0, WINDOW), s, MASK_NEG)
            s_parts.append(s)
            v_parts.append(v)
        return jnp.concatenate(s_parts, axis=1), jnp.concatenate(v_parts, axis=0)

    def cmp_logits(g, q_lhs):
        kc = kc_ref[0, kv[g]]
        cmp_end = lax.broadcasted_iota(jnp.int32, (rq, kc.shape[0]), 1) * CMP_STRIDE + (CMP_LEN - 1)
        return jnp.where(cmp_end <= blk[g] * qt + heads(t_tok), _nt_dot(q_lhs, kc), MASK_NEG)

    if exact_shift:
        pre_shifts = [shift_lane(m) for g in groups
                      for m in (row_max(window_logits(g, q_pads[g])[0]), row_max(cmp_logits(g, q_pads[g])))]
    else:
        pre_shifts = [shift_lane(logit_bound)] * (2 * len(chains))

    ovl = ovl_ref[...]
    j_idx = lax.broadcasted_iota(jnp.int32, (n_blk_lanes, qt), 0)
    row_in_grp = lax.broadcasted_iota(jnp.int32, (SUBLANES, qt), 0)

    def selected_blocks(p, i):
        blk_t = (i * qt + lax.broadcasted_iota(jnp.int32, (n_blk_lanes, qt), 1)) >> int(math.log2(SLC_LEN))
        slc_valid = j_idx <= blk_t
        forced = (j_idx == 0) | (j_idx == blk_t) | (j_idx == blk_t - 1)
        p_sum = p[0:qt]
        for r in range(1, NSA_GROUP):
            p_sum = p_sum + p[r * qt:(r + 1) * qt]
        p_slc = sum(_nt_dot(ovl, part) for part in _split3(p_sum))
        score = jnp.where(slc_valid, p_slc + jnp.where(forced, FORCE_BONUS, 0.0), -jnp.inf)
        n_grp = n_blk_lanes // SUBLANES
        grp_rows = [score[c * SUBLANES:(c + 1) * SUBLANES] for c in range(n_grp)]
        grp_rank = [jnp.zeros((SUBLANES, qt), F32) for _ in range(n_grp)]
        for ii in range(n_blk_lanes):
            s_i = score[ii:ii + 1, :]
            for c in range(n_grp):
                rows = grp_rows[c]
                if c * SUBLANES > ii:
                    beats = jnp.where(s_i >= rows, 1.0, 0.0)
                elif (c + 1) * SUBLANES - 1 <= ii:
                    beats = jnp.where(s_i > rows, 1.0, 0.0)
                else:
                    beats = jnp.where(row_in_grp > ii - c * SUBLANES,
                                      jnp.where(s_i >= rows, 1.0, 0.0), jnp.where(s_i > rows, 1.0, 0.0))
                grp_rank[c] = grp_rank[c] + beats
        rank = jnp.concatenate(grp_rank, axis=0)
        sel_t = jnp.where(slc_valid, jnp.where(rank < SLC_TOPN, 1.0, 0.0), 0.0)
        return heads(jnp.concatenate([jnp.ones((HEAD_DIM, qt), F32), sel_t], axis=0).T)

    acc_w, o_cmp, sel_rows = [], [], []
    for g in groups:
        s_win, v_win = window_logits(g, q_pads[g] + pre_shifts[2 * g])
        acc_w.append(_dot(jnp.exp2(s_win).astype(_MXU_DTYPE), v_win))
        p = jnp.exp2(cmp_logits(g, q_pads[g] + pre_shifts[2 * g + 1]))
        p = p / jnp.maximum(jnp.sum(p, axis=-1, keepdims=True), 1e-30)
        o_cmp.append(_dot(p.astype(_MXU_DTYPE), vc_ref[0, kv[g]]))
        sel_rows.append(selected_blocks(p, blk[g]))

    n_kt = pl.program_id(1) + 1

    def augmented_q(g, shift):
        bias = jnp.where(lane_r >= HEAD_DIM, jnp.where(sel_rows[g] > 0.5, -shift, MASK_NEG), 0.0)
        return q_pads[g] + bias.astype(q_pads[g].dtype)

    def slc_logits(g, q_aug, kt):
        s = _nt_dot(q_aug, ksa_ref[0, kv[g], pl.ds(pl.multiple_of(kt * kw, kw), kw), :])
        return jnp.where(rel_w >= kt * kw - blk[g] * qt, s, MASK_NEG)

    def slc_row_max(g):
        q_aug = augmented_q(g, 0.0)
        return lax.fori_loop(0, n_kt, lambda kt, m_run: jnp.maximum(m_run, row_max(slc_logits(g, q_aug, kt))),
                             jnp.full((rq, 1), M_INIT, F32))

    q_augs = [augmented_q(g, slc_row_max(g) if exact_shift else logit_bound) for g in groups]

    def slc_weights(g, kt):
        return jnp.exp2(slc_logits(g, q_augs[g], kt)).astype(_MXU_DTYPE)

    def slc_weighted_values(g, kt):
        return _dot(p_scr[g], vsa_ref[0, kv[g], pl.ds(pl.multiple_of(kt * kw, kw), kw), :])

    for g in groups:
        p_scr[g] = slc_weights(g, 0)
        acc_scr[g] = jnp.zeros((rq, LANES), F32)

    def slc_body(kt, _):
        for g in groups:
            acc_scr[g] += slc_weighted_values(g, kt)
            p_scr[g] = slc_weights(g, kt + 1)
        return 0

    lax.fori_loop(0, n_kt - 1, slc_body, 0)
    acc_s = [acc_scr[g] + slc_weighted_values(g, n_kt - 1) for g in groups]

    head = lambda a, r: a[r * qt:(r + 1) * qt]

    def token_major(o):
        return jnp.concatenate([jnp.where(low_half, head(o, r), head(o, r + 1))
                                for r in range(0, NSA_GROUP, 2)], axis=1)

    def token_major_normalised(acc):
        inv = 1.0 / jnp.where(lane_r >= HEAD_DIM, acc, 1.0)
        pairs = []
        for r in range(0, NSA_GROUP, 2):
            even = head(acc, r) * pltpu.roll(head(inv, r), HEAD_DIM, axis=1)
            odd = pltpu.roll(head(acc, r + 1), HEAD_DIM, axis=1) * head(inv, r + 1)
            pairs.append(jnp.where(low_half, even, odd))
        return jnp.concatenate(pairs, axis=1)

    gw_cols = NSA_GROUP * HEAD_DIM
    for g, (kv_head, half) in enumerate(chains):
        rows = slice(half * qt, (half + 1) * qt)
        cols = slice(kv_head * gw_cols, (kv_head + 1) * gw_cols)
        out = (gc_ref[0, rows, cols] * token_major(o_cmp[g])
               + gs_ref[0, rows, cols] * token_major_normalised(acc_s[g])
               + gw_ref[0, rows, cols] * token_major_normalised(acc_w[g]))
        o_ref[0, rows, cols] = out.astype(o_ref.dtype)


def _sb_kernel(q_ref, k_ref, v_ref, uu_ref, g_ref, o_ref, *state_scr):
    i = pl.program_id(2)
    rows = q_ref.shape[1]
    pair_lanes = [slice(pr * LANES, (pr + 1) * LANES) for pr in range(q_ref.shape[2] // LANES)]
    band = rows // Q_BLOCK
    n_tiles = (i + 1) * band
    low_half = lax.broadcasted_iota(jnp.int32, (rows, LANES), 1) < HEAD_DIM
    low_half_k = lax.broadcasted_iota(jnp.int32, (Q_BLOCK, LANES), 1) < HEAD_DIM
    q_heads = []
    for lanes in pair_lanes:
        q_pair = q_ref[0, :, lanes]
        zero = jnp.zeros_like(q_pair)
        q_heads.append((jnp.where(low_half, q_pair, zero), jnp.where(low_half, zero, q_pair)))

    tri = (lax.broadcasted_iota(jnp.int32, (Q_BLOCK, LANES), 1)
           < lax.broadcasted_iota(jnp.int32, (Q_BLOCK, LANES), 0))

    def update_rows(x, r0, fn):
        return fn(x) if r0 == 0 else jnp.concatenate([x[:r0], fn(x[r0:])], axis=0)

    def on_diagonal(x, fill):
        masked = jnp.where(tri, x[:Q_BLOCK], fill)
        return masked if x.shape[0] == Q_BLOCK else jnp.concatenate([masked, x[Q_BLOCK:]], axis=0)

    def tile_step(start, carry, r0=0, diagonal=False):
        new_carry = []
        for pr, lanes in enumerate(pair_lanes):
            new_carry.extend(pair_tile_step(pr, lanes, start, carry[3 * pr:3 * pr + 3], r0, diagonal))
        return tuple(new_carry)

    def pair_tile_step(pr, lanes, start, carry, r0, diagonal):
        acc, laters = carry[0], carry[1:]
        k_pair = k_ref[0, pl.ds(start, Q_BLOCK), lanes]
        v_pair = v_ref[0, pl.ds(start, Q_BLOCK), lanes]
        zero_v = jnp.zeros_like(v_pair)
        v_bd = jnp.concatenate([jnp.where(low_half_k, v_pair, zero_v),
                                jnp.where(low_half_k, zero_v, v_pair)], axis=0)
        weights, new_laters = [], []
        for q_h, later in zip(q_heads[pr], laters):
            z = _nt_dot(q_h[r0:] if r0 else q_h, k_pair)
            sp = jnp.maximum(z, 0.0) + jnp.log2(1.0 + jnp.exp2(-jnp.abs(z)))
            if diagonal:
                sp = on_diagonal(sp, 0.0)
            hi, lo = _split2(sp)
            r = _dot(jnp.concatenate([hi, lo], axis=1), uu_ref[...])
            after = r[:, :LANES] + (later[r0:] if r0 else later)
            a = jnp.exp2(z - sp - after)
            if diagonal:
                a = on_diagonal(a, 0.0)
            weights.append(a.astype(_MXU_DTYPE))
            new_laters.append(update_rows(later, r0, lambda part, r=r: part + r[:, LANES:]))
        pv = _dot(jnp.concatenate(weights, axis=1), v_bd)
        return (update_rows(acc, r0, lambda part: part + pv), *new_laters)

    zeros = jnp.zeros((rows, LANES), F32)
    carry = (zeros,) * (3 * len(pair_lanes))
    for c in reversed(range(band)):
        carry = tile_step(pl.multiple_of(i * rows + c * Q_BLOCK, Q_BLOCK), carry, r0=c * Q_BLOCK, diagonal=True)

    def main_body(kg, carry):
        for u in range(SB_UNROLL):
            first_key = (n_tiles - band - 1 - (kg * SB_UNROLL + u)) * Q_BLOCK
            carry = tile_step(pl.multiple_of(first_key, Q_BLOCK), carry)
        return carry

    n_trips = (n_tiles - band) // SB_UNROLL

    def settled(carry):
        laters = [x for n, x in enumerate(carry) if n % 3]
        return jnp.min(functools.reduce(jnp.minimum, laters)) >= SB_UNDERFLOW_BITS

    def keep_going(state):
        kg, done = state
        return jnp.logical_and(kg < n_trips, jnp.logical_not(done))

    def save(carry):
        for ref, value in zip(state_scr, carry):
            ref[...] = value

    def main_step(state):
        kg, _ = state
        carry = main_body(kg, tuple(ref[...] for ref in state_scr))
        save(carry)
        return kg + 1, settled(carry)

    save(carry)
    lax.while_loop(keep_going, main_step, (0, settled(carry)))
    for pr, lanes in enumerate(pair_lanes):
        o_ref[0, :, lanes] = (state_scr[3 * pr][...] * g_ref[0, :, lanes]).astype(o_ref.dtype)


def _outproj_kernel(x_ref, on_ref, os_ref, w_ref, o_ref):
    o_ref[...] = (x_ref[...] + _dot(on_ref[...], w_ref[:NSA_WIDTH, :])
                  + _dot(os_ref[...], w_ref[NSA_WIDTH:, :]))


def _rope_tables(pos, reps):
    inv_freq = jnp.power(ROPE_THETA, -jnp.arange(0, ROPE_DIM, 2, dtype=F32) / ROPE_DIM)
    ang = pos.astype(F32)[:, None] * inv_freq[None, :]
    cos, sin = jnp.cos(ang), jnp.sin(ang)
    n = pos.shape[0]
    rest = HEAD_DIM - ROPE_DIM
    cos_h = jnp.concatenate([cos, cos, jnp.ones((n, rest), F32)], axis=1)
    sin_h = jnp.concatenate([-sin, sin, jnp.zeros((n, rest), F32)], axis=1)
    return jnp.tile(cos_h, (1, reps)), jnp.tile(sin_h, (1, reps))


def _const_spec(shape):
    return pl.BlockSpec(shape, lambda *_: (0,) * len(shape))


def kernel(x, norm_gain, w_in, q_norm_gain, k_norm_cmp, k_norm_slc, k_norm_win,
           cmp_k_pos, cmp_k_w1, cmp_k_b1, cmp_k_w2, cmp_v_pos, cmp_v_w1, cmp_v_b1, cmp_v_w2, w_out):
    B, S, DM = x.shape
    D, G, R = HEAD_DIM, NSA_KV_HEADS, NSA_GROUP
    mxu = _MXU_DTYPE
    n_tok = B * S
    tm = ROW_TILE
    n_sblk = S // tm
    nq = S // Q_BLOCK
    ncp = S // CMP_STRIDE
    n_slc = S // SLC_LEN
    n_cmp = (S - CMP_LEN) // CMP_STRIDE + 1
    assert S % tm == 0 and ncp % LANES == 0 and n_slc <= LANES - D and n_slc >= SLC_TOPN

    n_gl = NSA_HEADS * N_BRANCH
    gl0 = C_GN + n_gl
    w_mxu = w_in.astype(mxu)
    w_cat = jnp.concatenate([w_mxu[:, :C_GN], w_mxu[:, gl0:], w_mxu[:, C_GN:gl0],
                             jnp.zeros((DM, LANES - n_gl), mxu)], axis=1)
    assert w_cat.shape[1] == N_COLS
    pos = jnp.arange(S, dtype=jnp.int32)
    cos_t, sin_t = _rope_tables(pos, LANES // D)
    cmp_end = jnp.arange(ncp, dtype=jnp.int32) * CMP_STRIDE + (CMP_LEN - 1)
    cos_c, sin_c = _rope_tables(cmp_end, LANES // D)
    lane_i = np.arange(LANES)
    bd = (lane_i[:, None] // D == lane_i[None, :] // D).astype(np.float32)
    bd = jnp.asarray(np.concatenate([bd, bd], axis=0), mxu)
    eg = np.zeros((LANES, N_BRANCH * NSA_WIDTH), np.float32)
    for hh in range(NSA_HEADS):
        for br in range(N_BRANCH):
            eg[hh * N_BRANCH + br, br * NSA_WIDTH + hh * D:br * NSA_WIDTH + (hh + 1) * D] = 1.0
    eg = jnp.asarray(np.concatenate([eg, eg], axis=0), mxu)
    perm = np.zeros((LANES, LANES), np.float32)
    for c in range(ROPE_HALF):
        perm[c + ROPE_HALF, c] = 1.0
        perm[c, c + ROPE_HALF] = 1.0
    perm = jnp.asarray(perm, mxu)
    cs = np.arange(ncp) * CMP_STRIDE
    ss = np.arange(LANES - D) * SLC_LEN
    ovl = np.clip(np.minimum(cs[None, :] + CMP_LEN, ss[:, None] + SLC_LEN)
                  - np.maximum(cs[None, :], ss[:, None]), 0, None).astype(np.float32) / CMP_LEN
    ovl[:, n_cmp:] = 0.0
    ovl[n_slc:, :] = 0.0
    ovl = jnp.asarray(ovl, mxu)
    sidx = np.arange(LANES)
    tri = (sidx[:, None] > sidx[None, :]).astype(np.float32)
    uu_half = np.concatenate([tri, np.ones((LANES, LANES), np.float32)], axis=1)
    uu = jnp.asarray(np.concatenate([uu_half, uu_half], axis=0), mxu)

    row = lambda v: v.reshape(1, -1).astype(F32)
    x2 = x.reshape(n_tok, DM)

    tok_spec = lambda w: pl.BlockSpec((tm, w), lambda t: (t, 0))
    head_spec = lambda nh, w: pl.BlockSpec((1, nh, tm, w), lambda t: (t // n_sblk, 0, t % n_sblk, 0))
    tab_spec = pl.BlockSpec((tm, LANES), lambda t: (t % n_sblk, 0))
    chunk_w = CMP_STRIDE * D
    chunk_spec = pl.BlockSpec((1, G, tm // CMP_STRIDE, chunk_w), lambda t: (t // n_sblk, 0, t % n_sblk, 0))
    sds = jax.ShapeDtypeStruct
    outs = pl.pallas_call(
        functools.partial(_inproj_kernel, n_sblk=n_sblk),
        grid=(n_tok // tm,),
        in_specs=[tok_spec(DM), _const_spec((1, DM)), _const_spec((DM, N_COLS)),
                  _const_spec((1, NSA_WIDTH)), _const_spec((1, 2 * NSA_KV_WIDTH)),
                  _const_spec((2 * LANES, LANES)), tab_spec, tab_spec,
                  _const_spec((2 * LANES, N_BRANCH * NSA_WIDTH))],
        out_specs=[head_spec(NSA_HEADS, LANES), chunk_spec, chunk_spec,
                   head_spec(G, LANES), head_spec(G, LANES), head_spec(G, LANES), head_spec(G, LANES),
                   tok_spec(NSA_WIDTH), tok_spec(NSA_WIDTH), tok_spec(NSA_WIDTH),
                   tok_spec(SB_WIDTH), tok_spec(SB_WIDTH), tok_spec(SB_WIDTH), tok_spec(SB_WIDTH)],
        out_shape=[sds((B, NSA_HEADS, S, LANES), mxu), sds((B, G, ncp, chunk_w), F32), sds((B, G, ncp, chunk_w), F32),
                   sds((B, G, S, LANES), mxu), sds((B, G, S, LANES), mxu), sds((B, G, S, LANES), mxu),
                   sds((B, G, S, LANES), mxu),
                   sds((n_tok, NSA_WIDTH), F32), sds((n_tok, NSA_WIDTH), F32), sds((n_tok, NSA_WIDTH), F32),
                   sds((n_tok, SB_WIDTH), mxu), sds((n_tok, SB_WIDTH), mxu), sds((n_tok, SB_WIDTH), mxu),
                   sds((n_tok, SB_WIDTH), F32)],
        scratch_shapes=[pltpu.VMEM((2, tm, NSA_KV_WIDTH), F32)],
        compiler_params=pltpu.CompilerParams(dimension_semantics=("parallel",), vmem_limit_bytes=VMEM_LIMIT),
        name="inproj",
    )(x2, row(norm_gain), w_cat, row(jnp.tile(q_norm_gain, NSA_HEADS)),
      row(jnp.concatenate([jnp.tile(k_norm_slc, G), jnp.tile(k_norm_win, G)])), bd, cos_t, sin_t, eg)
    (q_nsa, kc_raw, vc_raw, ks_aug, vs_aug, k_win, vw_aug, g_cmp, g_slc, g_win,
     q_sb, k_sb, v_sb, g_sb) = outs

    chunks = lambda a: a.reshape(B * G, ncp, chunk_w)
    bg_spec = lambda r, w: pl.BlockSpec((1, r, w), lambda t: (t, 0, 0))
    k_cmp, v_cmp = pl.pallas_call(
        _compress_kernel,
        grid=(B * G,),
        in_specs=[bg_spec(ncp, chunk_w), bg_spec(ncp, chunk_w),
                  _const_spec((2, chunk_w)), _const_spec((2, chunk_w)),
                  _const_spec((CMP_LEN * D, D)), _const_spec((1, D)), _const_spec((D, LANES)),
                  _const_spec((CMP_LEN * D, D)), _const_spec((1, D)), _const_spec((D, LANES)),
                  _const_spec((1, LANES)), _const_spec((ncp, LANES)), _const_spec((ncp, LANES)),
                  _const_spec((LANES, LANES))],
        out_specs=[bg_spec(ncp, LANES), bg_spec(ncp, LANES)],
        out_shape=[sds((B * G, ncp, LANES), mxu), sds((B * G, ncp, LANES), mxu)],
        compiler_params=pltpu.CompilerParams(dimension_semantics=("parallel",), vmem_limit_bytes=VMEM_LIMIT),
        name="compress",
    )(chunks(kc_raw), chunks(vc_raw), cmp_k_pos.reshape(2, chunk_w), cmp_v_pos.reshape(2, chunk_w),
      cmp_k_w1.astype(mxu), row(cmp_k_b1), jnp.pad(cmp_k_w2, ((0, 0), (0, LANES - D))).astype(mxu),
      cmp_v_w1.astype(mxu), row(cmp_v_b1), jnp.tile(cmp_v_w2, (1, LANES // D)).astype(mxu),
      row(jnp.pad(k_norm_cmp, (0, LANES - D))), cos_c, sin_c, perm)
    k_cmp = k_cmp.reshape(B, G, ncp, LANES)
    v_cmp = v_cmp.reshape(B, G, ncp, LANES)

    k_gain = jnp.max(jnp.abs(jnp.stack([k_norm_cmp, k_norm_slc, k_norm_win])))
    logit_bound = jnp.max(jnp.abs(q_norm_gain)) * k_gain * (1.02 * D * SCALE * LOG2E)
    logit_bound = logit_bound.astype(F32).reshape(1)
    kv_spec = lambda r, w: pl.BlockSpec((1, G, r, w), lambda b, i: (b, 0, 0, 0))
    gate_spec = pl.BlockSpec((1, 2 * NSA_QT, NSA_WIDTH), lambda b, i: (b, i, 0))
    g3 = lambda a: a.reshape(B, S, NSA_WIDTH)
    nsa_call = lambda exact_shift: pl.pallas_call(
        functools.partial(_nsa_kernel, exact_shift=exact_shift),
        grid=(B, S // (2 * NSA_QT)),
        in_specs=[pl.BlockSpec(memory_space=pltpu.SMEM),
                  pl.BlockSpec((1, NSA_HEADS, 2 * NSA_QT, LANES), lambda b, i: (b, 0, i, 0)),
                  kv_spec(ncp, LANES), kv_spec(ncp, LANES), kv_spec(S, LANES), kv_spec(S, LANES),
                  kv_spec(S, LANES), kv_spec(S, LANES), _const_spec((LANES - D, ncp)),
                  gate_spec, gate_spec, gate_spec],
        out_specs=gate_spec,
        out_shape=sds((B, S, NSA_WIDTH), mxu),
        scratch_shapes=[pltpu.VMEM((2 * G, R * NSA_QT, 2 * NSA_QT), mxu),
                        pltpu.VMEM((2 * G, R * NSA_QT, LANES), F32)],
        compiler_params=pltpu.CompilerParams(dimension_semantics=("parallel", "arbitrary"),
                                             vmem_limit_bytes=VMEM_LIMIT),
        name="nsa_exact_shift" if exact_shift else "nsa",
    )
    nsa_operands = (logit_bound, q_nsa, k_cmp, v_cmp, ks_aug, vs_aug, k_win, vw_aug, ovl,
                    g3(g_cmp), g3(g_slc), g3(g_win))
    o_nsa = lax.cond(logit_bound[0] <= MAX_STATIC_SHIFT,
                     lambda ops: nsa_call(False)(*ops), lambda ops: nsa_call(True)(*ops), nsa_operands)

    sb3 = lambda a: a.reshape(B, S, SB_WIDTH)
    pair_q = pl.BlockSpec((1, SB_ROWS, SB_PAIRS * LANES), lambda b, hp, i: (b, i, hp))
    pair_kv = pl.BlockSpec((1, S, SB_PAIRS * LANES), lambda b, hp, i: (b, 0, hp))
    o_sb = pl.pallas_call(
        _sb_kernel,
        grid=(B, SB_WIDTH // (SB_PAIRS * LANES), S // SB_ROWS),
        in_specs=[pair_q, pair_kv, pair_kv, _const_spec((2 * LANES, 2 * LANES)), pair_q],
        out_specs=pair_q,
        out_shape=sds((B, S, SB_WIDTH), mxu),
        scratch_shapes=[pltpu.VMEM((SB_ROWS, LANES), F32)] * (3 * SB_PAIRS),
        compiler_params=pltpu.CompilerParams(dimension_semantics=("parallel", "parallel", "arbitrary"),
                                             vmem_limit_bytes=VMEM_LIMIT),
        name="stickbreak",
    )(sb3(q_sb), sb3(k_sb), sb3(v_sb), uu, sb3(g_sb))

    out = pl.pallas_call(
        _outproj_kernel,
        grid=(n_tok // tm,),
        in_specs=[tok_spec(DM), tok_spec(NSA_WIDTH), tok_spec(SB_WIDTH),
                  _const_spec((NSA_WIDTH + SB_WIDTH, DM))],
        out_specs=tok_spec(DM),
        out_shape=sds((n_tok, DM), x.dtype),
        compiler_params=pltpu.CompilerParams(dimension_semantics=("parallel",), vmem_limit_bytes=VMEM_LIMIT),
        name="outproj",
    )(x2, o_nsa.reshape(n_tok, NSA_WIDTH), o_sb.reshape(n_tok, SB_WIDTH), w_out.astype(mxu))
    return out.reshape(B, S, DM)
```

```python
import functools
import math

import numpy as np
import jax
import jax.numpy as jnp
from jax import lax
from jax.experimental import pallas as pl
from jax.experimental.pallas import tpu as pltpu

HEAD_DIM = 64
NSA_HEADS = 8
NSA_KV_HEADS = 2
NSA_GROUP = NSA_HEADS // NSA_KV_HEADS
SB_HEADS = 8
NSA_WIDTH = NSA_HEADS * HEAD_DIM
SB_WIDTH = SB_HEADS * HEAD_DIM
NSA_KV_WIDTH = NSA_KV_HEADS * HEAD_DIM
N_BRANCH = 3
CMP_LEN = 32
CMP_STRIDE = 16
SLC_LEN = 64
SLC_TOPN = 16
WINDOW = 512
Q_BLOCK = 128
ROPE_DIM = HEAD_DIM // 4
ROPE_HALF = ROPE_DIM // 2
ROPE_THETA = 500000.0
EPS = 1e-6
FORCE_BONUS = 1.0e4
SCALE = 1.0 / math.sqrt(HEAD_DIM)
LOG2E = math.log2(math.e)

LANES = 128
SUBLANES = 8
MASK_NEG = -1.0e30
M_INIT = -3.0e38
ROW_TILE = 512
OUT_ROW_TILE = 2048
NSA_QT = 256
SB_ROWS = 512
SB_UNROLL = 2
SB_PAIRS = 4
MAX_STATIC_SHIFT = 60.0
SB_UNDERFLOW_BITS = 160.0
VMEM_LIMIT = 56 * 1024 * 1024

_MXU_DTYPE = jnp.bfloat16
F32 = jnp.float32

C_Q = 0
C_KC = 512
C_KS = 768
C_KW = 1024
C_GN = 1280
C_QSB = 1792
C_KSB = 2304
C_VSB = 2816
C_GSB = 3328
C_GL = 3840
N_COLS = 3968


def _nt_dot(a, b):
    return lax.dot_general(a, b, (((1,), (1,)), ((), ())), preferred_element_type=F32)


def _dot(a, b):
    return jnp.dot(a, b, preferred_element_type=F32)


def _split2(v):
    hi = v.astype(_MXU_DTYPE)
    lo = (v - hi.astype(F32)).astype(_MXU_DTYPE)
    return hi, lo


def _split3(v):
    hi = v.astype(_MXU_DTYPE)
    r1 = v - hi.astype(F32)
    mid = r1.astype(_MXU_DTYPE)
    lo = (r1 - mid.astype(F32)).astype(_MXU_DTYPE)
    return hi, mid, lo


def _inproj_kernel(x_ref, ng_ref, w_ref, qg_ref, kg_ref, bd_ref, cos_ref, sin_ref, eg_ref,
                   q_ref, kc_ref, vc_ref, ksa_ref, vsa_ref, kw_ref, vwa_ref,
                   gc_ref, gs_ref, gw_ref, qsb_ref, ksb_ref, vsb_ref, gsb_ref, cmp_scr, *, n_sblk):
    tm = x_ref.shape[0]
    x = x_ref[...]
    ms = jnp.mean(x * x, axis=-1, keepdims=True)
    h = (x * lax.rsqrt(ms + EPS) * ng_ref[...]).astype(_MXU_DTYPE)

    def proj(lo, width):
        return _dot(h, w_ref[:, lo:lo + width])

    lane = lax.broadcasted_iota(jnp.int32, (tm, LANES), 1)
    low_half = lane < HEAD_DIM

    def head_norm_rope(y, gain):
        width = y.shape[1]
        rep = width // LANES
        hi, lo = _split2(y * y)
        ssum = jnp.concatenate(
            [_dot(jnp.concatenate([hi[:, c:c + LANES], lo[:, c:c + LANES]], axis=1), bd_ref[...])
             for c in range(0, width, LANES)], axis=1)
        yn = y * lax.rsqrt(ssum * (1.0 / HEAD_DIM) + EPS) * gain
        cos = jnp.concatenate([cos_ref[...]] * rep, axis=1) if rep > 1 else cos_ref[...]
        sin = jnp.concatenate([sin_ref[...]] * rep, axis=1) if rep > 1 else sin_ref[...]
        fwd = pltpu.roll(yn, ROPE_HALF, axis=1)
        bwd = pltpu.roll(yn, width - ROPE_HALF, axis=1)
        lane_w = lax.broadcasted_iota(jnp.int32, (tm, width), 1)
        partner = jnp.where((lane_w & (HEAD_DIM - 1)) < ROPE_HALF, bwd, fwd)
        return yn * cos + partner * sin

    def head_pair(slab, p):
        chunk = slab[:, p * LANES:(p + 1) * LANES]
        return chunk, pltpu.roll(chunk, HEAD_DIM, axis=1)

    qn = head_norm_rope(proj(C_Q, NSA_WIDTH), qg_ref[...]) * (SCALE * LOG2E)
    for p in range(NSA_HEADS // 2):
        ev, od = head_pair(qn, p)
        q_ref[0, 2 * p] = jnp.where(low_half, ev, 0.0).astype(q_ref.dtype)
        q_ref[0, 2 * p + 1] = jnp.where(low_half, od, 0.0).astype(q_ref.dtype)

    kv_cmp = proj(C_KC, 2 * NSA_KV_WIDTH)
    kv_slc = proj(C_KS, 2 * NSA_KV_WIDTH)
    kv_win = proj(C_KW, 2 * NSA_KV_WIDTH)

    n_rows = tm // CMP_STRIDE
    low_half_c = lax.broadcasted_iota(jnp.int32, (n_rows, LANES), 1) < HEAD_DIM
    for ref, p in ((kc_ref, 0), (vc_ref, 1)):
        cmp_scr[p] = kv_cmp[:, p * LANES:(p + 1) * LANES]
        for j in range(CMP_STRIDE // 2):
            a = cmp_scr[p, pl.ds(2 * j, n_rows, stride=CMP_STRIDE), :]
            b = cmp_scr[p, pl.ds(2 * j + 1, n_rows, stride=CMP_STRIDE), :]
            ref[0, 0, :, j * LANES:(j + 1) * LANES] = jnp.where(low_half_c, a, pltpu.roll(b, HEAD_DIM, axis=1))
            ref[0, 1, :, j * LANES:(j + 1) * LANES] = jnp.where(low_half_c, pltpu.roll(a, HEAD_DIM, axis=1), b)

    sblk = lax.rem(pl.program_id(0), n_sblk)
    row = lax.broadcasted_iota(jnp.int32, (tm, LANES), 0)
    key_blk = (sblk * tm + row) >> int(math.log2(SLC_LEN))
    onehot = jnp.where(lane - HEAD_DIM == key_blk, 1.0, 0.0)
    k_sw = head_norm_rope(jnp.concatenate([kv_slc[:, :NSA_KV_WIDTH], kv_win[:, :NSA_KV_WIDTH]], axis=1),
                          kg_ref[...])
    ev, od = head_pair(k_sw, 0)
    ksa_ref[0, 0] = jnp.where(low_half, ev, onehot).astype(ksa_ref.dtype)
    ksa_ref[0, 1] = jnp.where(low_half, od, onehot).astype(ksa_ref.dtype)
    ev, od = head_pair(kv_slc, 1)
    vsa_ref[0, 0] = jnp.where(low_half, ev, 1.0).astype(vsa_ref.dtype)
    vsa_ref[0, 1] = jnp.where(low_half, od, 1.0).astype(vsa_ref.dtype)

    one_lane = jnp.where(lane == HEAD_DIM, 1.0, 0.0)
    ev, od = head_pair(k_sw, 1)
    kw_ref[0, 0] = jnp.where(low_half, ev, one_lane).astype(kw_ref.dtype)
    kw_ref[0, 1] = jnp.where(low_half, od, one_lane).astype(kw_ref.dtype)
    ev, od = head_pair(kv_win, 1)
    vwa_ref[0, 0] = jnp.where(low_half, ev, 1.0).astype(vwa_ref.dtype)
    vwa_ref[0, 1] = jnp.where(low_half, od, 1.0).astype(vwa_ref.dtype)

    gn = proj(C_GN, NSA_WIDTH)
    silu_n = gn * jax.nn.sigmoid(gn)
    gl_split = jnp.concatenate(_split2(proj(C_GL, LANES)), axis=1)
    for br, ref in enumerate((gc_ref, gs_ref, gw_ref)):
        ref[...] = jax.nn.sigmoid(_dot(gl_split, eg_ref[:, br * NSA_WIDTH:(br + 1) * NSA_WIDTH])) * silu_n

    qsb_ref[...] = (proj(C_QSB, SB_WIDTH) * (SCALE * LOG2E)).astype(qsb_ref.dtype)
    ksb_ref[...] = proj(C_KSB, SB_WIDTH).astype(ksb_ref.dtype)
    vsb_ref[...] = proj(C_VSB, SB_WIDTH).astype(vsb_ref.dtype)
    gsb = proj(C_GSB, SB_WIDTH)
    gsb_ref[...] = gsb * jax.nn.sigmoid(gsb)


def _compress_kernel(kc_ref, vc_ref, posk_ref, posv_ref, w1k_ref, b1k_ref, w2k_ref,
                     w1v_ref, b1v_ref, w2v_ref, kg_ref, cos_ref, sin_ref, perm_ref,
                     kcmp_ref, vcmp_ref):
    half = CMP_STRIDE * HEAD_DIM

    def phi(c_ref, pos_ref, w1_ref, b1_ref, w2_ref):
        c = c_ref[0]
        n = c.shape[0]
        top = _dot((c + pos_ref[0:1, :]).astype(_MXU_DTYPE), w1_ref[:half, :])
        bot = _dot((c + pos_ref[1:2, :]).astype(_MXU_DTYPE), w1_ref[half:, :])
        hid = top + pltpu.roll(bot, n - 1, axis=0) + b1_ref[...]
        return _dot((hid * jax.nn.sigmoid(hid)).astype(_MXU_DTYPE), w2_ref[...])

    k = phi(kc_ref, posk_ref, w1k_ref, b1k_ref, w2k_ref)
    ms = jnp.sum(k * k, axis=-1, keepdims=True) * (1.0 / HEAD_DIM)
    kn = k * lax.rsqrt(ms + EPS) * kg_ref[...]
    hi, lo = _split2(kn)
    partner = _dot(hi, perm_ref[...]) + _dot(lo, perm_ref[...])
    one_lane = jnp.where(lax.broadcasted_iota(jnp.int32, k.shape, 1) == HEAD_DIM, 1.0, 0.0)
    kcmp_ref[0] = (kn * cos_ref[...] + partner * sin_ref[...] + one_lane).astype(kcmp_ref.dtype)
    vcmp_ref[0] = phi(vc_ref, posv_ref, w1v_ref, b1v_ref, w2v_ref).astype(vcmp_ref.dtype)


def _nsa_kernel(bound_ref, q_ref, kc_ref, vc_ref, ksa_ref, vsa_ref, kw_ref, vwa_ref, ovl_ref,
                gc_ref, gs_ref, gw_ref, o_ref, acc_scr, *, exact_shift):
    qt = NSA_QT
    kw = 2 * qt
    assert q_ref.shape[2] == kw
    chains = [(g, half) for half in range(2) for g in range(kc_ref.shape[1])]
    groups = range(len(chains))
    kv = [g for g, _ in chains]
    blk = [2 * pl.program_id(1) + half for _, half in chains]
    rq = NSA_GROUP * qt
    n_blk_lanes = LANES - HEAD_DIM
    q_pads = [q_ref[0, g * NSA_GROUP:(g + 1) * NSA_GROUP, half * qt:(half + 1) * qt].reshape(rq, LANES)
              for g, half in chains]
    heads = lambda x: jnp.concatenate([x] * NSA_GROUP, axis=0)
    t_tok = lax.broadcasted_iota(jnp.int32, (qt, 1), 0)
    rel_w = heads(t_tok - lax.broadcasted_iota(jnp.int32, (qt, kw), 1))
    rel = rel_w[:, :qt]
    lane_q = lax.broadcasted_iota(jnp.int32, (qt, LANES), 1)
    lane_r = heads(lane_q)
    low_half = lane_q < HEAD_DIM

    logit_bound = bound_ref[0]
    row_max = lambda s: jnp.max(s, axis=-1, keepdims=True)

    def shift_lane(shift):
        return jnp.where(lane_r == HEAD_DIM, -shift, 0.0).astype(q_pads[0].dtype)

    def window_logits(g, q_lhs):
        i = blk[g]
        s_parts, v_parts = [], []
        for back in range(WINDOW // qt + 1):
            start = pl.multiple_of(jnp.maximum(i - back, 0) * qt, qt)
            s = _nt_dot(q_lhs, kw_ref[0, kv[g], pl.ds(start, qt), :])
            v = vwa_ref[0, kv[g], pl.ds(start, qt), :]
            if back == 0:
                s = jnp.where(rel >= 0, s, MASK_NEG)
            elif (back + 1) * qt <= WINDOW:
                v = v * jnp.where(i >= back, 1.0, 0.0).astype(v.dtype)
            else:
                s = jnp.where(rel < (WINDOW - back * qt) - jnp.where(i >= back, 0, WINDOW), s, MASK_NEG)
            s_parts.append(s)
            v_parts.append(v)
        return jnp.concatenate(s_parts, axis=1), jnp.concatenate(v_parts, axis=0)

    def cmp_logits(g, q_lhs):
        kc = kc_ref[0, kv[g]]
        cmp_end = lax.broadcasted_iota(jnp.int32, (rq, kc.shape[0]), 1) * CMP_STRIDE + (CMP_LEN - 1)
        return jnp.where(cmp_end <= blk[g] * qt + heads(t_tok), _nt_dot(q_lhs, kc), MASK_NEG)

    if exact_shift:
        pre_shifts = [shift_lane(m) for g in groups
                      for m in (row_max(window_logits(g, q_pads[g])[0]), row_max(cmp_logits(g, q_pads[g])))]
    else:
        pre_shifts = [shift_lane(logit_bound)] * (2 * len(chains))

    ovl = ovl_ref[...]
    j_idx = lax.broadcasted_iota(jnp.int32, (n_blk_lanes, qt), 0)
    row_in_grp = lax.broadcasted_iota(jnp.int32, (SUBLANES, qt), 0)

    def selected_blocks(p, i):
        blk_t = (i * qt + lax.broadcasted_iota(jnp.int32, (n_blk_lanes, qt), 1)) >> int(math.log2(SLC_LEN))
        slc_valid = j_idx <= blk_t
        forced = (j_idx == 0) | (j_idx == blk_t) | (j_idx == blk_t - 1)
        p_sum = p[0:qt]
        for r in range(1, NSA_GROUP):
            p_sum = p_sum + p[r * qt:(r + 1) * qt]
        p_slc = sum(_nt_dot(ovl, part) for part in _split3(p_sum))
        score = jnp.where(slc_valid, p_slc + jnp.where(forced, FORCE_BONUS, 0.0), -jnp.inf)
        n_grp = n_blk_lanes // SUBLANES
        grp_rows = [score[c * SUBLANES:(c + 1) * SUBLANES] for c in range(n_grp)]
        grp_rank = [jnp.zeros((SUBLANES, qt), F32) for _ in range(n_grp)]
        for ii in range(n_blk_lanes):
            s_i = score[ii:ii + 1, :]
            for c in range(n_grp):
                rows = grp_rows[c]
                if c * SUBLANES > ii:
                    beats = jnp.where(s_i >= rows, 1.0, 0.0)
                elif (c + 1) * SUBLANES - 1 <= ii:
                    beats = jnp.where(s_i > rows, 1.0, 0.0)
                else:
                    beats = jnp.where(row_in_grp > ii - c * SUBLANES,
                                      jnp.where(s_i >= rows, 1.0, 0.0), jnp.where(s_i > rows, 1.0, 0.0))
                grp_rank[c] = grp_rank[c] + beats
        rank = jnp.concatenate(grp_rank, axis=0)
        sel_t = jnp.where(slc_valid, jnp.where(rank < SLC_TOPN, 1.0, 0.0), 0.0)
        return heads(jnp.concatenate([jnp.ones((HEAD_DIM, qt), F32), sel_t], axis=0).T)

    acc_w, o_cmp, sel_rows = [], [], []
    for g in groups:
        s_win, v_win = window_logits(g, q_pads[g] + pre_shifts[2 * g])
        acc_w.append(_dot(jnp.exp2(s_win).astype(_MXU_DTYPE), v_win))
        p = jnp.exp2(cmp_logits(g, q_pads[g] + pre_shifts[2 * g + 1]))
        p = p / jnp.maximum(jnp.sum(p, axis=-1, keepdims=True), 1e-30)
        o_cmp.append(_dot(p.astype(_MXU_DTYPE), vc_ref[0, kv[g]]))
        sel_rows.append(selected_blocks(p, blk[g]))

    n_kt = pl.program_id(1) + 1

    def augmented_q(g, shift):
        bias = jnp.where(lane_r >= HEAD_DIM, jnp.where(sel_rows[g] > 0.5, -shift, MASK_NEG), 0.0)
        return q_pads[g] + bias.astype(q_pads[g].dtype)

    def slc_logits(g, q_aug, kt):
        s = _nt_dot(q_aug, ksa_ref[0, kv[g], pl.ds(pl.multiple_of(kt * kw, kw), kw), :])
        return jnp.where(rel_w >= kt * kw - blk[g] * qt, s, MASK_NEG)

    def slc_row_max(g):
        q_aug = augmented_q(g, 0.0)
        return lax.fori_loop(0, n_kt, lambda kt, m_run: jnp.maximum(m_run, row_max(slc_logits(g, q_aug, kt))),
                             jnp.full((rq, 1), M_INIT, F32))

    q_augs = [augmented_q(g, slc_row_max(g) if exact_shift else logit_bound) for g in groups]

    def slc_tile(g, kt, width):
        start = pl.multiple_of(kt * kw, kw)
        s = _nt_dot(q_augs[g], ksa_ref[0, kv[g], pl.ds(start, width), :])
        s = jnp.where(rel_w[:, :width] >= kt * kw - blk[g] * qt, s, MASK_NEG)
        return _dot(jnp.exp2(s).astype(_MXU_DTYPE), vsa_ref[0, kv[g], pl.ds(start, width), :])

    for g in groups:
        acc_scr[g] = jnp.zeros((rq, LANES), F32)

    def slc_body(kt, _):
        for g in groups:
            acc_scr[g] += slc_tile(g, kt, kw)
        return 0

    lax.fori_loop(0, n_kt - 1, slc_body, 0)
    acc_s = [acc_scr[g] + slc_tile(g, n_kt - 1, qt if half == 0 else kw) for g, (_, half) in enumerate(chains)]

    head = lambda a, r: a[r * qt:(r + 1) * qt]

    def token_major(o):
        return jnp.concatenate([jnp.where(low_half, head(o, r), head(o, r + 1))
                                for r in range(0, NSA_GROUP, 2)], axis=1)

    def token_major_normalised(acc):
        inv = 1.0 / jnp.where(lane_r >= HEAD_DIM, acc, 1.0)
        pairs = []
        for r in range(0, NSA_GROUP, 2):
            even = head(acc, r) * pltpu.roll(head(inv, r), HEAD_DIM, axis=1)
            odd = pltpu.roll(head(acc, r + 1), HEAD_DIM, axis=1) * head(inv, r + 1)
            pairs.append(jnp.where(low_half, even, odd))
        return jnp.concatenate(pairs, axis=1)

    gw_cols = NSA_GROUP * HEAD_DIM
    for g, (kv_head, half) in enumerate(chains):
        rows = slice(half * qt, (half + 1) * qt)
        cols = slice(kv_head * gw_cols, (kv_head + 1) * gw_cols)
        out = (gc_ref[0, rows, cols] * token_major(o_cmp[g])
               + gs_ref[0, rows, cols] * token_major_normalised(acc_s[g])
               + gw_ref[0, rows, cols] * token_major_normalised(acc_w[g]))
        o_ref[0, rows, cols] = out.astype(o_ref.dtype)


def _sb_kernel(q_ref, k_ref, v_ref, uu_ref, g_ref, o_ref, *state_scr):
    i = pl.program_id(2)
    rows = q_ref.shape[1]
    pair_lanes = [slice(pr * LANES, (pr + 1) * LANES) for pr in range(q_ref.shape[2] // LANES)]
    band = rows // Q_BLOCK
    n_tiles = (i + 1) * band
    low_half = lax.broadcasted_iota(jnp.int32, (rows, LANES), 1) < HEAD_DIM
    low_half_k = lax.broadcasted_iota(jnp.int32, (Q_BLOCK, LANES), 1) < HEAD_DIM
    q_heads = []
    for lanes in pair_lanes:
        q_pair = q_ref[0, :, lanes]
        zero = jnp.zeros_like(q_pair)
        q_heads.append((jnp.where(low_half, q_pair, zero), jnp.where(low_half, zero, q_pair)))

    tri = (lax.broadcasted_iota(jnp.int32, (Q_BLOCK, LANES), 1)
           < lax.broadcasted_iota(jnp.int32, (Q_BLOCK, LANES), 0))

    def update_rows(x, r0, fn):
        return fn(x) if r0 == 0 else jnp.concatenate([x[:r0], fn(x[r0:])], axis=0)

    def on_diagonal(x, fill):
        masked = jnp.where(tri, x[:Q_BLOCK], fill)
        return masked if x.shape[0] == Q_BLOCK else jnp.concatenate([masked, x[Q_BLOCK:]], axis=0)

    def tile_step(start, carry, r0=0, diagonal=False):
        new_carry = []
        for pr, lanes in enumerate(pair_lanes):
            new_carry.extend(pair_tile_step(pr, lanes, start, carry[3 * pr:3 * pr + 3], r0, diagonal))
        return tuple(new_carry)

    def pair_tile_step(pr, lanes, start, carry, r0, diagonal):
        acc, laters = carry[0], carry[1:]
        k_pair = k_ref[0, pl.ds(start, Q_BLOCK), lanes]
        v_pair = v_ref[0, pl.ds(start, Q_BLOCK), lanes]
        zero_v = jnp.zeros_like(v_pair)
        v_bd = jnp.concatenate([jnp.where(low_half_k, v_pair, zero_v),
                                jnp.where(low_half_k, zero_v, v_pair)], axis=0)
        weights, new_laters = [], []
        for q_h, later in zip(q_heads[pr], laters):
            z = _nt_dot(q_h[r0:] if r0 else q_h, k_pair)
            sp = jnp.maximum(z, 0.0) + jnp.log2(1.0 + jnp.exp2(-jnp.abs(z)))
            if diagonal:
                sp = on_diagonal(sp, 0.0)
            hi, lo = _split2(sp)
            r = _dot(jnp.concatenate([hi, lo], axis=1), uu_ref[...])
            after = r[:, :LANES] + (later[r0:] if r0 else later)
            a = jnp.exp2(z - sp - after)
            if diagonal:
                a = on_diagonal(a, 0.0)
            weights.append(a.astype(_MXU_DTYPE))
            new_laters.append(update_rows(later, r0, lambda part, r=r: part + r[:, LANES:]))
        pv = _dot(jnp.concatenate(weights, axis=1), v_bd)
        return (update_rows(acc, r0, lambda part: part + pv), *new_laters)

    zeros = jnp.zeros((rows, LANES), F32)
    carry = (zeros,) * (3 * len(pair_lanes))
    for c in reversed(range(band)):
        carry = tile_step(pl.multiple_of(i * rows + c * Q_BLOCK, Q_BLOCK), carry, r0=c * Q_BLOCK, diagonal=True)

    def main_body(kg, carry):
        for u in range(SB_UNROLL):
            first_key = (n_tiles - band - 1 - (kg * SB_UNROLL + u)) * Q_BLOCK
            carry = tile_step(pl.multiple_of(first_key, Q_BLOCK), carry)
        return carry

    n_trips = (n_tiles - band) // SB_UNROLL

    def settled(carry):
        laters = [x for n, x in enumerate(carry) if n % 3]
        return jnp.min(functools.reduce(jnp.minimum, laters)) >= SB_UNDERFLOW_BITS

    def keep_going(state):
        kg, done = state
        return jnp.logical_and(kg < n_trips, jnp.logical_not(done))

    def save(carry):
        for ref, value in zip(state_scr, carry):
            ref[...] = value

    def main_step(state):
        kg, _ = state
        carry = main_body(kg, tuple(ref[...] for ref in state_scr))
        save(carry)
        return kg + 1, settled(carry)

    save(carry)
    lax.while_loop(keep_going, main_step, (0, False))
    for pr, lanes in enumerate(pair_lanes):
        o_ref[0, :, lanes] = (state_scr[3 * pr][...] * g_ref[0, :, lanes]).astype(o_ref.dtype)


def _outproj_kernel(x_ref, on_ref, os_ref, w_ref, o_ref):
    o_ref[...] = (x_ref[...] + _dot(on_ref[...], w_ref[:NSA_WIDTH, :])
                  + _dot(os_ref[...], w_ref[NSA_WIDTH:, :]))


def _rope_tables(pos, reps):
    inv_freq = jnp.power(ROPE_THETA, -jnp.arange(0, ROPE_DIM, 2, dtype=F32) / ROPE_DIM)
    ang = pos.astype(F32)[:, None] * inv_freq[None, :]
    cos, sin = jnp.cos(ang), jnp.sin(ang)
    n = pos.shape[0]
    rest = HEAD_DIM - ROPE_DIM
    cos_h = jnp.concatenate([cos, cos, jnp.ones((n, rest), F32)], axis=1)
    sin_h = jnp.concatenate([-sin, sin, jnp.zeros((n, rest), F32)], axis=1)
    return jnp.tile(cos_h, (1, reps)), jnp.tile(sin_h, (1, reps))


def _const_spec(shape):
    return pl.BlockSpec(shape, lambda *_: (0,) * len(shape))


def kernel(x, norm_gain, w_in, q_norm_gain, k_norm_cmp, k_norm_slc, k_norm_win,
           cmp_k_pos, cmp_k_w1, cmp_k_b1, cmp_k_w2, cmp_v_pos, cmp_v_w1, cmp_v_b1, cmp_v_w2, w_out):
    B, S, DM = x.shape
    D, G, R = HEAD_DIM, NSA_KV_HEADS, NSA_GROUP
    mxu = _MXU_DTYPE
    n_tok = B * S
    tm = ROW_TILE
    n_sblk = S // tm
    ncp = S // CMP_STRIDE
    n_slc = S // SLC_LEN
    n_cmp = (S - CMP_LEN) // CMP_STRIDE + 1
    assert S % tm == 0 and ncp % LANES == 0 and n_slc <= LANES - D and n_slc >= SLC_TOPN

    n_gl = NSA_HEADS * N_BRANCH
    gl0 = C_GN + n_gl
    w_mxu = w_in.astype(mxu)
    w_cat = jnp.concatenate([w_mxu[:, :C_GN], w_mxu[:, gl0:], w_mxu[:, C_GN:gl0],
                             jnp.zeros((DM, LANES - n_gl), mxu)], axis=1)
    assert w_cat.shape[1] == N_COLS
    pos = jnp.arange(S, dtype=jnp.int32)
    cos_t, sin_t = _rope_tables(pos, LANES // D)
    cmp_end = jnp.arange(ncp, dtype=jnp.int32) * CMP_STRIDE + (CMP_LEN - 1)
    cos_c, sin_c = _rope_tables(cmp_end, LANES // D)
    lane_i = np.arange(LANES)
    bd = (lane_i[:, None] // D == lane_i[None, :] // D).astype(np.float32)
    bd = jnp.asarray(np.concatenate([bd, bd], axis=0), mxu)
    eg = np.zeros((LANES, N_BRANCH * NSA_WIDTH), np.float32)
    for hh in range(NSA_HEADS):
        for br in range(N_BRANCH):
            eg[hh * N_BRANCH + br, br * NSA_WIDTH + hh * D:br * NSA_WIDTH + (hh + 1) * D] = 1.0
    eg = jnp.asarray(np.concatenate([eg, eg], axis=0), mxu)
    perm = np.zeros((LANES, LANES), np.float32)
    for c in range(ROPE_HALF):
        perm[c + ROPE_HALF, c] = 1.0
        perm[c, c + ROPE_HALF] = 1.0
    perm = jnp.asarray(perm, mxu)
    cs = np.arange(ncp) * CMP_STRIDE
    ss = np.arange(LANES - D) * SLC_LEN
    ovl = np.clip(np.minimum(cs[None, :] + CMP_LEN, ss[:, None] + SLC_LEN)
                  - np.maximum(cs[None, :], ss[:, None]), 0, None).astype(np.float32) / CMP_LEN
    ovl[:, n_cmp:] = 0.0
    ovl[n_slc:, :] = 0.0
    ovl = jnp.asarray(ovl, mxu)
    sidx = np.arange(LANES)
    tri = (sidx[:, None] > sidx[None, :]).astype(np.float32)
    uu_half = np.concatenate([tri, np.ones((LANES, LANES), np.float32)], axis=1)
    uu = jnp.asarray(np.concatenate([uu_half, uu_half], axis=0), mxu)

    row = lambda v: v.reshape(1, -1).astype(F32)
    x2 = x.reshape(n_tok, DM)

    tok_spec = lambda w: pl.BlockSpec((tm, w), lambda t: (t, 0))
    head_spec = lambda nh, w: pl.BlockSpec((1, nh, tm, w), lambda t: (t // n_sblk, 0, t % n_sblk, 0))
    tab_spec = pl.BlockSpec((tm, LANES), lambda t: (t % n_sblk, 0))
    chunk_w = CMP_STRIDE * D
    chunk_spec = pl.BlockSpec((1, G, tm // CMP_STRIDE, chunk_w), lambda t: (t // n_sblk, 0, t % n_sblk, 0))
    sds = jax.ShapeDtypeStruct
    outs = pl.pallas_call(
        functools.partial(_inproj_kernel, n_sblk=n_sblk),
        grid=(n_tok // tm,),
        in_specs=[tok_spec(DM), _const_spec((1, DM)), _const_spec((DM, N_COLS)),
                  _const_spec((1, NSA_WIDTH)), _const_spec((1, 2 * NSA_KV_WIDTH)),
                  _const_spec((2 * LANES, LANES)), tab_spec, tab_spec,
                  _const_spec((2 * LANES, N_BRANCH * NSA_WIDTH))],
        out_specs=[head_spec(NSA_HEADS, LANES), chunk_spec, chunk_spec,
                   head_spec(G, LANES), head_spec(G, LANES), head_spec(G, LANES), head_spec(G, LANES),
                   tok_spec(NSA_WIDTH), tok_spec(NSA_WIDTH), tok_spec(NSA_WIDTH),
                   tok_spec(SB_WIDTH), tok_spec(SB_WIDTH), tok_spec(SB_WIDTH), tok_spec(SB_WIDTH)],
        out_shape=[sds((B, NSA_HEADS, S, LANES), mxu), sds((B, G, ncp, chunk_w), F32), sds((B, G, ncp, chunk_w), F32),
                   sds((B, G, S, LANES), mxu), sds((B, G, S, LANES), mxu), sds((B, G, S, LANES), mxu),
                   sds((B, G, S, LANES), mxu),
                   sds((n_tok, NSA_WIDTH), F32), sds((n_tok, NSA_WIDTH), F32), sds((n_tok, NSA_WIDTH), F32),
                   sds((n_tok, SB_WIDTH), mxu), sds((n_tok, SB_WIDTH), mxu), sds((n_tok, SB_WIDTH), mxu),
                   sds((n_tok, SB_WIDTH), F32)],
        scratch_shapes=[pltpu.VMEM((2, tm, NSA_KV_WIDTH), F32)],
        compiler_params=pltpu.CompilerParams(dimension_semantics=("parallel",), vmem_limit_bytes=VMEM_LIMIT),
        name="inproj",
    )(x2, row(norm_gain), w_cat, row(jnp.tile(q_norm_gain, NSA_HEADS)),
      row(jnp.concatenate([jnp.tile(k_norm_slc, G), jnp.tile(k_norm_win, G)])), bd, cos_t, sin_t, eg)
    (q_nsa, kc_raw, vc_raw, ks_aug, vs_aug, k_win, vw_aug, g_cmp, g_slc, g_win,
     q_sb, k_sb, v_sb, g_sb) = outs

    chunks = lambda a: a.reshape(B * G, ncp, chunk_w)
    bg_spec = lambda r, w: pl.BlockSpec((1, r, w), lambda t: (t, 0, 0))
    k_cmp, v_cmp = pl.pallas_call(
        _compress_kernel,
        grid=(B * G,),
        in_specs=[bg_spec(ncp, chunk_w), bg_spec(ncp, chunk_w),
                  _const_spec((2, chunk_w)), _const_spec((2, chunk_w)),
                  _const_spec((CMP_LEN * D, D)), _const_spec((1, D)), _const_spec((D, LANES)),
                  _const_spec((CMP_LEN * D, D)), _const_spec((1, D)), _const_spec((D, LANES)),
                  _const_spec((1, LANES)), _const_spec((ncp, LANES)), _const_spec((ncp, LANES)),
                  _const_spec((LANES, LANES))],
        out_specs=[bg_spec(ncp, LANES), bg_spec(ncp, LANES)],
        out_shape=[sds((B * G, ncp, LANES), mxu), sds((B * G, ncp, LANES), mxu)],
        compiler_params=pltpu.CompilerParams(dimension_semantics=("parallel",), vmem_limit_bytes=VMEM_LIMIT),
        name="compress",
    )(chunks(kc_raw), chunks(vc_raw), cmp_k_pos.reshape(2, chunk_w), cmp_v_pos.reshape(2, chunk_w),
      cmp_k_w1.astype(mxu), row(cmp_k_b1), jnp.pad(cmp_k_w2, ((0, 0), (0, LANES - D))).astype(mxu),
      cmp_v_w1.astype(mxu), row(cmp_v_b1), jnp.tile(cmp_v_w2, (1, LANES // D)).astype(mxu),
      row(jnp.pad(k_norm_cmp, (0, LANES - D))), cos_c, sin_c, perm)
    k_cmp = k_cmp.reshape(B, G, ncp, LANES)
    v_cmp = v_cmp.reshape(B, G, ncp, LANES)

    k_gain = jnp.max(jnp.abs(jnp.stack([k_norm_cmp, k_norm_slc, k_norm_win])))
    logit_bound = jnp.max(jnp.abs(q_norm_gain)) * k_gain * (1.02 * D * SCALE * LOG2E)
    logit_bound = logit_bound.astype(F32).reshape(1)
    kv_spec = lambda r, w: pl.BlockSpec((1, G, r, w), lambda b, i: (b, 0, 0, 0))
    gate_spec = pl.BlockSpec((1, 2 * NSA_QT, NSA_WIDTH), lambda b, i: (b, i, 0))
    g3 = lambda a: a.reshape(B, S, NSA_WIDTH)
    nsa_call = lambda exact_shift: pl.pallas_call(
        functools.partial(_nsa_kernel, exact_shift=exact_shift),
        grid=(B, S // (2 * NSA_QT)),
        in_specs=[pl.BlockSpec(memory_space=pltpu.SMEM),
                  pl.BlockSpec((1, NSA_HEADS, 2 * NSA_QT, LANES), lambda b, i: (b, 0, i, 0)),
                  kv_spec(ncp, LANES), kv_spec(ncp, LANES), kv_spec(S, LANES), kv_spec(S, LANES),
                  kv_spec(S, LANES), kv_spec(S, LANES), _const_spec((LANES - D, ncp)),
                  gate_spec, gate_spec, gate_spec],
        out_specs=gate_spec,
        out_shape=sds((B, S, NSA_WIDTH), mxu),
        scratch_shapes=[pltpu.VMEM((2 * G, R * NSA_QT, LANES), F32)],
        compiler_params=pltpu.CompilerParams(dimension_semantics=("parallel", "arbitrary"),
                                             vmem_limit_bytes=VMEM_LIMIT),
        name="nsa_exact_shift" if exact_shift else "nsa",
    )
    nsa_operands = (logit_bound, q_nsa, k_cmp, v_cmp, ks_aug, vs_aug, k_win, vw_aug, ovl,
                    g3(g_cmp), g3(g_slc), g3(g_win))
    o_nsa = lax.cond(logit_bound[0] <= MAX_STATIC_SHIFT,
                     lambda ops: nsa_call(False)(*ops), lambda ops: nsa_call(True)(*ops), nsa_operands)

    sb3 = lambda a: a.reshape(B, S, SB_WIDTH)
    pair_q = pl.BlockSpec((1, SB_ROWS, SB_PAIRS * LANES), lambda b, hp, i: (b, i, hp))
    pair_kv = pl.BlockSpec((1, S, SB_PAIRS * LANES), lambda b, hp, i: (b, 0, hp))
    o_sb = pl.pallas_call(
        _sb_kernel,
        grid=(B, SB_WIDTH // (SB_PAIRS * LANES), S // SB_ROWS),
        in_specs=[pair_q, pair_kv, pair_kv, _const_spec((2 * LANES, 2 * LANES)), pair_q],
        out_specs=pair_q,
        out_shape=sds((B, S, SB_WIDTH), mxu),
        scratch_shapes=[pltpu.VMEM((SB_ROWS, LANES), F32)] * (3 * SB_PAIRS),
        compiler_params=pltpu.CompilerParams(dimension_semantics=("parallel", "parallel", "arbitrary"),
                                             vmem_limit_bytes=VMEM_LIMIT),
        name="stickbreak",
    )(sb3(q_sb), sb3(k_sb), sb3(v_sb), uu, sb3(g_sb))

    out_spec = lambda w: pl.BlockSpec((OUT_ROW_TILE, w), lambda t: (t, 0))
    out = pl.pallas_call(
        _outproj_kernel,
        grid=(n_tok // OUT_ROW_TILE,),
        in_specs=[out_spec(DM), out_spec(NSA_WIDTH), out_spec(SB_WIDTH),
                  _const_spec((NSA_WIDTH + SB_WIDTH, DM))],
        out_specs=out_spec(DM),
        out_shape=sds((n_tok, DM), x.dtype),
        compiler_params=pltpu.CompilerParams(dimension_semantics=("parallel",), vmem_limit_bytes=VMEM_LIMIT),
        name="outproj",
    )(x2, o_nsa.reshape(n_tok, NSA_WIDTH), o_sb.reshape(n_tok, SB_WIDTH), w_out.astype(mxu))
    return out.reshape(B, S, DM)
```

```python
import functools
import math

import numpy as np
import jax
import jax.numpy as jnp
from jax import lax
from jax.experimental import pallas as pl
from jax.experimental.pallas import tpu as pltpu

HEAD_DIM = 64
NSA_HEADS = 8
NSA_KV_HEADS = 2
NSA_GROUP = NSA_HEADS // NSA_KV_HEADS
SB_HEADS = 8
NSA_WIDTH = NSA_HEADS * HEAD_DIM
SB_WIDTH = SB_HEADS * HEAD_DIM
NSA_KV_WIDTH = NSA_KV_HEADS * HEAD_DIM
N_BRANCH = 3
CMP_LEN = 32
CMP_STRIDE = 16
SLC_LEN = 64
SLC_TOPN = 16
WINDOW = 512
Q_BLOCK = 128
ROPE_DIM = HEAD_DIM // 4
ROPE_HALF = ROPE_DIM // 2
ROPE_THETA = 500000.0
EPS = 1e-6
FORCE_BONUS = 1.0e4
SCALE = 1.0 / math.sqrt(HEAD_DIM)
LOG2E = math.log2(math.e)

LANES = 128
SUBLANES = 8
MASK_NEG = -1.0e30
M_INIT = -3.0e38
ROW_TILE = 512
OUT_ROW_TILE = 2048
NSA_QT = 256
SB_ROWS = 512
SB_UNROLL = 2
SB_PAIRS = 4
MAX_STATIC_SHIFT = 60.0
SB_UNDERFLOW_BITS = 160.0
VMEM_LIMIT = 56 * 1024 * 1024

_MXU_DTYPE = jnp.bfloat16
F32 = jnp.float32

C_Q = 0
C_KC = 512
C_KS = 768
C_KW = 1024
C_GL = 1280
C_GN = 1408
C_QSB = 1920
C_KSB = 2432
C_VSB = 2944
C_GSB = 3456
N_COLS = 3968


def _nt_dot(a, b):
    return lax.dot_general(a, b, (((1,), (1,)), ((), ())), preferred_element_type=F32)


def _dot(a, b):
    return jnp.dot(a, b, preferred_element_type=F32)


def _split2(v):
    hi = v.astype(_MXU_DTYPE)
    lo = (v - hi.astype(F32)).astype(_MXU_DTYPE)
    return hi, lo


def _split3(v):
    hi = v.astype(_MXU_DTYPE)
    r1 = v - hi.astype(F32)
    mid = r1.astype(_MXU_DTYPE)
    lo = (r1 - mid.astype(F32)).astype(_MXU_DTYPE)
    return hi, mid, lo


def _inproj_kernel(x_ref, ng_ref, w_ref, qg_ref, kg_ref, bd_ref, cos_ref, sin_ref, eg_ref,
                   q_ref, kc_ref, vc_ref, ksa_ref, vsa_ref, kw_ref, vwa_ref,
                   gc_ref, gs_ref, gw_ref, qsb_ref, ksb_ref, vsb_ref, gsb_ref, cmp_scr, *, n_sblk):
    tm = x_ref.shape[0]
    x = x_ref[...]
    ms = jnp.mean(x * x, axis=-1, keepdims=True)
    h = (x * lax.rsqrt(ms + EPS) * ng_ref[...]).astype(_MXU_DTYPE)

    def proj(lo, width):
        return _dot(h, w_ref[:, lo:lo + width])

    lane = lax.broadcasted_iota(jnp.int32, (tm, LANES), 1)
    low_half = lane < HEAD_DIM

    def head_norm_rope(y, gain):
        width = y.shape[1]
        rep = width // LANES
        hi, lo = _split2(y * y)
        ssum = jnp.concatenate(
            [_dot(jnp.concatenate([hi[:, c:c + LANES], lo[:, c:c + LANES]], axis=1), bd_ref[...])
             for c in range(0, width, LANES)], axis=1)
        yn = y * lax.rsqrt(ssum * (1.0 / HEAD_DIM) + EPS) * gain
        cos = jnp.concatenate([cos_ref[...]] * rep, axis=1) if rep > 1 else cos_ref[...]
        sin = jnp.concatenate([sin_ref[...]] * rep, axis=1) if rep > 1 else sin_ref[...]
        fwd = pltpu.roll(yn, ROPE_HALF, axis=1)
        bwd = pltpu.roll(yn, width - ROPE_HALF, axis=1)
        lane_w = lax.broadcasted_iota(jnp.int32, (tm, width), 1)
        partner = jnp.where((lane_w & (HEAD_DIM - 1)) < ROPE_HALF, bwd, fwd)
        return yn * cos + partner * sin

    def head_pair(slab, p):
        chunk = slab[:, p * LANES:(p + 1) * LANES]
        return chunk, pltpu.roll(chunk, HEAD_DIM, axis=1)

    qn = head_norm_rope(proj(C_Q, NSA_WIDTH), qg_ref[...]) * (SCALE * LOG2E)
    for p in range(NSA_HEADS // 2):
        ev, od = head_pair(qn, p)
        q_ref[0, 2 * p] = jnp.where(low_half, ev, 0.0).astype(q_ref.dtype)
        q_ref[0, 2 * p + 1] = jnp.where(low_half, od, 0.0).astype(q_ref.dtype)

    kv_cmp = proj(C_KC, 2 * NSA_KV_WIDTH)
    kv_slc = proj(C_KS, 2 * NSA_KV_WIDTH)
    kv_win = proj(C_KW, 2 * NSA_KV_WIDTH)

    n_rows = tm // CMP_STRIDE
    low_half_c = lax.broadcasted_iota(jnp.int32, (n_rows, LANES), 1) < HEAD_DIM
    for ref, p in ((kc_ref, 0), (vc_ref, 1)):
        cmp_scr[p] = kv_cmp[:, p * LANES:(p + 1) * LANES]
        for j in range(CMP_STRIDE // 2):
            a = cmp_scr[p, pl.ds(2 * j, n_rows, stride=CMP_STRIDE), :]
            b = cmp_scr[p, pl.ds(2 * j + 1, n_rows, stride=CMP_STRIDE), :]
            ref[0, 0, :, j * LANES:(j + 1) * LANES] = jnp.where(low_half_c, a, pltpu.roll(b, HEAD_DIM, axis=1))
            ref[0, 1, :, j * LANES:(j + 1) * LANES] = jnp.where(low_half_c, pltpu.roll(a, HEAD_DIM, axis=1), b)

    sblk = lax.rem(pl.program_id(0), n_sblk)
    row = lax.broadcasted_iota(jnp.int32, (tm, LANES), 0)
    key_blk = (sblk * tm + row) >> int(math.log2(SLC_LEN))
    onehot = jnp.where(lane - HEAD_DIM == key_blk, 1.0, 0.0)
    k_sw = head_norm_rope(jnp.concatenate([kv_slc[:, :NSA_KV_WIDTH], kv_win[:, :NSA_KV_WIDTH]], axis=1),
                          kg_ref[...])
    ev, od = head_pair(k_sw, 0)
    ksa_ref[0, 0] = jnp.where(low_half, ev, onehot).astype(ksa_ref.dtype)
    ksa_ref[0, 1] = jnp.where(low_half, od, onehot).astype(ksa_ref.dtype)
    ev, od = head_pair(kv_slc, 1)
    vsa_ref[0, 0] = jnp.where(low_half, ev, 1.0).astype(vsa_ref.dtype)
    vsa_ref[0, 1] = jnp.where(low_half, od, 1.0).astype(vsa_ref.dtype)

    one_lane = jnp.where(lane == HEAD_DIM, 1.0, 0.0)
    ev, od = head_pair(k_sw, 1)
    kw_ref[0, 0] = jnp.where(low_half, ev, one_lane).astype(kw_ref.dtype)
    kw_ref[0, 1] = jnp.where(low_half, od, one_lane).astype(kw_ref.dtype)
    ev, od = head_pair(kv_win, 1)
    vwa_ref[0, 0] = jnp.where(low_half, ev, 1.0).astype(vwa_ref.dtype)
    vwa_ref[0, 1] = jnp.where(low_half, od, 1.0).astype(vwa_ref.dtype)

    gn = proj(C_GN, NSA_WIDTH)
    silu_n = gn * jax.nn.sigmoid(gn)
    gl_split = jnp.concatenate(_split2(proj(C_GL, LANES)), axis=1)
    for br, ref in enumerate((gc_ref, gs_ref, gw_ref)):
        ref[...] = jax.nn.sigmoid(_dot(gl_split, eg_ref[:, br * NSA_WIDTH:(br + 1) * NSA_WIDTH])) * silu_n

    qsb_ref[...] = (proj(C_QSB, SB_WIDTH) * (SCALE * LOG2E)).astype(qsb_ref.dtype)
    ksb_ref[...] = proj(C_KSB, SB_WIDTH).astype(ksb_ref.dtype)
    vsb_ref[...] = proj(C_VSB, SB_WIDTH).astype(vsb_ref.dtype)
    gsb = proj(C_GSB, SB_WIDTH)
    gsb_ref[...] = gsb * jax.nn.sigmoid(gsb)


def _compress_kernel(kc_ref, vc_ref, posk_ref, posv_ref, w1k_ref, b1k_ref, w2k_ref,
                     w1v_ref, b1v_ref, w2v_ref, kg_ref, cos_ref, sin_ref, perm_ref,
                     kcmp_ref, vcmp_ref):
    half = CMP_STRIDE * HEAD_DIM

    def phi(c_ref, pos_ref, w1_ref, b1_ref, w2_ref):
        c = c_ref[0]
        n = c.shape[0]
        top = _dot((c + pos_ref[0:1, :]).astype(_MXU_DTYPE), w1_ref[:half, :])
        bot = _dot((c + pos_ref[1:2, :]).astype(_MXU_DTYPE), w1_ref[half:, :])
        hid = top + pltpu.roll(bot, n - 1, axis=0) + b1_ref[...]
        return _dot((hid * jax.nn.sigmoid(hid)).astype(_MXU_DTYPE), w2_ref[...])

    k = phi(kc_ref, posk_ref, w1k_ref, b1k_ref, w2k_ref)
    ms = jnp.sum(k * k, axis=-1, keepdims=True) * (1.0 / HEAD_DIM)
    kn = k * lax.rsqrt(ms + EPS) * kg_ref[...]
    hi, lo = _split2(kn)
    partner = _dot(hi, perm_ref[...]) + _dot(lo, perm_ref[...])
    one_lane = jnp.where(lax.broadcasted_iota(jnp.int32, k.shape, 1) == HEAD_DIM, 1.0, 0.0)
    kcmp_ref[0] = (kn * cos_ref[...] + partner * sin_ref[...] + one_lane).astype(kcmp_ref.dtype)
    vcmp_ref[0] = phi(vc_ref, posv_ref, w1v_ref, b1v_ref, w2v_ref).astype(vcmp_ref.dtype)


def _nsa_kernel(bound_ref, q_ref, kc_ref, vc_ref, ksa_ref, vsa_ref, kw_ref, vwa_ref, ovl_ref,
                gc_ref, gs_ref, gw_ref, o_ref, acc_scr, *, exact_shift):
    qt = NSA_QT
    kw = 2 * qt
    assert q_ref.shape[2] == kw
    chains = [(g, half) for half in range(2) for g in range(kc_ref.shape[1])]
    groups = range(len(chains))
    kv = [g for g, _ in chains]
    blk = [2 * pl.program_id(1) + half for _, half in chains]
    rq = NSA_GROUP * qt
    n_blk_lanes = LANES - HEAD_DIM
    q_pads = [q_ref[0, g * NSA_GROUP:(g + 1) * NSA_GROUP, half * qt:(half + 1) * qt].reshape(rq, LANES)
              for g, half in chains]
    heads = lambda x: jnp.concatenate([x] * NSA_GROUP, axis=0)
    t_tok = lax.broadcasted_iota(jnp.int32, (qt, 1), 0)
    rel_w = heads(t_tok - lax.broadcasted_iota(jnp.int32, (qt, kw), 1))
    rel = rel_w[:, :qt]
    lane_q = lax.broadcasted_iota(jnp.int32, (qt, LANES), 1)
    lane_r = heads(lane_q)
    low_half = lane_q < HEAD_DIM

    logit_bound = bound_ref[0]
    row_max = lambda s: jnp.max(s, axis=-1, keepdims=True)

    def shift_lane(shift):
        return jnp.where(lane_r == HEAD_DIM, -shift, 0.0).astype(q_pads[0].dtype)

    def window_logits(g, q_lhs):
        i = blk[g]
        s_parts, v_parts = [], []
        for back in range(WINDOW // qt + 1):
            start = pl.multiple_of(jnp.maximum(i - back, 0) * qt, qt)
            s = _nt_dot(q_lhs, kw_ref[0, kv[g], pl.ds(start, qt), :])
            v = vwa_ref[0, kv[g], pl.ds(start, qt), :]
            if back == 0:
                s = jnp.where(rel >= 0, s, MASK_NEG)
            elif (back + 1) * qt <= WINDOW:
                v = v * jnp.where(i >= back, 1.0, 0.0).astype(v.dtype)
            else:
                s = jnp.where(rel < (WINDOW - back * qt) - jnp.where(i >= back, 0, WINDOW), s, MASK_NEG)
            s_parts.append(s)
            v_parts.append(v)
        return jnp.concatenate(s_parts, axis=1), jnp.concatenate(v_parts, axis=0)

    def cmp_logits(g, q_lhs):
        kc = kc_ref[0, kv[g]]
        cmp_end = lax.broadcasted_iota(jnp.int32, (rq, kc.shape[0]), 1) * CMP_STRIDE + (CMP_LEN - 1)
        return jnp.where(cmp_end <= blk[g] * qt + heads(t_tok), _nt_dot(q_lhs, kc), MASK_NEG)

    if exact_shift:
        pre_shifts = [shift_lane(m) for g in groups
                      for m in (row_max(window_logits(g, q_pads[g])[0]), row_max(cmp_logits(g, q_pads[g])))]
    else:
        pre_shifts = [shift_lane(logit_bound)] * (2 * len(chains))

    ovl = ovl_ref[...]
    j_idx = lax.broadcasted_iota(jnp.int32, (n_blk_lanes, qt), 0)
    row_in_grp = lax.broadcasted_iota(jnp.int32, (SUBLANES, qt), 0)

    def selected_blocks(p, i):
        blk_t = (i * qt + lax.broadcasted_iota(jnp.int32, (n_blk_lanes, qt), 1)) >> int(math.log2(SLC_LEN))
        slc_valid = j_idx <= blk_t
        forced = (j_idx == 0) | (j_idx == blk_t) | (j_idx == blk_t - 1)
        p_sum = p[0:qt]
        for r in range(1, NSA_GROUP):
            p_sum = p_sum + p[r * qt:(r + 1) * qt]
        p_slc = sum(_nt_dot(ovl, part) for part in _split3(p_sum))
        score = jnp.where(slc_valid, p_slc + jnp.where(forced, FORCE_BONUS, 0.0), -jnp.inf)
        n_grp = n_blk_lanes // SUBLANES
        grp_rows = [score[c * SUBLANES:(c + 1) * SUBLANES] for c in range(n_grp)]
        grp_rank = [jnp.zeros((SUBLANES, qt), F32) for _ in range(n_grp)]
        for ii in range(n_blk_lanes):
            s_i = score[ii:ii + 1, :]
            for c in range(n_grp):
                rows = grp_rows[c]
                if c * SUBLANES > ii:
                    beats = jnp.where(s_i >= rows, 1.0, 0.0)
                elif (c + 1) * SUBLANES - 1 <= ii:
                    beats = jnp.where(s_i > rows, 1.0, 0.0)
                else:
                    beats = jnp.where(row_in_grp > ii - c * SUBLANES,
                                      jnp.where(s_i >= rows, 1.0, 0.0), jnp.where(s_i > rows, 1.0, 0.0))
                grp_rank[c] = grp_rank[c] + beats
        rank = jnp.concatenate(grp_rank, axis=0)
        sel_t = jnp.where(slc_valid, jnp.where(rank < SLC_TOPN, 1.0, 0.0), 0.0)
        return heads(jnp.concatenate([jnp.ones((HEAD_DIM, qt), F32), sel_t], axis=0).T)

    acc_w, o_cmp, sel_rows = [], [], []
    for g in groups:
        s_win, v_win = window_logits(g, q_pads[g] + pre_shifts[2 * g])
        acc_w.append(_dot(jnp.exp2(s_win).astype(_MXU_DTYPE), v_win))
        p = jnp.exp2(cmp_logits(g, q_pads[g] + pre_shifts[2 * g + 1]))
        p = p / jnp.maximum(jnp.sum(p, axis=-1, keepdims=True), 1e-30)
        o_cmp.append(_dot(p.astype(_MXU_DTYPE), vc_ref[0, kv[g]]))
        sel_rows.append(selected_blocks(p, blk[g]))

    n_kt = pl.program_id(1) + 1

    def augmented_q(g, shift):
        bias = jnp.where(lane_r >= HEAD_DIM, jnp.where(sel_rows[g] > 0.5, -shift, MASK_NEG), 0.0)
        return q_pads[g] + bias.astype(q_pads[g].dtype)

    def slc_logits(g, q_aug, kt):
        s = _nt_dot(q_aug, ksa_ref[0, kv[g], pl.ds(pl.multiple_of(kt * kw, kw), kw), :])
        return jnp.where(rel_w >= kt * kw - blk[g] * qt, s, MASK_NEG)

    def slc_row_max(g):
        q_aug = augmented_q(g, 0.0)
        return lax.fori_loop(0, n_kt, lambda kt, m_run: jnp.maximum(m_run, row_max(slc_logits(g, q_aug, kt))),
                             jnp.full((rq, 1), M_INIT, F32))

    q_augs = [augmented_q(g, slc_row_max(g) if exact_shift else logit_bound) for g in groups]

    def slc_tile(g, kt, width):
        start = pl.multiple_of(kt * kw, kw)
        s = _nt_dot(q_augs[g], ksa_ref[0, kv[g], pl.ds(start, width), :])
        s = jnp.where(rel_w[:, :width] >= kt * kw - blk[g] * qt, s, MASK_NEG)
        return _dot(jnp.exp2(s).astype(_MXU_DTYPE), vsa_ref[0, kv[g], pl.ds(start, width), :])

    for g in groups:
        acc_scr[g] = jnp.zeros((rq, LANES), F32)

    def slc_body(kt, _):
        for g in groups:
            acc_scr[g] += slc_tile(g, kt, kw)
        return 0

    lax.fori_loop(0, n_kt - 1, slc_body, 0)
    acc_s = [acc_scr[g] + slc_tile(g, n_kt - 1, qt if half == 0 else kw) for g, (_, half) in enumerate(chains)]

    head = lambda a, r: a[r * qt:(r + 1) * qt]

    def token_major(o):
        return jnp.concatenate([jnp.where(low_half, head(o, r), head(o, r + 1))
                                for r in range(0, NSA_GROUP, 2)], axis=1)

    def token_major_normalised(acc):
        inv = 1.0 / jnp.where(lane_r >= HEAD_DIM, acc, 1.0)
        pairs = []
        for r in range(0, NSA_GROUP, 2):
            even = head(acc, r) * pltpu.roll(head(inv, r), HEAD_DIM, axis=1)
            odd = pltpu.roll(head(acc, r + 1), HEAD_DIM, axis=1) * head(inv, r + 1)
            pairs.append(jnp.where(low_half, even, odd))
        return jnp.concatenate(pairs, axis=1)

    gw_cols = NSA_GROUP * HEAD_DIM
    for g, (kv_head, half) in enumerate(chains):
        rows = slice(half * qt, (half + 1) * qt)
        cols = slice(kv_head * gw_cols, (kv_head + 1) * gw_cols)
        out = (gc_ref[0, rows, cols] * token_major(o_cmp[g])
               + gs_ref[0, rows, cols] * token_major_normalised(acc_s[g])
               + gw_ref[0, rows, cols] * token_major_normalised(acc_w[g]))
        o_ref[0, rows, cols] = out.astype(o_ref.dtype)


def _sb_kernel(q_ref, k_ref, v_ref, uu_ref, g_ref, o_ref, *state_scr):
    i = pl.program_id(2)
    rows = q_ref.shape[1]
    pair_lanes = [slice(pr * LANES, (pr + 1) * LANES) for pr in range(q_ref.shape[2] // LANES)]
    band = rows // Q_BLOCK
    n_tiles = (i + 1) * band
    low_half = lax.broadcasted_iota(jnp.int32, (rows, LANES), 1) < HEAD_DIM
    low_half_k = lax.broadcasted_iota(jnp.int32, (Q_BLOCK, LANES), 1) < HEAD_DIM
    q_heads = []
    for lanes in pair_lanes:
        q_pair = q_ref[0, :, lanes]
        zero = jnp.zeros_like(q_pair)
        q_heads.append((jnp.where(low_half, q_pair, zero), jnp.where(low_half, zero, q_pair)))

    tri = (lax.broadcasted_iota(jnp.int32, (Q_BLOCK, LANES), 1)
           < lax.broadcasted_iota(jnp.int32, (Q_BLOCK, LANES), 0))

    def update_rows(x, r0, fn):
        return fn(x) if r0 == 0 else jnp.concatenate([x[:r0], fn(x[r0:])], axis=0)

    def on_diagonal(x, fill):
        masked = jnp.where(tri, x[:Q_BLOCK], fill)
        return masked if x.shape[0] == Q_BLOCK else jnp.concatenate([masked, x[Q_BLOCK:]], axis=0)

    def tile_step(start, carry, r0=0, diagonal=False):
        new_carry = []
        for pr, lanes in enumerate(pair_lanes):
            new_carry.extend(pair_tile_step(pr, lanes, start, carry[3 * pr:3 * pr + 3], r0, diagonal))
        return tuple(new_carry)

    def pair_tile_step(pr, lanes, start, carry, r0, diagonal):
        acc, laters = carry[0], carry[1:]
        k_pair = k_ref[0, pl.ds(start, Q_BLOCK), lanes]
        v_pair = v_ref[0, pl.ds(start, Q_BLOCK), lanes]
        zero_v = jnp.zeros_like(v_pair)
        v_bd = jnp.concatenate([jnp.where(low_half_k, v_pair, zero_v),
                                jnp.where(low_half_k, zero_v, v_pair)], axis=0)
        weights, new_laters = [], []
        for q_h, later in zip(q_heads[pr], laters):
            z = _nt_dot(q_h[r0:] if r0 else q_h, k_pair)
            sp = jnp.maximum(z, 0.0) + jnp.log2(1.0 + jnp.exp2(-jnp.abs(z)))
            if diagonal:
                sp = on_diagonal(sp, 0.0)
            hi, lo = _split2(sp)
            r = _dot(jnp.concatenate([hi, lo], axis=1), uu_ref[...])
            after = r[:, :LANES] + (later[r0:] if r0 else later)
            a = jnp.exp2(z - sp - after)
            if diagonal:
                a = on_diagonal(a, 0.0)
            weights.append(a.astype(_MXU_DTYPE))
            new_laters.append(update_rows(later, r0, lambda part, r=r: part + r[:, LANES:]))
        pv = _dot(jnp.concatenate(weights, axis=1), v_bd)
        return (update_rows(acc, r0, lambda part: part + pv), *new_laters)

    zeros = jnp.zeros((rows, LANES), F32)
    carry = (zeros,) * (3 * len(pair_lanes))
    for c in reversed(range(band)):
        carry = tile_step(pl.multiple_of(i * rows + c * Q_BLOCK, Q_BLOCK), carry, r0=c * Q_BLOCK, diagonal=True)

    def main_body(kg, carry):
        for u in range(SB_UNROLL):
            first_key = (n_tiles - band - 1 - (kg * SB_UNROLL + u)) * Q_BLOCK
            carry = tile_step(pl.multiple_of(first_key, Q_BLOCK), carry)
        return carry

    n_trips = (n_tiles - band) // SB_UNROLL

    def settled(carry):
        laters = [x for n, x in enumerate(carry) if n % 3]
        return jnp.min(functools.reduce(jnp.minimum, laters)) >= SB_UNDERFLOW_BITS

    def keep_going(state):
        kg, done = state
        return jnp.logical_and(kg < n_trips, jnp.logical_not(done))

    def save(carry):
        for ref, value in zip(state_scr, carry):
            ref[...] = value

    def main_step(state):
        kg, _ = state
        carry = main_body(kg, tuple(ref[...] for ref in state_scr))
        save(carry)
        return kg + 1, settled(carry)

    save(carry)
    lax.while_loop(keep_going, main_step, (0, False))
    for pr, lanes in enumerate(pair_lanes):
        o_ref[0, :, lanes] = (state_scr[3 * pr][...] * g_ref[0, :, lanes]).astype(o_ref.dtype)


def _outproj_kernel(x_ref, on_ref, os_ref, w_ref, o_ref):
    o_ref[...] = (x_ref[...] + _dot(on_ref[...], w_ref[:NSA_WIDTH, :])
                  + _dot(os_ref[...], w_ref[NSA_WIDTH:, :]))


def _rope_tables(pos, reps):
    inv_freq = jnp.power(ROPE_THETA, -jnp.arange(0, ROPE_DIM, 2, dtype=F32) / ROPE_DIM)
    ang = pos.astype(F32)[:, None] * inv_freq[None, :]
    cos, sin = jnp.cos(ang), jnp.sin(ang)
    n = pos.shape[0]
    rest = HEAD_DIM - ROPE_DIM
    cos_h = jnp.concatenate([cos, cos, jnp.ones((n, rest), F32)], axis=1)
    sin_h = jnp.concatenate([-sin, sin, jnp.zeros((n, rest), F32)], axis=1)
    return jnp.tile(cos_h, (1, reps)), jnp.tile(sin_h, (1, reps))


def _const_spec(shape):
    return pl.BlockSpec(shape, lambda *_: (0,) * len(shape))


def kernel(x, norm_gain, w_in, q_norm_gain, k_norm_cmp, k_norm_slc, k_norm_win,
           cmp_k_pos, cmp_k_w1, cmp_k_b1, cmp_k_w2, cmp_v_pos, cmp_v_w1, cmp_v_b1, cmp_v_w2, w_out):
    B, S, DM = x.shape
    D, G, R = HEAD_DIM, NSA_KV_HEADS, NSA_GROUP
    mxu = _MXU_DTYPE
    n_tok = B * S
    tm = ROW_TILE
    n_sblk = S // tm
    ncp = S // CMP_STRIDE
    n_slc = S // SLC_LEN
    n_cmp = (S - CMP_LEN) // CMP_STRIDE + 1
    assert S % tm == 0 and ncp % LANES == 0 and n_slc <= LANES - D and n_slc >= SLC_TOPN

    gl_end = C_GL + NSA_HEADS * N_BRANCH
    w_cat = (jnp.zeros((DM, N_COLS), w_in.dtype).at[:, :gl_end].set(w_in[:, :gl_end])
             .at[:, C_GN:].set(w_in[:, gl_end:])).astype(mxu)
    assert w_cat.shape[1] == N_COLS
    pos = jnp.arange(S, dtype=jnp.int32)
    cos_t, sin_t = _rope_tables(pos, LANES // D)
    cmp_end = jnp.arange(ncp, dtype=jnp.int32) * CMP_STRIDE + (CMP_LEN - 1)
    cos_c, sin_c = _rope_tables(cmp_end, LANES // D)
    lane_i = np.arange(LANES)
    bd = (lane_i[:, None] // D == lane_i[None, :] // D).astype(np.float32)
    bd = jnp.asarray(np.concatenate([bd, bd], axis=0), mxu)
    eg = np.zeros((LANES, N_BRANCH * NSA_WIDTH), np.float32)
    for hh in range(NSA_HEADS):
        for br in range(N_BRANCH):
            eg[hh * N_BRANCH + br, br * NSA_WIDTH + hh * D:br * NSA_WIDTH + (hh + 1) * D] = 1.0
    eg = jnp.asarray(np.concatenate([eg, eg], axis=0), mxu)
    perm = np.zeros((LANES, LANES), np.float32)
    for c in range(ROPE_HALF):
        perm[c + ROPE_HALF, c] = 1.0
        perm[c, c + ROPE_HALF] = 1.0
    perm = jnp.asarray(perm, mxu)
    cs = np.arange(ncp) * CMP_STRIDE
    ss = np.arange(LANES - D) * SLC_LEN
    ovl = np.clip(np.minimum(cs[None, :] + CMP_LEN, ss[:, None] + SLC_LEN)
                  - np.maximum(cs[None, :], ss[:, None]), 0, None).astype(np.float32) / CMP_LEN
    ovl[:, n_cmp:] = 0.0
    ovl[n_slc:, :] = 0.0
    ovl = jnp.asarray(ovl, mxu)
    sidx = np.arange(LANES)
    tri = (sidx[:, None] > sidx[None, :]).astype(np.float32)
    uu_half = np.concatenate([tri, np.ones((LANES, LANES), np.float32)], axis=1)
    uu = jnp.asarray(np.concatenate([uu_half, uu_half], axis=0), mxu)

    row = lambda v: v.reshape(1, -1).astype(F32)
    x2 = x.reshape(n_tok, DM)

    tok_spec = lambda w: pl.BlockSpec((tm, w), lambda t: (t, 0))
    head_spec = lambda nh, w: pl.BlockSpec((1, nh, tm, w), lambda t: (t // n_sblk, 0, t % n_sblk, 0))
    tab_spec = pl.BlockSpec((tm, LANES), lambda t: (t % n_sblk, 0))
    chunk_w = CMP_STRIDE * D
    chunk_spec = pl.BlockSpec((1, G, tm // CMP_STRIDE, chunk_w), lambda t: (t // n_sblk, 0, t % n_sblk, 0))
    sds = jax.ShapeDtypeStruct
    outs = pl.pallas_call(
        functools.partial(_inproj_kernel, n_sblk=n_sblk),
        grid=(n_tok // tm,),
        in_specs=[tok_spec(DM), _const_spec((1, DM)), _const_spec((DM, N_COLS)),
                  _const_spec((1, NSA_WIDTH)), _const_spec((1, 2 * NSA_KV_WIDTH)),
                  _const_spec((2 * LANES, LANES)), tab_spec, tab_spec,
                  _const_spec((2 * LANES, N_BRANCH * NSA_WIDTH))],
        out_specs=[head_spec(NSA_HEADS, LANES), chunk_spec, chunk_spec,
                   head_spec(G, LANES), head_spec(G, LANES), head_spec(G, LANES), head_spec(G, LANES),
                   tok_spec(NSA_WIDTH), tok_spec(NSA_WIDTH), tok_spec(NSA_WIDTH),
                   tok_spec(SB_WIDTH), tok_spec(SB_WIDTH), tok_spec(SB_WIDTH), tok_spec(SB_WIDTH)],
        out_shape=[sds((B, NSA_HEADS, S, LANES), mxu), sds((B, G, ncp, chunk_w), F32), sds((B, G, ncp, chunk_w), F32),
                   sds((B, G, S, LANES), mxu), sds((B, G, S, LANES), mxu), sds((B, G, S, LANES), mxu),
                   sds((B, G, S, LANES), mxu),
                   sds((n_tok, NSA_WIDTH), F32), sds((n_tok, NSA_WIDTH), F32), sds((n_tok, NSA_WIDTH), F32),
                   sds((n_tok, SB_WIDTH), mxu), sds((n_tok, SB_WIDTH), mxu), sds((n_tok, SB_WIDTH), mxu),
                   sds((n_tok, SB_WIDTH), F32)],
        scratch_shapes=[pltpu.VMEM((2, tm, NSA_KV_WIDTH), F32)],
        compiler_params=pltpu.CompilerParams(dimension_semantics=("parallel",), vmem_limit_bytes=VMEM_LIMIT),
        name="inproj",
    )(x2, row(norm_gain), w_cat, row(jnp.tile(q_norm_gain, NSA_HEADS)),
      row(jnp.concatenate([jnp.tile(k_norm_slc, G), jnp.tile(k_norm_win, G)])), bd, cos_t, sin_t, eg)
    (q_nsa, kc_raw, vc_raw, ks_aug, vs_aug, k_win, vw_aug, g_cmp, g_slc, g_win,
     q_sb, k_sb, v_sb, g_sb) = outs

    chunks = lambda a: a.reshape(B * G, ncp, chunk_w)
    bg_spec = lambda r, w: pl.BlockSpec((1, r, w), lambda t: (t, 0, 0))
    k_cmp, v_cmp = pl.pallas_call(
        _compress_kernel,
        grid=(B * G,),
        in_specs=[bg_spec(ncp, chunk_w), bg_spec(ncp, chunk_w),
                  _const_spec((2, chunk_w)), _const_spec((2, chunk_w)),
                  _const_spec((CMP_LEN * D, D)), _const_spec((1, D)), _const_spec((D, LANES)),
                  _const_spec((CMP_LEN * D, D)), _const_spec((1, D)), _const_spec((D, LANES)),
                  _const_spec((1, LANES)), _const_spec((ncp, LANES)), _const_spec((ncp, LANES)),
                  _const_spec((LANES, LANES))],
        out_specs=[bg_spec(ncp, LANES), bg_spec(ncp, LANES)],
        out_shape=[sds((B * G, ncp, LANES), mxu), sds((B * G, ncp, LANES), mxu)],
        compiler_params=pltpu.CompilerParams(dimension_semantics=("parallel",), vmem_limit_bytes=VMEM_LIMIT),
        name="compress",
    )(chunks(kc_raw), chunks(vc_raw), cmp_k_pos.reshape(2, chunk_w), cmp_v_pos.reshape(2, chunk_w),
      cmp_k_w1.astype(mxu), row(cmp_k_b1), jnp.pad(cmp_k_w2, ((0, 0), (0, LANES - D))).astype(mxu),
      cmp_v_w1.astype(mxu), row(cmp_v_b1), jnp.tile(cmp_v_w2, (1, LANES // D)).astype(mxu),
      row(jnp.pad(k_norm_cmp, (0, LANES - D))), cos_c, sin_c, perm)
    k_cmp = k_cmp.reshape(B, G, ncp, LANES)
    v_cmp = v_cmp.reshape(B, G, ncp, LANES)

    k_gain = jnp.max(jnp.abs(jnp.stack([k_norm_cmp, k_norm_slc, k_norm_win])))
    logit_bound = jnp.max(jnp.abs(q_norm_gain)) * k_gain * (1.02 * D * SCALE * LOG2E)
    logit_bound = logit_bound.astype(F32).reshape(1)
    kv_spec = lambda r, w: pl.BlockSpec((1, G, r, w), lambda b, i: (b, 0, 0, 0))
    gate_spec = pl.BlockSpec((1, 2 * NSA_QT, NSA_WIDTH), lambda b, i: (b, i, 0))
    g3 = lambda a: a.reshape(B, S, NSA_WIDTH)
    nsa_call = lambda exact_shift: pl.pallas_call(
        functools.partial(_nsa_kernel, exact_shift=exact_shift),
        grid=(B, S // (2 * NSA_QT)),
        in_specs=[pl.BlockSpec(memory_space=pltpu.SMEM),
                  pl.BlockSpec((1, NSA_HEADS, 2 * NSA_QT, LANES), lambda b, i: (b, 0, i, 0)),
                  kv_spec(ncp, LANES), kv_spec(ncp, LANES), kv_spec(S, LANES), kv_spec(S, LANES),
                  kv_spec(S, LANES), kv_spec(S, LANES), _const_spec((LANES - D, ncp)),
                  gate_spec, gate_spec, gate_spec],
        out_specs=gate_spec,
        out_shape=sds((B, S, NSA_WIDTH), mxu),
        scratch_shapes=[pltpu.VMEM((2 * G, R * NSA_QT, LANES), F32)],
        compiler_params=pltpu.CompilerParams(dimension_semantics=("parallel", "arbitrary"),
                                             vmem_limit_bytes=VMEM_LIMIT),
        name="nsa_exact_shift" if exact_shift else "nsa",
    )
    nsa_operands = (logit_bound, q_nsa, k_cmp, v_cmp, ks_aug, vs_aug, k_win, vw_aug, ovl,
                    g3(g_cmp), g3(g_slc), g3(g_win))
    o_nsa = lax.cond(logit_bound[0] <= MAX_STATIC_SHIFT,
                     lambda ops: nsa_call(False)(*ops), lambda ops: nsa_call(True)(*ops), nsa_operands)

    sb3 = lambda a: a.reshape(B, S, SB_WIDTH)
    pair_q = pl.BlockSpec((1, SB_ROWS, SB_PAIRS * LANES), lambda b, hp, i: (b, i, hp))
    pair_kv = pl.BlockSpec((1, S, SB_PAIRS * LANES), lambda b, hp, i: (b, 0, hp))
    o_sb = pl.pallas_call(
        _sb_kernel,
        grid=(B, SB_WIDTH // (SB_PAIRS * LANES), S // SB_ROWS),
        in_specs=[pair_q, pair_kv, pair_kv, _const_spec((2 * LANES, 2 * LANES)), pair_q],
        out_specs=pair_q,
        out_shape=sds((B, S, SB_WIDTH), mxu),
        scratch_shapes=[pltpu.VMEM((SB_ROWS, LANES), F32)] * (3 * SB_PAIRS),
        compiler_params=pltpu.CompilerParams(dimension_semantics=("parallel", "parallel", "arbitrary"),
                                             vmem_limit_bytes=VMEM_LIMIT),
        name="stickbreak",
    )(sb3(q_sb), sb3(k_sb), sb3(v_sb), uu, sb3(g_sb))

    out_spec = lambda w: pl.BlockSpec((OUT_ROW_TILE, w), lambda t: (t, 0))
    out = pl.pallas_call(
        _outproj_kernel,
        grid=(n_tok // OUT_ROW_TILE,),
        in_specs=[out_spec(DM), out_spec(NSA_WIDTH), out_spec(SB_WIDTH),
                  _const_spec((NSA_WIDTH + SB_WIDTH, DM))],
        out_specs=out_spec(DM),
        out_shape=sds((n_tok, DM), x.dtype),
        compiler_params=pltpu.CompilerParams(dimension_semantics=("parallel",), vmem_limit_bytes=VMEM_LIMIT),
        name="outproj",
    )(x2, o_nsa.reshape(n_tok, NSA_WIDTH), o_sb.reshape(n_tok, SB_WIDTH), w_out.astype(mxu))
    return out.reshape(B, S, DM)
```

```python
import functools
import math

import numpy as np
import jax
import jax.numpy as jnp
from jax import lax
from jax.experimental import pallas as pl
from jax.experimental.pallas import tpu as pltpu

HEAD_DIM = 64
NSA_HEADS = 8
NSA_KV_HEADS = 2
NSA_GROUP = NSA_HEADS // NSA_KV_HEADS
SB_HEADS = 8
NSA_WIDTH = NSA_HEADS * HEAD_DIM
SB_WIDTH = SB_HEADS * HEAD_DIM
NSA_KV_WIDTH = NSA_KV_HEADS * HEAD_DIM
N_BRANCH = 3
CMP_LEN = 32
CMP_STRIDE = 16
SLC_LEN = 64
SLC_TOPN = 16
WINDOW = 512
Q_BLOCK = 128
ROPE_DIM = HEAD_DIM // 4
ROPE_HALF = ROPE_DIM // 2
ROPE_THETA = 500000.0
EPS = 1e-6
FORCE_BONUS = 1.0e4
SCALE = 1.0 / math.sqrt(HEAD_DIM)
LOG2E = math.log2(math.e)

LANES = 128
SUBLANES = 8
MASK_NEG = -1.0e30
M_INIT = -3.0e38
ROW_TILE = 512
OUT_ROW_TILE = 2048
NSA_QT = 256
SB_ROWS = 512
SB_UNROLL = 2
SB_PAIRS = 4
MAX_STATIC_SHIFT = 60.0
SB_UNDERFLOW_BITS = 160.0
VMEM_LIMIT = 56 * 1024 * 1024

_MXU_DTYPE = jnp.bfloat16
F32 = jnp.float32

C_Q = 0
C_KC = 512
C_KS = 768
C_KW = 1024
C_GL = 1280
C_GN = 1408
C_QSB = 1920
C_KSB = 2432
C_VSB = 2944
C_GSB = 3456
N_COLS = 3968


def _nt_dot(a, b):
    return lax.dot_general(a, b, (((1,), (1,)), ((), ())), preferred_element_type=F32)


def _dot(a, b):
    return jnp.dot(a, b, preferred_element_type=F32)


def _split2(v):
    hi = v.astype(_MXU_DTYPE)
    lo = (v - hi.astype(F32)).astype(_MXU_DTYPE)
    return hi, lo


def _split3(v):
    hi = v.astype(_MXU_DTYPE)
    r1 = v - hi.astype(F32)
    mid = r1.astype(_MXU_DTYPE)
    lo = (r1 - mid.astype(F32)).astype(_MXU_DTYPE)
    return hi, mid, lo


def _inproj_kernel(x_ref, ng_ref, w_ref, qg_ref, kg_ref, bd_ref, cos_ref, sin_ref, eg_ref,
                   q_ref, kc_ref, vc_ref, ksa_ref, vsa_ref, kw_ref, vwa_ref,
                   gc_ref, gs_ref, gw_ref, qsb_ref, ksb_ref, vsb_ref, gsb_ref, cmp_scr, *, n_sblk):
    tm = x_ref.shape[0]
    x = x_ref[...]
    ms = jnp.mean(x * x, axis=-1, keepdims=True)
    h = (x * lax.rsqrt(ms + EPS) * ng_ref[...]).astype(_MXU_DTYPE)

    def proj(lo, width):
        return _dot(h, w_ref[:, lo:lo + width])

    lane = lax.broadcasted_iota(jnp.int32, (tm, LANES), 1)
    low_half = lane < HEAD_DIM

    def head_norm_rope(y, gain):
        width = y.shape[1]
        rep = width // LANES
        hi, lo = _split2(y * y)
        ssum = jnp.concatenate(
            [_dot(jnp.concatenate([hi[:, c:c + LANES], lo[:, c:c + LANES]], axis=1), bd_ref[...])
             for c in range(0, width, LANES)], axis=1)
        yn = y * lax.rsqrt(ssum * (1.0 / HEAD_DIM) + EPS) * gain
        cos = jnp.concatenate([cos_ref[...]] * rep, axis=1) if rep > 1 else cos_ref[...]
        sin = jnp.concatenate([sin_ref[...]] * rep, axis=1) if rep > 1 else sin_ref[...]
        fwd = pltpu.roll(yn, ROPE_HALF, axis=1)
        bwd = pltpu.roll(yn, width - ROPE_HALF, axis=1)
        lane_w = lax.broadcasted_iota(jnp.int32, (tm, width), 1)
        partner = jnp.where((lane_w & (HEAD_DIM - 1)) < ROPE_HALF, bwd, fwd)
        return yn * cos + partner * sin

    def head_pair(slab, p):
        chunk = slab[:, p * LANES:(p + 1) * LANES]
        return chunk, pltpu.roll(chunk, HEAD_DIM, axis=1)

    qn = head_norm_rope(proj(C_Q, NSA_WIDTH), qg_ref[...]) * (SCALE * LOG2E)
    for p in range(NSA_HEADS // 2):
        ev, od = head_pair(qn, p)
        q_ref[0, 2 * p] = jnp.where(low_half, ev, 0.0).astype(q_ref.dtype)
        q_ref[0, 2 * p + 1] = jnp.where(low_half, od, 0.0).astype(q_ref.dtype)

    kv_cmp = proj(C_KC, 2 * NSA_KV_WIDTH)
    kv_slc = proj(C_KS, 2 * NSA_KV_WIDTH)
    kv_win = proj(C_KW, 2 * NSA_KV_WIDTH)

    n_rows = tm // CMP_STRIDE
    low_half_c = lax.broadcasted_iota(jnp.int32, (n_rows, LANES), 1) < HEAD_DIM
    for ref, p in ((kc_ref, 0), (vc_ref, 1)):
        cmp_scr[p] = kv_cmp[:, p * LANES:(p + 1) * LANES]
        for j in range(CMP_STRIDE // 2):
            a = cmp_scr[p, pl.ds(2 * j, n_rows, stride=CMP_STRIDE), :]
            b = cmp_scr[p, pl.ds(2 * j + 1, n_rows, stride=CMP_STRIDE), :]
            ref[0, 0, :, j * LANES:(j + 1) * LANES] = jnp.where(low_half_c, a, pltpu.roll(b, HEAD_DIM, axis=1))
            ref[0, 1, :, j * LANES:(j + 1) * LANES] = jnp.where(low_half_c, pltpu.roll(a, HEAD_DIM, axis=1), b)

    sblk = lax.rem(pl.program_id(0), n_sblk)
    row = lax.broadcasted_iota(jnp.int32, (tm, LANES), 0)
    key_blk = (sblk * tm + row) >> int(math.log2(SLC_LEN))
    onehot = jnp.where(lane - HEAD_DIM == key_blk, 1.0, 0.0)
    k_sw = head_norm_rope(jnp.concatenate([kv_slc[:, :NSA_KV_WIDTH], kv_win[:, :NSA_KV_WIDTH]], axis=1),
                          kg_ref[...])
    ev, od = head_pair(k_sw, 0)
    ksa_ref[0, 0] = jnp.where(low_half, ev, onehot).astype(ksa_ref.dtype)
    ksa_ref[0, 1] = jnp.where(low_half, od, onehot).astype(ksa_ref.dtype)
    ev, od = head_pair(kv_slc, 1)
    vsa_ref[0, 0] = jnp.where(low_half, ev, 1.0).astype(vsa_ref.dtype)
    vsa_ref[0, 1] = jnp.where(low_half, od, 1.0).astype(vsa_ref.dtype)

    one_lane = jnp.where(lane == HEAD_DIM, 1.0, 0.0)
    ev, od = head_pair(k_sw, 1)
    kw_ref[0, 0] = jnp.where(low_half, ev, one_lane).astype(kw_ref.dtype)
    kw_ref[0, 1] = jnp.where(low_half, od, one_lane).astype(kw_ref.dtype)
    ev, od = head_pair(kv_win, 1)
    vwa_ref[0, 0] = jnp.where(low_half, ev, 1.0).astype(vwa_ref.dtype)
    vwa_ref[0, 1] = jnp.where(low_half, od, 1.0).astype(vwa_ref.dtype)

    gn = proj(C_GN, NSA_WIDTH)
    silu_n = gn * jax.nn.sigmoid(gn)
    gl_split = jnp.concatenate(_split2(proj(C_GL, LANES)), axis=1)
    for br, ref in enumerate((gc_ref, gs_ref, gw_ref)):
        ref[...] = jax.nn.sigmoid(_dot(gl_split, eg_ref[:, br * NSA_WIDTH:(br + 1) * NSA_WIDTH])) * silu_n

    qsb_ref[...] = (proj(C_QSB, SB_WIDTH) * (SCALE * LOG2E)).astype(qsb_ref.dtype)
    ksb_ref[...] = proj(C_KSB, SB_WIDTH).astype(ksb_ref.dtype)
    vsb_ref[...] = proj(C_VSB, SB_WIDTH).astype(vsb_ref.dtype)
    gsb = proj(C_GSB, SB_WIDTH)
    gsb_ref[...] = gsb * jax.nn.sigmoid(gsb)


def _compress_kernel(kc_ref, vc_ref, posk_ref, posv_ref, w1k_ref, b1k_ref, w2k_ref,
                     w1v_ref, b1v_ref, w2v_ref, kg_ref, cos_ref, sin_ref, perm_ref,
                     kcmp_ref, vcmp_ref):
    half = CMP_STRIDE * HEAD_DIM

    def phi(c_ref, pos_ref, w1_ref, b1_ref, w2_ref):
        c = c_ref[0]
        n = c.shape[0]
        top = _dot((c + pos_ref[0:1, :]).astype(_MXU_DTYPE), w1_ref[:half, :])
        bot = _dot((c + pos_ref[1:2, :]).astype(_MXU_DTYPE), w1_ref[half:, :])
        hid = top + pltpu.roll(bot, n - 1, axis=0) + b1_ref[...]
        return _dot((hid * jax.nn.sigmoid(hid)).astype(_MXU_DTYPE), w2_ref[...])

    k = phi(kc_ref, posk_ref, w1k_ref, b1k_ref, w2k_ref)
    ms = jnp.sum(k * k, axis=-1, keepdims=True) * (1.0 / HEAD_DIM)
    kn = k * lax.rsqrt(ms + EPS) * kg_ref[...]
    hi, lo = _split2(kn)
    partner = _dot(hi, perm_ref[...]) + _dot(lo, perm_ref[...])
    one_lane = jnp.where(lax.broadcasted_iota(jnp.int32, k.shape, 1) == HEAD_DIM, 1.0, 0.0)
    kcmp_ref[0] = (kn * cos_ref[...] + partner * sin_ref[...] + one_lane).astype(kcmp_ref.dtype)
    vcmp_ref[0] = phi(vc_ref, posv_ref, w1v_ref, b1v_ref, w2v_ref).astype(vcmp_ref.dtype)


def _nsa_kernel(bound_ref, q_ref, kc_ref, vc_ref, ksa_ref, vsa_ref, kw_ref, vwa_ref, ovl_ref,
                gc_ref, gs_ref, gw_ref, o_ref, acc_scr, *, exact_shift):
    qt = NSA_QT
    kw = 2 * qt
    assert q_ref.shape[2] == kw
    chains = [(g, half) for half in range(2) for g in range(kc_ref.shape[1])]
    groups = range(len(chains))
    kv = [g for g, _ in chains]
    blk = [2 * pl.program_id(1) + half for _, half in chains]
    rq = NSA_GROUP * qt
    n_blk_lanes = LANES - HEAD_DIM
    q_pads = [q_ref[0, g * NSA_GROUP:(g + 1) * NSA_GROUP, half * qt:(half + 1) * qt].reshape(rq, LANES)
              for g, half in chains]
    heads = lambda x: jnp.concatenate([x] * NSA_GROUP, axis=0)
    t_tok = lax.broadcasted_iota(jnp.int32, (qt, 1), 0)
    rel_w = heads(t_tok - lax.broadcasted_iota(jnp.int32, (qt, kw), 1))
    rel = rel_w[:, :qt]
    lane_q = lax.broadcasted_iota(jnp.int32, (qt, LANES), 1)
    lane_r = heads(lane_q)
    low_half = lane_q < HEAD_DIM

    logit_bound = bound_ref[0]
    row_max = lambda s: jnp.max(s, axis=-1, keepdims=True)

    def shift_lane(shift):
        return jnp.where(lane_r == HEAD_DIM, -shift, 0.0).astype(q_pads[0].dtype)

    def window_logits(g, q_lhs):
        i = blk[g]
        s_parts, v_parts = [], []
        for back in range(WINDOW // qt + 1):
            start = pl.multiple_of(jnp.maximum(i - back, 0) * qt, qt)
            s = _nt_dot(q_lhs, kw_ref[0, kv[g], pl.ds(start, qt), :])
            v = vwa_ref[0, kv[g], pl.ds(start, qt), :]
            if back == 0:
                s = jnp.where(rel >= 0, s, MASK_NEG)
            elif (back + 1) * qt <= WINDOW:
                v = v * jnp.where(i >= back, 1.0, 0.0).astype(v.dtype)
            else:
                s = jnp.where(rel < (WINDOW - back * qt) - jnp.where(i >= back, 0, WINDOW), s, MASK_NEG)
            s_parts.append(s)
            v_parts.append(v)
        return jnp.concatenate(s_parts, axis=1), jnp.concatenate(v_parts, axis=0)

    def cmp_logits(g, q_lhs):
        kc = kc_ref[0, kv[g]]
        cmp_end = lax.broadcasted_iota(jnp.int32, (rq, kc.shape[0]), 1) * CMP_STRIDE + (CMP_LEN - 1)
        return jnp.where(cmp_end <= blk[g] * qt + heads(t_tok), _nt_dot(q_lhs, kc), MASK_NEG)

    if exact_shift:
        pre_shifts = [shift_lane(m) for g in groups
                      for m in (row_max(window_logits(g, q_pads[g])[0]), row_max(cmp_logits(g, q_pads[g])))]
    else:
        pre_shifts = [shift_lane(logit_bound)] * (2 * len(chains))

    ovl = ovl_ref[...]
    j_idx = lax.broadcasted_iota(jnp.int32, (n_blk_lanes, qt), 0)
    row_in_grp = lax.broadcasted_iota(jnp.int32, (SUBLANES, qt), 0)

    def selected_blocks(p, i):
        blk_t = (i * qt + lax.broadcasted_iota(jnp.int32, (n_blk_lanes, qt), 1)) >> int(math.log2(SLC_LEN))
        slc_valid = j_idx <= blk_t
        forced = (j_idx == 0) | (j_idx == blk_t) | (j_idx == blk_t - 1)
        p_sum = p[0:qt]
        for r in range(1, NSA_GROUP):
            p_sum = p_sum + p[r * qt:(r + 1) * qt]
        p_slc = sum(_nt_dot(ovl, part) for part in _split3(p_sum))
        score = jnp.where(slc_valid, p_slc + jnp.where(forced, FORCE_BONUS, 0.0), -jnp.inf)
        n_grp = n_blk_lanes // SUBLANES
        grp_rows = [score[c * SUBLANES:(c + 1) * SUBLANES] for c in range(n_grp)]
        grp_rank = [jnp.zeros((SUBLANES, qt), F32) for _ in range(n_grp)]
        for ii in range(n_blk_lanes):
            s_i = score[ii:ii + 1, :]
            for c in range(n_grp):
                rows = grp_rows[c]
                if c * SUBLANES > ii:
                    beats = jnp.where(s_i >= rows, 1.0, 0.0)
                elif (c + 1) * SUBLANES - 1 <= ii:
                    beats = jnp.where(s_i > rows, 1.0, 0.0)
                else:
                    beats = jnp.where(row_in_grp > ii - c * SUBLANES,
                                      jnp.where(s_i >= rows, 1.0, 0.0), jnp.where(s_i > rows, 1.0, 0.0))
                grp_rank[c] = grp_rank[c] + beats
        rank = jnp.concatenate(grp_rank, axis=0)
        sel_t = jnp.where(slc_valid, jnp.where(rank < SLC_TOPN, 1.0, 0.0), 0.0)
        return heads(jnp.concatenate([jnp.ones((HEAD_DIM, qt), F32), sel_t], axis=0).T)

    acc_w, o_cmp, sel_rows = [], [], []
    for g in groups:
        s_win, v_win = window_logits(g, q_pads[g] + pre_shifts[2 * g])
        acc_w.append(_dot(jnp.exp2(s_win).astype(_MXU_DTYPE), v_win))
        p = jnp.exp2(cmp_logits(g, q_pads[g] + pre_shifts[2 * g + 1]))
        p = p / jnp.maximum(jnp.sum(p, axis=-1, keepdims=True), 1e-30)
        o_cmp.append(_dot(p.astype(_MXU_DTYPE), vc_ref[0, kv[g]]))
        sel_rows.append(selected_blocks(p, blk[g]))

    n_kt = pl.program_id(1) + 1

    def augmented_q(g, shift):
        bias = jnp.where(lane_r >= HEAD_DIM, jnp.where(sel_rows[g] > 0.5, -shift, MASK_NEG), 0.0)
        return q_pads[g] + bias.astype(q_pads[g].dtype)

    def slc_logits(g, q_aug, kt):
        s = _nt_dot(q_aug, ksa_ref[0, kv[g], pl.ds(pl.multiple_of(kt * kw, kw), kw), :])
        return jnp.where(rel_w >= kt * kw - blk[g] * qt, s, MASK_NEG)

    def slc_row_max(g):
        q_aug = augmented_q(g, 0.0)
        return lax.fori_loop(0, n_kt, lambda kt, m_run: jnp.maximum(m_run, row_max(slc_logits(g, q_aug, kt))),
                             jnp.full((rq, 1), M_INIT, F32))

    q_augs = [augmented_q(g, slc_row_max(g) if exact_shift else logit_bound) for g in groups]

    def slc_tile(g, kt, width):
        start = pl.multiple_of(kt * kw, kw)
        s = _nt_dot(q_augs[g], ksa_ref[0, kv[g], pl.ds(start, width), :])
        s = jnp.where(rel_w[:, :width] >= kt * kw - blk[g] * qt, s, MASK_NEG)
        return _dot(jnp.exp2(s).astype(_MXU_DTYPE), vsa_ref[0, kv[g], pl.ds(start, width), :])

    for g in groups:
        acc_scr[g] = jnp.zeros((rq, LANES), F32)

    def slc_body(kt, _):
        for g in groups:
            acc_scr[g] += slc_tile(g, kt, kw)
        return 0

    lax.fori_loop(0, n_kt - 1, slc_body, 0)
    acc_s = [acc_scr[g] + slc_tile(g, n_kt - 1, qt if half == 0 else kw) for g, (_, half) in enumerate(chains)]

    head = lambda a, r: a[r * qt:(r + 1) * qt]

    def token_major(o):
        return jnp.concatenate([jnp.where(low_half, head(o, r), head(o, r + 1))
                                for r in range(0, NSA_GROUP, 2)], axis=1)

    def token_major_normalised(acc):
        inv = 1.0 / jnp.where(lane_r >= HEAD_DIM, acc, 1.0)
        pairs = []
        for r in range(0, NSA_GROUP, 2):
            even = head(acc, r) * pltpu.roll(head(inv, r), HEAD_DIM, axis=1)
            odd = pltpu.roll(head(acc, r + 1), HEAD_DIM, axis=1) * head(inv, r + 1)
            pairs.append(jnp.where(low_half, even, odd))
        return jnp.concatenate(pairs, axis=1)

    gw_cols = NSA_GROUP * HEAD_DIM
    for g, (kv_head, half) in enumerate(chains):
        rows = slice(half * qt, (half + 1) * qt)
        cols = slice(kv_head * gw_cols, (kv_head + 1) * gw_cols)
        out = (gc_ref[0, rows, cols] * token_major(o_cmp[g])
               + gs_ref[0, rows, cols] * token_major_normalised(acc_s[g])
               + gw_ref[0, rows, cols] * token_major_normalised(acc_w[g]))
        o_ref[0, rows, cols] = out.astype(o_ref.dtype)


def _sb_kernel(q_ref, k_ref, v_ref, uu_ref, g_ref, o_ref, *state_scr):
    i = pl.program_id(2)
    rows = q_ref.shape[1]
    pair_lanes = [slice(pr * LANES, (pr + 1) * LANES) for pr in range(q_ref.shape[2] // LANES)]
    band = rows // Q_BLOCK
    n_tiles = (i + 1) * band
    low_half = lax.broadcasted_iota(jnp.int32, (rows, LANES), 1) < HEAD_DIM
    low_half_k = lax.broadcasted_iota(jnp.int32, (Q_BLOCK, LANES), 1) < HEAD_DIM
    q_heads = []
    for lanes in pair_lanes:
        q_pair = q_ref[0, :, lanes]
        zero = jnp.zeros_like(q_pair)
        q_heads.append((jnp.where(low_half, q_pair, zero), jnp.where(low_half, zero, q_pair)))

    tri = (lax.broadcasted_iota(jnp.int32, (Q_BLOCK, LANES), 1)
           < lax.broadcasted_iota(jnp.int32, (Q_BLOCK, LANES), 0))

    def update_rows(x, r0, fn):
        return fn(x) if r0 == 0 else jnp.concatenate([x[:r0], fn(x[r0:])], axis=0)

    def on_diagonal(x, fill):
        masked = jnp.where(tri, x[:Q_BLOCK], fill)
        return masked if x.shape[0] == Q_BLOCK else jnp.concatenate([masked, x[Q_BLOCK:]], axis=0)

    def tile_step(start, carry, r0=0, diagonal=False):
        new_carry = []
        for pr, lanes in enumerate(pair_lanes):
            new_carry.extend(pair_tile_step(pr, lanes, start, carry[3 * pr:3 * pr + 3], r0, diagonal))
        return tuple(new_carry)

    def pair_tile_step(pr, lanes, start, carry, r0, diagonal):
        acc, laters = carry[0], carry[1:]
        k_pair = k_ref[0, pl.ds(start, Q_BLOCK), lanes]
        v_pair = v_ref[0, pl.ds(start, Q_BLOCK), lanes]
        zero_v = jnp.zeros_like(v_pair)
        v_bd = jnp.concatenate([jnp.where(low_half_k, v_pair, zero_v),
                                jnp.where(low_half_k, zero_v, v_pair)], axis=0)
        weights, new_laters = [], []
        for q_h, later in zip(q_heads[pr], laters):
            z = _nt_dot(q_h[r0:] if r0 else q_h, k_pair)
            sp = jnp.maximum(z, 0.0) + jnp.log2(1.0 + jnp.exp2(-jnp.abs(z)))
            if diagonal:
                sp = on_diagonal(sp, 0.0)
            hi, lo = _split2(sp)
            r = _dot(jnp.concatenate([hi, lo], axis=1), uu_ref[...])
            after = r[:, :LANES] + (later[r0:] if r0 else later)
            a = jnp.exp2(z - sp - after)
            if diagonal:
                a = on_diagonal(a, 0.0)
            weights.append(a.astype(_MXU_DTYPE))
            new_laters.append(update_rows(later, r0, lambda part, r=r: part + r[:, LANES:]))
        pv = _dot(jnp.concatenate(weights, axis=1), v_bd)
        return (update_rows(acc, r0, lambda part: part + pv), *new_laters)

    zeros = jnp.zeros((rows, LANES), F32)
    carry = (zeros,) * (3 * len(pair_lanes))
    for c in reversed(range(band)):
        carry = tile_step(pl.multiple_of(i * rows + c * Q_BLOCK, Q_BLOCK), carry, r0=c * Q_BLOCK, diagonal=True)

    def main_body(kg, carry):
        for u in range(SB_UNROLL):
            first_key = (n_tiles - band - 1 - (kg * SB_UNROLL + u)) * Q_BLOCK
            carry = tile_step(pl.multiple_of(first_key, Q_BLOCK), carry)
        return carry

    n_trips = (n_tiles - band) // SB_UNROLL

    def settled(carry):
        laters = [x for n, x in enumerate(carry) if n % 3]
        return jnp.min(functools.reduce(jnp.minimum, laters)) >= SB_UNDERFLOW_BITS

    def keep_going(state):
        kg, done = state
        return jnp.logical_and(kg < n_trips, jnp.logical_not(done))

    def save(carry):
        for ref, value in zip(state_scr, carry):
            ref[...] = value

    def main_step(state):
        kg, _ = state
        carry = main_body(kg, tuple(ref[...] for ref in state_scr))
        save(carry)
        return kg + 1, settled(carry)

    save(carry)
    lax.while_loop(keep_going, main_step, (0, False))
    for pr, lanes in enumerate(pair_lanes):
        o_ref[0, :, lanes] = (state_scr[3 * pr][...] * g_ref[0, :, lanes]).astype(o_ref.dtype)


def _outproj_kernel(x_ref, on_ref, os_ref, w_ref, o_ref):
    o_ref[...] = (x_ref[...] + _dot(on_ref[...], w_ref[:NSA_WIDTH, :])
                  + _dot(os_ref[...], w_ref[NSA_WIDTH:, :]))


def _rope_tables(pos, reps):
    inv_freq = np.power(ROPE_THETA, -np.arange(0, ROPE_DIM, 2, dtype=np.float64) / ROPE_DIM)
    ang = pos.astype(np.float64)[:, None] * inv_freq[None, :]
    cos, sin = np.cos(ang), np.sin(ang)
    n = pos.shape[0]
    rest = HEAD_DIM - ROPE_DIM
    cos_h = np.concatenate([cos, cos, np.ones((n, rest))], axis=1)
    sin_h = np.concatenate([-sin, sin, np.zeros((n, rest))], axis=1)
    return jnp.asarray(np.tile(cos_h, (1, reps)), F32), jnp.asarray(np.tile(sin_h, (1, reps)), F32)


def _const_spec(shape):
    return pl.BlockSpec(shape, lambda *_: (0,) * len(shape))


def kernel(x, norm_gain, w_in, q_norm_gain, k_norm_cmp, k_norm_slc, k_norm_win,
           cmp_k_pos, cmp_k_w1, cmp_k_b1, cmp_k_w2, cmp_v_pos, cmp_v_w1, cmp_v_b1, cmp_v_w2, w_out):
    B, S, DM = x.shape
    D, G, R = HEAD_DIM, NSA_KV_HEADS, NSA_GROUP
    mxu = _MXU_DTYPE
    n_tok = B * S
    tm = ROW_TILE
    n_sblk = S // tm
    ncp = S // CMP_STRIDE
    n_slc = S // SLC_LEN
    n_cmp = (S - CMP_LEN) // CMP_STRIDE + 1
    assert S % tm == 0 and ncp % LANES == 0 and n_slc <= LANES - D and n_slc >= SLC_TOPN

    gl_end = C_GL + NSA_HEADS * N_BRANCH
    w_mxu = w_in.astype(mxu)
    w_cat = (jnp.zeros((DM, N_COLS), mxu).at[:, :gl_end].set(w_mxu[:, :gl_end])
             .at[:, C_GN:].set(w_mxu[:, gl_end:]))
    assert w_cat.shape[1] == N_COLS
    cos_t, sin_t = _rope_tables(np.arange(S), LANES // D)
    cos_c, sin_c = _rope_tables(np.arange(ncp) * CMP_STRIDE + (CMP_LEN - 1), LANES // D)
    lane_i = np.arange(LANES)
    bd = (lane_i[:, None] // D == lane_i[None, :] // D).astype(np.float32)
    bd = jnp.asarray(np.concatenate([bd, bd], axis=0), mxu)
    eg = np.zeros((LANES, N_BRANCH * NSA_WIDTH), np.float32)
    for hh in range(NSA_HEADS):
        for br in range(N_BRANCH):
            eg[hh * N_BRANCH + br, br * NSA_WIDTH + hh * D:br * NSA_WIDTH + (hh + 1) * D] = 1.0
    eg = jnp.asarray(np.concatenate([eg, eg], axis=0), mxu)
    perm = np.zeros((LANES, LANES), np.float32)
    for c in range(ROPE_HALF):
        perm[c + ROPE_HALF, c] = 1.0
        perm[c, c + ROPE_HALF] = 1.0
    perm = jnp.asarray(perm, mxu)
    cs = np.arange(ncp) * CMP_STRIDE
    ss = np.arange(LANES - D) * SLC_LEN
    ovl = np.clip(np.minimum(cs[None, :] + CMP_LEN, ss[:, None] + SLC_LEN)
                  - np.maximum(cs[None, :], ss[:, None]), 0, None).astype(np.float32) / CMP_LEN
    ovl[:, n_cmp:] = 0.0
    ovl[n_slc:, :] = 0.0
    ovl = jnp.asarray(ovl, mxu)
    sidx = np.arange(LANES)
    tri = (sidx[:, None] > sidx[None, :]).astype(np.float32)
    uu_half = np.concatenate([tri, np.ones((LANES, LANES), np.float32)], axis=1)
    uu = jnp.asarray(np.concatenate([uu_half, uu_half], axis=0), mxu)

    row = lambda v: v.reshape(1, -1).astype(F32)
    x2 = x.reshape(n_tok, DM)

    tok_spec = lambda w: pl.BlockSpec((tm, w), lambda t: (t, 0))
    head_spec = lambda nh, w: pl.BlockSpec((1, nh, tm, w), lambda t: (t // n_sblk, 0, t % n_sblk, 0))
    tab_spec = pl.BlockSpec((tm, LANES), lambda t: (t % n_sblk, 0))
    chunk_w = CMP_STRIDE * D
    chunk_spec = pl.BlockSpec((1, G, tm // CMP_STRIDE, chunk_w), lambda t: (t // n_sblk, 0, t % n_sblk, 0))
    sds = jax.ShapeDtypeStruct
    outs = pl.pallas_call(
        functools.partial(_inproj_kernel, n_sblk=n_sblk),
        grid=(n_tok // tm,),
        in_specs=[tok_spec(DM), _const_spec((1, DM)), _const_spec((DM, N_COLS)),
                  _const_spec((1, NSA_WIDTH)), _const_spec((1, 2 * NSA_KV_WIDTH)),
                  _const_spec((2 * LANES, LANES)), tab_spec, tab_spec,
                  _const_spec((2 * LANES, N_BRANCH * NSA_WIDTH))],
        out_specs=[head_spec(NSA_HEADS, LANES), chunk_spec, chunk_spec,
                   head_spec(G, LANES), head_spec(G, LANES), head_spec(G, LANES), head_spec(G, LANES),
                   tok_spec(NSA_WIDTH), tok_spec(NSA_WIDTH), tok_spec(NSA_WIDTH),
                   tok_spec(SB_WIDTH), tok_spec(SB_WIDTH), tok_spec(SB_WIDTH), tok_spec(SB_WIDTH)],
        out_shape=[sds((B, NSA_HEADS, S, LANES), mxu), sds((B, G, ncp, chunk_w), F32), sds((B, G, ncp, chunk_w), F32),
                   sds((B, G, S, LANES), mxu), sds((B, G, S, LANES), mxu), sds((B, G, S, LANES), mxu),
                   sds((B, G, S, LANES), mxu),
                   sds((n_tok, NSA_WIDTH), F32), sds((n_tok, NSA_WIDTH), F32), sds((n_tok, NSA_WIDTH), F32),
                   sds((n_tok, SB_WIDTH), mxu), sds((n_tok, SB_WIDTH), mxu), sds((n_tok, SB_WIDTH), mxu),
                   sds((n_tok, SB_WIDTH), F32)],
        scratch_shapes=[pltpu.VMEM((2, tm, NSA_KV_WIDTH), F32)],
        compiler_params=pltpu.CompilerParams(dimension_semantics=("parallel",), vmem_limit_bytes=VMEM_LIMIT),
        name="inproj",
    )(x2, row(norm_gain), w_cat, row(jnp.tile(q_norm_gain, NSA_HEADS)),
      row(jnp.concatenate([jnp.tile(k_norm_slc, G), jnp.tile(k_norm_win, G)])), bd, cos_t, sin_t, eg)
    (q_nsa, kc_raw, vc_raw, ks_aug, vs_aug, k_win, vw_aug, g_cmp, g_slc, g_win,
     q_sb, k_sb, v_sb, g_sb) = outs

    chunks = lambda a: a.reshape(B * G, ncp, chunk_w)
    bg_spec = lambda r, w: pl.BlockSpec((1, r, w), lambda t: (t, 0, 0))
    k_cmp, v_cmp = pl.pallas_call(
        _compress_kernel,
        grid=(B * G,),
        in_specs=[bg_spec(ncp, chunk_w), bg_spec(ncp, chunk_w),
                  _const_spec((2, chunk_w)), _const_spec((2, chunk_w)),
                  _const_spec((CMP_LEN * D, D)), _const_spec((1, D)), _const_spec((D, LANES)),
                  _const_spec((CMP_LEN * D, D)), _const_spec((1, D)), _const_spec((D, LANES)),
                  _const_spec((1, LANES)), _const_spec((ncp, LANES)), _const_spec((ncp, LANES)),
                  _const_spec((LANES, LANES))],
        out_specs=[bg_spec(ncp, LANES), bg_spec(ncp, LANES)],
        out_shape=[sds((B * G, ncp, LANES), mxu), sds((B * G, ncp, LANES), mxu)],
        compiler_params=pltpu.CompilerParams(dimension_semantics=("parallel",), vmem_limit_bytes=VMEM_LIMIT),
        name="compress",
    )(chunks(kc_raw), chunks(vc_raw), cmp_k_pos.reshape(2, chunk_w), cmp_v_pos.reshape(2, chunk_w),
      cmp_k_w1.astype(mxu), row(cmp_k_b1), jnp.pad(cmp_k_w2, ((0, 0), (0, LANES - D))).astype(mxu),
      cmp_v_w1.astype(mxu), row(cmp_v_b1), jnp.tile(cmp_v_w2, (1, LANES // D)).astype(mxu),
      row(jnp.pad(k_norm_cmp, (0, LANES - D))), cos_c, sin_c, perm)
    k_cmp = k_cmp.reshape(B, G, ncp, LANES)
    v_cmp = v_cmp.reshape(B, G, ncp, LANES)

    k_gain = jnp.max(jnp.abs(jnp.stack([k_norm_cmp, k_norm_slc, k_norm_win])))
    logit_bound = jnp.max(jnp.abs(q_norm_gain)) * k_gain * (1.02 * D * SCALE * LOG2E)
    logit_bound = logit_bound.astype(F32).reshape(1)
    kv_spec = lambda r, w: pl.BlockSpec((1, G, r, w), lambda b, i: (b, 0, 0, 0))
    gate_spec = pl.BlockSpec((1, 2 * NSA_QT, NSA_WIDTH), lambda b, i: (b, i, 0))
    g3 = lambda a: a.reshape(B, S, NSA_WIDTH)
    nsa_call = lambda exact_shift: pl.pallas_call(
        functools.partial(_nsa_kernel, exact_shift=exact_shift),
        grid=(B, S // (2 * NSA_QT)),
        in_specs=[pl.BlockSpec(memory_space=pltpu.SMEM),
                  pl.BlockSpec((1, NSA_HEADS, 2 * NSA_QT, LANES), lambda b, i: (b, 0, i, 0)),
                  kv_spec(ncp, LANES), kv_spec(ncp, LANES), kv_spec(S, LANES), kv_spec(S, LANES),
                  kv_spec(S, LANES), kv_spec(S, LANES), _const_spec((LANES - D, ncp)),
                  gate_spec, gate_spec, gate_spec],
        out_specs=gate_spec,
        out_shape=sds((B, S, NSA_WIDTH), mxu),
        scratch_shapes=[pltpu.VMEM((2 * G, R * NSA_QT, LANES), F32)],
        compiler_params=pltpu.CompilerParams(dimension_semantics=("parallel", "arbitrary"),
                                             vmem_limit_bytes=VMEM_LIMIT),
        name="nsa_exact_shift" if exact_shift else "nsa",
    )
    nsa_operands = (logit_bound, q_nsa, k_cmp, v_cmp, ks_aug, vs_aug, k_win, vw_aug, ovl,
                    g3(g_cmp), g3(g_slc), g3(g_win))
    o_nsa = lax.cond(logit_bound[0] <= MAX_STATIC_SHIFT,
                     lambda ops: nsa_call(False)(*ops), lambda ops: nsa_call(True)(*ops), nsa_operands)

    sb3 = lambda a: a.reshape(B, S, SB_WIDTH)
    pair_q = pl.BlockSpec((1, SB_ROWS, SB_PAIRS * LANES), lambda b, hp, i: (b, i, hp))
    pair_kv = pl.BlockSpec((1, S, SB_PAIRS * LANES), lambda b, hp, i: (b, 0, hp))
    o_sb = pl.pallas_call(
        _sb_kernel,
        grid=(B, SB_WIDTH // (SB_PAIRS * LANES), S // SB_ROWS),
        in_specs=[pair_q, pair_kv, pair_kv, _const_spec((2 * LANES, 2 * LANES)), pair_q],
        out_specs=pair_q,
        out_shape=sds((B, S, SB_WIDTH), mxu),
        scratch_shapes=[pltpu.VMEM((SB_ROWS, LANES), F32)] * (3 * SB_PAIRS),
        compiler_params=pltpu.CompilerParams(dimension_semantics=("parallel", "parallel", "arbitrary"),
                                             vmem_limit_bytes=VMEM_LIMIT),
        name="stickbreak",
    )(sb3(q_sb), sb3(k_sb), sb3(v_sb), uu, sb3(g_sb))

    out_spec = lambda w: pl.BlockSpec((OUT_ROW_TILE, w), lambda t: (t, 0))
    out = pl.pallas_call(
        _outproj_kernel,
        grid=(n_tok // OUT_ROW_TILE,),
        in_specs=[out_spec(DM), out_spec(NSA_WIDTH), out_spec(SB_WIDTH),
                  _const_spec((NSA_WIDTH + SB_WIDTH, DM))],
        out_specs=out_spec(DM),
        out_shape=sds((n_tok, DM), x.dtype),
        compiler_params=pltpu.CompilerParams(dimension_semantics=("parallel",), vmem_limit_bytes=VMEM_LIMIT),
        name="outproj",
    )(x2, o_nsa.reshape(n_tok, NSA_WIDTH), o_sb.reshape(n_tok, SB_WIDTH), w_out.astype(mxu))
    return out.reshape(B, S, DM)
```

```python
import functools
import math

import numpy as np
import jax
import jax.numpy as jnp
from jax import lax
from jax.experimental import pallas as pl
from jax.experimental.pallas import tpu as pltpu

HEAD_DIM = 64
NSA_HEADS = 8
NSA_KV_HEADS = 2
NSA_GROUP = NSA_HEADS // NSA_KV_HEADS
SB_HEADS = 8
NSA_WIDTH = NSA_HEADS * HEAD_DIM
SB_WIDTH = SB_HEADS * HEAD_DIM
NSA_KV_WIDTH = NSA_KV_HEADS * HEAD_DIM
N_BRANCH = 3
CMP_LEN = 32
CMP_STRIDE = 16
SLC_LEN = 64
SLC_TOPN = 16
WINDOW = 512
Q_BLOCK = 128
ROPE_DIM = HEAD_DIM // 4
ROPE_HALF = ROPE_DIM // 2
ROPE_THETA = 500000.0
EPS = 1e-6
FORCE_BONUS = 1.0e4
SCALE = 1.0 / math.sqrt(HEAD_DIM)
LOG2E = math.log2(math.e)

LANES = 128
SUBLANES = 8
MASK_NEG = -1.0e30
M_INIT = -3.0e38
ROW_TILE = 512
OUT_ROW_TILE = 2048
NSA_QT = 256
SB_ROWS = 512
SB_UNROLL = 2
SB_PAIRS = 4
MAX_STATIC_SHIFT = 60.0
SB_UNDERFLOW_BITS = 160.0
VMEM_LIMIT = 56 * 1024 * 1024

_MXU_DTYPE = jnp.bfloat16
F32 = jnp.float32

C_Q = 0
C_KC = 512
C_KS = 768
C_KW = 1024
C_GL = 1280
C_GN = 1408
C_QSB = 1920
C_KSB = 2432
C_VSB = 2944
C_GSB = 3456
N_COLS = 3968


def _nt_dot(a, b):
    return lax.dot_general(a, b, (((1,), (1,)), ((), ())), preferred_element_type=F32)


def _dot(a, b):
    return jnp.dot(a, b, preferred_element_type=F32)


def _split2(v):
    hi = v.astype(_MXU_DTYPE)
    lo = (v - hi.astype(F32)).astype(_MXU_DTYPE)
    return hi, lo


def _split3(v):
    hi = v.astype(_MXU_DTYPE)
    r1 = v - hi.astype(F32)
    mid = r1.astype(_MXU_DTYPE)
    lo = (r1 - mid.astype(F32)).astype(_MXU_DTYPE)
    return hi, mid, lo


def _inproj_kernel(x_ref, ng_ref, w_ref, qg_ref, kg_ref, bd_ref, cos_ref, sin_ref, eg_ref,
                   q_ref, kc_ref, vc_ref, ksa_ref, vsa_ref, kw_ref, vwa_ref,
                   gc_ref, gs_ref, gw_ref, qsb_ref, ksb_ref, vsb_ref, gsb_ref, cmp_scr, *, n_sblk):
    tm = x_ref.shape[0]
    x = x_ref[...]
    ms = jnp.mean(x * x, axis=-1, keepdims=True)
    h = (x * lax.rsqrt(ms + EPS) * ng_ref[...]).astype(_MXU_DTYPE)

    def proj(lo, width):
        return _dot(h, w_ref[:, lo:lo + width])

    lane = lax.broadcasted_iota(jnp.int32, (tm, LANES), 1)
    low_half = lane < HEAD_DIM

    def head_norm_rope(y, gain):
        width = y.shape[1]
        rep = width // LANES
        hi, lo = _split2(y * y)
        ssum = jnp.concatenate(
            [_dot(jnp.concatenate([hi[:, c:c + LANES], lo[:, c:c + LANES]], axis=1), bd_ref[...])
             for c in range(0, width, LANES)], axis=1)
        yn = y * lax.rsqrt(ssum * (1.0 / HEAD_DIM) + EPS) * gain
        cos = jnp.concatenate([cos_ref[...]] * rep, axis=1) if rep > 1 else cos_ref[...]
        sin = jnp.concatenate([sin_ref[...]] * rep, axis=1) if rep > 1 else sin_ref[...]
        fwd = pltpu.roll(yn, ROPE_HALF, axis=1)
        bwd = pltpu.roll(yn, width - ROPE_HALF, axis=1)
        lane_w = lax.broadcasted_iota(jnp.int32, (tm, width), 1)
        partner = jnp.where((lane_w & (HEAD_DIM - 1)) < ROPE_HALF, bwd, fwd)
        return yn * cos + partner * sin

    def head_pair(slab, p):
        chunk = slab[:, p * LANES:(p + 1) * LANES]
        return chunk, pltpu.roll(chunk, HEAD_DIM, axis=1)

    qn = head_norm_rope(proj(C_Q, NSA_WIDTH), qg_ref[...]) * (SCALE * LOG2E)
    for p in range(NSA_HEADS // 2):
        ev, od = head_pair(qn, p)
        q_ref[0, 2 * p] = jnp.where(low_half, ev, 0.0).astype(q_ref.dtype)
        q_ref[0, 2 * p + 1] = jnp.where(low_half, od, 0.0).astype(q_ref.dtype)

    kv_cmp = proj(C_KC, 2 * NSA_KV_WIDTH)
    kv_slc = proj(C_KS, 2 * NSA_KV_WIDTH)
    kv_win = proj(C_KW, 2 * NSA_KV_WIDTH)

    n_rows = tm // CMP_STRIDE
    low_half_c = lax.broadcasted_iota(jnp.int32, (n_rows, LANES), 1) < HEAD_DIM
    for ref, p in ((kc_ref, 0), (vc_ref, 1)):
        cmp_scr[p] = kv_cmp[:, p * LANES:(p + 1) * LANES]
        for j in range(CMP_STRIDE // 2):
            a = cmp_scr[p, pl.ds(2 * j, n_rows, stride=CMP_STRIDE), :]
            b = cmp_scr[p, pl.ds(2 * j + 1, n_rows, stride=CMP_STRIDE), :]
            ref[0, 0, :, j * LANES:(j + 1) * LANES] = jnp.where(low_half_c, a, pltpu.roll(b, HEAD_DIM, axis=1))
            ref[0, 1, :, j * LANES:(j + 1) * LANES] = jnp.where(low_half_c, pltpu.roll(a, HEAD_DIM, axis=1), b)

    sblk = lax.rem(pl.program_id(0), n_sblk)
    row = lax.broadcasted_iota(jnp.int32, (tm, LANES), 0)
    key_blk = (sblk * tm + row) >> int(math.log2(SLC_LEN))
    onehot = jnp.where(lane - HEAD_DIM == key_blk, 1.0, 0.0)
    k_sw = head_norm_rope(jnp.concatenate([kv_slc[:, :NSA_KV_WIDTH], kv_win[:, :NSA_KV_WIDTH]], axis=1),
                          kg_ref[...])
    ev, od = head_pair(k_sw, 0)
    ksa_ref[0, 0] = jnp.where(low_half, ev, onehot).astype(ksa_ref.dtype)
    ksa_ref[0, 1] = jnp.where(low_half, od, onehot).astype(ksa_ref.dtype)
    ev, od = head_pair(kv_slc, 1)
    vsa_ref[0, 0] = jnp.where(low_half, ev, 1.0).astype(vsa_ref.dtype)
    vsa_ref[0, 1] = jnp.where(low_half, od, 1.0).astype(vsa_ref.dtype)

    one_lane = jnp.where(lane == HEAD_DIM, 1.0, 0.0)
    ev, od = head_pair(k_sw, 1)
    kw_ref[0, 0] = jnp.where(low_half, ev, one_lane).astype(kw_ref.dtype)
    kw_ref[0, 1] = jnp.where(low_half, od, one_lane).astype(kw_ref.dtype)
    ev, od = head_pair(kv_win, 1)
    vwa_ref[0, 0] = jnp.where(low_half, ev, 1.0).astype(vwa_ref.dtype)
    vwa_ref[0, 1] = jnp.where(low_half, od, 1.0).astype(vwa_ref.dtype)

    gn = proj(C_GN, NSA_WIDTH)
    silu_n = gn * jax.nn.sigmoid(gn)
    gl_split = jnp.concatenate(_split2(proj(C_GL, LANES)), axis=1)
    for br, ref in enumerate((gc_ref, gs_ref, gw_ref)):
        ref[...] = jax.nn.sigmoid(_dot(gl_split, eg_ref[:, br * NSA_WIDTH:(br + 1) * NSA_WIDTH])) * silu_n

    qsb_ref[...] = (proj(C_QSB, SB_WIDTH) * (SCALE * LOG2E)).astype(qsb_ref.dtype)
    ksb_ref[...] = proj(C_KSB, SB_WIDTH).astype(ksb_ref.dtype)
    vsb_ref[...] = proj(C_VSB, SB_WIDTH).astype(vsb_ref.dtype)
    gsb = proj(C_GSB, SB_WIDTH)
    gsb_ref[...] = gsb * jax.nn.sigmoid(gsb)


def _compress_kernel(kc_ref, vc_ref, posk_ref, posv_ref, w1k_ref, b1k_ref, w2k_ref,
                     w1v_ref, b1v_ref, w2v_ref, kg_ref, cos_ref, sin_ref, perm_ref,
                     kcmp_ref, vcmp_ref):
    half = CMP_STRIDE * HEAD_DIM

    def phi(c_ref, pos_ref, w1_ref, b1_ref, w2_ref):
        c = c_ref[0]
        n = c.shape[0]
        top = _dot((c + pos_ref[0:1, :]).astype(_MXU_DTYPE), w1_ref[:half, :])
        bot = _dot((c + pos_ref[1:2, :]).astype(_MXU_DTYPE), w1_ref[half:, :])
        hid = top + pltpu.roll(bot, n - 1, axis=0) + b1_ref[...]
        return _dot((hid * jax.nn.sigmoid(hid)).astype(_MXU_DTYPE), w2_ref[...])

    k = phi(kc_ref, posk_ref, w1k_ref, b1k_ref, w2k_ref)
    ms = jnp.sum(k * k, axis=-1, keepdims=True) * (1.0 / HEAD_DIM)
    kn = k * lax.rsqrt(ms + EPS) * kg_ref[...]
    hi, lo = _split2(kn)
    partner = _dot(hi, perm_ref[...]) + _dot(lo, perm_ref[...])
    one_lane = jnp.where(lax.broadcasted_iota(jnp.int32, k.shape, 1) == HEAD_DIM, 1.0, 0.0)
    kcmp_ref[0] = (kn * cos_ref[...] + partner * sin_ref[...] + one_lane).astype(kcmp_ref.dtype)
    vcmp_ref[0] = phi(vc_ref, posv_ref, w1v_ref, b1v_ref, w2v_ref).astype(vcmp_ref.dtype)


def _nsa_kernel(bound_ref, q_ref, kc_ref, vc_ref, ksa_ref, vsa_ref, kw_ref, vwa_ref, ovl_ref,
                gc_ref, gs_ref, gw_ref, o_ref, acc_scr, *, exact_shift):
    qt = NSA_QT
    kw = 2 * qt
    assert q_ref.shape[2] == kw
    chains = [(g, half) for half in range(2) for g in range(kc_ref.shape[1])]
    groups = range(len(chains))
    kv = [g for g, _ in chains]
    blk = [2 * pl.program_id(1) + half for _, half in chains]
    rq = NSA_GROUP * qt
    n_blk_lanes = LANES - HEAD_DIM
    q_pads = [q_ref[0, g * NSA_GROUP:(g + 1) * NSA_GROUP, half * qt:(half + 1) * qt].reshape(rq, LANES)
              for g, half in chains]
    heads = lambda x: jnp.concatenate([x] * NSA_GROUP, axis=0)
    t_tok = lax.broadcasted_iota(jnp.int32, (qt, 1), 0)
    rel_w = heads(t_tok - lax.broadcasted_iota(jnp.int32, (qt, kw), 1))
    rel = rel_w[:, :qt]
    lane_q = lax.broadcasted_iota(jnp.int32, (qt, LANES), 1)
    lane_r = heads(lane_q)
    low_half = lane_q < HEAD_DIM

    logit_bound = bound_ref[0]
    row_max = lambda s: jnp.max(s, axis=-1, keepdims=True)

    def shift_lane(shift):
        return jnp.where(lane_r == HEAD_DIM, -shift, 0.0).astype(q_pads[0].dtype)

    def window_logits(g, q_lhs):
        i = blk[g]
        s_parts, v_parts = [], []
        for back in range(WINDOW // qt + 1):
            start = pl.multiple_of(jnp.maximum(i - back, 0) * qt, qt)
            s = _nt_dot(q_lhs, kw_ref[0, kv[g], pl.ds(start, qt), :])
            v = vwa_ref[0, kv[g], pl.ds(start, qt), :]
            if back == 0:
                s = jnp.where(rel >= 0, s, MASK_NEG)
            elif (back + 1) * qt <= WINDOW:
                v = v * jnp.where(i >= back, 1.0, 0.0).astype(v.dtype)
            else:
                s = jnp.where(rel < (WINDOW - back * qt) - jnp.where(i >= back, 0, WINDOW), s, MASK_NEG)
            s_parts.append(s)
            v_parts.append(v)
        return jnp.concatenate(s_parts, axis=1), jnp.concatenate(v_parts, axis=0)

    def cmp_logits(g, q_lhs):
        kc = kc_ref[0, kv[g]]
        cmp_end = lax.broadcasted_iota(jnp.int32, (rq, kc.shape[0]), 1) * CMP_STRIDE + (CMP_LEN - 1)
        return jnp.where(cmp_end <= blk[g] * qt + heads(t_tok), _nt_dot(q_lhs, kc), MASK_NEG)

    if exact_shift:
        pre_shifts = [shift_lane(m) for g in groups
                      for m in (row_max(window_logits(g, q_pads[g])[0]), row_max(cmp_logits(g, q_pads[g])))]
    else:
        pre_shifts = [shift_lane(logit_bound)] * (2 * len(chains))

    ovl = ovl_ref[...]
    j_idx = lax.broadcasted_iota(jnp.int32, (n_blk_lanes, qt), 0)
    row_in_grp = lax.broadcasted_iota(jnp.int32, (SUBLANES, qt), 0)

    def selected_blocks(p, i):
        blk_t = (i * qt + lax.broadcasted_iota(jnp.int32, (n_blk_lanes, qt), 1)) >> int(math.log2(SLC_LEN))
        slc_valid = j_idx <= blk_t
        forced = (j_idx == 0) | (j_idx == blk_t) | (j_idx == blk_t - 1)
        p_sum = p[0:qt]
        for r in range(1, NSA_GROUP):
            p_sum = p_sum + p[r * qt:(r + 1) * qt]
        p_slc = sum(_nt_dot(ovl, part) for part in _split3(p_sum))
        score = jnp.where(slc_valid, p_slc + jnp.where(forced, FORCE_BONUS, 0.0), -jnp.inf)
        n_grp = n_blk_lanes // SUBLANES
        grp_rows = [score[c * SUBLANES:(c + 1) * SUBLANES] for c in range(n_grp)]
        grp_rank = [jnp.zeros((SUBLANES, qt), F32) for _ in range(n_grp)]
        for ii in range(n_blk_lanes):
            s_i = score[ii:ii + 1, :]
            for c in range(n_grp):
                rows = grp_rows[c]
                if c * SUBLANES > ii:
                    beats = jnp.where(s_i >= rows, 1.0, 0.0)
                elif (c + 1) * SUBLANES - 1 <= ii:
                    beats = jnp.where(s_i > rows, 1.0, 0.0)
                else:
                    beats = jnp.where(row_in_grp > ii - c * SUBLANES,
                                      jnp.where(s_i >= rows, 1.0, 0.0), jnp.where(s_i > rows, 1.0, 0.0))
                grp_rank[c] = grp_rank[c] + beats
        rank = jnp.concatenate(grp_rank, axis=0)
        sel_t = jnp.where(slc_valid, jnp.where(rank < SLC_TOPN, 1.0, 0.0), 0.0)
        return heads(jnp.concatenate([jnp.ones((HEAD_DIM, qt), F32), sel_t], axis=0).T)

    acc_w, o_cmp, sel_rows = [], [], []
    for g in groups:
        s_win, v_win = window_logits(g, q_pads[g] + pre_shifts[2 * g])
        acc_w.append(_dot(jnp.exp2(s_win).astype(_MXU_DTYPE), v_win))
        p = jnp.exp2(cmp_logits(g, q_pads[g] + pre_shifts[2 * g + 1]))
        p = p / jnp.maximum(jnp.sum(p, axis=-1, keepdims=True), 1e-30)
        o_cmp.append(_dot(p.astype(_MXU_DTYPE), vc_ref[0, kv[g]]))
        sel_rows.append(selected_blocks(p, blk[g]))

    n_kt = pl.program_id(1) + 1

    def augmented_q(g, shift):
        bias = jnp.where(lane_r >= HEAD_DIM, jnp.where(sel_rows[g] > 0.5, -shift, MASK_NEG), 0.0)
        return q_pads[g] + bias.astype(q_pads[g].dtype)

    def slc_logits(g, q_aug, kt):
        s = _nt_dot(q_aug, ksa_ref[0, kv[g], pl.ds(pl.multiple_of(kt * kw, kw), kw), :])
        return jnp.where(rel_w >= kt * kw - blk[g] * qt, s, MASK_NEG)

    def slc_row_max(g):
        q_aug = augmented_q(g, 0.0)
        return lax.fori_loop(0, n_kt, lambda kt, m_run: jnp.maximum(m_run, row_max(slc_logits(g, q_aug, kt))),
                             jnp.full((rq, 1), M_INIT, F32))

    q_augs = [augmented_q(g, slc_row_max(g) if exact_shift else logit_bound) for g in groups]

    def slc_tile(g, kt, width):
        start = pl.multiple_of(kt * kw, kw)
        s = _nt_dot(q_augs[g], ksa_ref[0, kv[g], pl.ds(start, width), :])
        s = jnp.where(rel_w[:, :width] >= kt * kw - blk[g] * qt, s, MASK_NEG)
        return _dot(jnp.exp2(s).astype(_MXU_DTYPE), vsa_ref[0, kv[g], pl.ds(start, width), :])

    for g in groups:
        acc_scr[g] = jnp.zeros((rq, LANES), F32)

    def slc_body(kt, _):
        for g in groups:
            acc_scr[g] += slc_tile(g, kt, kw)
        return 0

    lax.fori_loop(0, n_kt - 1, slc_body, 0)
    acc_s = [acc_scr[g] + slc_tile(g, n_kt - 1, qt if half == 0 else kw) for g, (_, half) in enumerate(chains)]

    head = lambda a, r: a[r * qt:(r + 1) * qt]

    def token_major(o):
        return jnp.concatenate([jnp.where(low_half, head(o, r), head(o, r + 1))
                                for r in range(0, NSA_GROUP, 2)], axis=1)

    def token_major_normalised(acc):
        inv = 1.0 / jnp.where(lane_r >= HEAD_DIM, acc, 1.0)
        pairs = []
        for r in range(0, NSA_GROUP, 2):
            even = head(acc, r) * pltpu.roll(head(inv, r), HEAD_DIM, axis=1)
            odd = pltpu.roll(head(acc, r + 1), HEAD_DIM, axis=1) * head(inv, r + 1)
            pairs.append(jnp.where(low_half, even, odd))
        return jnp.concatenate(pairs, axis=1)

    gw_cols = NSA_GROUP * HEAD_DIM
    for g, (kv_head, half) in enumerate(chains):
        rows = slice(half * qt, (half + 1) * qt)
        cols = slice(kv_head * gw_cols, (kv_head + 1) * gw_cols)
        out = (gc_ref[0, rows, cols] * token_major(o_cmp[g])
               + gs_ref[0, rows, cols] * token_major_normalised(acc_s[g])
               + gw_ref[0, rows, cols] * token_major_normalised(acc_w[g]))
        o_ref[0, rows, cols] = out.astype(o_ref.dtype)


def _sb_kernel(q_ref, k_ref, v_ref, uu_ref, g_ref, o_ref, *state_scr):
    i = pl.program_id(2)
    rows = q_ref.shape[1]
    pair_lanes = [slice(pr * LANES, (pr + 1) * LANES) for pr in range(q_ref.shape[2] // LANES)]
    band = rows // Q_BLOCK
    n_tiles = (i + 1) * band
    low_half = lax.broadcasted_iota(jnp.int32, (rows, LANES), 1) < HEAD_DIM
    low_half_k = lax.broadcasted_iota(jnp.int32, (Q_BLOCK, LANES), 1) < HEAD_DIM
    q_heads = []
    for lanes in pair_lanes:
        q_pair = q_ref[0, :, lanes]
        zero = jnp.zeros_like(q_pair)
        q_heads.append((jnp.where(low_half, q_pair, zero), jnp.where(low_half, zero, q_pair)))

    tri = (lax.broadcasted_iota(jnp.int32, (Q_BLOCK, LANES), 1)
           < lax.broadcasted_iota(jnp.int32, (Q_BLOCK, LANES), 0))

    def update_rows(x, r0, fn):
        return fn(x) if r0 == 0 else jnp.concatenate([x[:r0], fn(x[r0:])], axis=0)

    def on_diagonal(x, fill):
        masked = jnp.where(tri, x[:Q_BLOCK], fill)
        return masked if x.shape[0] == Q_BLOCK else jnp.concatenate([masked, x[Q_BLOCK:]], axis=0)

    def rows_from(x, r0):
        return x[r0:] if r0 else x

    def softplus_bits(z, diagonal):
        sp = jnp.maximum(z, 0.0) + jnp.log2(1.0 + jnp.exp2(-jnp.abs(z)))
        return on_diagonal(sp, 0.0) if diagonal else sp

    def later_sums(sp):
        return _dot(jnp.concatenate(_split2(sp), axis=1), uu_ref[...])

    def head_weights(z, sp, sums, later, r0, diagonal):
        after = sums[:, :LANES] + rows_from(later, r0)
        a = jnp.exp2(z - sp - after)
        return (on_diagonal(a, 0.0) if diagonal else a).astype(_MXU_DTYPE)

    def pair_update(v_pair, weights, sums, state, r0):
        zero_v = jnp.zeros_like(v_pair)
        v_bd = jnp.concatenate([jnp.where(low_half_k, v_pair, zero_v),
                                jnp.where(low_half_k, zero_v, v_pair)], axis=0)
        pv = _dot(jnp.concatenate(weights, axis=1), v_bd)
        return (update_rows(state[0], r0, lambda part: part + pv),
                *(update_rows(later, r0, lambda part, t=t: part + t[:, LANES:]) for later, t in zip(state[1:], sums)))

    def tile_step(start, carry, r0=0, diagonal=False, stagewise=False):
        n_pairs = len(pair_lanes)
        k_pairs = [k_ref[0, pl.ds(start, Q_BLOCK), lanes] for lanes in pair_lanes]
        v_pairs = [v_ref[0, pl.ds(start, Q_BLOCK), lanes] for lanes in pair_lanes]
        states = [carry[3 * pr:3 * pr + 3] for pr in range(n_pairs)]
        logits = lambda pr: [_nt_dot(rows_from(q_h, r0), k_pairs[pr]) for q_h in q_heads[pr]]
        finish = lambda pr, z, sp, sums: pair_update(
            v_pairs[pr], [head_weights(z[h], sp[h], sums[h], states[pr][1 + h], r0, diagonal) for h in range(2)],
            sums, states[pr], r0)
        new_carry = []
        if stagewise:
            z = [logits(pr) for pr in range(n_pairs)]
            sp = [[softplus_bits(x, diagonal) for x in z_pr] for z_pr in z]
            sums = [[later_sums(x) for x in sp_pr] for sp_pr in sp]
            for pr in range(n_pairs):
                new_carry.extend(finish(pr, z[pr], sp[pr], sums[pr]))
        else:
            for pr in range(n_pairs):
                z = logits(pr)
                sp = [softplus_bits(x, diagonal) for x in z]
                new_carry.extend(finish(pr, z, sp, [later_sums(x) for x in sp]))
        return tuple(new_carry)

    zeros = jnp.zeros((rows, LANES), F32)
    carry = (zeros,) * (3 * len(pair_lanes))
    for c in reversed(range(band)):
        carry = tile_step(pl.multiple_of(i * rows + c * Q_BLOCK, Q_BLOCK), carry, r0=c * Q_BLOCK, diagonal=True,
                          stagewise=True)

    def main_body(kg, carry):
        for u in range(SB_UNROLL):
            first_key = (n_tiles - band - 1 - (kg * SB_UNROLL + u)) * Q_BLOCK
            carry = tile_step(pl.multiple_of(first_key, Q_BLOCK), carry)
        return carry

    n_trips = (n_tiles - band) // SB_UNROLL

    def settled(carry):
        laters = [x for n, x in enumerate(carry) if n % 3]
        return jnp.min(functools.reduce(jnp.minimum, laters)) >= SB_UNDERFLOW_BITS

    def keep_going(state):
        kg, done = state
        return jnp.logical_and(kg < n_trips, jnp.logical_not(done))

    def save(carry):
        for ref, value in zip(state_scr, carry):
            ref[...] = value

    def main_step(state):
        kg, _ = state
        carry = main_body(kg, tuple(ref[...] for ref in state_scr))
        save(carry)
        return kg + 1, settled(carry)

    save(carry)
    lax.while_loop(keep_going, main_step, (0, False))
    for pr, lanes in enumerate(pair_lanes):
        o_ref[0, :, lanes] = (state_scr[3 * pr][...] * g_ref[0, :, lanes]).astype(o_ref.dtype)


def _outproj_kernel(x_ref, on_ref, os_ref, w_ref, o_ref):
    o_ref[...] = (x_ref[...] + _dot(on_ref[...], w_ref[:NSA_WIDTH, :])
                  + _dot(os_ref[...], w_ref[NSA_WIDTH:, :]))


def _rope_tables(pos, reps):
    inv_freq = np.power(ROPE_THETA, -np.arange(0, ROPE_DIM, 2, dtype=np.float64) / ROPE_DIM)
    ang = pos.astype(np.float64)[:, None] * inv_freq[None, :]
    cos, sin = np.cos(ang), np.sin(ang)
    n = pos.shape[0]
    rest = HEAD_DIM - ROPE_DIM
    cos_h = np.concatenate([cos, cos, np.ones((n, rest))], axis=1)
    sin_h = np.concatenate([-sin, sin, np.zeros((n, rest))], axis=1)
    return jnp.asarray(np.tile(cos_h, (1, reps)), F32), jnp.asarray(np.tile(sin_h, (1, reps)), F32)


def _const_spec(shape):
    return pl.BlockSpec(shape, lambda *_: (0,) * len(shape))


def kernel(x, norm_gain, w_in, q_norm_gain, k_norm_cmp, k_norm_slc, k_norm_win,
           cmp_k_pos, cmp_k_w1, cmp_k_b1, cmp_k_w2, cmp_v_pos, cmp_v_w1, cmp_v_b1, cmp_v_w2, w_out):
    B, S, DM = x.shape
    D, G, R = HEAD_DIM, NSA_KV_HEADS, NSA_GROUP
    mxu = _MXU_DTYPE
    n_tok = B * S
    tm = ROW_TILE
    n_sblk = S // tm
    ncp = S // CMP_STRIDE
    n_slc = S // SLC_LEN
    n_cmp = (S - CMP_LEN) // CMP_STRIDE + 1
    assert S % tm == 0 and ncp % LANES == 0 and n_slc <= LANES - D and n_slc >= SLC_TOPN

    gl_end = C_GL + NSA_HEADS * N_BRANCH
    w_mxu = w_in.astype(mxu)
    w_cat = (jnp.zeros((DM, N_COLS), mxu).at[:, :gl_end].set(w_mxu[:, :gl_end])
             .at[:, C_GN:].set(w_mxu[:, gl_end:]))
    assert w_cat.shape[1] == N_COLS
    cos_t, sin_t = _rope_tables(np.arange(S), LANES // D)
    cos_c, sin_c = _rope_tables(np.arange(ncp) * CMP_STRIDE + (CMP_LEN - 1), LANES // D)
    lane_i = np.arange(LANES)
    bd = (lane_i[:, None] // D == lane_i[None, :] // D).astype(np.float32)
    bd = jnp.asarray(np.concatenate([bd, bd], axis=0), mxu)
    eg = np.zeros((LANES, N_BRANCH * NSA_WIDTH), np.float32)
    for hh in range(NSA_HEADS):
        for br in range(N_BRANCH):
            eg[hh * N_BRANCH + br, br * NSA_WIDTH + hh * D:br * NSA_WIDTH + (hh + 1) * D] = 1.0
    eg = jnp.asarray(np.concatenate([eg, eg], axis=0), mxu)
    perm = np.zeros((LANES, LANES), np.float32)
    for c in range(ROPE_HALF):
        perm[c + ROPE_HALF, c] = 1.0
        perm[c, c + ROPE_HALF] = 1.0
    perm = jnp.asarray(perm, mxu)
    cs = np.arange(ncp) * CMP_STRIDE
    ss = np.arange(LANES - D) * SLC_LEN
    ovl = np.clip(np.minimum(cs[None, :] + CMP_LEN, ss[:, None] + SLC_LEN)
                  - np.maximum(cs[None, :], ss[:, None]), 0, None).astype(np.float32) / CMP_LEN
    ovl[:, n_cmp:] = 0.0
    ovl[n_slc:, :] = 0.0
    ovl = jnp.asarray(ovl, mxu)
    sidx = np.arange(LANES)
    tri = (sidx[:, None] > sidx[None, :]).astype(np.float32)
    uu_half = np.concatenate([tri, np.ones((LANES, LANES), np.float32)], axis=1)
    uu = jnp.asarray(np.concatenate([uu_half, uu_half], axis=0), mxu)

    row = lambda v: v.reshape(1, -1).astype(F32)
    x2 = x.reshape(n_tok, DM)

    tok_spec = lambda w: pl.BlockSpec((tm, w), lambda t: (t, 0))
    head_spec = lambda nh, w: pl.BlockSpec((1, nh, tm, w), lambda t: (t // n_sblk, 0, t % n_sblk, 0))
    tab_spec = pl.BlockSpec((tm, LANES), lambda t: (t % n_sblk, 0))
    chunk_w = CMP_STRIDE * D
    chunk_spec = pl.BlockSpec((1, G, tm // CMP_STRIDE, chunk_w), lambda t: (t // n_sblk, 0, t % n_sblk, 0))
    sds = jax.ShapeDtypeStruct
    outs = pl.pallas_call(
        functools.partial(_inproj_kernel, n_sblk=n_sblk),
        grid=(n_tok // tm,),
        in_specs=[tok_spec(DM), _const_spec((1, DM)), _const_spec((DM, N_COLS)),
                  _const_spec((1, NSA_WIDTH)), _const_spec((1, 2 * NSA_KV_WIDTH)),
                  _const_spec((2 * LANES, LANES)), tab_spec, tab_spec,
                  _const_spec((2 * LANES, N_BRANCH * NSA_WIDTH))],
        out_specs=[head_spec(NSA_HEADS, LANES), chunk_spec, chunk_spec,
                   head_spec(G, LANES), head_spec(G, LANES), head_spec(G, LANES), head_spec(G, LANES),
                   tok_spec(NSA_WIDTH), tok_spec(NSA_WIDTH), tok_spec(NSA_WIDTH),
                   tok_spec(SB_WIDTH), tok_spec(SB_WIDTH), tok_spec(SB_WIDTH), tok_spec(SB_WIDTH)],
        out_shape=[sds((B, NSA_HEADS, S, LANES), mxu), sds((B, G, ncp, chunk_w), F32), sds((B, G, ncp, chunk_w), F32),
                   sds((B, G, S, LANES), mxu), sds((B, G, S, LANES), mxu), sds((B, G, S, LANES), mxu),
                   sds((B, G, S, LANES), mxu),
                   sds((n_tok, NSA_WIDTH), F32), sds((n_tok, NSA_WIDTH), F32), sds((n_tok, NSA_WIDTH), F32),
                   sds((n_tok, SB_WIDTH), mxu), sds((n_tok, SB_WIDTH), mxu), sds((n_tok, SB_WIDTH), mxu),
                   sds((n_tok, SB_WIDTH), F32)],
        scratch_shapes=[pltpu.VMEM((2, tm, NSA_KV_WIDTH), F32)],
        compiler_params=pltpu.CompilerParams(dimension_semantics=("parallel",), vmem_limit_bytes=VMEM_LIMIT),
        name="inproj",
    )(x2, row(norm_gain), w_cat, row(jnp.tile(q_norm_gain, NSA_HEADS)),
      row(jnp.concatenate([jnp.tile(k_norm_slc, G), jnp.tile(k_norm_win, G)])), bd, cos_t, sin_t, eg)
    (q_nsa, kc_raw, vc_raw, ks_aug, vs_aug, k_win, vw_aug, g_cmp, g_slc, g_win,
     q_sb, k_sb, v_sb, g_sb) = outs

    chunks = lambda a: a.reshape(B * G, ncp, chunk_w)
    bg_spec = lambda r, w: pl.BlockSpec((1, r, w), lambda t: (t, 0, 0))
    k_cmp, v_cmp = pl.pallas_call(
        _compress_kernel,
        grid=(B * G,),
        in_specs=[bg_spec(ncp, chunk_w), bg_spec(ncp, chunk_w),
                  _const_spec((2, chunk_w)), _const_spec((2, chunk_w)),
                  _const_spec((CMP_LEN * D, D)), _const_spec((1, D)), _const_spec((D, LANES)),
                  _const_spec((CMP_LEN * D, D)), _const_spec((1, D)), _const_spec((D, LANES)),
                  _const_spec((1, LANES)), _const_spec((ncp, LANES)), _const_spec((ncp, LANES)),
                  _const_spec((LANES, LANES))],
        out_specs=[bg_spec(ncp, LANES), bg_spec(ncp, LANES)],
        out_shape=[sds((B * G, ncp, LANES), mxu), sds((B * G, ncp, LANES), mxu)],
        compiler_params=pltpu.CompilerParams(dimension_semantics=("parallel",), vmem_limit_bytes=VMEM_LIMIT),
        name="compress",
    )(chunks(kc_raw), chunks(vc_raw), cmp_k_pos.reshape(2, chunk_w), cmp_v_pos.reshape(2, chunk_w),
      cmp_k_w1.astype(mxu), row(cmp_k_b1), jnp.pad(cmp_k_w2, ((0, 0), (0, LANES - D))).astype(mxu),
      cmp_v_w1.astype(mxu), row(cmp_v_b1), jnp.tile(cmp_v_w2, (1, LANES // D)).astype(mxu),
      row(jnp.pad(k_norm_cmp, (0, LANES - D))), cos_c, sin_c, perm)
    k_cmp = k_cmp.reshape(B, G, ncp, LANES)
    v_cmp = v_cmp.reshape(B, G, ncp, LANES)

    k_gain = jnp.max(jnp.abs(jnp.stack([k_norm_cmp, k_norm_slc, k_norm_win])))
    logit_bound = jnp.max(jnp.abs(q_norm_gain)) * k_gain * (1.02 * D * SCALE * LOG2E)
    logit_bound = logit_bound.astype(F32).reshape(1)
    kv_spec = lambda r, w: pl.BlockSpec((1, G, r, w), lambda b, i: (b, 0, 0, 0))
    gate_spec = pl.BlockSpec((1, 2 * NSA_QT, NSA_WIDTH), lambda b, i: (b, i, 0))
    g3 = lambda a: a.reshape(B, S, NSA_WIDTH)
    nsa_call = lambda exact_shift: pl.pallas_call(
        functools.partial(_nsa_kernel, exact_shift=exact_shift),
        grid=(B, S // (2 * NSA_QT)),
        in_specs=[pl.BlockSpec(memory_space=pltpu.SMEM),
                  pl.BlockSpec((1, NSA_HEADS, 2 * NSA_QT, LANES), lambda b, i: (b, 0, i, 0)),
                  kv_spec(ncp, LANES), kv_spec(ncp, LANES), kv_spec(S, LANES), kv_spec(S, LANES),
                  kv_spec(S, LANES), kv_spec(S, LANES), _const_spec((LANES - D, ncp)),
                  gate_spec, gate_spec, gate_spec],
        out_specs=gate_spec,
        out_shape=sds((B, S, NSA_WIDTH), mxu),
        scratch_shapes=[pltpu.VMEM((2 * G, R * NSA_QT, LANES), F32)],
        compiler_params=pltpu.CompilerParams(dimension_semantics=("parallel", "arbitrary"),
                                             vmem_limit_bytes=VMEM_LIMIT),
        name="nsa_exact_shift" if exact_shift else "nsa",
    )
    nsa_operands = (logit_bound, q_nsa, k_cmp, v_cmp, ks_aug, vs_aug, k_win, vw_aug, ovl,
                    g3(g_cmp), g3(g_slc), g3(g_win))
    o_nsa = lax.cond(logit_bound[0] <= MAX_STATIC_SHIFT,
                     lambda ops: nsa_call(False)(*ops), lambda ops: nsa_call(True)(*ops), nsa_operands)

    sb3 = lambda a: a.reshape(B, S, SB_WIDTH)
    pair_q = pl.BlockSpec((1, SB_ROWS, SB_PAIRS * LANES), lambda b, hp, i: (b, i, hp))
    pair_kv = pl.BlockSpec((1, S, SB_PAIRS * LANES), lambda b, hp, i: (b, 0, hp))
    o_sb = pl.pallas_call(
        _sb_kernel,
        grid=(B, SB_WIDTH // (SB_PAIRS * LANES), S // SB_ROWS),
        in_specs=[pair_q, pair_kv, pair_kv, _const_spec((2 * LANES, 2 * LANES)), pair_q],
        out_specs=pair_q,
        out_shape=sds((B, S, SB_WIDTH), mxu),
        scratch_shapes=[pltpu.VMEM((SB_ROWS, LANES), F32)] * (3 * SB_PAIRS),
        compiler_params=pltpu.CompilerParams(dimension_semantics=("parallel", "parallel", "arbitrary"),
                                             vmem_limit_bytes=VMEM_LIMIT),
        name="stickbreak",
    )(sb3(q_sb), sb3(k_sb), sb3(v_sb), uu, sb3(g_sb))

    out_spec = lambda w: pl.BlockSpec((OUT_ROW_TILE, w), lambda t: (t, 0))
    out = pl.pallas_call(
        _outproj_kernel,
        grid=(n_tok // OUT_ROW_TILE,),
        in_specs=[out_spec(DM), out_spec(NSA_WIDTH), out_spec(SB_WIDTH),
                  _const_spec((NSA_WIDTH + SB_WIDTH, DM))],
        out_specs=out_spec(DM),
        out_shape=sds((n_tok, DM), x.dtype),
        compiler_params=pltpu.CompilerParams(dimension_semantics=("parallel",), vmem_limit_bytes=VMEM_LIMIT),
        name="outproj",
    )(x2, o_nsa.reshape(n_tok, NSA_WIDTH), o_sb.reshape(n_tok, SB_WIDTH), w_out.astype(mxu))
    return out.reshape(B, S, DM)
```

```python
import functools
import math

import numpy as np
import jax
import jax.numpy as jnp
from jax import lax
from jax.experimental import pallas as pl
from jax.experimental.pallas import tpu as pltpu

HEAD_DIM = 64
NSA_HEADS = 8
NSA_KV_HEADS = 2
NSA_GROUP = NSA_HEADS // NSA_KV_HEADS
SB_HEADS = 8
NSA_WIDTH = NSA_HEADS * HEAD_DIM
SB_WIDTH = SB_HEADS * HEAD_DIM
NSA_KV_WIDTH = NSA_KV_HEADS * HEAD_DIM
N_BRANCH = 3
CMP_LEN = 32
CMP_STRIDE = 16
SLC_LEN = 64
SLC_TOPN = 16
WINDOW = 512
Q_BLOCK = 128
ROPE_DIM = HEAD_DIM // 4
ROPE_HALF = ROPE_DIM // 2
ROPE_THETA = 500000.0
EPS = 1e-6
FORCE_BONUS = 1.0e4
SCALE = 1.0 / math.sqrt(HEAD_DIM)
LOG2E = math.log2(math.e)

LANES = 128
SUBLANES = 8
MASK_NEG = -1.0e30
M_INIT = -3.0e38
ROW_TILE = 512
OUT_ROW_TILE = 2048
NSA_QT = 256
SB_ROWS = 512
SB_UNROLL = 2
SB_PAIRS = 4
MAX_STATIC_SHIFT = 60.0
SB_UNDERFLOW_BITS = float("inf")
VMEM_LIMIT = 56 * 1024 * 1024

_MXU_DTYPE = jnp.bfloat16
F32 = jnp.float32

C_Q = 0
C_KC = 512
C_KS = 768
C_KW = 1024
C_GL = 1280
C_GN = 1408
C_QSB = 1920
C_KSB = 2432
C_VSB = 2944
C_GSB = 3456
N_COLS = 3968


def _nt_dot(a, b):
    return lax.dot_general(a, b, (((1,), (1,)), ((), ())), preferred_element_type=F32)


def _dot(a, b):
    return jnp.dot(a, b, preferred_element_type=F32)


def _split2(v):
    hi = v.astype(_MXU_DTYPE)
    lo = (v - hi.astype(F32)).astype(_MXU_DTYPE)
    return hi, lo


def _split3(v):
    hi = v.astype(_MXU_DTYPE)
    r1 = v - hi.astype(F32)
    mid = r1.astype(_MXU_DTYPE)
    lo = (r1 - mid.astype(F32)).astype(_MXU_DTYPE)
    return hi, mid, lo


def _inproj_kernel(x_ref, ng_ref, w_ref, qg_ref, kg_ref, bd_ref, cos_ref, sin_ref, eg_ref,
                   q_ref, kc_ref, vc_ref, ksa_ref, vsa_ref, kw_ref, vwa_ref,
                   gc_ref, gs_ref, gw_ref, qsb_ref, ksb_ref, vsb_ref, gsb_ref, cmp_scr, *, n_sblk):
    tm = x_ref.shape[0]
    x = x_ref[...]
    ms = jnp.mean(x * x, axis=-1, keepdims=True)
    h = (x * lax.rsqrt(ms + EPS) * ng_ref[...]).astype(_MXU_DTYPE)

    def proj(lo, width):
        return _dot(h, w_ref[:, lo:lo + width])

    lane = lax.broadcasted_iota(jnp.int32, (tm, LANES), 1)
    low_half = lane < HEAD_DIM

    def head_norm_rope(y, gain):
        width = y.shape[1]
        rep = width // LANES
        hi, lo = _split2(y * y)
        ssum = jnp.concatenate(
            [_dot(jnp.concatenate([hi[:, c:c + LANES], lo[:, c:c + LANES]], axis=1), bd_ref[...])
             for c in range(0, width, LANES)], axis=1)
        yn = y * lax.rsqrt(ssum * (1.0 / HEAD_DIM) + EPS) * gain
        cos = jnp.concatenate([cos_ref[...]] * rep, axis=1) if rep > 1 else cos_ref[...]
        sin = jnp.concatenate([sin_ref[...]] * rep, axis=1) if rep > 1 else sin_ref[...]
        fwd = pltpu.roll(yn, ROPE_HALF, axis=1)
        bwd = pltpu.roll(yn, width - ROPE_HALF, axis=1)
        lane_w = lax.broadcasted_iota(jnp.int32, (tm, width), 1)
        partner = jnp.where((lane_w & (HEAD_DIM - 1)) < ROPE_HALF, bwd, fwd)
        return yn * cos + partner * sin

    def head_pair(slab, p):
        chunk = slab[:, p * LANES:(p + 1) * LANES]
        return chunk, pltpu.roll(chunk, HEAD_DIM, axis=1)

    qn = head_norm_rope(proj(C_Q, NSA_WIDTH), qg_ref[...]) * (SCALE * LOG2E)
    for p in range(NSA_HEADS // 2):
        ev, od = head_pair(qn, p)
        q_ref[0, 2 * p] = jnp.where(low_half, ev, 0.0).astype(q_ref.dtype)
        q_ref[0, 2 * p + 1] = jnp.where(low_half, od, 0.0).astype(q_ref.dtype)

    kv_cmp = proj(C_KC, 2 * NSA_KV_WIDTH)
    kv_slc = proj(C_KS, 2 * NSA_KV_WIDTH)
    kv_win = proj(C_KW, 2 * NSA_KV_WIDTH)

    n_rows = tm // CMP_STRIDE
    low_half_c = lax.broadcasted_iota(jnp.int32, (n_rows, LANES), 1) < HEAD_DIM
    for ref, p in ((kc_ref, 0), (vc_ref, 1)):
        cmp_scr[p] = kv_cmp[:, p * LANES:(p + 1) * LANES]
        for j in range(CMP_STRIDE // 2):
            a = cmp_scr[p, pl.ds(2 * j, n_rows, stride=CMP_STRIDE), :]
            b = cmp_scr[p, pl.ds(2 * j + 1, n_rows, stride=CMP_STRIDE), :]
            ref[0, 0, :, j * LANES:(j + 1) * LANES] = jnp.where(low_half_c, a, pltpu.roll(b, HEAD_DIM, axis=1))
            ref[0, 1, :, j * LANES:(j + 1) * LANES] = jnp.where(low_half_c, pltpu.roll(a, HEAD_DIM, axis=1), b)

    sblk = lax.rem(pl.program_id(0), n_sblk)
    row = lax.broadcasted_iota(jnp.int32, (tm, LANES), 0)
    key_blk = (sblk * tm + row) >> int(math.log2(SLC_LEN))
    onehot = jnp.where(lane - HEAD_DIM == key_blk, 1.0, 0.0)
    k_sw = head_norm_rope(jnp.concatenate([kv_slc[:, :NSA_KV_WIDTH], kv_win[:, :NSA_KV_WIDTH]], axis=1),
                          kg_ref[...])
    ev, od = head_pair(k_sw, 0)
    ksa_ref[0, 0] = jnp.where(low_half, ev, onehot).astype(ksa_ref.dtype)
    ksa_ref[0, 1] = jnp.where(low_half, od, onehot).astype(ksa_ref.dtype)
    ev, od = head_pair(kv_slc, 1)
    vsa_ref[0, 0] = jnp.where(low_half, ev, 1.0).astype(vsa_ref.dtype)
    vsa_ref[0, 1] = jnp.where(low_half, od, 1.0).astype(vsa_ref.dtype)

    one_lane = jnp.where(lane == HEAD_DIM, 1.0, 0.0)
    ev, od = head_pair(k_sw, 1)
    kw_ref[0, 0] = jnp.where(low_half, ev, one_lane).astype(kw_ref.dtype)
    kw_ref[0, 1] = jnp.where(low_half, od, one_lane).astype(kw_ref.dtype)
    ev, od = head_pair(kv_win, 1)
    vwa_ref[0, 0] = jnp.where(low_half, ev, 1.0).astype(vwa_ref.dtype)
    vwa_ref[0, 1] = jnp.where(low_half, od, 1.0).astype(vwa_ref.dtype)

    gn = proj(C_GN, NSA_WIDTH)
    silu_n = gn * jax.nn.sigmoid(gn)
    gl_split = jnp.concatenate(_split2(proj(C_GL, LANES)), axis=1)
    for br, ref in enumerate((gc_ref, gs_ref, gw_ref)):
        ref[...] = jax.nn.sigmoid(_dot(gl_split, eg_ref[:, br * NSA_WIDTH:(br + 1) * NSA_WIDTH])) * silu_n

    qsb_ref[...] = (proj(C_QSB, SB_WIDTH) * (SCALE * LOG2E)).astype(qsb_ref.dtype)
    ksb_ref[...] = proj(C_KSB, SB_WIDTH).astype(ksb_ref.dtype)
    vsb_ref[...] = proj(C_VSB, SB_WIDTH).astype(vsb_ref.dtype)
    gsb = proj(C_GSB, SB_WIDTH)
    gsb_ref[...] = gsb * jax.nn.sigmoid(gsb)


def _compress_kernel(kc_ref, vc_ref, posk_ref, posv_ref, w1k_ref, b1k_ref, w2k_ref,
                     w1v_ref, b1v_ref, w2v_ref, kg_ref, cos_ref, sin_ref, perm_ref,
                     kcmp_ref, vcmp_ref):
    half = CMP_STRIDE * HEAD_DIM

    def phi(c_ref, pos_ref, w1_ref, b1_ref, w2_ref):
        c = c_ref[0]
        n = c.shape[0]
        top = _dot((c + pos_ref[0:1, :]).astype(_MXU_DTYPE), w1_ref[:half, :])
        bot = _dot((c + pos_ref[1:2, :]).astype(_MXU_DTYPE), w1_ref[half:, :])
        hid = top + pltpu.roll(bot, n - 1, axis=0) + b1_ref[...]
        return _dot((hid * jax.nn.sigmoid(hid)).astype(_MXU_DTYPE), w2_ref[...])

    k = phi(kc_ref, posk_ref, w1k_ref, b1k_ref, w2k_ref)
    ms = jnp.sum(k * k, axis=-1, keepdims=True) * (1.0 / HEAD_DIM)
    kn = k * lax.rsqrt(ms + EPS) * kg_ref[...]
    hi, lo = _split2(kn)
    partner = _dot(hi, perm_ref[...]) + _dot(lo, perm_ref[...])
    one_lane = jnp.where(lax.broadcasted_iota(jnp.int32, k.shape, 1) == HEAD_DIM, 1.0, 0.0)
    kcmp_ref[0] = (kn * cos_ref[...] + partner * sin_ref[...] + one_lane).astype(kcmp_ref.dtype)
    vcmp_ref[0] = phi(vc_ref, posv_ref, w1v_ref, b1v_ref, w2v_ref).astype(vcmp_ref.dtype)


def _nsa_kernel(bound_ref, q_ref, kc_ref, vc_ref, ksa_ref, vsa_ref, kw_ref, vwa_ref, ovl_ref,
                gc_ref, gs_ref, gw_ref, o_ref, acc_scr, *, exact_shift):
    qt = NSA_QT
    kw = 2 * qt
    assert q_ref.shape[2] == kw
    chains = [(g, half) for half in range(2) for g in range(kc_ref.shape[1])]
    groups = range(len(chains))
    kv = [g for g, _ in chains]
    blk = [2 * pl.program_id(1) + half for _, half in chains]
    rq = NSA_GROUP * qt
    n_blk_lanes = LANES - HEAD_DIM
    q_pads = [q_ref[0, g * NSA_GROUP:(g + 1) * NSA_GROUP, half * qt:(half + 1) * qt].reshape(rq, LANES)
              for g, half in chains]
    heads = lambda x: jnp.concatenate([x] * NSA_GROUP, axis=0)
    t_tok = lax.broadcasted_iota(jnp.int32, (qt, 1), 0)
    rel_w = heads(t_tok - lax.broadcasted_iota(jnp.int32, (qt, kw), 1))
    rel = rel_w[:, :qt]
    lane_q = lax.broadcasted_iota(jnp.int32, (qt, LANES), 1)
    lane_r = heads(lane_q)
    low_half = lane_q < HEAD_DIM

    logit_bound = bound_ref[0]
    row_max = lambda s: jnp.max(s, axis=-1, keepdims=True)

    def shift_lane(shift):
        return jnp.where(lane_r == HEAD_DIM, -shift, 0.0).astype(q_pads[0].dtype)

    def window_logits(g, q_lhs):
        i = blk[g]
        s_parts, v_parts = [], []
        for back in range(WINDOW // qt + 1):
            start = pl.multiple_of(jnp.maximum(i - back, 0) * qt, qt)
            s = _nt_dot(q_lhs, kw_ref[0, kv[g], pl.ds(start, qt), :])
            v = vwa_ref[0, kv[g], pl.ds(start, qt), :]
            if back == 0:
                s = jnp.where(rel >= 0, s, MASK_NEG)
            elif (back + 1) * qt <= WINDOW:
                v = v * jnp.where(i >= back, 1.0, 0.0).astype(v.dtype)
            else:
                s = jnp.where(rel < (WINDOW - back * qt) - jnp.where(i >= back, 0, WINDOW), s, MASK_NEG)
            s_parts.append(s)
            v_parts.append(v)
        return jnp.concatenate(s_parts, axis=1), jnp.concatenate(v_parts, axis=0)

    def cmp_logits(g, q_lhs):
        kc = kc_ref[0, kv[g]]
        cmp_end = lax.broadcasted_iota(jnp.int32, (rq, kc.shape[0]), 1) * CMP_STRIDE + (CMP_LEN - 1)
        return jnp.where(cmp_end <= blk[g] * qt + heads(t_tok), _nt_dot(q_lhs, kc), MASK_NEG)

    if exact_shift:
        pre_shifts = [shift_lane(m) for g in groups
                      for m in (row_max(window_logits(g, q_pads[g])[0]), row_max(cmp_logits(g, q_pads[g])))]
    else:
        pre_shifts = [shift_lane(logit_bound)] * (2 * len(chains))

    ovl = ovl_ref[...]
    j_idx = lax.broadcasted_iota(jnp.int32, (n_blk_lanes, qt), 0)
    row_in_grp = lax.broadcasted_iota(jnp.int32, (SUBLANES, qt), 0)

    def selected_blocks(p, i):
        blk_t = (i * qt + lax.broadcasted_iota(jnp.int32, (n_blk_lanes, qt), 1)) >> int(math.log2(SLC_LEN))
        slc_valid = j_idx <= blk_t
        forced = (j_idx == 0) | (j_idx == blk_t) | (j_idx == blk_t - 1)
        p_sum = p[0:qt]
        for r in range(1, NSA_GROUP):
            p_sum = p_sum + p[r * qt:(r + 1) * qt]
        p_slc = sum(_nt_dot(ovl, part) for part in _split3(p_sum))
        score = jnp.where(slc_valid, p_slc + jnp.where(forced, FORCE_BONUS, 0.0), -jnp.inf)
        n_grp = n_blk_lanes // SUBLANES
        grp_rows = [score[c * SUBLANES:(c + 1) * SUBLANES] for c in range(n_grp)]
        grp_rank = [jnp.zeros((SUBLANES, qt), F32) for _ in range(n_grp)]
        for ii in range(n_blk_lanes):
            s_i = score[ii:ii + 1, :]
            for c in range(n_grp):
                rows = grp_rows[c]
                if c * SUBLANES > ii:
                    beats = jnp.where(s_i >= rows, 1.0, 0.0)
                elif (c + 1) * SUBLANES - 1 <= ii:
                    beats = jnp.where(s_i > rows, 1.0, 0.0)
                else:
                    beats = jnp.where(row_in_grp > ii - c * SUBLANES,
                                      jnp.where(s_i >= rows, 1.0, 0.0), jnp.where(s_i > rows, 1.0, 0.0))
                grp_rank[c] = grp_rank[c] + beats
        rank = jnp.concatenate(grp_rank, axis=0)
        sel_t = jnp.where(slc_valid, jnp.where(rank < SLC_TOPN, 1.0, 0.0), 0.0)
        return heads(jnp.concatenate([jnp.ones((HEAD_DIM, qt), F32), sel_t], axis=0).T)

    acc_w, o_cmp, sel_rows = [], [], []
    for g in groups:
        s_win, v_win = window_logits(g, q_pads[g] + pre_shifts[2 * g])
        acc_w.append(_dot(jnp.exp2(s_win).astype(_MXU_DTYPE), v_win))
        p = jnp.exp2(cmp_logits(g, q_pads[g] + pre_shifts[2 * g + 1]))
        p = p / jnp.maximum(jnp.sum(p, axis=-1, keepdims=True), 1e-30)
        o_cmp.append(_dot(p.astype(_MXU_DTYPE), vc_ref[0, kv[g]]))
        sel_rows.append(selected_blocks(p, blk[g]))

    n_kt = pl.program_id(1) + 1

    def augmented_q(g, shift):
        bias = jnp.where(lane_r >= HEAD_DIM, jnp.where(sel_rows[g] > 0.5, -shift, MASK_NEG), 0.0)
        return q_pads[g] + bias.astype(q_pads[g].dtype)

    def slc_logits(g, q_aug, kt):
        s = _nt_dot(q_aug, ksa_ref[0, kv[g], pl.ds(pl.multiple_of(kt * kw, kw), kw), :])
        return jnp.where(rel_w >= kt * kw - blk[g] * qt, s, MASK_NEG)

    def slc_row_max(g):
        q_aug = augmented_q(g, 0.0)
        return lax.fori_loop(0, n_kt, lambda kt, m_run: jnp.maximum(m_run, row_max(slc_logits(g, q_aug, kt))),
                             jnp.full((rq, 1), M_INIT, F32))

    q_augs = [augmented_q(g, slc_row_max(g) if exact_shift else logit_bound) for g in groups]

    def slc_tile(g, kt, width):
        start = pl.multiple_of(kt * kw, kw)
        s = _nt_dot(q_augs[g], ksa_ref[0, kv[g], pl.ds(start, width), :])
        s = jnp.where(rel_w[:, :width] >= kt * kw - blk[g] * qt, s, MASK_NEG)
        return _dot(jnp.exp2(s).astype(_MXU_DTYPE), vsa_ref[0, kv[g], pl.ds(start, width), :])

    for g in groups:
        acc_scr[g] = jnp.zeros((rq, LANES), F32)

    def slc_body(kt, _):
        for g in groups:
            acc_scr[g] += slc_tile(g, kt, kw)
        return 0

    lax.fori_loop(0, n_kt - 1, slc_body, 0)
    acc_s = [acc_scr[g] + slc_tile(g, n_kt - 1, qt if half == 0 else kw) for g, (_, half) in enumerate(chains)]

    head = lambda a, r: a[r * qt:(r + 1) * qt]

    def token_major(o):
        return jnp.concatenate([jnp.where(low_half, head(o, r), head(o, r + 1))
                                for r in range(0, NSA_GROUP, 2)], axis=1)

    def token_major_normalised(acc):
        inv = 1.0 / jnp.where(lane_r >= HEAD_DIM, acc, 1.0)
        pairs = []
        for r in range(0, NSA_GROUP, 2):
            even = head(acc, r) * pltpu.roll(head(inv, r), HEAD_DIM, axis=1)
            odd = pltpu.roll(head(acc, r + 1), HEAD_DIM, axis=1) * head(inv, r + 1)
            pairs.append(jnp.where(low_half, even, odd))
        return jnp.concatenate(pairs, axis=1)

    gw_cols = NSA_GROUP * HEAD_DIM
    for g, (kv_head, half) in enumerate(chains):
        rows = slice(half * qt, (half + 1) * qt)
        cols = slice(kv_head * gw_cols, (kv_head + 1) * gw_cols)
        out = (gc_ref[0, rows, cols] * token_major(o_cmp[g])
               + gs_ref[0, rows, cols] * token_major_normalised(acc_s[g])
               + gw_ref[0, rows, cols] * token_major_normalised(acc_w[g]))
        o_ref[0, rows, cols] = out.astype(o_ref.dtype)


def _sb_kernel(q_ref, k_ref, v_ref, uu_ref, g_ref, o_ref, *state_scr):
    i = pl.program_id(2)
    rows = q_ref.shape[1]
    pair_lanes = [slice(pr * LANES, (pr + 1) * LANES) for pr in range(q_ref.shape[2] // LANES)]
    band = rows // Q_BLOCK
    n_tiles = (i + 1) * band
    low_half = lax.broadcasted_iota(jnp.int32, (rows, LANES), 1) < HEAD_DIM
    low_half_k = lax.broadcasted_iota(jnp.int32, (Q_BLOCK, LANES), 1) < HEAD_DIM
    q_heads = []
    for lanes in pair_lanes:
        q_pair = q_ref[0, :, lanes]
        zero = jnp.zeros_like(q_pair)
        q_heads.append((jnp.where(low_half, q_pair, zero), jnp.where(low_half, zero, q_pair)))

    tri = (lax.broadcasted_iota(jnp.int32, (Q_BLOCK, LANES), 1)
           < lax.broadcasted_iota(jnp.int32, (Q_BLOCK, LANES), 0))

    def update_rows(x, r0, fn):
        return fn(x) if r0 == 0 else jnp.concatenate([x[:r0], fn(x[r0:])], axis=0)

    def on_diagonal(x, fill):
        masked = jnp.where(tri, x[:Q_BLOCK], fill)
        return masked if x.shape[0] == Q_BLOCK else jnp.concatenate([masked, x[Q_BLOCK:]], axis=0)

    def rows_from(x, r0):
        return x[r0:] if r0 else x

    def softplus_bits(z, diagonal):
        sp = jnp.maximum(z, 0.0) + jnp.log2(1.0 + jnp.exp2(-jnp.abs(z)))
        return on_diagonal(sp, 0.0) if diagonal else sp

    def later_sums(sp):
        return _dot(jnp.concatenate(_split2(sp), axis=1), uu_ref[...])

    def head_weights(z, sp, sums, later, r0, diagonal):
        after = sums[:, :LANES] + rows_from(later, r0)
        a = jnp.exp2(z - sp - after)
        return (on_diagonal(a, 0.0) if diagonal else a).astype(_MXU_DTYPE)

    def pair_update(v_pair, weights, sums, state, r0):
        zero_v = jnp.zeros_like(v_pair)
        v_bd = jnp.concatenate([jnp.where(low_half_k, v_pair, zero_v),
                                jnp.where(low_half_k, zero_v, v_pair)], axis=0)
        pv = _dot(jnp.concatenate(weights, axis=1), v_bd)
        return (update_rows(state[0], r0, lambda part: part + pv),
                *(update_rows(later, r0, lambda part, t=t: part + t[:, LANES:]) for later, t in zip(state[1:], sums)))

    def tile_step(start, carry, r0=0, diagonal=False, stagewise=False):
        n_pairs = len(pair_lanes)
        k_pairs = [k_ref[0, pl.ds(start, Q_BLOCK), lanes] for lanes in pair_lanes]
        v_pairs = [v_ref[0, pl.ds(start, Q_BLOCK), lanes] for lanes in pair_lanes]
        states = [carry[3 * pr:3 * pr + 3] for pr in range(n_pairs)]
        logits = lambda pr: [_nt_dot(rows_from(q_h, r0), k_pairs[pr]) for q_h in q_heads[pr]]
        finish = lambda pr, z, sp, sums: pair_update(
            v_pairs[pr], [head_weights(z[h], sp[h], sums[h], states[pr][1 + h], r0, diagonal) for h in range(2)],
            sums, states[pr], r0)
        new_carry = []
        if stagewise:
            z = [logits(pr) for pr in range(n_pairs)]
            sp = [[softplus_bits(x, diagonal) for x in z_pr] for z_pr in z]
            sums = [[later_sums(x) for x in sp_pr] for sp_pr in sp]
            for pr in range(n_pairs):
                new_carry.extend(finish(pr, z[pr], sp[pr], sums[pr]))
        else:
            for pr in range(n_pairs):
                z = logits(pr)
                sp = [softplus_bits(x, diagonal) for x in z]
                new_carry.extend(finish(pr, z, sp, [later_sums(x) for x in sp]))
        return tuple(new_carry)

    zeros = jnp.zeros((rows, LANES), F32)
    carry = (zeros,) * (3 * len(pair_lanes))
    for c in reversed(range(band)):
        carry = tile_step(pl.multiple_of(i * rows + c * Q_BLOCK, Q_BLOCK), carry, r0=c * Q_BLOCK, diagonal=True,
                          stagewise=True)

    def main_body(kg, carry):
        for u in range(SB_UNROLL):
            first_key = (n_tiles - band - 1 - (kg * SB_UNROLL + u)) * Q_BLOCK
            carry = tile_step(pl.multiple_of(first_key, Q_BLOCK), carry)
        return carry

    n_trips = (n_tiles - band) // SB_UNROLL

    def settled(carry):
        laters = [x for n, x in enumerate(carry) if n % 3]
        return jnp.min(functools.reduce(jnp.minimum, laters)) >= SB_UNDERFLOW_BITS

    def keep_going(state):
        kg, done = state
        return jnp.logical_and(kg < n_trips, jnp.logical_not(done))

    def save(carry):
        for ref, value in zip(state_scr, carry):
            ref[...] = value

    def main_step(state):
        kg, _ = state
        carry = main_body(kg, tuple(ref[...] for ref in state_scr))
        save(carry)
        return kg + 1, settled(carry)

    save(carry)
    lax.while_loop(keep_going, main_step, (0, False))
    for pr, lanes in enumerate(pair_lanes):
        o_ref[0, :, lanes] = (state_scr[3 * pr][...] * g_ref[0, :, lanes]).astype(o_ref.dtype)


def _outproj_kernel(x_ref, on_ref, os_ref, w_ref, o_ref):
    o_ref[...] = (x_ref[...] + _dot(on_ref[...], w_ref[:NSA_WIDTH, :])
                  + _dot(os_ref[...], w_ref[NSA_WIDTH:, :]))


def _rope_tables(pos, reps):
    inv_freq = np.power(ROPE_THETA, -np.arange(0, ROPE_DIM, 2, dtype=np.float64) / ROPE_DIM)
    ang = pos.astype(np.float64)[:, None] * inv_freq[None, :]
    cos, sin = np.cos(ang), np.sin(ang)
    n = pos.shape[0]
    rest = HEAD_DIM - ROPE_DIM
    cos_h = np.concatenate([cos, cos, np.ones((n, rest))], axis=1)
    sin_h = np.concatenate([-sin, sin, np.zeros((n, rest))], axis=1)
    return jnp.asarray(np.tile(cos_h, (1, reps)), F32), jnp.asarray(np.tile(sin_h, (1, reps)), F32)


def _const_spec(shape):
    return pl.BlockSpec(shape, lambda *_: (0,) * len(shape))


def kernel(x, norm_gain, w_in, q_norm_gain, k_norm_cmp, k_norm_slc, k_norm_win,
           cmp_k_pos, cmp_k_w1, cmp_k_b1, cmp_k_w2, cmp_v_pos, cmp_v_w1, cmp_v_b1, cmp_v_w2, w_out):
    B, S, DM = x.shape
    D, G, R = HEAD_DIM, NSA_KV_HEADS, NSA_GROUP
    mxu = _MXU_DTYPE
    n_tok = B * S
    tm = ROW_TILE
    n_sblk = S // tm
    ncp = S // CMP_STRIDE
    n_slc = S // SLC_LEN
    n_cmp = (S - CMP_LEN) // CMP_STRIDE + 1
    assert S % tm == 0 and ncp % LANES == 0 and n_slc <= LANES - D and n_slc >= SLC_TOPN

    gl_end = C_GL + NSA_HEADS * N_BRANCH
    w_mxu = w_in.astype(mxu)
    w_cat = (jnp.zeros((DM, N_COLS), mxu).at[:, :gl_end].set(w_mxu[:, :gl_end])
             .at[:, C_GN:].set(w_mxu[:, gl_end:]))
    assert w_cat.shape[1] == N_COLS
    cos_t, sin_t = _rope_tables(np.arange(S), LANES // D)
    cos_c, sin_c = _rope_tables(np.arange(ncp) * CMP_STRIDE + (CMP_LEN - 1), LANES // D)
    lane_i = np.arange(LANES)
    bd = (lane_i[:, None] // D == lane_i[None, :] // D).astype(np.float32)
    bd = jnp.asarray(np.concatenate([bd, bd], axis=0), mxu)
    eg = np.zeros((LANES, N_BRANCH * NSA_WIDTH), np.float32)
    for hh in range(NSA_HEADS):
        for br in range(N_BRANCH):
            eg[hh * N_BRANCH + br, br * NSA_WIDTH + hh * D:br * NSA_WIDTH + (hh + 1) * D] = 1.0
    eg = jnp.asarray(np.concatenate([eg, eg], axis=0), mxu)
    perm = np.zeros((LANES, LANES), np.float32)
    for c in range(ROPE_HALF):
        perm[c + ROPE_HALF, c] = 1.0
        perm[c, c + ROPE_HALF] = 1.0
    perm = jnp.asarray(perm, mxu)
    cs = np.arange(ncp) * CMP_STRIDE
    ss = np.arange(LANES - D) * SLC_LEN
    ovl = np.clip(np.minimum(cs[None, :] + CMP_LEN, ss[:, None] + SLC_LEN)
                  - np.maximum(cs[None, :], ss[:, None]), 0, None).astype(np.float32) / CMP_LEN
    ovl[:, n_cmp:] = 0.0
    ovl[n_slc:, :] = 0.0
    ovl = jnp.asarray(ovl, mxu)
    sidx = np.arange(LANES)
    tri = (sidx[:, None] > sidx[None, :]).astype(np.float32)
    uu_half = np.concatenate([tri, np.ones((LANES, LANES), np.float32)], axis=1)
    uu = jnp.asarray(np.concatenate([uu_half, uu_half], axis=0), mxu)

    row = lambda v: v.reshape(1, -1).astype(F32)
    x2 = x.reshape(n_tok, DM)

    tok_spec = lambda w: pl.BlockSpec((tm, w), lambda t: (t, 0))
    head_spec = lambda nh, w: pl.BlockSpec((1, nh, tm, w), lambda t: (t // n_sblk, 0, t % n_sblk, 0))
    tab_spec = pl.BlockSpec((tm, LANES), lambda t: (t % n_sblk, 0))
    chunk_w = CMP_STRIDE * D
    chunk_spec = pl.BlockSpec((1, G, tm // CMP_STRIDE, chunk_w), lambda t: (t // n_sblk, 0, t % n_sblk, 0))
    sds = jax.ShapeDtypeStruct
    outs = pl.pallas_call(
        functools.partial(_inproj_kernel, n_sblk=n_sblk),
        grid=(n_tok // tm,),
        in_specs=[tok_spec(DM), _const_spec((1, DM)), _const_spec((DM, N_COLS)),
                  _const_spec((1, NSA_WIDTH)), _const_spec((1, 2 * NSA_KV_WIDTH)),
                  _const_spec((2 * LANES, LANES)), tab_spec, tab_spec,
                  _const_spec((2 * LANES, N_BRANCH * NSA_WIDTH))],
        out_specs=[head_spec(NSA_HEADS, LANES), chunk_spec, chunk_spec,
                   head_spec(G, LANES), head_spec(G, LANES), head_spec(G, LANES), head_spec(G, LANES),
                   tok_spec(NSA_WIDTH), tok_spec(NSA_WIDTH), tok_spec(NSA_WIDTH),
                   tok_spec(SB_WIDTH), tok_spec(SB_WIDTH), tok_spec(SB_WIDTH), tok_spec(SB_WIDTH)],
        out_shape=[sds((B, NSA_HEADS, S, LANES), mxu), sds((B, G, ncp, chunk_w), F32), sds((B, G, ncp, chunk_w), F32),
                   sds((B, G, S, LANES), mxu), sds((B, G, S, LANES), mxu), sds((B, G, S, LANES), mxu),
                   sds((B, G, S, LANES), mxu),
                   sds((n_tok, NSA_WIDTH), F32), sds((n_tok, NSA_WIDTH), F32), sds((n_tok, NSA_WIDTH), F32),
                   sds((n_tok, SB_WIDTH), mxu), sds((n_tok, SB_WIDTH), mxu), sds((n_tok, SB_WIDTH), mxu),
                   sds((n_tok, SB_WIDTH), F32)],
        scratch_shapes=[pltpu.VMEM((2, tm, NSA_KV_WIDTH), F32)],
        compiler_params=pltpu.CompilerParams(dimension_semantics=("parallel",), vmem_limit_bytes=VMEM_LIMIT),
        name="inproj",
    )(x2, row(norm_gain), w_cat, row(jnp.tile(q_norm_gain, NSA_HEADS)),
      row(jnp.concatenate([jnp.tile(k_norm_slc, G), jnp.tile(k_norm_win, G)])), bd, cos_t, sin_t, eg)
    (q_nsa, kc_raw, vc_raw, ks_aug, vs_aug, k_win, vw_aug, g_cmp, g_slc, g_win,
     q_sb, k_sb, v_sb, g_sb) = outs

    chunks = lambda a: a.reshape(B * G, ncp, chunk_w)
    bg_spec = lambda r, w: pl.BlockSpec((1, r, w), lambda t: (t, 0, 0))
    k_cmp, v_cmp = pl.pallas_call(
        _compress_kernel,
        grid=(B * G,),
        in_specs=[bg_spec(ncp, chunk_w), bg_spec(ncp, chunk_w),
                  _const_spec((2, chunk_w)), _const_spec((2, chunk_w)),
                  _const_spec((CMP_LEN * D, D)), _const_spec((1, D)), _const_spec((D, LANES)),
                  _const_spec((CMP_LEN * D, D)), _const_spec((1, D)), _const_spec((D, LANES)),
                  _const_spec((1, LANES)), _const_spec((ncp, LANES)), _const_spec((ncp, LANES)),
                  _const_spec((LANES, LANES))],
        out_specs=[bg_spec(ncp, LANES), bg_spec(ncp, LANES)],
        out_shape=[sds((B * G, ncp, LANES), mxu), sds((B * G, ncp, LANES), mxu)],
        compiler_params=pltpu.CompilerParams(dimension_semantics=("parallel",), vmem_limit_bytes=VMEM_LIMIT),
        name="compress",
    )(chunks(kc_raw), chunks(vc_raw), cmp_k_pos.reshape(2, chunk_w), cmp_v_pos.reshape(2, chunk_w),
      cmp_k_w1.astype(mxu), row(cmp_k_b1), jnp.pad(cmp_k_w2, ((0, 0), (0, LANES - D))).astype(mxu),
      cmp_v_w1.astype(mxu), row(cmp_v_b1), jnp.tile(cmp_v_w2, (1, LANES // D)).astype(mxu),
      row(jnp.pad(k_norm_cmp, (0, LANES - D))), cos_c, sin_c, perm)
    k_cmp = k_cmp.reshape(B, G, ncp, LANES)
    v_cmp = v_cmp.reshape(B, G, ncp, LANES)

    k_gain = jnp.max(jnp.abs(jnp.stack([k_norm_cmp, k_norm_slc, k_norm_win])))
    logit_bound = jnp.max(jnp.abs(q_norm_gain)) * k_gain * (1.02 * D * SCALE * LOG2E)
    logit_bound = logit_bound.astype(F32).reshape(1)
    kv_spec = lambda r, w: pl.BlockSpec((1, G, r, w), lambda b, i: (b, 0, 0, 0))
    gate_spec = pl.BlockSpec((1, 2 * NSA_QT, NSA_WIDTH), lambda b, i: (b, i, 0))
    g3 = lambda a: a.reshape(B, S, NSA_WIDTH)
    nsa_call = lambda exact_shift: pl.pallas_call(
        functools.partial(_nsa_kernel, exact_shift=exact_shift),
        grid=(B, S // (2 * NSA_QT)),
        in_specs=[pl.BlockSpec(memory_space=pltpu.SMEM),
                  pl.BlockSpec((1, NSA_HEADS, 2 * NSA_QT, LANES), lambda b, i: (b, 0, i, 0)),
                  kv_spec(ncp, LANES), kv_spec(ncp, LANES), kv_spec(S, LANES), kv_spec(S, LANES),
                  kv_spec(S, LANES), kv_spec(S, LANES), _const_spec((LANES - D, ncp)),
                  gate_spec, gate_spec, gate_spec],
        out_specs=gate_spec,
        out_shape=sds((B, S, NSA_WIDTH), mxu),
        scratch_shapes=[pltpu.VMEM((2 * G, R * NSA_QT, LANES), F32)],
        compiler_params=pltpu.CompilerParams(dimension_semantics=("parallel", "arbitrary"),
                                             vmem_limit_bytes=VMEM_LIMIT),
        name="nsa_exact_shift" if exact_shift else "nsa",
    )
    nsa_operands = (logit_bound, q_nsa, k_cmp, v_cmp, ks_aug, vs_aug, k_win, vw_aug, ovl,
                    g3(g_cmp), g3(g_slc), g3(g_win))
    o_nsa = lax.cond(logit_bound[0] <= MAX_STATIC_SHIFT,
                     lambda ops: nsa_call(False)(*ops), lambda ops: nsa_call(True)(*ops), nsa_operands)

    sb3 = lambda a: a.reshape(B, S, SB_WIDTH)
    pair_q = pl.BlockSpec((1, SB_ROWS, SB_PAIRS * LANES), lambda b, hp, i: (b, i, hp))
    pair_kv = pl.BlockSpec((1, S, SB_PAIRS * LANES), lambda b, hp, i: (b, 0, hp))
    o_sb = pl.pallas_call(
        _sb_kernel,
        grid=(B, SB_WIDTH // (SB_PAIRS * LANES), S // SB_ROWS),
        in_specs=[pair_q, pair_kv, pair_kv, _const_spec((2 * LANES, 2 * LANES)), pair_q],
        out_specs=pair_q,
        out_shape=sds((B, S, SB_WIDTH), mxu),
        scratch_shapes=[pltpu.VMEM((SB_ROWS, LANES), F32)] * (3 * SB_PAIRS),
        compiler_params=pltpu.CompilerParams(dimension_semantics=("parallel", "parallel", "arbitrary"),
                                             vmem_limit_bytes=VMEM_LIMIT),
        name="stickbreak",
    )(sb3(q_sb), sb3(k_sb), sb3(v_sb), uu, sb3(g_sb))

    out_spec = lambda w: pl.BlockSpec((OUT_ROW_TILE, w), lambda t: (t, 0))
    out = pl.pallas_call(
        _outproj_kernel,
        grid=(n_tok // OUT_ROW_TILE,),
        in_specs=[out_spec(DM), out_spec(NSA_WIDTH), out_spec(SB_WIDTH),
                  _const_spec((NSA_WIDTH + SB_WIDTH, DM))],
        out_specs=out_spec(DM),
        out_shape=sds((n_tok, DM), x.dtype),
        compiler_params=pltpu.CompilerParams(dimension_semantics=("parallel",), vmem_limit_bytes=VMEM_LIMIT),
        name="outproj",
    )(x2, o_nsa.reshape(n_tok, NSA_WIDTH), o_sb.reshape(n_tok, SB_WIDTH), w_out.astype(mxu))
    return out.reshape(B, S, DM)
```
